```python
import jax
import jax.numpy as jnp
from jax import lax
import numpy as np

D_MODEL = 2048
BATCH = 8
SEQ = 4096
DEPTH = 2

CHUNK = 64
MIX_WIDTH = 2 * D_MODEL
POOL_WIDTH = MIX_WIDTH // 4
N_POOL_GROUPS = 4
POOL_GROUP_DIM = POOL_WIDTH // N_POOL_GROUPS
POOL_WINDOWS = (2, 4, 8, 16)
SB_WIDTH = MIX_WIDTH // 2
SB_HEAD_DIM = 128
SB_HEADS = SB_WIDTH // SB_HEAD_DIM
Q_BLOCK = 128
CONV_CH = MIX_WIDTH // 4
CONV_WIDTH = 3
N_BRANCHES = 3
IN_PROJ_DIM = 2 * POOL_WIDTH + 4 * SB_WIDTH + 4 * CONV_CH
RMS_EPS = 1e-6

kernel_name = "hybrid_pool_stickbreak_shortconv_trunk"


def rms_norm(x, g):
    xf = x.astype(jnp.float32)
    y = xf * lax.rsqrt(jnp.mean(xf * xf, axis=-1, keepdims=True) + RMS_EPS)
    return (y * g.astype(jnp.float32)).astype(x.dtype)


def split_input_projection(proj):
    sizes = (POOL_WIDTH, POOL_WIDTH,
             SB_WIDTH, SB_WIDTH, SB_WIDTH, SB_WIDTH,
             CONV_CH, CONV_CH, CONV_CH, CONV_CH)
    parts, off = [], 0
    for n in sizes:
        parts.append(proj[..., off:off + n])
        off += n
    return parts


def multiscale_pool(xa, pool_w, pool_scale):
    b, s, _ = xa.shape
    xg = xa.astype(jnp.float32).reshape(b, s, N_POOL_GROUPS, POOL_GROUP_DIM)
    cs = jnp.pad(jnp.cumsum(xg, axis=1), ((0, 0), (1, 0), (0, 0), (0, 0)))
    t = jnp.arange(s)
    pooled = []
    for g, w in enumerate(POOL_WINDOWS):
        start = jnp.maximum(t + 1 - w, 0)
        count = (t + 1 - start).astype(jnp.float32)
        win_sum = cs[:, 1:, g] - cs[:, start, g]
        pooled.append(win_sum / count[None, :, None])
    mixed = jnp.stack(pooled, axis=2) - xg
    y = jnp.einsum('bsgc,gcd->bsgd', mixed, pool_w.astype(jnp.float32))
    return (y.reshape(b, s, POOL_WIDTH) * pool_scale).astype(xa.dtype)


def stick_breaking_attention(q, k, v):
    b, s, h, dh = q.shape
    qf = q.astype(jnp.float32) * (dh ** -0.5)
    kf = k.astype(jnp.float32)
    vf = v.astype(jnp.float32)
    outs = []
    for i in range(s // Q_BLOCK):
        q0 = i * Q_BLOCK
        kv_len = q0 + Q_BLOCK
        z = jnp.einsum('bqhd,bkhd->bhqk', qf[:, q0:kv_len], kf[:, :kv_len])
        t_idx = q0 + jnp.arange(Q_BLOCK)[:, None]
        s_idx = jnp.arange(kv_len)[None, :]
        strict = s_idx < t_idx
        log_keep = jnp.where(strict, -jax.nn.softplus(z), 0.0)
        between = lax.cumsum(log_keep, axis=3, reverse=True) - log_keep
        log_a = jax.nn.log_sigmoid(z) + between
        a = jnp.where(strict, jnp.exp(log_a), 0.0)
        outs.append(jnp.einsum('bhqk,bkhd->bqhd', a, vf[:, :kv_len]))
    return jnp.concatenate(outs, axis=1).astype(q.dtype)


def short_gated_conv(u, b_gate, c_gate, conv_w):
    s = u.shape[1]
    v = c_gate * u
    vp = jnp.pad(v, ((0, 0), (CONV_WIDTH - 1, 0), (0, 0)))
    y = conv_w[0] * vp[:, 0:s]
    for i in range(1, CONV_WIDTH):
        y = y + conv_w[i] * vp[:, i:i + s]
    return b_gate * y


def hybrid_layer(x, c, norm_g, w_ada, b_ada, w_in, pool_w, pool_scale, conv_w,
                 w_br_a, w_br_b, w_br_c, w_gate, b_gate, w_out):
    b, s, _ = x.shape
    mod = jax.nn.silu(c) @ w_ada + b_ada
    shift, scale, res_gate = jnp.split(mod, 3, axis=-1)
    h = rms_norm(x, norm_g) * (1.0 + scale[:, None, :]) + shift[:, None, :]
    xa, za, q, k, v, zb, u, bg, cg, zc = split_input_projection(h @ w_in)
    ya = multiscale_pool(xa, pool_w, pool_scale) * jax.nn.silu(za)
    hs = (b, s, SB_HEADS, SB_HEAD_DIM)
    yb = stick_breaking_attention(q.reshape(hs), k.reshape(hs), v.reshape(hs))
    yb = yb.reshape(b, s, SB_WIDTH) * jax.nn.silu(zb)
    yc = short_gated_conv(u, bg, cg, conv_w) * jax.nn.silu(zc)
    gates = jax.nn.sigmoid((h @ w_gate + b_gate).astype(jnp.float32)).astype(x.dtype)
    gates = gates.reshape(b, s, N_BRANCHES, D_MODEL)
    merged = (gates[:, :, 0] * (ya @ w_br_a)
              + gates[:, :, 1] * (yb @ w_br_b)
              + gates[:, :, 2] * (yc @ w_br_c))
    return x + res_gate[:, None, :] * (merged @ w_out)


def _fwd_setup_inputs(seed: int = 0) -> dict:
    key = jax.random.key(seed)
    ks = jax.random.split(key, 18)

    def nrm(k, shape, scale):
        return scale * jax.random.normal(k, shape, jnp.float32)

    return {
        "x": nrm(ks[0], (BATCH, SEQ, D_MODEL), 1.0),
        "c": nrm(ks[1], (BATCH, D_MODEL), 1.0),
        "norm_g": 1.0 + nrm(ks[2], (DEPTH, D_MODEL), 0.1),
        "w_ada": nrm(ks[3], (DEPTH, D_MODEL, 3 * D_MODEL), D_MODEL ** -0.5),
        "b_ada": nrm(ks[4], (DEPTH, 3 * D_MODEL), 0.02),
        "w_in": nrm(ks[5], (DEPTH, D_MODEL, IN_PROJ_DIM), D_MODEL ** -0.5),
        "pool_w": nrm(ks[6], (DEPTH, N_POOL_GROUPS, POOL_GROUP_DIM, POOL_GROUP_DIM), POOL_GROUP_DIM ** -0.5),
        "pool_scale": 1.0 + nrm(ks[7], (DEPTH, POOL_WIDTH), 0.1),
        "conv_w": nrm(ks[8], (DEPTH, CONV_WIDTH, CONV_CH), CONV_WIDTH ** -0.5),
        "w_br_a": nrm(ks[9], (DEPTH, POOL_WIDTH, D_MODEL), POOL_WIDTH ** -0.5),
        "w_br_b": nrm(ks[10], (DEPTH, SB_WIDTH, D_MODEL), SB_WIDTH ** -0.5),
        "w_br_c": nrm(ks[11], (DEPTH, CONV_CH, D_MODEL), CONV_CH ** -0.5),
        "w_gate": nrm(ks[12], (DEPTH, D_MODEL, N_BRANCHES * D_MODEL), D_MODEL ** -0.5),
        "b_gate": nrm(ks[13], (DEPTH, N_BRANCHES * D_MODEL), 0.1),
        "w_out": nrm(ks[14], (DEPTH, D_MODEL, D_MODEL), D_MODEL ** -0.5),
        "final_g": 1.0 + nrm(ks[15], (D_MODEL,), 0.1),
    }


def _fwd_reference(x, c, norm_g, w_ada, b_ada, w_in, pool_w, pool_scale, conv_w,
              w_br_a, w_br_b, w_br_c, w_gate, b_gate, w_out, final_g):
    for l in range(DEPTH):
        x = hybrid_layer(x, c, norm_g[l], w_ada[l], b_ada[l], w_in[l], pool_w[l],
                         pool_scale[l], conv_w[l], w_br_a[l], w_br_b[l], w_br_c[l],
                         w_gate[l], b_gate[l], w_out[l])
    return rms_norm(x, final_g)


import jax as _jax
import jax.numpy as _jnp

TWIN_FORMAT = 'train_step'
FWD_PARAMS = ['x', 'c', 'norm_g', 'w_ada', 'b_ada', 'w_in', 'pool_w', 'pool_scale', 'conv_w', 'w_br_a', 'w_br_b', 'w_br_c', 'w_gate', 'b_gate', 'w_out', 'final_g']
TWIN_WEIGHTS = ['norm_g', 'w_ada', 'b_ada', 'w_in', 'pool_w', 'pool_scale', 'conv_w', 'w_br_a', 'w_br_b', 'w_br_c', 'w_gate', 'b_gate', 'w_out', 'final_g']
TWIN_DIFF_INPUT = 'x'
TWIN_INPUTS = ['x', 'c', 'norm_g', 'w_ada', 'b_ada', 'w_in', 'pool_w', 'pool_scale', 'conv_w', 'w_br_a', 'w_br_b', 'w_br_c', 'w_gate', 'b_gate', 'w_out', 'final_g', 'loss_target', 'm_norm_g', 'm_w_ada', 'm_b_ada', 'm_w_in', 'm_pool_w', 'm_pool_scale', 'm_conv_w', 'm_w_br_a', 'm_w_br_b', 'm_w_br_c', 'm_w_gate', 'm_b_gate', 'm_w_out', 'm_final_g', 'v_norm_g', 'v_w_ada', 'v_b_ada', 'v_w_in', 'v_pool_w', 'v_pool_scale', 'v_conv_w', 'v_w_br_a', 'v_w_br_b', 'v_w_br_c', 'v_w_gate', 'v_b_gate', 'v_w_out', 'v_final_g']
TWIN_OUTPUTS = ['loss', 'grad_x', 'grad_norm_g', 'grad_w_ada', 'grad_b_ada', 'grad_w_in', 'grad_pool_w', 'grad_pool_scale', 'grad_conv_w', 'grad_w_br_a', 'grad_w_br_b', 'grad_w_br_c', 'grad_w_gate', 'grad_b_gate', 'grad_w_out', 'grad_final_g', 'delta_norm_g', 'delta_w_ada', 'delta_b_ada', 'delta_w_in', 'delta_pool_w', 'delta_pool_scale', 'delta_conv_w', 'delta_w_br_a', 'delta_w_br_b', 'delta_w_br_c', 'delta_w_gate', 'delta_b_gate', 'delta_w_out', 'delta_final_g', 'new_m_norm_g', 'new_m_w_ada', 'new_m_b_ada', 'new_m_w_in', 'new_m_pool_w', 'new_m_pool_scale', 'new_m_conv_w', 'new_m_w_br_a', 'new_m_w_br_b', 'new_m_w_br_c', 'new_m_w_gate', 'new_m_b_gate', 'new_m_w_out', 'new_m_final_g', 'new_v_norm_g', 'new_v_w_ada', 'new_v_b_ada', 'new_v_w_in', 'new_v_pool_w', 'new_v_pool_scale', 'new_v_conv_w', 'new_v_w_br_a', 'new_v_w_br_b', 'new_v_w_br_c', 'new_v_w_gate', 'new_v_b_gate', 'new_v_w_out', 'new_v_final_g']
TWIN_LEAF_KINDS = {'loss': 'loss', 'grad_x': 'grad_x', 'grad_norm_g': 'grad_w', 'grad_w_ada': 'grad_w', 'grad_b_ada': 'grad_w', 'grad_w_in': 'grad_w', 'grad_pool_w': 'grad_w', 'grad_pool_scale': 'grad_w', 'grad_conv_w': 'grad_w', 'grad_w_br_a': 'grad_w', 'grad_w_br_b': 'grad_w', 'grad_w_br_c': 'grad_w', 'grad_w_gate': 'grad_w', 'grad_b_gate': 'grad_w', 'grad_w_out': 'grad_w', 'grad_final_g': 'grad_w', 'delta_norm_g': 'delta_w', 'delta_w_ada': 'delta_w', 'delta_b_ada': 'delta_w', 'delta_w_in': 'delta_w', 'delta_pool_w': 'delta_w', 'delta_pool_scale': 'delta_w', 'delta_conv_w': 'delta_w', 'delta_w_br_a': 'delta_w', 'delta_w_br_b': 'delta_w', 'delta_w_br_c': 'delta_w', 'delta_w_gate': 'delta_w', 'delta_b_gate': 'delta_w', 'delta_w_out': 'delta_w', 'delta_final_g': 'delta_w', 'new_m_norm_g': 'new_m', 'new_m_w_ada': 'new_m', 'new_m_b_ada': 'new_m', 'new_m_w_in': 'new_m', 'new_m_pool_w': 'new_m', 'new_m_pool_scale': 'new_m', 'new_m_conv_w': 'new_m', 'new_m_w_br_a': 'new_m', 'new_m_w_br_b': 'new_m', 'new_m_w_br_c': 'new_m', 'new_m_w_gate': 'new_m', 'new_m_b_gate': 'new_m', 'new_m_w_out': 'new_m', 'new_m_final_g': 'new_m', 'new_v_norm_g': 'new_v', 'new_v_w_ada': 'new_v', 'new_v_b_ada': 'new_v', 'new_v_w_in': 'new_v', 'new_v_pool_w': 'new_v', 'new_v_pool_scale': 'new_v', 'new_v_conv_w': 'new_v', 'new_v_w_br_a': 'new_v', 'new_v_w_br_b': 'new_v', 'new_v_w_br_c': 'new_v', 'new_v_w_gate': 'new_v', 'new_v_b_gate': 'new_v', 'new_v_w_out': 'new_v', 'new_v_final_g': 'new_v'}


def _forward(args):
    return _fwd_reference(*[args[k] for k in FWD_PARAMS])


def _output_shape():
    def fwd():
        inp = _fwd_setup_inputs(0)
        return _fwd_reference(*[inp[k] for k in FWD_PARAMS])
    out = _jax.eval_shape(fwd)
    return out.shape, out.dtype

N_MICROBATCH = 1
ADAM_LR = 0.001
ADAM_B1 = 0.9
ADAM_B2 = 0.999
ADAM_EPS = 1e-08
ADAM_WD = 0.01
ADAM_STEP = 10
PER_EXAMPLE_BATCH_AXIS = {'x': 0, 'c': 0, 'loss_target': 0}
SHARED_INPUTS = []
_WEIGHT_DTYPES = {'norm_g': _jnp.float32, 'w_ada': _jnp.float32, 'b_ada': _jnp.float32, 'w_in': _jnp.float32, 'pool_w': _jnp.float32, 'pool_scale': _jnp.float32, 'conv_w': _jnp.float32, 'w_br_a': _jnp.float32, 'w_br_b': _jnp.float32, 'w_br_c': _jnp.float32, 'w_gate': _jnp.float32, 'b_gate': _jnp.float32, 'w_out': _jnp.float32, 'final_g': _jnp.float32}
MOMENT_SCALE = {'norm_g': 1.000365e-01, 'w_ada': 1.591459e-01, 'b_ada': 3.602698e-01, 'w_in': 4.089256e-02, 'pool_w': 3.171573e-02, 'pool_scale': 3.225318e-02, 'conv_w': 7.034844e-02, 'w_br_a': 2.263576e-02, 'w_br_b': 2.278941e-02, 'w_br_c': 5.075210e-02, 'w_gate': 1.534452e-02, 'b_gate': 1.231689e-02, 'w_out': 6.180236e-02, 'final_g': 1.630405e+01}


def _to_microbatches(a, axis):
    t = _jnp.moveaxis(a, axis, 0)
    t = t.reshape((N_MICROBATCH, t.shape[0] // N_MICROBATCH) + t.shape[1:])
    return _jnp.moveaxis(t, 1, axis + 1)


def setup_inputs(seed: int = 0) -> dict:
    inp = _fwd_setup_inputs(seed)
    key = _jax.random.fold_in(_jax.random.key(seed), 7919)
    shape, _ = _output_shape()
    out = dict(inp)
    out["loss_target"] = _jax.random.normal(_jax.random.fold_in(key, 0), shape, _jnp.float32)
    for i, name in enumerate(TWIN_WEIGHTS):
        w = inp[name].astype(_jnp.float32)
        if MOMENT_SCALE is None:
            s = _jnp.sqrt(_jnp.mean(_jnp.square(w)) + 1e-30)
        else:
            s = MOMENT_SCALE[name]
        km, kv = _jax.random.split(_jax.random.fold_in(key, i + 1))
        out[name] = w
        out["m_" + name] = s * _jax.random.normal(km, w.shape, _jnp.float32)
        out["v_" + name] = (s * s) * _jax.random.uniform(kv, w.shape, _jnp.float32, 0.5, 1.5)
    if N_MICROBATCH > 1:
        for name, axis in PER_EXAMPLE_BATCH_AXIS.items():
            out[name] = _to_microbatches(out[name], axis)
    return {'x': out['x'], 'c': out['c'], 'norm_g': out['norm_g'], 'w_ada': out['w_ada'], 'b_ada': out['b_ada'], 'w_in': out['w_in'], 'pool_w': out['pool_w'], 'pool_scale': out['pool_scale'], 'conv_w': out['conv_w'], 'w_br_a': out['w_br_a'], 'w_br_b': out['w_br_b'], 'w_br_c': out['w_br_c'], 'w_gate': out['w_gate'], 'b_gate': out['b_gate'], 'w_out': out['w_out'], 'final_g': out['final_g'], 'loss_target': out['loss_target'], 'm_norm_g': out['m_norm_g'], 'm_w_ada': out['m_w_ada'], 'm_b_ada': out['m_b_ada'], 'm_w_in': out['m_w_in'], 'm_pool_w': out['m_pool_w'], 'm_pool_scale': out['m_pool_scale'], 'm_conv_w': out['m_conv_w'], 'm_w_br_a': out['m_w_br_a'], 'm_w_br_b': out['m_w_br_b'], 'm_w_br_c': out['m_w_br_c'], 'm_w_gate': out['m_w_gate'], 'm_b_gate': out['m_b_gate'], 'm_w_out': out['m_w_out'], 'm_final_g': out['m_final_g'], 'v_norm_g': out['v_norm_g'], 'v_w_ada': out['v_w_ada'], 'v_b_ada': out['v_b_ada'], 'v_w_in': out['v_w_in'], 'v_pool_w': out['v_pool_w'], 'v_pool_scale': out['v_pool_scale'], 'v_conv_w': out['v_conv_w'], 'v_w_br_a': out['v_w_br_a'], 'v_w_br_b': out['v_w_br_b'], 'v_w_br_c': out['v_w_br_c'], 'v_w_gate': out['v_w_gate'], 'v_b_gate': out['v_b_gate'], 'v_w_out': out['v_w_out'], 'v_final_g': out['v_final_g']}


def _loss(weights, diff, rest, loss_target):
    with _jax.named_scope("forward"):
        args = {**rest, TWIN_DIFF_INPUT: diff, **{k: w.astype(_WEIGHT_DTYPES[k]) for k, w in weights.items()}}
        y = _forward(args)
    with _jax.named_scope("loss_head"):
        err = _jnp.square(y.astype(_jnp.float32) - loss_target)
        return 0.5 * _jnp.sum(_jnp.mean(err, axis=-1)) if err.ndim else 0.5 * err


def _adamw(w, g, m, v):
    m = ADAM_B1 * m + (1.0 - ADAM_B1) * g
    v = ADAM_B2 * v + (1.0 - ADAM_B2) * _jnp.square(g)
    m_hat = m / (1.0 - ADAM_B1 ** ADAM_STEP)
    v_hat = v / (1.0 - ADAM_B2 ** ADAM_STEP)
    delta = -ADAM_LR * (m_hat / (_jnp.sqrt(v_hat) + ADAM_EPS) + ADAM_WD * w)
    return delta, m, v


def reference(x, c, norm_g, w_ada, b_ada, w_in, pool_w, pool_scale, conv_w, w_br_a, w_br_b, w_br_c, w_gate, b_gate, w_out, final_g, loss_target, m_norm_g, m_w_ada, m_b_ada, m_w_in, m_pool_w, m_pool_scale, m_conv_w, m_w_br_a, m_w_br_b, m_w_br_c, m_w_gate, m_b_gate, m_w_out, m_final_g, v_norm_g, v_w_ada, v_b_ada, v_w_in, v_pool_w, v_pool_scale, v_conv_w, v_w_br_a, v_w_br_b, v_w_br_c, v_w_gate, v_b_gate, v_w_out, v_final_g):
    given = dict(x=x, c=c, norm_g=norm_g, w_ada=w_ada, b_ada=b_ada, w_in=w_in, pool_w=pool_w, pool_scale=pool_scale, conv_w=conv_w, w_br_a=w_br_a, w_br_b=w_br_b, w_br_c=w_br_c, w_gate=w_gate, b_gate=b_gate, w_out=w_out, final_g=final_g, loss_target=loss_target, m_norm_g=m_norm_g, m_w_ada=m_w_ada, m_b_ada=m_b_ada, m_w_in=m_w_in, m_pool_w=m_pool_w, m_pool_scale=m_pool_scale, m_conv_w=m_conv_w, m_w_br_a=m_w_br_a, m_w_br_b=m_w_br_b, m_w_br_c=m_w_br_c, m_w_gate=m_w_gate, m_b_gate=m_b_gate, m_w_out=m_w_out, m_final_g=m_final_g, v_norm_g=v_norm_g, v_w_ada=v_w_ada, v_b_ada=v_b_ada, v_w_in=v_w_in, v_pool_w=v_pool_w, v_pool_scale=v_pool_scale, v_conv_w=v_conv_w, v_w_br_a=v_w_br_a, v_w_br_b=v_w_br_b, v_w_br_c=v_w_br_c, v_w_gate=v_w_gate, v_b_gate=v_b_gate, v_w_out=v_w_out, v_final_g=v_final_g)
    weights = {n: given[n] for n in TWIN_WEIGHTS}
    shared = {n: given[n] for n in SHARED_INPUTS}
    per_example = {n: given[n] for n in ['x', 'c']}
    grad_fn = _jax.value_and_grad(_loss, argnums=(0, 1))

    def one_microbatch(ex, loss_target):
        ex = dict(ex)
        diff = ex.pop(TWIN_DIFF_INPUT)
        return grad_fn(weights, diff, {**shared, **ex}, loss_target)

    if N_MICROBATCH == 1:
        loss, (grad_w, grad_x) = one_microbatch(per_example, given["loss_target"])
    else:
        def body(carry, xs):
            loss_sum, grad_sum = carry
            l_k, (gw_k, gx_k) = one_microbatch(xs[0], xs[1])
            with _jax.named_scope("update"):
                return (loss_sum + l_k, _jax.tree.map(_jnp.add, grad_sum, gw_k)), gx_k

        init = (_jnp.zeros((), _jnp.float32), _jax.tree.map(_jnp.zeros_like, weights))
        (loss, grad_w), grad_x = _jax.lax.scan(body, init, (per_example, given["loss_target"]))
    with _jax.named_scope("update"):
        delta_w, new_m, new_v = {}, {}, {}
        for n in TWIN_WEIGHTS:
            delta_w[n], new_m[n], new_v[n] = _adamw(weights[n], grad_w[n], given["m_" + n], given["v_" + n])
    return (loss, grad_x, *[grad_w[n] for n in TWIN_WEIGHTS], *[delta_w[n] for n in TWIN_WEIGHTS],
            *[new_m[n] for n in TWIN_WEIGHTS], *[new_v[n] for n in TWIN_WEIGHTS])
```

```python
import functools

import jax
import jax.numpy as jnp
from jax import lax
from jax.experimental import pallas as pl
from jax.experimental.pallas import tpu as pltpu

F32 = jnp.float32
BF16 = jnp.bfloat16
MESH = pl.DeviceIdType.MESH

N_CHIPS = 4
N_DEV = 8
N_GROUPS = 4
POOL_WINDOWS = (2, 4, 8, 16)
POOL_HALO = 16
CONV_HALO = 8
HEAD_DIM = 128
RMS_EPS = 1e-6
ADAM_LR = 0.001
ADAM_B1 = 0.9
ADAM_B2 = 0.999
ADAM_EPS = 1e-08
ADAM_WD = 0.01
ADAM_STEP = 10
V7X_VMEM_LIMIT = 56 * 1024 * 1024


def _tile(n, pref, mult=128):
    best = None
    t = mult
    while t <= min(n, pref):
        if n % t == 0:
            best = t
        t += mult
    return n if best is None else best


def _params(sem=None):
    return pltpu.CompilerParams(dimension_semantics=sem, vmem_limit_bytes=V7X_VMEM_LIMIT)


def _sigmoid(z):
    return jax.nn.sigmoid(z)


def _dsilu(z, sg):
    return sg * (1.0 + z * (1.0 - sg))


NN = ((1,), (0,))
NT = ((1,), (1,))
TN = ((0,), (0,))


def _mm(name, a, b, *, dims, grid, a_spec, b_spec, acc_shape, outs, extras=(), epilogue=None):
    nk = grid[2]
    ne, no = len(extras), len(outs)
    if epilogue is None:
        epilogue = lambda acc: (acc,)

    def body(a_ref, b_ref, *rest):
        ex, orefs, acc = rest[:ne], rest[ne:ne + no], rest[ne + no]
        k = pl.program_id(2)

        @pl.when(k == 0)
        def _():
            acc[...] = jnp.zeros_like(acc)

        acc[...] += lax.dot_general(a_ref[...].astype(BF16), b_ref[...].astype(BF16), (dims, ((), ())),
                                    preferred_element_type=F32)

        @pl.when(k == nk - 1)
        def _():
            vals = epilogue(acc[...], *[e[...] for e in ex])
            for o, v in zip(orefs, vals):
                o[...] = v.astype(o.dtype)

    res = pl.pallas_call(
        body, name=name, grid=grid,
        in_specs=[a_spec, b_spec] + [s for _, s in extras],
        out_specs=[s for _, s in outs], out_shape=[sh for sh, _ in outs],
        scratch_shapes=[pltpu.VMEM(acc_shape, F32)],
        compiler_params=_params(("parallel", "parallel", "arbitrary")),
    )(a, b, *[e for e, _ in extras])
    return res


class _Sharded:
    def __init__(self, arr, kind, layer=None):
        self.arr, self.kind, self.layer = arr, kind, layer
        r, c = arr.shape[-2:]
        self.rows = r * (N_CHIPS if kind == "row" else 1)
        self.cols = c * (N_CHIPS if kind == "col" else 1)
        self.sr, self.sc = r, c

    def spec(self, br, bc, f):
        lead = (None,) if self.layer is None else (None, None)
        layer = self.layer
        if self.kind == "col":
            per = self.sc // bc
            assert per * bc == self.sc and self.sr % br == 0, (self.arr.shape, br, bc)

            def idx(*g):
                rb, cb = f(*g)
                return ((cb // per,) + (() if layer is None else (layer,)) + (rb, cb % per))
        else:
            per = self.sr // br
            assert per * br == self.sr and self.sc % bc == 0, (self.arr.shape, br, bc)

            def idx(*g):
                rb, cb = f(*g)
                return ((rb // per,) + (() if layer is None else (layer,)) + (rb % per, cb))
        return pl.BlockSpec(lead + (br, bc), idx)


def _grad_buffer(rows, cols, kind):
    if kind == "col":
        return jax.ShapeDtypeStruct((N_CHIPS, rows, cols // N_CHIPS), F32)
    return jax.ShapeDtypeStruct((N_CHIPS, rows // N_CHIPS, cols), F32)


def _mm_nn(name, a, w, *, tm, tn, tk, outs=None, extras=(), epilogue=None, out_dtype=F32):
    m, kdim = a.shape
    n = w.cols
    grid = (m // tm, n // tn, kdim // tk)
    if outs is None:
        outs = [(jax.ShapeDtypeStruct((m, n), out_dtype), pl.BlockSpec((tm, tn), lambda i, j, k: (i, j)))]
    return _mm(name, a, w.arr, dims=NN, grid=grid,
               a_spec=pl.BlockSpec((tm, tk), lambda i, j, k: (i, k)),
               b_spec=w.spec(tk, tn, lambda i, j, k: (k, j)),
               acc_shape=(tm, tn), outs=outs, extras=extras, epilogue=epilogue)


def _mm_nt(name, a, w, *, tm, tn, tk, extras=(), epilogue=None, out_dtype=F32):
    m, kdim = a.shape
    n = w.rows
    grid = (m // tm, n // tn, kdim // tk)
    outs = [(jax.ShapeDtypeStruct((m, n), out_dtype), pl.BlockSpec((tm, tn), lambda i, j, k: (i, j)))]
    return _mm(name, a, w.arr, dims=NT, grid=grid,
               a_spec=pl.BlockSpec((tm, tk), lambda i, j, k: (i, k)),
               b_spec=w.spec(tn, tk, lambda i, j, k: (j, k)),
               acc_shape=(tm, tn), outs=outs, extras=extras, epilogue=epilogue)[0]


def _mm_tn(name, a, b, kind, *, tm, tn, tk):
    kdim, m = a.shape
    n = b.shape[1]
    grid = (m // tm, n // tn, kdim // tk)
    out = _Sharded(_grad_buffer(m, n, kind), kind)
    outs = [(out.arr, out.spec(tm, tn, lambda i, j, k: (i, j)))]
    return _mm(name, a, b, dims=TN, grid=grid,
               a_spec=pl.BlockSpec((tk, tm), lambda i, j, k: (k, i)),
               b_spec=pl.BlockSpec((tk, tn), lambda i, j, k: (k, j)),
               acc_shape=(tm, tn), outs=outs)[0]


def _row_spec(tr, w, col=0):
    return pl.BlockSpec((tr, w), lambda i: (i, col))


def _vec_spec(w, col=0, rows=1):
    return pl.BlockSpec((rows, w), lambda i: (0, col))


def _norm_fwd(x, g, scale, shift, tr):
    s, d = x.shape

    def body(x_ref, g_ref, sc_ref, sh_ref, h_ref):
        xv = x_ref[...]
        r = lax.rsqrt(jnp.mean(xv * xv, axis=-1, keepdims=True) + RMS_EPS)
        y = xv * r * g_ref[...]
        h_ref[...] = (y * (1.0 + sc_ref[...]) + sh_ref[...]).astype(BF16)

    return pl.pallas_call(
        body, name="norm_fwd", grid=(s // tr,),
        in_specs=[_row_spec(tr, d), _vec_spec(d), _vec_spec(d), _vec_spec(d)],
        out_specs=_row_spec(tr, d), out_shape=jax.ShapeDtypeStruct((s, d), BF16),
        compiler_params=_params(("parallel",)),
    )(x, g, scale, shift)


def _norm_bwd(dh, x, g, scale, dxo, tr):
    s, d = x.shape

    def body(dh_ref, x_ref, g_ref, sc_ref, dxo_ref, dx_ref, dsh_ref, dsc_ref, dg_ref):
        @pl.when(pl.program_id(0) == 0)
        def _():
            dsh_ref[...] = jnp.zeros_like(dsh_ref)
            dsc_ref[...] = jnp.zeros_like(dsc_ref)
            dg_ref[...] = jnp.zeros_like(dg_ref)

        xv, dhv, gv = x_ref[...], dh_ref[...], g_ref[...]
        r = lax.rsqrt(jnp.mean(xv * xv, axis=-1, keepdims=True) + RMS_EPS)
        xn = xv * r
        dsh_ref[...] += jnp.sum(dhv, axis=0, keepdims=True)
        dsc_ref[...] += jnp.sum(dhv * (xn * gv), axis=0, keepdims=True)
        dyg = dhv * (1.0 + sc_ref[...])
        dg_ref[...] += jnp.sum(dyg * xn, axis=0, keepdims=True)
        dxn = dyg * gv
        dx_ref[...] = dxo_ref[...] + r * (dxn - xn * jnp.mean(dxn * xn, axis=-1, keepdims=True))

    vec = jax.ShapeDtypeStruct((1, d), F32)
    return pl.pallas_call(
        body, name="norm_bwd", grid=(s // tr,),
        in_specs=[_row_spec(tr, d), _row_spec(tr, d), _vec_spec(d), _vec_spec(d), _row_spec(tr, d)],
        out_specs=[_row_spec(tr, d), _vec_spec(d), _vec_spec(d), _vec_spec(d)],
        out_shape=[jax.ShapeDtypeStruct((s, d), F32), vec, vec, vec],
        compiler_params=_params(("arbitrary",)),
    )(dh, x, g, scale, dxo)


def _final_loss(x, g, target, tr):
    s, d = x.shape

    def body(x_ref, g_ref, t_ref, loss_ref, dx_ref, dg_ref):
        @pl.when(pl.program_id(0) == 0)
        def _():
            loss_ref[...] = jnp.zeros_like(loss_ref)
            dg_ref[...] = jnp.zeros_like(dg_ref)

        xv, gv = x_ref[...], g_ref[...]
        r = lax.rsqrt(jnp.mean(xv * xv, axis=-1, keepdims=True) + RMS_EPS)
        xn = xv * r
        err = xn * gv - t_ref[...]
        per_row = jnp.mean(err * err, axis=-1, keepdims=True)
        loss_ref[...] += 0.5 * jnp.sum(per_row, axis=0, keepdims=True)
        dy = err * (1.0 / d)
        dg_ref[...] += jnp.sum(dy * xn, axis=0, keepdims=True)
        dxn = dy * gv
        dx_ref[...] = r * (dxn - xn * jnp.mean(dxn * xn, axis=-1, keepdims=True))

    return pl.pallas_call(
        body, name="final_loss", grid=(s // tr,),
        in_specs=[_row_spec(tr, d), _vec_spec(d), _row_spec(tr, d)],
        out_specs=[pl.BlockSpec((8, 128), lambda i: (0, 0)), _row_spec(tr, d), _vec_spec(d)],
        out_shape=[jax.ShapeDtypeStruct((8, 128), F32), jax.ShapeDtypeStruct((s, d), F32),
                   jax.ShapeDtypeStruct((1, d), F32)],
        compiler_params=_params(("arbitrary",)),
    )(x, g, target)


def _halo_before(tr, halo, w, col):
    per = tr // halo
    return pl.BlockSpec((halo, w), lambda i: (jnp.maximum(i * per - 1, 0), col))


def _halo_after(tr, halo, w, col, n_tiles):
    per = tr // halo
    return pl.BlockSpec((halo, w), lambda i: (jnp.minimum((i + 1) * per, n_tiles * per - 1), col))


def _pool_fwd(proj, pool_w, pool_scale, d, tr):
    s = proj.shape[0]
    pw, gd, hl = d // 2, d // 8, POOL_HALO

    def body(xa_ref, xh_ref, za_ref, w_ref, ps_ref, ya_ref, buf):
        i = pl.program_id(0)
        buf[0:hl, :] = jnp.where(i > 0, xh_ref[...], 0.0)
        buf[hl:hl + tr, :] = xa_ref[...]
        row = i * tr + lax.broadcasted_iota(jnp.int32, (tr, 1), 0)
        za = za_ref[...]
        gate = za * _sigmoid(za)
        for g, win in enumerate(POOL_WINDOWS):
            cs = slice(g * gd, (g + 1) * gd)
            xg = buf[hl:hl + tr, cs]
            acc = xg
            for j in range(1, win):
                acc = acc + buf[hl - j:hl - j + tr, cs]
            cnt = jnp.minimum(row + 1, win).astype(F32)
            mixed = acc / cnt - xg
            y = jnp.dot(mixed.astype(BF16), w_ref[g], preferred_element_type=F32)
            ya_ref[:, cs] = ((y * ps_ref[:, cs]) * gate[:, cs]).astype(BF16)

    return pl.pallas_call(
        body, name="pool_fwd", grid=(s // tr,),
        in_specs=[_row_spec(tr, pw, 0), _halo_before(tr, hl, pw, 0), _row_spec(tr, pw, 1),
                  pl.BlockSpec((N_GROUPS, gd, gd), lambda i: (0, 0, 0)), _vec_spec(pw)],
        out_specs=_row_spec(tr, pw), out_shape=jax.ShapeDtypeStruct((s, pw), BF16),
        scratch_shapes=[pltpu.VMEM((tr + hl, pw), F32)],
        compiler_params=_params(("parallel",)),
    )(proj, proj, proj, pool_w, pool_scale)


def _pool_bwd(proj, dya, pool_w, pool_scale, d, tr):
    s = proj.shape[0]
    pw, gd, hl = d // 2, d // 8, POOL_HALO
    n_tiles = s // tr

    def body(xa_ref, xh_ref, za_ref, zh_ref, dya_ref, dyh_ref, w_ref, ps_ref, da_ref, gw_ref, gs_ref, buf, dbuf):
        i = pl.program_id(0)

        @pl.when(i == 0)
        def _():
            gw_ref[...] = jnp.zeros_like(gw_ref)
            gs_ref[...] = jnp.zeros_like(gs_ref)

        buf[0:hl, :] = jnp.where(i > 0, xh_ref[...], 0.0)
        buf[hl:hl + tr, :] = xa_ref[...]
        row = i * tr + lax.broadcasted_iota(jnp.int32, (tr, 1), 0)
        row_h = (i + 1) * tr + lax.broadcasted_iota(jnp.int32, (hl, 1), 0)
        za, dya_t = za_ref[...], dya_ref[...]
        sg = _sigmoid(za)
        gate = za * sg
        dpre = dya_t * gate
        zh = zh_ref[...]
        dpre_h = jnp.where(i < n_tiles - 1, dyh_ref[...], 0.0) * (zh * _sigmoid(zh))
        ps = ps_ref[...]
        for g, win in enumerate(POOL_WINDOWS):
            cs = slice(g * gd, (g + 1) * gd)
            wg = w_ref[g]
            xg = buf[hl:hl + tr, cs]
            acc = xg
            for j in range(1, win):
                acc = acc + buf[hl - j:hl - j + tr, cs]
            cnt = jnp.minimum(row + 1, win).astype(F32)
            mixed = (acc / cnt - xg).astype(BF16)
            ylin = jnp.dot(mixed, wg, preferred_element_type=F32)
            gs_ref[:, cs] += jnp.sum(dpre[:, cs] * ylin, axis=0, keepdims=True)
            da_ref[:, pw + g * gd:pw + (g + 1) * gd] = (
                dya_t[:, cs] * (ylin * ps[:, cs]) * _dsilu(za[:, cs], sg[:, cs])).astype(BF16)
            dyl = (dpre[:, cs] * ps[:, cs]).astype(BF16)
            gw_ref[g] += lax.dot_general(mixed, dyl, (TN, ((), ())), preferred_element_type=F32)
            dmix = lax.dot_general(dyl, wg, (NT, ((), ())), preferred_element_type=F32)
            dyl_h = (dpre_h[:, cs] * ps[:, cs]).astype(BF16)
            dmix_h = lax.dot_general(dyl_h, wg, (NT, ((), ())), preferred_element_type=F32)
            cnt_h = jnp.minimum(row_h + 1, win).astype(F32)
            dbuf[0:tr, cs] = dmix / cnt
            dbuf[tr:tr + hl, cs] = dmix_h / cnt_h
            dx = dbuf[0:tr, cs] - dmix
            for j in range(1, win):
                dx = dx + dbuf[j:j + tr, cs]
            da_ref[:, cs] = dx.astype(BF16)

    return pl.pallas_call(
        body, name="pool_bwd", grid=(n_tiles,),
        in_specs=[_row_spec(tr, pw, 0), _halo_before(tr, hl, pw, 0),
                  _row_spec(tr, pw, 1), _halo_after(tr, hl, pw, 1, n_tiles),
                  _row_spec(tr, pw, 0), _halo_after(tr, hl, pw, 0, n_tiles),
                  pl.BlockSpec((N_GROUPS, gd, gd), lambda i: (0, 0, 0)), _vec_spec(pw)],
        out_specs=[_row_spec(tr, 2 * pw), pl.BlockSpec((N_GROUPS, gd, gd), lambda i: (0, 0, 0)), _vec_spec(pw)],
        out_shape=[jax.ShapeDtypeStruct((s, 2 * pw), BF16), jax.ShapeDtypeStruct((N_GROUPS, gd, gd), F32),
                   jax.ShapeDtypeStruct((1, pw), F32)],
        scratch_shapes=[pltpu.VMEM((tr + hl, pw), F32), pltpu.VMEM((tr + hl, pw), F32)],
        compiler_params=_params(("arbitrary",)),
    )(proj, proj, proj, proj, dya, dya, pool_w, pool_scale)


def _conv_fwd(proj, conv_w, d, tr):
    s = proj.shape[0]
    cc, hl = d // 2, CONV_HALO
    cu, cb, cg_, cz = 10, 11, 12, 13

    def body(u_ref, uh_ref, bg_ref, cg_ref, ch_ref, zc_ref, w_ref, yc_ref, buf):
        i = pl.program_id(0)
        buf[0:hl, :] = jnp.where(i > 0, ch_ref[...] * uh_ref[...], 0.0)
        buf[hl:hl + tr, :] = cg_ref[...] * u_ref[...]
        y = w_ref[0:1, :] * buf[hl - 2:hl - 2 + tr, :]
        y = y + w_ref[1:2, :] * buf[hl - 1:hl - 1 + tr, :]
        y = y + w_ref[2:3, :] * buf[hl:hl + tr, :]
        zc = zc_ref[...]
        yc_ref[...] = ((bg_ref[...] * y) * (zc * _sigmoid(zc))).astype(BF16)

    return pl.pallas_call(
        body, name="conv_fwd", grid=(s // tr,),
        in_specs=[_row_spec(tr, cc, cu), _halo_before(tr, hl, cc, cu), _row_spec(tr, cc, cb),
                  _row_spec(tr, cc, cg_), _halo_before(tr, hl, cc, cg_), _row_spec(tr, cc, cz),
                  _vec_spec(cc, rows=3)],
        out_specs=_row_spec(tr, cc), out_shape=jax.ShapeDtypeStruct((s, cc), BF16),
        scratch_shapes=[pltpu.VMEM((tr + hl, cc), F32)],
        compiler_params=_params(("parallel",)),
    )(proj, proj, proj, proj, proj, proj, conv_w)


def _conv_bwd(proj, dyc, conv_w, d, tr):
    s = proj.shape[0]
    cc, hl = d // 2, CONV_HALO
    cu, cb, cg_, cz = 10, 11, 12, 13
    n_tiles = s // tr

    def body(u_ref, uh_ref, bg_ref, bh_ref, cg_ref, ch_ref, zc_ref, zh_ref, dy_ref, dyh_ref, w_ref,
             dc_ref, gw_ref, buf, dbuf):
        i = pl.program_id(0)

        @pl.when(i == 0)
        def _():
            gw_ref[...] = jnp.zeros_like(gw_ref)

        u, bg, cg, zc, dyc_t = u_ref[...], bg_ref[...], cg_ref[...], zc_ref[...], dy_ref[...]
        buf[0:hl, :] = jnp.where(i > 0, ch_ref[...] * uh_ref[...], 0.0)
        buf[hl:hl + tr, :] = cg * u
        v2, v1, v0 = buf[hl - 2:hl - 2 + tr, :], buf[hl - 1:hl - 1 + tr, :], buf[hl:hl + tr, :]
        w0, w1, w2 = w_ref[0:1, :], w_ref[1:2, :], w_ref[2:3, :]
        y = w0 * v2 + w1 * v1 + w2 * v0
        sg = _sigmoid(zc)
        gate = zc * sg
        dc_ref[:, cc:2 * cc] = (dyc_t * y * gate).astype(BF16)
        dc_ref[:, 3 * cc:4 * cc] = (dyc_t * bg * y * _dsilu(zc, sg)).astype(BF16)
        dy = dyc_t * bg * gate
        zh = zh_ref[...]
        dy_h = jnp.where(i < n_tiles - 1, dyh_ref[...], 0.0) * bh_ref[...] * (zh * _sigmoid(zh))
        gw_ref[0:1, :] += jnp.sum(dy * v2, axis=0, keepdims=True)
        gw_ref[1:2, :] += jnp.sum(dy * v1, axis=0, keepdims=True)
        gw_ref[2:3, :] += jnp.sum(dy * v0, axis=0, keepdims=True)
        dbuf[0:tr, :] = dy
        dbuf[tr:tr + hl, :] = dy_h
        dv = w2 * dy + w1 * dbuf[1:1 + tr, :] + w0 * dbuf[2:2 + tr, :]
        dc_ref[:, 0:cc] = (dv * cg).astype(BF16)
        dc_ref[:, 2 * cc:3 * cc] = (dv * u).astype(BF16)

    return pl.pallas_call(
        body, name="conv_bwd", grid=(n_tiles,),
        in_specs=[_row_spec(tr, cc, cu), _halo_before(tr, hl, cc, cu),
                  _row_spec(tr, cc, cb), _halo_after(tr, hl, cc, cb, n_tiles),
                  _row_spec(tr, cc, cg_), _halo_before(tr, hl, cc, cg_),
                  _row_spec(tr, cc, cz), _halo_after(tr, hl, cc, cz, n_tiles),
                  _row_spec(tr, cc, 0), _halo_after(tr, hl, cc, 0, n_tiles),
                  _vec_spec(cc, rows=3)],
        out_specs=[_row_spec(tr, 4 * cc), _vec_spec(cc, rows=3)],
        out_shape=[jax.ShapeDtypeStruct((s, 4 * cc), BF16), jax.ShapeDtypeStruct((3, cc), F32)],
        scratch_shapes=[pltpu.VMEM((tr + hl, cc), F32), pltpu.VMEM((tr + hl, cc), F32)],
        compiler_params=_params(("arbitrary",)),
    )(proj, proj, proj, proj, proj, proj, proj, proj, dyc, dyc, conv_w)


def _merge_fwd(gates, pa, pb, pc, tr):
    s, d = pa.shape

    def body(g_ref, a_ref, b_ref, c_ref, m_ref):
        m = g_ref[:, 0:d] * a_ref[...] + g_ref[:, d:2 * d] * b_ref[...] + g_ref[:, 2 * d:3 * d] * c_ref[...]
        m_ref[...] = m.astype(BF16)

    return pl.pallas_call(
        body, name="merge_fwd", grid=(s // tr,),
        in_specs=[_row_spec(tr, 3 * d), _row_spec(tr, d), _row_spec(tr, d), _row_spec(tr, d)],
        out_specs=_row_spec(tr, d), out_shape=jax.ShapeDtypeStruct((s, d), BF16),
        compiler_params=_params(("parallel",)),
    )(gates, pa, pb, pc)


def _merge_bwd(dm, gates, pa, pb, pc, tr):
    s, d = pa.shape

    def body(dm_ref, g_ref, a_ref, b_ref, c_ref, da_ref, db_ref, dc_ref, dg_ref, gb_ref):
        @pl.when(pl.program_id(0) == 0)
        def _():
            gb_ref[...] = jnp.zeros_like(gb_ref)

        dmv = dm_ref[...]
        for n, (p_ref, o_ref) in enumerate(((a_ref, da_ref), (b_ref, db_ref), (c_ref, dc_ref))):
            gv = g_ref[:, n * d:(n + 1) * d]
            o_ref[...] = (dmv * gv).astype(BF16)
            dlogit = (dmv * p_ref[...]) * (gv * (1.0 - gv))
            dg_ref[:, n * d:(n + 1) * d] = dlogit.astype(BF16)
            gb_ref[:, n * d:(n + 1) * d] += jnp.sum(dlogit, axis=0, keepdims=True)

    act = jax.ShapeDtypeStruct((s, d), BF16)
    return pl.pallas_call(
        body, name="merge_bwd", grid=(s // tr,),
        in_specs=[_row_spec(tr, d), _row_spec(tr, 3 * d), _row_spec(tr, d), _row_spec(tr, d), _row_spec(tr, d)],
        out_specs=[_row_spec(tr, d), _row_spec(tr, d), _row_spec(tr, d), _row_spec(tr, 3 * d), _vec_spec(3 * d)],
        out_shape=[act, act, act, jax.ShapeDtypeStruct((s, 3 * d), BF16), jax.ShapeDtypeStruct((1, 3 * d), F32)],
        compiler_params=_params(("arbitrary",)),
    )(dm, gates, pa, pb, pc)


def _resid_bwd(dxo, out, rg, tr):
    s, d = dxo.shape

    def body(dx_ref, o_ref, rg_ref, do_ref, drg_ref):
        @pl.when(pl.program_id(0) == 0)
        def _():
            drg_ref[...] = jnp.zeros_like(drg_ref)

        dxv = dx_ref[...]
        do_ref[...] = (dxv * rg_ref[...]).astype(BF16)
        drg_ref[...] += jnp.sum(dxv * o_ref[...], axis=0, keepdims=True)

    return pl.pallas_call(
        body, name="resid_bwd", grid=(s // tr,),
        in_specs=[_row_spec(tr, d), _row_spec(tr, d), _vec_spec(d)],
        out_specs=[_row_spec(tr, d), _vec_spec(d)],
        out_shape=[jax.ShapeDtypeStruct((s, d), BF16), jax.ShapeDtypeStruct((1, d), F32)],
        compiler_params=_params(("arbitrary",)),
    )(dxo, out, rg)


def _split_bf16(v):
    hi = v.astype(BF16)
    return hi, (v - hi.astype(F32)).astype(BF16)


def _log_keep(z):
    e = jnp.exp(-jnp.abs(z))
    return -(jnp.maximum(z, 0.0) + jnp.log(1.0 + e)), e


def _sb_fwd(proj, d, blk):
    s = proj.shape[0]
    heads, nq = d // HEAD_DIM, s // blk
    qc, kc, vc, zc = d // HEAD_DIM, 2 * d // HEAD_DIM, 3 * d // HEAD_DIM, 4 * d // HEAD_DIM
    scale = HEAD_DIM ** -0.5
    assert nq <= HEAD_DIM

    def body(q_ref, k_ref, v_ref, zb_ref, att_ref, yb_ref, rs_ref):
        i = pl.program_id(1)
        q = (q_ref[...] * scale).astype(BF16)
        r_io = lax.broadcasted_iota(jnp.int32, (blk, blk), 0)
        c_io = lax.broadcasted_iota(jnp.int32, (blk, blk), 1)
        tri = (r_io >= c_io).astype(BF16)
        strict = c_io < r_io
        lane = lax.broadcasted_iota(jnp.int32, (blk, HEAD_DIM), 1)

        def block(j, carry, masked):
            run, acc, rs = carry
            ks = pl.multiple_of(j * blk, blk)
            kb = k_ref[pl.ds(ks, blk), :].astype(BF16)
            vb = v_ref[pl.ds(ks, blk), :].astype(BF16)
            z = lax.dot_general(q, kb, (NT, ((), ())), preferred_element_type=F32)
            lk, _ = _log_keep(z)
            if masked:
                lk = jnp.where(strict, lk, 0.0)
            rs = jnp.where(lane == j, run, rs)
            hi, lo = _split_bf16(lk)
            csum = (run + jnp.dot(hi, tri, preferred_element_type=F32)
                    + jnp.dot(lo, tri, preferred_element_type=F32))
            a = jnp.exp(z + csum)
            if masked:
                a = jnp.where(strict, a, 0.0)
            acc = acc + jnp.dot(a.astype(BF16), vb, preferred_element_type=F32)
            return csum[:, 0:1], acc, rs

        carry = (jnp.zeros((blk, 1), F32), jnp.zeros((blk, HEAD_DIM), F32), jnp.zeros((blk, HEAD_DIM), F32))
        carry = block(i, carry, True)
        _, acc, rs = lax.fori_loop(0, i, lambda jj, cr: block(i - 1 - jj, cr, False), carry)
        att_ref[...] = acc
        rs_ref[...] = rs
        zb = zb_ref[...]
        yb_ref[...] = (acc * (zb * _sigmoid(zb))).astype(BF16)

    blk_spec = pl.BlockSpec((blk, HEAD_DIM), lambda h, i: (i, h))
    return pl.pallas_call(
        body, name="sb_fwd", grid=(heads, nq),
        in_specs=[pl.BlockSpec((blk, HEAD_DIM), lambda h, i: (i, qc + h)),
                  pl.BlockSpec((s, HEAD_DIM), lambda h, i: (0, kc + h)),
                  pl.BlockSpec((s, HEAD_DIM), lambda h, i: (0, vc + h)),
                  pl.BlockSpec((blk, HEAD_DIM), lambda h, i: (i, zc + h))],
        out_specs=[blk_spec, blk_spec, blk_spec],
        out_shape=[jax.ShapeDtypeStruct((s, d), F32), jax.ShapeDtypeStruct((s, d), BF16),
                   jax.ShapeDtypeStruct((s, d), F32)],
        compiler_params=_params(("parallel", "arbitrary")),
    )(proj, proj, proj, proj)


def _sb_bwd(proj, att, dyb, rsave, d, blk):
    s = proj.shape[0]
    heads, nq = d // HEAD_DIM, s // blk
    qc, kc, vc, zc = d // HEAD_DIM, 2 * d // HEAD_DIM, 3 * d // HEAD_DIM, 4 * d // HEAD_DIM
    scale = HEAD_DIM ** -0.5

    def body(q_ref, k_ref, v_ref, zb_ref, att_ref, dyb_ref, rs_ref, dq_ref, dk_ref, dv_ref, dzb_ref):
        i = pl.program_id(1)

        @pl.when(i == 0)
        def _():
            dk_ref[...] = jnp.zeros_like(dk_ref)
            dv_ref[...] = jnp.zeros_like(dv_ref)

        q = (q_ref[...] * scale).astype(BF16)
        zb, dyb_t, att_t = zb_ref[...], dyb_ref[...], att_ref[...]
        sg = _sigmoid(zb)
        dzb_ref[...] = dyb_t * att_t * _dsilu(zb, sg)
        do = (dyb_t * (zb * sg)).astype(BF16)
        r_io = lax.broadcasted_iota(jnp.int32, (blk, blk), 0)
        c_io = lax.broadcasted_iota(jnp.int32, (blk, blk), 1)
        tri = (r_io >= c_io).astype(BF16)
        tri_up = (r_io <= c_io).astype(BF16)
        strict = c_io < r_io
        lane = lax.broadcasted_iota(jnp.int32, (blk, HEAD_DIM), 1)
        rs = rs_ref[...]

        def block(j, carry, masked):
            gsum, dq = carry
            ks = pl.multiple_of(j * blk, blk)
            kb = k_ref[pl.ds(ks, blk), :].astype(BF16)
            vb = v_ref[pl.ds(ks, blk), :].astype(BF16)
            z = lax.dot_general(q, kb, (NT, ((), ())), preferred_element_type=F32)
            lk, e = _log_keep(z)
            beta = jnp.where(z >= 0, 1.0, e) / (1.0 + e)
            if masked:
                lk = jnp.where(strict, lk, 0.0)
            run = jnp.sum(jnp.where(lane == j, rs, 0.0), axis=1, keepdims=True)
            hi, lo = _split_bf16(lk)
            csum = (run + jnp.dot(hi, tri, preferred_element_type=F32)
                    + jnp.dot(lo, tri, preferred_element_type=F32))
            a = jnp.exp(z + csum)
            if masked:
                a = jnp.where(strict, a, 0.0)
            da = lax.dot_general(do, vb, (NT, ((), ())), preferred_element_type=F32)
            g = a * da
            ghi, glo = _split_bf16(g)
            gcum = (gsum + jnp.dot(ghi, tri_up, preferred_element_type=F32)
                    + jnp.dot(glo, tri_up, preferred_element_type=F32))
            dz = g - beta * gcum
            if masked:
                dz = jnp.where(strict, dz, 0.0)
            dzb16 = dz.astype(BF16)
            dq = dq + jnp.dot(dzb16, kb, preferred_element_type=F32)
            dk_ref[pl.ds(ks, blk), :] += lax.dot_general(dzb16, q, (TN, ((), ())), preferred_element_type=F32)
            dv_ref[pl.ds(ks, blk), :] += lax.dot_general(a.astype(BF16), do, (TN, ((), ())),
                                                        preferred_element_type=F32)
            return gcum[:, blk - 1:blk], dq

        carry = (jnp.zeros((blk, 1), F32), jnp.zeros((blk, HEAD_DIM), F32))
        carry = lax.fori_loop(0, i, lambda j, cr: block(j, cr, False), carry)
        _, dq = block(i, carry, True)
        dq_ref[...] = dq * scale

    blk_spec = pl.BlockSpec((blk, HEAD_DIM), lambda h, i: (i, h))
    full_spec = pl.BlockSpec((s, HEAD_DIM), lambda h, i: (0, h))
    act = jax.ShapeDtypeStruct((s, d), F32)
    return pl.pallas_call(
        body, name="sb_bwd", grid=(heads, nq),
        in_specs=[pl.BlockSpec((blk, HEAD_DIM), lambda h, i: (i, qc + h)),
                  pl.BlockSpec((s, HEAD_DIM), lambda h, i: (0, kc + h)),
                  pl.BlockSpec((s, HEAD_DIM), lambda h, i: (0, vc + h)),
                  pl.BlockSpec((blk, HEAD_DIM), lambda h, i: (i, zc + h)),
                  blk_spec, blk_spec, blk_spec],
        out_specs=[blk_spec, full_spec, full_spec, blk_spec],
        out_shape=[act, act, act, act],
        compiler_params=_params(("parallel", "arbitrary")),
    )(proj, proj, proj, proj, att, dyb, rsave)


def _adamw(w, g, m, v):
    shape = w.shape
    cols = shape[-1]
    rows = w.size // cols
    w2, g2, m2, v2 = (t.reshape(rows, cols) for t in (w, g, m, v))
    tr, tc = _tile(rows, 512, 8), _tile(cols, 1024, 128)
    c1 = 1.0 - ADAM_B1 ** ADAM_STEP
    c2 = 1.0 - ADAM_B2 ** ADAM_STEP

    def body(w_ref, g_ref, m_ref, v_ref, d_ref, nm_ref, nv_ref):
        gv = g_ref[...]
        nm = ADAM_B1 * m_ref[...] + (1.0 - ADAM_B1) * gv
        nv = ADAM_B2 * v_ref[...] + (1.0 - ADAM_B2) * (gv * gv)
        d_ref[...] = -ADAM_LR * ((nm / c1) / (jnp.sqrt(nv / c2) + ADAM_EPS) + ADAM_WD * w_ref[...])
        nm_ref[...] = nm
        nv_ref[...] = nv

    spec = pl.BlockSpec((tr, tc), lambda i, j: (i, j))
    sd = jax.ShapeDtypeStruct((rows, cols), F32)
    outs = pl.pallas_call(
        body, name="adamw", grid=(rows // tr, cols // tc),
        in_specs=[spec] * 4, out_specs=[spec] * 3, out_shape=[sd] * 3,
        compiler_params=_params(("parallel", "parallel")),
    )(w2, g2, m2, v2)
    return tuple(o.reshape(shape) for o in outs)


def _sum_blocks(gathered, n, rows):
    width = gathered.shape[1]
    tw = _tile(width, 8192)

    def body(g_ref, o_ref):
        acc = g_ref[0:rows, :]
        for b in range(1, n):
            acc = acc + g_ref[b * rows:(b + 1) * rows, :]
        o_ref[...] = acc

    return pl.pallas_call(
        body, name="sum_blocks", grid=(width // tw,),
        in_specs=[pl.BlockSpec((n * rows, tw), lambda i: (0, i))],
        out_specs=pl.BlockSpec((rows, tw), lambda i: (0, i)),
        out_shape=jax.ShapeDtypeStruct((rows, width), F32),
        compiler_params=_params(("parallel",)),
    )(gathered)


def _silu_bf16(c_rows):
    def body(c_ref, o_ref):
        cv = c_ref[...]
        o_ref[...] = (cv * _sigmoid(cv)).astype(BF16)

    return pl.pallas_call(
        body, name="silu_c", out_shape=jax.ShapeDtypeStruct(c_rows.shape, BF16),
        in_specs=[pl.BlockSpec(memory_space=pltpu.VMEM)], out_specs=pl.BlockSpec(memory_space=pltpu.VMEM),
    )(c_rows)


def _place():
    x, y, c = lax.axis_index("x"), lax.axis_index("y"), lax.axis_index("c")
    chips = [(1 - x, y), (x, 1 - y), (1 - x, 1 - y)]
    return x, y, c, chips


def _all_gather_small(block, name):
    m_per, n = block.shape

    def body(x_ref, out_ref, send_sems, recv_sems, local_sem):
        x, y, c, chips = _place()
        me, sibling = (x, y, c), (x, y, 1 - c)

        def rows(px, py, pc):
            return out_ref.at[pl.ds((4 * px + 2 * py + pc) * m_per, m_per), :]

        def copy(k, blk, to, src=None):
            return pltpu.make_async_remote_copy(
                src_ref=rows(*blk) if src is None else src, dst_ref=rows(*blk),
                send_sem=send_sems.at[k], recv_sem=recv_sems.at[k], device_id=to, device_id_type=MESH)

        mine = pltpu.make_async_copy(x_ref, rows(*me), local_sem)
        mine.start()
        first = [copy(0, me, sibling, src=x_ref)]
        first += [copy(1 + j, me, (*chip, c), src=x_ref) for j, chip in enumerate(chips)]
        for cp in first:
            cp.start()
        passed = [copy(4 + j, (*chip, c), sibling) for j, chip in enumerate(chips)]
        for j, chip in enumerate(chips):
            copy(1 + j, (*chip, c), me).wait_recv()
            passed[j].start()
        copy(0, sibling, me).wait_recv()
        for j, chip in enumerate(chips):
            copy(4 + j, (*chip, 1 - c), me).wait_recv()
        for cp in first + passed:
            cp.wait_send()
        mine.wait()

    return pl.pallas_call(
        body, name=name, out_shape=jax.ShapeDtypeStruct((N_DEV * m_per, n), block.dtype),
        in_specs=[pl.BlockSpec(memory_space=pltpu.VMEM)], out_specs=pl.BlockSpec(memory_space=pltpu.VMEM),
        scratch_shapes=[pltpu.SemaphoreType.DMA((7,)), pltpu.SemaphoreType.DMA((7,)), pltpu.SemaphoreType.DMA],
        compiler_params=pltpu.CompilerParams(vmem_limit_bytes=V7X_VMEM_LIMIT),
    )(block)


_ANY = pl.BlockSpec(memory_space=pl.ANY)


def _gather_weights(shards):
    n = len(shards)

    def body(*refs):
        ins, outs = refs[:n], refs[n:2 * n]
        send_sems, recv_sems, local_sems = refs[2 * n:]
        x, y, c, chips = _place()
        my_chip = 2 * x + y
        sibling = (x, y, 1 - c)

        def half(t, chip_idx, hc):
            h = ins[t].shape[1] // 2
            return outs[t].at[chip_idx, :, pl.ds(hc * h, h), :]

        def src_half(t):
            h = ins[t].shape[1] // 2
            return ins[t].at[:, pl.ds(c * h, h), :]

        def copy(t, k, src, dst, to):
            return pltpu.make_async_remote_copy(src_ref=src, dst_ref=dst, send_sem=send_sems.at[6 * t + k],
                                                recv_sem=recv_sems.at[6 * t + k], device_id=to, device_id_type=MESH)

        local = [pltpu.make_async_copy(ins[t], outs[t].at[my_chip], local_sems.at[t]) for t in range(n)]
        for cp in local:
            cp.start()
        sends = []
        for t in range(n):
            for k, chip in enumerate(chips):
                sends.append(copy(t, k, src_half(t), half(t, my_chip, c), (*chip, c)))
                sends[-1].start()
        for t in range(n):
            for k, (cx, cy) in enumerate(chips):
                landed = half(t, 2 * cx + cy, c)
                copy(t, k, landed, landed, (cx, cy, c)).wait_recv()
                sends.append(copy(t, 3 + k, landed, landed, sibling))
                sends[-1].start()
        for t in range(n):
            for k, (cx, cy) in enumerate(chips):
                landed = half(t, 2 * cx + cy, 1 - c)
                copy(t, 3 + k, landed, landed, sibling).wait_recv()
        for cp in sends:
            cp.wait_send()
        for cp in local:
            cp.wait()

    return pl.pallas_call(
        body, name="gather_weights",
        out_shape=[jax.ShapeDtypeStruct((N_CHIPS,) + s.shape, s.dtype) for s in shards],
        in_specs=[_ANY] * n, out_specs=[_ANY] * n,
        scratch_shapes=[pltpu.SemaphoreType.DMA((6 * n,)), pltpu.SemaphoreType.DMA((6 * n,)),
                        pltpu.SemaphoreType.DMA((n,))],
    )(*shards)


def _swap_halves(grads):
    n = len(grads)

    def body(*refs):
        ins, outs = refs[:n], refs[n:2 * n]
        send_sems, recv_sems = refs[2 * n:]
        x, y, c, _ = _place()
        copies = []
        for t in range(n):
            h = ins[t].shape[1] // 2
            copies.append(pltpu.make_async_remote_copy(
                src_ref=ins[t].at[:, pl.ds((1 - c) * h, h), :], dst_ref=outs[t],
                send_sem=send_sems.at[t], recv_sem=recv_sems.at[t], device_id=(x, y, 1 - c), device_id_type=MESH))
            copies[-1].start()
        for cp in copies:
            cp.wait()

    return pl.pallas_call(
        body, name="swap_halves",
        out_shape=[jax.ShapeDtypeStruct((g.shape[0], g.shape[1] // 2, g.shape[2]), g.dtype) for g in grads],
        in_specs=[_ANY] * n, out_specs=[_ANY] * n,
        scratch_shapes=[pltpu.SemaphoreType.DMA((n,)), pltpu.SemaphoreType.DMA((n,))],
    )(*grads)


def _send_to_owners(parts):
    n = len(parts)

    def body(*refs):
        ins, outs = refs[:n], refs[n:2 * n]
        send_sems, recv_sems = refs[2 * n:]
        x, y, c, chips = _place()
        copies = []
        for t in range(n):
            for k, (cx, cy) in enumerate(chips):
                copies.append(pltpu.make_async_remote_copy(
                    src_ref=ins[t].at[2 * cx + cy], dst_ref=outs[t].at[k],
                    send_sem=send_sems.at[3 * t + k], recv_sem=recv_sems.at[3 * t + k],
                    device_id=(cx, cy, c), device_id_type=MESH))
                copies[-1].start()
        for cp in copies:
            cp.wait()

    return pl.pallas_call(
        body, name="send_to_owners",
        out_shape=[jax.ShapeDtypeStruct((3,) + p.shape[1:], p.dtype) for p in parts],
        in_specs=[_ANY] * n, out_specs=[_ANY] * n,
        scratch_shapes=[pltpu.SemaphoreType.DMA((3 * n,)), pltpu.SemaphoreType.DMA((3 * n,))],
    )(*parts)


def _share_halves(halves, layers):
    n = len(halves)
    n_w = n // layers

    def body(*refs):
        ins, outs = refs[:n], refs[n:n + n_w]
        send_sems, recv_sems, local_sems = refs[n + n_w:]
        x, y, c, _ = _place()
        copies, local = [], []
        for t in range(n):
            w, l = divmod(t, layers)
            h = ins[t].shape[0]
            mine = outs[w].at[l, pl.ds(c * h, h), :]
            local.append(pltpu.make_async_copy(ins[t], mine, local_sems.at[t]))
            local[-1].start()
            copies.append(pltpu.make_async_remote_copy(
                src_ref=ins[t], dst_ref=mine, send_sem=send_sems.at[t], recv_sem=recv_sems.at[t],
                device_id=(x, y, 1 - c), device_id_type=MESH))
            copies[-1].start()
        for t in range(n):
            w, l = divmod(t, layers)
            h = ins[t].shape[0]
            theirs = outs[w].at[l, pl.ds((1 - c) * h, h), :]
            pltpu.make_async_remote_copy(
                src_ref=ins[t], dst_ref=theirs, send_sem=send_sems.at[t], recv_sem=recv_sems.at[t],
                device_id=(x, y, 1 - c), device_id_type=MESH).wait_recv()
        for cp in copies:
            cp.wait_send()
        for cp in local:
            cp.wait()

    return pl.pallas_call(
        body, name="share_halves",
        out_shape=[jax.ShapeDtypeStruct((layers, 2 * halves[w * layers].shape[0], halves[w * layers].shape[1]), F32)
                   for w in range(n_w)],
        in_specs=[_ANY] * n, out_specs=[_ANY] * n_w,
        scratch_shapes=[pltpu.SemaphoreType.DMA((n,)), pltpu.SemaphoreType.DMA((n,)), pltpu.SemaphoreType.DMA((n,))],
    )(*halves)


def _sum_sibling(grad, recv, c_idx):
    _, r, cols = grad.shape
    h = r // 2
    tr, tc = _tile(h, 512, 16), _tile(cols, 1024)
    per = h // tr

    def body(c_ref, g_ref, r_ref, o_ref):
        o_ref[...] = (g_ref[...] + r_ref[...]).astype(BF16)

    return pl.pallas_call(
        body, name="sum_sibling",
        grid_spec=pltpu.PrefetchScalarGridSpec(
            num_scalar_prefetch=1, grid=(N_CHIPS, per, cols // tc),
            in_specs=[pl.BlockSpec((None, tr, tc), lambda j, i, n, c: (j, c[0] * per + i, n)),
                      pl.BlockSpec((None, tr, tc), lambda j, i, n, c: (j, i, n))],
            out_specs=pl.BlockSpec((None, tr, tc), lambda j, i, n, c: (j, i, n))),
        out_shape=jax.ShapeDtypeStruct((N_CHIPS, h, cols), BF16),
        compiler_params=_params(("parallel", "parallel", "parallel")),
    )(c_idx, grad, recv)


def _sum_owner(parts, recv, chip_idx):
    _, h, cols = parts.shape
    tr, tc = _tile(h, 512, 16), _tile(cols, 1024)

    def body(j_ref, p_ref, r_ref, o_ref):
        o_ref[...] = (p_ref[...].astype(F32) + r_ref[0].astype(F32) + r_ref[1].astype(F32)
                      + r_ref[2].astype(F32))

    return pl.pallas_call(
        body, name="sum_owner",
        grid_spec=pltpu.PrefetchScalarGridSpec(
            num_scalar_prefetch=1, grid=(h // tr, cols // tc),
            in_specs=[pl.BlockSpec((None, tr, tc), lambda i, n, j: (j[0], i, n)),
                      pl.BlockSpec((3, tr, tc), lambda i, n, j: (0, i, n))],
            out_specs=pl.BlockSpec((tr, tc), lambda i, n, j: (i, n))),
        out_shape=jax.ShapeDtypeStruct((h, cols), F32),
        compiler_params=_params(("parallel", "parallel")),
    )(chip_idx, parts, recv)


def _layer_fwd(x, mod, p, cfg):
    d, tr, tm = cfg["d"], cfg["tr"], cfg["tm"]
    shift, scale, rg = mod[:, 0:d], mod[:, d:2 * d], mod[:, 2 * d:3 * d]
    h = _norm_fwd(x, p["norm_g"], scale, shift, tr)
    proj, = _mm_nn("proj", h, p["w_in"], tm=tm, tn=cfg["tn_in"], tk=d)
    gates, = _mm_nn("gates", h, p["w_gate"], tm=tm, tn=cfg["tn_gate"], tk=d,
                    extras=[(p["b_gate"], pl.BlockSpec((1, cfg["tn_gate"]), lambda i, j, k: (0, j)))],
                    epilogue=lambda acc, b: (_sigmoid(acc + b),))
    ya = _pool_fwd(proj, p["pool_w"], p["pool_scale"], d, tr)
    att, yb, rsave = _sb_fwd(proj, d, cfg["blk"])
    yc = _conv_fwd(proj, p["conv_w"], d, tr)
    pa, = _mm_nn("branch_a", ya, p["w_br_a"], tm=tm, tn=cfg["tn_d"], tk=d // 2)
    pb, = _mm_nn("branch_b", yb, p["w_br_b"], tm=tm, tn=cfg["tn_d"], tk=cfg["tk_row"])
    pc, = _mm_nn("branch_c", yc, p["w_br_c"], tm=tm, tn=cfg["tn_d"], tk=d // 2)
    merged = _merge_fwd(gates, pa, pb, pc, cfg["tr_small"])
    tn = cfg["tn_d"]
    s = x.shape[0]
    blk = pl.BlockSpec((tm, tn), lambda i, j, k: (i, j))
    sd = jax.ShapeDtypeStruct((s, d), F32)
    out, x_next = _mm_nn("out_proj", merged, p["w_out"], tm=tm, tn=tn, tk=cfg["tk_row"],
                         outs=[(sd, blk), (sd, blk)],
                         extras=[(x, blk), (rg, pl.BlockSpec((1, tn), lambda i, j, k: (0, j)))],
                         epilogue=lambda acc, xv, g: (acc, xv + g * acc))
    saved = dict(x=x, h=h, proj=proj, gates=gates, ya=ya, yb=yb, yc=yc, att=att, rsave=rsave,
                 pa=pa, pb=pb, pc=pc, merged=merged, out=out, scale=scale, rg=rg)
    return x_next, saved


def _layer_bwd(dxo, sv, p, cfg):
    d, tr, tm, tk_s = cfg["d"], cfg["tr"], cfg["tm"], cfg["tk_s"]
    dout, drg = _resid_bwd(dxo, sv["out"], sv["rg"], tr)
    dmerged = _mm_nt("d_merged", dout, p["w_out"], tm=tm, tn=cfg["tn_row"], tk=d)
    g_out = _mm_tn("g_w_out", sv["merged"], dout, "row", tm=cfg["tn_row"], tn=cfg["tn_d"], tk=tk_s)
    dpa, dpb, dpc, dlogit, g_bgate = _merge_bwd(dmerged, sv["gates"], sv["pa"], sv["pb"], sv["pc"], cfg["tr_small"])
    dya = _mm_nt("d_ya", dpa, p["w_br_a"], tm=tm, tn=cfg["tn_half"], tk=cfg["tn_d"])
    dyb = _mm_nt("d_yb", dpb, p["w_br_b"], tm=tm, tn=cfg["tn_row"], tk=d)
    dyc = _mm_nt("d_yc", dpc, p["w_br_c"], tm=tm, tn=cfg["tn_half"], tk=cfg["tn_d"])
    g_a = _mm_tn("g_w_br_a", sv["ya"], dpa, "col", tm=cfg["tn_half"], tn=cfg["tn_d"], tk=tk_s)
    g_b = _mm_tn("g_w_br_b", sv["yb"], dpb, "row", tm=cfg["tn_row"], tn=cfg["tn_d"], tk=tk_s)
    g_c = _mm_tn("g_w_br_c", sv["yc"], dpc, "col", tm=cfg["tn_half"], tn=cfg["tn_d"], tk=tk_s)
    d_a, g_pool_w, g_pool_scale = _pool_bwd(sv["proj"], dya, p["pool_w"], p["pool_scale"], d, tr)
    d_c, g_conv_w = _conv_bwd(sv["proj"], dyc, p["conv_w"], d, tr)
    dq, dk, dv, dzb = _sb_bwd(sv["proj"], sv["att"], dyb, sv["rsave"], d, cfg["blk"])
    dproj = jnp.concatenate([d_a, dq.astype(BF16), dk.astype(BF16), dv.astype(BF16), dzb.astype(BF16), d_c], axis=1)
    dh_in = _mm_nt("d_h_in", dproj, p["w_in"], tm=tm, tn=cfg["tn_d"], tk=cfg["tk_in"])
    tn = cfg["tn_d"]
    dh = _mm_nt("d_h_gate", dlogit, p["w_gate"], tm=tm, tn=tn, tk=cfg["tn_gate"],
                extras=[(dh_in, pl.BlockSpec((tm, tn), lambda i, j, k: (i, j)))],
                epilogue=lambda acc, prev: (prev + acc,))
    g_in = _mm_tn("g_w_in", sv["h"], dproj, "col", tm=cfg["tm_g"], tn=cfg["tn_in"], tk=tk_s)
    g_gate = _mm_tn("g_w_gate", sv["h"], dlogit, "col", tm=cfg["tm_g"], tn=cfg["tn_gate"], tk=tk_s)
    dx, dshift, dscale, g_norm = _norm_bwd(dh, sv["x"], p["norm_g"], sv["scale"], dxo, tr)
    big = dict(w_in=g_in, w_gate=g_gate, w_br_a=g_a, w_br_b=g_b, w_br_c=g_c, w_out=g_out)
    small = dict(dmod=jnp.concatenate([dshift, dscale, drg], axis=1), norm_g=g_norm, pool_scale=g_pool_scale,
                 b_gate=g_bgate, conv_w=g_conv_w, pool_w=g_pool_w)
    return dx, big, small


BIG = ("w_in", "w_gate", "w_br_a", "w_br_b", "w_br_c", "w_out")
BIG_KIND = dict(w_in="col", w_gate="col", w_br_a="col", w_br_b="row", w_br_c="col", w_out="row")


def _pad_to(v, n):
    return jnp.pad(v, (0, n - v.shape[0]))


def kernel(x, c, norm_g, w_ada, b_ada, w_in, pool_w, pool_scale, conv_w, w_br_a, w_br_b, w_br_c, w_gate, b_gate, w_out, final_g, loss_target, m_norm_g, m_w_ada, m_b_ada, m_w_in, m_pool_w, m_pool_scale, m_conv_w, m_w_br_a, m_w_br_b, m_w_br_c, m_w_gate, m_b_gate, m_w_out, m_final_g, v_norm_g, v_w_ada, v_b_ada, v_w_in, v_pool_w, v_pool_scale, v_conv_w, v_w_br_a, v_w_br_b, v_w_br_c, v_w_gate, v_b_gate, v_w_out, v_final_g):
    weights = dict(norm_g=norm_g, w_ada=w_ada, b_ada=b_ada, w_in=w_in, pool_w=pool_w, pool_scale=pool_scale,
                   conv_w=conv_w, w_br_a=w_br_a, w_br_b=w_br_b, w_br_c=w_br_c, w_gate=w_gate, b_gate=b_gate,
                   w_out=w_out, final_g=final_g)
    mom_m = dict(norm_g=m_norm_g, w_ada=m_w_ada, b_ada=m_b_ada, w_in=m_w_in, pool_w=m_pool_w,
                 pool_scale=m_pool_scale, conv_w=m_conv_w, w_br_a=m_w_br_a, w_br_b=m_w_br_b, w_br_c=m_w_br_c,
                 w_gate=m_w_gate, b_gate=m_b_gate, w_out=m_w_out, final_g=m_final_g)
    mom_v = dict(norm_g=v_norm_g, w_ada=v_w_ada, b_ada=v_b_ada, w_in=v_w_in, pool_w=v_pool_w,
                 pool_scale=v_pool_scale, conv_w=v_conv_w, w_br_a=v_w_br_a, w_br_b=v_w_br_b, w_br_c=v_w_br_c,
                 w_gate=v_w_gate, b_gate=v_b_gate, w_out=v_w_out, final_g=v_final_g)
    names = list(weights)

    _, s, d = x.shape
    layers = norm_g.shape[0]
    pw, gd, cc = d // 2, d // 8, d // 2
    ada_cols = 3 * d // N_CHIPS
    xi, yi, ci = lax.axis_index("x"), lax.axis_index("y"), lax.axis_index("c")
    me = 4 * xi + 2 * yi + ci
    my_chip = 2 * xi + yi
    c_idx = jnp.reshape(ci, (1,)).astype(jnp.int32)
    chip_idx = jnp.reshape(my_chip, (1,)).astype(jnp.int32)

    cfg = dict(
        d=d, tr=_tile(s, 256, 16), tr_small=_tile(s, 128, 16), tm=_tile(s, 1024, 16), tk_s=_tile(s, 1024, 16),
        blk=_tile(s, min(256, max(s // 4, 16)), 16),
        tn_in=_tile(7 * d // N_CHIPS, 512), tk_in=_tile(7 * d // N_CHIPS, 1792),
        tn_gate=_tile(3 * d // N_CHIPS, 512), tn_d=_tile(d // N_CHIPS, 512),
        tn_row=_tile(d // N_CHIPS, 512, 16), tk_row=_tile(d // N_CHIPS, 512, 16),
        tn_half=_tile(d // 2, 512), tm_g=_tile(d, 1024),
    )

    conv_flat = conv_w.reshape(-1)
    conv_len = -(-conv_flat.shape[0] // 1024) * 1024
    pack0 = jnp.concatenate([c.reshape(-1), _pad_to(conv_flat, conv_len), pool_w.reshape(-1)])
    w0 = -(-pack0.shape[0] // 1024) * 1024
    g0 = _all_gather_small(_pad_to(pack0, w0).reshape(8, w0 // 8), "gather_small").reshape(N_DEV, w0)
    c_all = g0[:, 0:d]
    chip_rows = g0[0::2]
    conv_full = jnp.concatenate(
        [chip_rows[j, d:d + conv_flat.shape[0]].reshape(conv_w.shape) for j in range(N_CHIPS)], axis=2)
    pool_full = jnp.concatenate(
        [chip_rows[j, d + conv_len:d + conv_len + pool_w.size].reshape(pool_w.shape) for j in range(N_CHIPS)],
        axis=2)
    pool_bf = pool_full.astype(BF16)

    sc16 = _silu_bf16(jnp.pad(c_all, ((0, 16 - N_DEV), (0, 0))))
    mods = []
    for l in range(layers):
        bias = lax.dynamic_slice(b_ada[l], (my_chip * ada_cols,), (ada_cols,)).reshape(1, ada_cols)
        tn = _tile(ada_cols, 512)
        mod_l, = _mm("mod", sc16, w_ada[l], dims=NN, grid=(1, ada_cols // tn, 1),
                     a_spec=pl.BlockSpec((16, d), lambda i, j, k: (0, 0)),
                     b_spec=pl.BlockSpec((d, tn), lambda i, j, k: (0, j)),
                     acc_shape=(16, tn),
                     outs=[(jax.ShapeDtypeStruct((16, ada_cols), F32), pl.BlockSpec((16, tn), lambda i, j, k: (0, j)))],
                     extras=[(bias, pl.BlockSpec((1, tn), lambda i, j, k: (0, j)))],
                     epilogue=lambda acc, b: (acc + b,))
        mods.append(mod_l[0:N_DEV])
    g1 = _all_gather_small(jnp.concatenate(mods, axis=1), "gather_mod")
    g1 = g1.reshape(N_CHIPS, 2, N_DEV, layers, ada_cols)[:, 0]
    mod_all = jnp.transpose(g1, (1, 2, 0, 3)).reshape(N_DEV, layers, 3 * d)
    mod_me = lax.dynamic_slice(mod_all, (me, 0, 0), (1, layers, 3 * d))[0]

    gathered = _gather_weights([weights[n].astype(BF16) for n in BIG])
    params = []
    for l in range(layers):
        p = {n: _Sharded(g, BIG_KIND[n], layer=l) for n, g in zip(BIG, gathered)}
        p.update(norm_g=norm_g[l:l + 1], pool_scale=pool_scale[l:l + 1], b_gate=b_gate[l:l + 1],
                 conv_w=conv_full[l], pool_w=pool_bf[l])
        params.append(p)

    act = x[0]
    saved = []
    for l in range(layers):
        act, sv = _layer_fwd(act, mod_me[l:l + 1], params[l], cfg)
        saved.append(sv)
    loss_part, dact, g_final = _final_loss(act, final_g.reshape(1, d), loss_target[0], cfg["tr"])
    loss = lax.psum(loss_part[0, 0], ("x", "y", "c"))
    big_grads, small_grads = [None] * layers, [None] * layers
    for l in reversed(range(layers)):
        dact, big_grads[l], small_grads[l] = _layer_bwd(dact, saved[l], params[l], cfg)
    grad_x = dact.reshape(x.shape)

    flat = [big_grads[l][n] for n in BIG for l in range(layers)]
    recv = _swap_halves(flat)
    parts = [_sum_sibling(g, r, c_idx) for g, r in zip(flat, recv)]
    theirs = _send_to_owners(parts)
    halves = [_sum_owner(p, t, chip_idx) for p, t in zip(parts, theirs)]
    full = _share_halves(halves, layers)
    grads = {n: f.reshape(weights[n].shape) for n, f in zip(BIG, full)}

    small_names = ("dmod", "norm_g", "pool_scale", "b_gate", "conv_w", "pool_w")
    pieces = [small_grads[l][n].reshape(-1) for n in small_names for l in range(layers)] + [g_final.reshape(-1)]
    pack1 = jnp.concatenate(pieces)
    w1 = -(-pack1.shape[0] // 1024) * 1024
    g2 = _all_gather_small(_pad_to(pack1, w1).reshape(8, w1 // 8), "gather_grads")
    total = _sum_blocks(g2, N_DEV, 8).reshape(-1)
    off = 0
    summed = {}
    for n in small_names:
        per = small_grads[0][n].size
        summed[n] = jnp.stack([total[off + l * per:off + (l + 1) * per].reshape(small_grads[0][n].shape)
                               for l in range(layers)])
        off += layers * per
    grads["final_g"] = total[off:off + d]
    grads["norm_g"] = summed["norm_g"].reshape(layers, d)
    grads["pool_scale"] = summed["pool_scale"].reshape(layers, pw)
    grads["b_gate"] = summed["b_gate"].reshape(layers, 3 * d)
    grads["b_ada"] = summed["dmod"].reshape(layers, 3 * d)
    cs = cc // N_CHIPS
    grads["conv_w"] = lax.dynamic_slice(summed["conv_w"], (0, 0, my_chip * cs), (layers, 3, cs))
    rs_ = gd // N_CHIPS
    grads["pool_w"] = lax.dynamic_slice(summed["pool_w"], (0, 0, my_chip * rs_, 0), (layers, N_GROUPS, rs_, gd))
    dmod_all = g2.reshape(N_DEV, w1)[:, 0:layers * 3 * d].reshape(N_DEV, layers, 3 * d)
    g_ada = []
    for l in range(layers):
        cols = lax.dynamic_slice(dmod_all[:, l], (0, my_chip * ada_cols), (N_DEV, ada_cols))
        cols16 = jnp.pad(cols, ((0, 16 - N_DEV), (0, 0)))
        tn = _tile(ada_cols, 512)
        tm = _tile(d, 1024)
        ga, = _mm("g_w_ada", sc16, cols16, dims=TN, grid=(d // tm, ada_cols // tn, 1),
                  a_spec=pl.BlockSpec((16, tm), lambda i, j, k: (0, i)),
                  b_spec=pl.BlockSpec((16, tn), lambda i, j, k: (0, j)),
                  acc_shape=(tm, tn),
                  outs=[(jax.ShapeDtypeStruct((d, ada_cols), F32), pl.BlockSpec((tm, tn), lambda i, j, k: (i, j)))])
        g_ada.append(ga)
    grads["w_ada"] = jnp.stack(g_ada)

    deltas, new_m, new_v = {}, {}, {}
    for n in names:
        deltas[n], new_m[n], new_v[n] = _adamw(weights[n], grads[n], mom_m[n], mom_v[n])
    return (loss, grad_x, *[grads[n] for n in names], *[deltas[n] for n in names],
            *[new_m[n] for n in names], *[new_v[n] for n in names])
```

```python
import functools

import jax
import jax.numpy as jnp
from jax import lax
from jax.experimental import pallas as pl
from jax.experimental.pallas import tpu as pltpu

F32 = jnp.float32
BF16 = jnp.bfloat16
MESH = pl.DeviceIdType.MESH

N_CHIPS = 4
N_DEV = 8
N_GROUPS = 4
POOL_WINDOWS = (2, 4, 8, 16)
POOL_HALO = 16
CONV_HALO = 8
HEAD_DIM = 128
HEADS_PER_STEP = 2
RMS_EPS = 1e-6
ADAM_LR = 0.001
ADAM_B1 = 0.9
ADAM_B2 = 0.999
ADAM_EPS = 1e-08
ADAM_WD = 0.01
ADAM_STEP = 10
V7X_VMEM_LIMIT = 56 * 1024 * 1024


def _tile(n, pref, mult=128):
    best = None
    t = mult
    while t <= min(n, pref):
        if n % t == 0:
            best = t
        t += mult
    return n if best is None else best


def _params(sem=None):
    return pltpu.CompilerParams(dimension_semantics=sem, vmem_limit_bytes=V7X_VMEM_LIMIT)


def _sigmoid(z):
    return jax.nn.sigmoid(z)


def _dsilu(z, sg):
    return sg * (1.0 + z * (1.0 - sg))


NN = ((1,), (0,))
NT = ((1,), (1,))
TN = ((0,), (0,))


def _mm(name, a, b, *, dims, grid, a_spec, b_spec, acc_shape, outs, extras=(), epilogue=None):
    nk = grid[2]
    ne, no = len(extras), len(outs)
    if epilogue is None:
        epilogue = lambda acc: (acc,)

    def body(a_ref, b_ref, *rest):
        ex, orefs, acc = rest[:ne], rest[ne:ne + no], rest[ne + no]
        k = pl.program_id(2)

        @pl.when(k == 0)
        def _():
            acc[...] = jnp.zeros_like(acc)

        acc[...] += lax.dot_general(a_ref[...].astype(BF16), b_ref[...].astype(BF16), (dims, ((), ())),
                                    preferred_element_type=F32)

        @pl.when(k == nk - 1)
        def _():
            vals = epilogue(acc[...], *[e[...] for e in ex])
            for o, v in zip(orefs, vals):
                o[...] = v.astype(o.dtype)

    res = pl.pallas_call(
        body, name=name, grid=grid,
        in_specs=[a_spec, b_spec] + [s for _, s in extras],
        out_specs=[s for _, s in outs], out_shape=[sh for sh, _ in outs],
        scratch_shapes=[pltpu.VMEM(acc_shape, F32)],
        compiler_params=_params(("parallel", "parallel", "arbitrary")),
    )(a, b, *[e for e, _ in extras])
    return res


class _Sharded:
    def __init__(self, arr, kind, layer=None):
        self.arr, self.kind, self.layer = arr, kind, layer
        r, c = arr.shape[-2:]
        self.rows = r * (N_CHIPS if kind == "row" else 1)
        self.cols = c * (N_CHIPS if kind == "col" else 1)
        self.sr, self.sc = r, c

    def spec(self, br, bc, f):
        lead = (None,) if self.layer is None else (None, None)
        layer = self.layer
        if self.kind == "col":
            per = self.sc // bc
            assert per * bc == self.sc and self.sr % br == 0, (self.arr.shape, br, bc)

            def idx(*g):
                rb, cb = f(*g)
                return ((cb // per,) + (() if layer is None else (layer,)) + (rb, cb % per))
        else:
            per = self.sr // br
            assert per * br == self.sr and self.sc % bc == 0, (self.arr.shape, br, bc)

            def idx(*g):
                rb, cb = f(*g)
                return ((rb // per,) + (() if layer is None else (layer,)) + (rb % per, cb))
        return pl.BlockSpec(lead + (br, bc), idx)


def _grad_buffer(rows, cols, kind):
    if kind == "col":
        return jax.ShapeDtypeStruct((N_CHIPS, rows, cols // N_CHIPS), F32)
    return jax.ShapeDtypeStruct((N_CHIPS, rows // N_CHIPS, cols), F32)


def _mm_nn(name, a, w, *, tm, tn, tk, outs=None, extras=(), epilogue=None, out_dtype=F32):
    m, kdim = a.shape
    n = w.cols
    grid = (m // tm, n // tn, kdim // tk)
    if outs is None:
        outs = [(jax.ShapeDtypeStruct((m, n), out_dtype), pl.BlockSpec((tm, tn), lambda i, j, k: (i, j)))]
    return _mm(name, a, w.arr, dims=NN, grid=grid,
               a_spec=pl.BlockSpec((tm, tk), lambda i, j, k: (i, k)),
               b_spec=w.spec(tk, tn, lambda i, j, k: (k, j)),
               acc_shape=(tm, tn), outs=outs, extras=extras, epilogue=epilogue)


def _mm_nt(name, a, w, *, tm, tn, tk, extras=(), epilogue=None, out_dtype=F32):
    m, kdim = a.shape
    n = w.rows
    grid = (m // tm, n // tn, kdim // tk)
    outs = [(jax.ShapeDtypeStruct((m, n), out_dtype), pl.BlockSpec((tm, tn), lambda i, j, k: (i, j)))]
    return _mm(name, a, w.arr, dims=NT, grid=grid,
               a_spec=pl.BlockSpec((tm, tk), lambda i, j, k: (i, k)),
               b_spec=w.spec(tn, tk, lambda i, j, k: (j, k)),
               acc_shape=(tm, tn), outs=outs, extras=extras, epilogue=epilogue)[0]


def _mm_tn(name, a, b, kind, *, tm, tn, tk):
    kdim, m = a.shape
    n = b.shape[1]
    grid = (m // tm, n // tn, kdim // tk)
    out = _Sharded(_grad_buffer(m, n, kind), kind)
    outs = [(out.arr, out.spec(tm, tn, lambda i, j, k: (i, j)))]
    return _mm(name, a, b, dims=TN, grid=grid,
               a_spec=pl.BlockSpec((tk, tm), lambda i, j, k: (k, i)),
               b_spec=pl.BlockSpec((tk, tn), lambda i, j, k: (k, j)),
               acc_shape=(tm, tn), outs=outs)[0]


def _row_spec(tr, w, col=0):
    return pl.BlockSpec((tr, w), lambda i: (i, col))


def _vec_spec(w, col=0, rows=1):
    return pl.BlockSpec((rows, w), lambda i: (0, col))


def _norm_fwd(x, g, scale, shift, tr):
    s, d = x.shape

    def body(x_ref, g_ref, sc_ref, sh_ref, h_ref):
        xv = x_ref[...]
        r = lax.rsqrt(jnp.mean(xv * xv, axis=-1, keepdims=True) + RMS_EPS)
        y = xv * r * g_ref[...]
        h_ref[...] = (y * (1.0 + sc_ref[...]) + sh_ref[...]).astype(BF16)

    return pl.pallas_call(
        body, name="norm_fwd", grid=(s // tr,),
        in_specs=[_row_spec(tr, d), _vec_spec(d), _vec_spec(d), _vec_spec(d)],
        out_specs=_row_spec(tr, d), out_shape=jax.ShapeDtypeStruct((s, d), BF16),
        compiler_params=_params(("parallel",)),
    )(x, g, scale, shift)


def _norm_bwd(dh, x, g, scale, dxo, tr):
    s, d = x.shape

    def body(dh_ref, x_ref, g_ref, sc_ref, dxo_ref, dx_ref, dsh_ref, dsc_ref, dg_ref):
        @pl.when(pl.program_id(0) == 0)
        def _():
            dsh_ref[...] = jnp.zeros_like(dsh_ref)
            dsc_ref[...] = jnp.zeros_like(dsc_ref)
            dg_ref[...] = jnp.zeros_like(dg_ref)

        xv, dhv, gv = x_ref[...], dh_ref[...], g_ref[...]
        r = lax.rsqrt(jnp.mean(xv * xv, axis=-1, keepdims=True) + RMS_EPS)
        xn = xv * r
        dsh_ref[...] += jnp.sum(dhv, axis=0, keepdims=True)
        dsc_ref[...] += jnp.sum(dhv * (xn * gv), axis=0, keepdims=True)
        dyg = dhv * (1.0 + sc_ref[...])
        dg_ref[...] += jnp.sum(dyg * xn, axis=0, keepdims=True)
        dxn = dyg * gv
        dx_ref[...] = dxo_ref[...] + r * (dxn - xn * jnp.mean(dxn * xn, axis=-1, keepdims=True))

    vec = jax.ShapeDtypeStruct((1, d), F32)
    return pl.pallas_call(
        body, name="norm_bwd", grid=(s // tr,),
        in_specs=[_row_spec(tr, d), _row_spec(tr, d), _vec_spec(d), _vec_spec(d), _row_spec(tr, d)],
        out_specs=[_row_spec(tr, d), _vec_spec(d), _vec_spec(d), _vec_spec(d)],
        out_shape=[jax.ShapeDtypeStruct((s, d), F32), vec, vec, vec],
        compiler_params=_params(("arbitrary",)),
    )(dh, x, g, scale, dxo)


def _final_loss(x, g, target, tr):
    s, d = x.shape

    def body(x_ref, g_ref, t_ref, loss_ref, dx_ref, dg_ref):
        @pl.when(pl.program_id(0) == 0)
        def _():
            loss_ref[...] = jnp.zeros_like(loss_ref)
            dg_ref[...] = jnp.zeros_like(dg_ref)

        xv, gv = x_ref[...], g_ref[...]
        r = lax.rsqrt(jnp.mean(xv * xv, axis=-1, keepdims=True) + RMS_EPS)
        xn = xv * r
        err = xn * gv - t_ref[...]
        per_row = jnp.mean(err * err, axis=-1, keepdims=True)
        loss_ref[...] += 0.5 * jnp.sum(per_row, axis=0, keepdims=True)
        dy = err * (1.0 / d)
        dg_ref[...] += jnp.sum(dy * xn, axis=0, keepdims=True)
        dxn = dy * gv
        dx_ref[...] = r * (dxn - xn * jnp.mean(dxn * xn, axis=-1, keepdims=True))

    return pl.pallas_call(
        body, name="final_loss", grid=(s // tr,),
        in_specs=[_row_spec(tr, d), _vec_spec(d), _row_spec(tr, d)],
        out_specs=[pl.BlockSpec((8, 128), lambda i: (0, 0)), _row_spec(tr, d), _vec_spec(d)],
        out_shape=[jax.ShapeDtypeStruct((8, 128), F32), jax.ShapeDtypeStruct((s, d), F32),
                   jax.ShapeDtypeStruct((1, d), F32)],
        compiler_params=_params(("arbitrary",)),
    )(x, g, target)


def _halo_before(tr, halo, w, col):
    per = tr // halo
    return pl.BlockSpec((halo, w), lambda i: (jnp.maximum(i * per - 1, 0), col))


def _halo_after(tr, halo, w, col, n_tiles):
    per = tr // halo
    return pl.BlockSpec((halo, w), lambda i: (jnp.minimum((i + 1) * per, n_tiles * per - 1), col))


def _pool_fwd(proj, pool_w, pool_scale, d, tr):
    s = proj.shape[0]
    pw, gd, hl = d // 2, d // 8, POOL_HALO

    def body(xa_ref, xh_ref, za_ref, w_ref, ps_ref, ya_ref, buf):
        i = pl.program_id(0)
        buf[0:hl, :] = jnp.where(i > 0, xh_ref[...], 0.0)
        buf[hl:hl + tr, :] = xa_ref[...]
        row = i * tr + lax.broadcasted_iota(jnp.int32, (tr, 1), 0)
        za = za_ref[...]
        gate = za * _sigmoid(za)
        for g, win in enumerate(POOL_WINDOWS):
            cs = slice(g * gd, (g + 1) * gd)
            xg = buf[hl:hl + tr, cs]
            acc = xg
            for j in range(1, win):
                acc = acc + buf[hl - j:hl - j + tr, cs]
            cnt = jnp.minimum(row + 1, win).astype(F32)
            mixed = acc / cnt - xg
            y = jnp.dot(mixed.astype(BF16), w_ref[g], preferred_element_type=F32)
            ya_ref[:, cs] = ((y * ps_ref[:, cs]) * gate[:, cs]).astype(BF16)

    return pl.pallas_call(
        body, name="pool_fwd", grid=(s // tr,),
        in_specs=[_row_spec(tr, pw, 0), _halo_before(tr, hl, pw, 0), _row_spec(tr, pw, 1),
                  pl.BlockSpec((N_GROUPS, gd, gd), lambda i: (0, 0, 0)), _vec_spec(pw)],
        out_specs=_row_spec(tr, pw), out_shape=jax.ShapeDtypeStruct((s, pw), BF16),
        scratch_shapes=[pltpu.VMEM((tr + hl, pw), F32)],
        compiler_params=_params(("parallel",)),
    )(proj, proj, proj, pool_w, pool_scale)


def _pool_bwd(proj, dya, pool_w, pool_scale, d, tr):
    s = proj.shape[0]
    pw, gd, hl = d // 2, d // 8, POOL_HALO
    n_tiles = s // tr

    def body(xa_ref, xh_ref, za_ref, zh_ref, dya_ref, dyh_ref, w_ref, ps_ref, da_ref, gw_ref, gs_ref, buf, dbuf):
        i = pl.program_id(0)

        @pl.when(i == 0)
        def _():
            gw_ref[...] = jnp.zeros_like(gw_ref)
            gs_ref[...] = jnp.zeros_like(gs_ref)

        buf[0:hl, :] = jnp.where(i > 0, xh_ref[...], 0.0)
        buf[hl:hl + tr, :] = xa_ref[...]
        row = i * tr + lax.broadcasted_iota(jnp.int32, (tr, 1), 0)
        row_h = (i + 1) * tr + lax.broadcasted_iota(jnp.int32, (hl, 1), 0)
        za, dya_t = za_ref[...], dya_ref[...]
        sg = _sigmoid(za)
        gate = za * sg
        dpre = dya_t * gate
        zh = zh_ref[...]
        dpre_h = jnp.where(i < n_tiles - 1, dyh_ref[...], 0.0) * (zh * _sigmoid(zh))
        ps = ps_ref[...]
        for g, win in enumerate(POOL_WINDOWS):
            cs = slice(g * gd, (g + 1) * gd)
            wg = w_ref[g]
            xg = buf[hl:hl + tr, cs]
            acc = xg
            for j in range(1, win):
                acc = acc + buf[hl - j:hl - j + tr, cs]
            cnt = jnp.minimum(row + 1, win).astype(F32)
            mixed = (acc / cnt - xg).astype(BF16)
            ylin = jnp.dot(mixed, wg, preferred_element_type=F32)
            gs_ref[:, cs] += jnp.sum(dpre[:, cs] * ylin, axis=0, keepdims=True)
            da_ref[:, pw + g * gd:pw + (g + 1) * gd] = (
                dya_t[:, cs] * (ylin * ps[:, cs]) * _dsilu(za[:, cs], sg[:, cs])).astype(BF16)
            dyl = (dpre[:, cs] * ps[:, cs]).astype(BF16)
            gw_ref[g] += lax.dot_general(mixed, dyl, (TN, ((), ())), preferred_element_type=F32)
            dmix = lax.dot_general(dyl, wg, (NT, ((), ())), preferred_element_type=F32)
            dyl_h = (dpre_h[:, cs] * ps[:, cs]).astype(BF16)
            dmix_h = lax.dot_general(dyl_h, wg, (NT, ((), ())), preferred_element_type=F32)
            cnt_h = jnp.minimum(row_h + 1, win).astype(F32)
            dbuf[0:tr, cs] = dmix / cnt
            dbuf[tr:tr + hl, cs] = dmix_h / cnt_h
            dx = dbuf[0:tr, cs] - dmix
            for j in range(1, win):
                dx = dx + dbuf[j:j + tr, cs]
            da_ref[:, cs] = dx.astype(BF16)

    return pl.pallas_call(
        body, name="pool_bwd", grid=(n_tiles,),
        in_specs=[_row_spec(tr, pw, 0), _halo_before(tr, hl, pw, 0),
                  _row_spec(tr, pw, 1), _halo_after(tr, hl, pw, 1, n_tiles),
                  _row_spec(tr, pw, 0), _halo_after(tr, hl, pw, 0, n_tiles),
                  pl.BlockSpec((N_GROUPS, gd, gd), lambda i: (0, 0, 0)), _vec_spec(pw)],
        out_specs=[_row_spec(tr, 2 * pw), pl.BlockSpec((N_GROUPS, gd, gd), lambda i: (0, 0, 0)), _vec_spec(pw)],
        out_shape=[jax.ShapeDtypeStruct((s, 2 * pw), BF16), jax.ShapeDtypeStruct((N_GROUPS, gd, gd), F32),
                   jax.ShapeDtypeStruct((1, pw), F32)],
        scratch_shapes=[pltpu.VMEM((tr + hl, pw), F32), pltpu.VMEM((tr + hl, pw), F32)],
        compiler_params=_params(("arbitrary",)),
    )(proj, proj, proj, proj, dya, dya, pool_w, pool_scale)


def _conv_fwd(proj, conv_w, d, tr):
    s = proj.shape[0]
    cc, hl = d // 2, CONV_HALO
    cu, cb, cg_, cz = 10, 11, 12, 13

    def body(u_ref, uh_ref, bg_ref, cg_ref, ch_ref, zc_ref, w_ref, yc_ref, buf):
        i = pl.program_id(0)
        buf[0:hl, :] = jnp.where(i > 0, ch_ref[...] * uh_ref[...], 0.0)
        buf[hl:hl + tr, :] = cg_ref[...] * u_ref[...]
        y = w_ref[0:1, :] * buf[hl - 2:hl - 2 + tr, :]
        y = y + w_ref[1:2, :] * buf[hl - 1:hl - 1 + tr, :]
        y = y + w_ref[2:3, :] * buf[hl:hl + tr, :]
        zc = zc_ref[...]
        yc_ref[...] = ((bg_ref[...] * y) * (zc * _sigmoid(zc))).astype(BF16)

    return pl.pallas_call(
        body, name="conv_fwd", grid=(s // tr,),
        in_specs=[_row_spec(tr, cc, cu), _halo_before(tr, hl, cc, cu), _row_spec(tr, cc, cb),
                  _row_spec(tr, cc, cg_), _halo_before(tr, hl, cc, cg_), _row_spec(tr, cc, cz),
                  _vec_spec(cc, rows=3)],
        out_specs=_row_spec(tr, cc), out_shape=jax.ShapeDtypeStruct((s, cc), BF16),
        scratch_shapes=[pltpu.VMEM((tr + hl, cc), F32)],
        compiler_params=_params(("parallel",)),
    )(proj, proj, proj, proj, proj, proj, conv_w)


def _conv_bwd(proj, dyc, conv_w, d, tr):
    s = proj.shape[0]
    cc, hl = d // 2, CONV_HALO
    cu, cb, cg_, cz = 10, 11, 12, 13
    n_tiles = s // tr

    def body(u_ref, uh_ref, bg_ref, bh_ref, cg_ref, ch_ref, zc_ref, zh_ref, dy_ref, dyh_ref, w_ref,
             dc_ref, gw_ref, buf, dbuf):
        i = pl.program_id(0)

        @pl.when(i == 0)
        def _():
            gw_ref[...] = jnp.zeros_like(gw_ref)

        u, bg, cg, zc, dyc_t = u_ref[...], bg_ref[...], cg_ref[...], zc_ref[...], dy_ref[...]
        buf[0:hl, :] = jnp.where(i > 0, ch_ref[...] * uh_ref[...], 0.0)
        buf[hl:hl + tr, :] = cg * u
        v2, v1, v0 = buf[hl - 2:hl - 2 + tr, :], buf[hl - 1:hl - 1 + tr, :], buf[hl:hl + tr, :]
        w0, w1, w2 = w_ref[0:1, :], w_ref[1:2, :], w_ref[2:3, :]
        y = w0 * v2 + w1 * v1 + w2 * v0
        sg = _sigmoid(zc)
        gate = zc * sg
        dc_ref[:, cc:2 * cc] = (dyc_t * y * gate).astype(BF16)
        dc_ref[:, 3 * cc:4 * cc] = (dyc_t * bg * y * _dsilu(zc, sg)).astype(BF16)
        dy = dyc_t * bg * gate
        zh = zh_ref[...]
        dy_h = jnp.where(i < n_tiles - 1, dyh_ref[...], 0.0) * bh_ref[...] * (zh * _sigmoid(zh))
        gw_ref[0:1, :] += jnp.sum(dy * v2, axis=0, keepdims=True)
        gw_ref[1:2, :] += jnp.sum(dy * v1, axis=0, keepdims=True)
        gw_ref[2:3, :] += jnp.sum(dy * v0, axis=0, keepdims=True)
        dbuf[0:tr, :] = dy
        dbuf[tr:tr + hl, :] = dy_h
        dv = w2 * dy + w1 * dbuf[1:1 + tr, :] + w0 * dbuf[2:2 + tr, :]
        dc_ref[:, 0:cc] = (dv * cg).astype(BF16)
        dc_ref[:, 2 * cc:3 * cc] = (dv * u).astype(BF16)

    return pl.pallas_call(
        body, name="conv_bwd", grid=(n_tiles,),
        in_specs=[_row_spec(tr, cc, cu), _halo_before(tr, hl, cc, cu),
                  _row_spec(tr, cc, cb), _halo_after(tr, hl, cc, cb, n_tiles),
                  _row_spec(tr, cc, cg_), _halo_before(tr, hl, cc, cg_),
                  _row_spec(tr, cc, cz), _halo_after(tr, hl, cc, cz, n_tiles),
                  _row_spec(tr, cc, 0), _halo_after(tr, hl, cc, 0, n_tiles),
                  _vec_spec(cc, rows=3)],
        out_specs=[_row_spec(tr, 4 * cc), _vec_spec(cc, rows=3)],
        out_shape=[jax.ShapeDtypeStruct((s, 4 * cc), BF16), jax.ShapeDtypeStruct((3, cc), F32)],
        scratch_shapes=[pltpu.VMEM((tr + hl, cc), F32), pltpu.VMEM((tr + hl, cc), F32)],
        compiler_params=_params(("arbitrary",)),
    )(proj, proj, proj, proj, proj, proj, proj, proj, dyc, dyc, conv_w)


def _merge_fwd(gates, pa, pb, pc, tr):
    s, d = pa.shape

    def body(g_ref, a_ref, b_ref, c_ref, m_ref):
        m = g_ref[:, 0:d] * a_ref[...] + g_ref[:, d:2 * d] * b_ref[...] + g_ref[:, 2 * d:3 * d] * c_ref[...]
        m_ref[...] = m.astype(BF16)

    return pl.pallas_call(
        body, name="merge_fwd", grid=(s // tr,),
        in_specs=[_row_spec(tr, 3 * d), _row_spec(tr, d), _row_spec(tr, d), _row_spec(tr, d)],
        out_specs=_row_spec(tr, d), out_shape=jax.ShapeDtypeStruct((s, d), BF16),
        compiler_params=_params(("parallel",)),
    )(gates, pa, pb, pc)


def _merge_bwd(dm, gates, pa, pb, pc, tr):
    s, d = pa.shape

    def body(dm_ref, g_ref, a_ref, b_ref, c_ref, da_ref, db_ref, dc_ref, dg_ref, gb_ref):
        @pl.when(pl.program_id(0) == 0)
        def _():
            gb_ref[...] = jnp.zeros_like(gb_ref)

        dmv = dm_ref[...]
        for n, (p_ref, o_ref) in enumerate(((a_ref, da_ref), (b_ref, db_ref), (c_ref, dc_ref))):
            gv = g_ref[:, n * d:(n + 1) * d]
            o_ref[...] = (dmv * gv).astype(BF16)
            dlogit = (dmv * p_ref[...]) * (gv * (1.0 - gv))
            dg_ref[:, n * d:(n + 1) * d] = dlogit.astype(BF16)
            gb_ref[:, n * d:(n + 1) * d] += jnp.sum(dlogit, axis=0, keepdims=True)

    act = jax.ShapeDtypeStruct((s, d), BF16)
    return pl.pallas_call(
        body, name="merge_bwd", grid=(s // tr,),
        in_specs=[_row_spec(tr, d), _row_spec(tr, 3 * d), _row_spec(tr, d), _row_spec(tr, d), _row_spec(tr, d)],
        out_specs=[_row_spec(tr, d), _row_spec(tr, d), _row_spec(tr, d), _row_spec(tr, 3 * d), _vec_spec(3 * d)],
        out_shape=[act, act, act, jax.ShapeDtypeStruct((s, 3 * d), BF16), jax.ShapeDtypeStruct((1, 3 * d), F32)],
        compiler_params=_params(("arbitrary",)),
    )(dm, gates, pa, pb, pc)


def _resid_bwd(dxo, out, rg, tr):
    s, d = dxo.shape

    def body(dx_ref, o_ref, rg_ref, do_ref, drg_ref):
        @pl.when(pl.program_id(0) == 0)
        def _():
            drg_ref[...] = jnp.zeros_like(drg_ref)

        dxv = dx_ref[...]
        do_ref[...] = (dxv * rg_ref[...]).astype(BF16)
        drg_ref[...] += jnp.sum(dxv * o_ref[...], axis=0, keepdims=True)

    return pl.pallas_call(
        body, name="resid_bwd", grid=(s // tr,),
        in_specs=[_row_spec(tr, d), _row_spec(tr, d), _vec_spec(d)],
        out_specs=[_row_spec(tr, d), _vec_spec(d)],
        out_shape=[jax.ShapeDtypeStruct((s, d), BF16), jax.ShapeDtypeStruct((1, d), F32)],
        compiler_params=_params(("arbitrary",)),
    )(dxo, out, rg)


def _split_bf16(v):
    hi = v.astype(BF16)
    return hi, (v - hi.astype(F32)).astype(BF16)


def _log_keep(z):
    e = jnp.exp(-jnp.abs(z))
    return -(jnp.maximum(z, 0.0) + jnp.log(1.0 + e)), e


def _sb_fwd(proj, d, blk):
    s = proj.shape[0]
    hps, wid = 4, 4 * HEAD_DIM
    groups, nq = d // wid, s // blk
    qc, kc, vc, zc = d // wid, 2 * d // wid, 3 * d // wid, 4 * d // wid
    scale = HEAD_DIM ** -0.5
    assert nq <= HEAD_DIM

    def body(q_ref, k_ref, v_ref, zb_ref, att_ref, yb_ref, rs_ref):
        i = pl.program_id(1)
        r_io = lax.broadcasted_iota(jnp.int32, (blk, blk), 0)
        c_io = lax.broadcasted_iota(jnp.int32, (blk, blk), 1)
        tri = (r_io >= c_io).astype(BF16)
        strict = c_io < r_io
        lane = lax.broadcasted_iota(jnp.int32, (blk, HEAD_DIM), 1)
        heads = [slice(n * HEAD_DIM, (n + 1) * HEAD_DIM) for n in range(hps)]
        qs = [(q_ref[:, hd] * scale).astype(BF16) for hd in heads]

        def block(j, carry, masked):
            ks = pl.multiple_of(j * blk, blk)
            rng = range(hps)
            zs = [lax.dot_general(qs[n], k_ref[pl.ds(ks, blk), heads[n]].astype(BF16), (NT, ((), ())),
                                  preferred_element_type=F32) for n in rng]
            splits = []
            for n in rng:
                lk, _ = _log_keep(zs[n])
                if masked:
                    lk = jnp.where(strict, lk, 0.0)
                splits.append(_split_bf16(lk))
            csums = [carry[n][0] + jnp.dot(splits[n][0], tri, preferred_element_type=F32)
                     + jnp.dot(splits[n][1], tri, preferred_element_type=F32) for n in rng]
            probs = []
            for n in rng:
                a = jnp.exp(zs[n] + csums[n])
                if masked:
                    a = jnp.where(strict, a, 0.0)
                probs.append(a.astype(BF16))
            out = []
            for n in rng:
                run, acc, rs = carry[n]
                acc = acc + jnp.dot(probs[n], v_ref[pl.ds(ks, blk), heads[n]].astype(BF16),
                                    preferred_element_type=F32)
                out.append((csums[n][:, 0:1], acc, jnp.where(lane == j, run, rs)))
            return tuple(out)

        zero = (jnp.zeros((blk, 1), F32), jnp.zeros((blk, HEAD_DIM), F32), jnp.zeros((blk, HEAD_DIM), F32))
        carry = block(i, (zero,) * hps, True)
        carry = lax.fori_loop(0, i, lambda jj, cr: block(i - 1 - jj, cr, False), carry)
        for hd, (_, acc, rs) in zip(heads, carry):
            att_ref[:, hd] = acc
            rs_ref[:, hd] = rs
            zb = zb_ref[:, hd]
            yb_ref[:, hd] = (acc * (zb * _sigmoid(zb))).astype(BF16)

    blk_spec = pl.BlockSpec((blk, wid), lambda h, i: (i, h))
    return pl.pallas_call(
        body, name="sb_fwd", grid=(groups, nq),
        in_specs=[pl.BlockSpec((blk, wid), lambda h, i: (i, qc + h)),
                  pl.BlockSpec((s, wid), lambda h, i: (0, kc + h)),
                  pl.BlockSpec((s, wid), lambda h, i: (0, vc + h)),
                  pl.BlockSpec((blk, wid), lambda h, i: (i, zc + h))],
        out_specs=[blk_spec, blk_spec, blk_spec],
        out_shape=[jax.ShapeDtypeStruct((s, d), F32), jax.ShapeDtypeStruct((s, d), BF16),
                   jax.ShapeDtypeStruct((s, d), F32)],
        compiler_params=_params(("parallel", "arbitrary")),
    )(proj, proj, proj, proj)


def _sb_bwd(proj, att, dyb, rsave, d, blk):
    s = proj.shape[0]
    hps, wid = HEADS_PER_STEP, HEADS_PER_STEP * HEAD_DIM
    groups, nq = d // wid, s // blk
    qc, kc, vc, zc = d // wid, 2 * d // wid, 3 * d // wid, 4 * d // wid
    scale = HEAD_DIM ** -0.5

    def body(q_ref, k_ref, v_ref, zb_ref, att_ref, dyb_ref, rs_ref, dq_ref, dk_ref, dv_ref, dzb_ref):
        i = pl.program_id(1)

        @pl.when(i == 0)
        def _():
            dk_ref[...] = jnp.zeros_like(dk_ref)
            dv_ref[...] = jnp.zeros_like(dv_ref)

        r_io = lax.broadcasted_iota(jnp.int32, (blk, blk), 0)
        c_io = lax.broadcasted_iota(jnp.int32, (blk, blk), 1)
        tri = (r_io >= c_io).astype(BF16)
        tri_up = (r_io <= c_io).astype(BF16)
        strict = c_io < r_io
        lane = lax.broadcasted_iota(jnp.int32, (blk, HEAD_DIM), 1)
        heads = [slice(n * HEAD_DIM, (n + 1) * HEAD_DIM) for n in range(hps)]
        qs, dos, rss = [], [], []
        for hd in heads:
            qs.append((q_ref[:, hd] * scale).astype(BF16))
            zb, dyb_t = zb_ref[:, hd], dyb_ref[:, hd]
            sg = _sigmoid(zb)
            dzb_ref[:, hd] = dyb_t * att_ref[:, hd] * _dsilu(zb, sg)
            dos.append((dyb_t * (zb * sg)).astype(BF16))
            rss.append(rs_ref[:, hd])

        def block(j, carry, masked):
            ks = pl.multiple_of(j * blk, blk)
            rng = range(hps)
            kbs = [k_ref[pl.ds(ks, blk), heads[n]].astype(BF16) for n in rng]
            vbs = [v_ref[pl.ds(ks, blk), heads[n]].astype(BF16) for n in rng]
            zs = [lax.dot_general(qs[n], kbs[n], (NT, ((), ())), preferred_element_type=F32) for n in rng]
            das = [lax.dot_general(dos[n], vbs[n], (NT, ((), ())), preferred_element_type=F32) for n in rng]
            splits, betas = [], []
            for n in rng:
                lk, e = _log_keep(zs[n])
                betas.append(jnp.where(zs[n] >= 0, 1.0, e) / (1.0 + e))
                if masked:
                    lk = jnp.where(strict, lk, 0.0)
                splits.append(_split_bf16(lk))
            csums = []
            for n in rng:
                run = jnp.sum(jnp.where(lane == j, rss[n], 0.0), axis=1, keepdims=True)
                csums.append(run + jnp.dot(splits[n][0], tri, preferred_element_type=F32)
                             + jnp.dot(splits[n][1], tri, preferred_element_type=F32))
            probs, gs = [], []
            for n in rng:
                a = jnp.exp(zs[n] + csums[n])
                if masked:
                    a = jnp.where(strict, a, 0.0)
                probs.append(a.astype(BF16))
                gs.append(a * das[n])
            gcums = [carry[n][0] + jnp.dot(gs[n].astype(BF16), tri_up, preferred_element_type=F32) for n in rng]
            for n in rng:
                dv_ref[pl.ds(ks, blk), heads[n]] += lax.dot_general(probs[n], dos[n], (TN, ((), ())),
                                                                   preferred_element_type=F32)
            dzs = []
            for n in rng:
                dz = gs[n] - betas[n] * gcums[n]
                if masked:
                    dz = jnp.where(strict, dz, 0.0)
                dzs.append(dz.astype(BF16))
            out = []
            for n in rng:
                dq = carry[n][1] + jnp.dot(dzs[n], kbs[n], preferred_element_type=F32)
                dk_ref[pl.ds(ks, blk), heads[n]] += lax.dot_general(dzs[n], qs[n], (TN, ((), ())),
                                                                   preferred_element_type=F32)
                out.append((gcums[n][:, blk - 1:blk], dq))
            return tuple(out)

        zero = (jnp.zeros((blk, 1), F32), jnp.zeros((blk, HEAD_DIM), F32))
        carry = lax.fori_loop(0, i, lambda j, cr: block(j, cr, False), (zero,) * hps)
        carry = block(i, carry, True)
        for hd, (_, dq) in zip(heads, carry):
            dq_ref[:, hd] = dq * scale

    blk_spec = pl.BlockSpec((blk, wid), lambda h, i: (i, h))
    full_spec = pl.BlockSpec((s, wid), lambda h, i: (0, h))
    act = jax.ShapeDtypeStruct((s, d), F32)
    return pl.pallas_call(
        body, name="sb_bwd", grid=(groups, nq),
        in_specs=[pl.BlockSpec((blk, wid), lambda h, i: (i, qc + h)),
                  pl.BlockSpec((s, wid), lambda h, i: (0, kc + h)),
                  pl.BlockSpec((s, wid), lambda h, i: (0, vc + h)),
                  pl.BlockSpec((blk, wid), lambda h, i: (i, zc + h)),
                  blk_spec, blk_spec, blk_spec],
        out_specs=[blk_spec, full_spec, full_spec, blk_spec],
        out_shape=[act, act, act, act],
        compiler_params=_params(("parallel", "arbitrary")),
    )(proj, proj, proj, proj, att, dyb, rsave)


def _adamw(w, g, m, v):
    shape = w.shape
    cols = shape[-1]
    rows = w.size // cols
    w2, g2, m2, v2 = (t.reshape(rows, cols) for t in (w, g, m, v))
    tr, tc = _tile(rows, 512, 8), _tile(cols, 1024, 128)
    c1 = 1.0 - ADAM_B1 ** ADAM_STEP
    c2 = 1.0 - ADAM_B2 ** ADAM_STEP

    def body(w_ref, g_ref, m_ref, v_ref, d_ref, nm_ref, nv_ref):
        gv = g_ref[...]
        nm = ADAM_B1 * m_ref[...] + (1.0 - ADAM_B1) * gv
        nv = ADAM_B2 * v_ref[...] + (1.0 - ADAM_B2) * (gv * gv)
        d_ref[...] = -ADAM_LR * ((nm / c1) / (jnp.sqrt(nv / c2) + ADAM_EPS) + ADAM_WD * w_ref[...])
        nm_ref[...] = nm
        nv_ref[...] = nv

    spec = pl.BlockSpec((tr, tc), lambda i, j: (i, j))
    sd = jax.ShapeDtypeStruct((rows, cols), F32)
    outs = pl.pallas_call(
        body, name="adamw", grid=(rows // tr, cols // tc),
        in_specs=[spec] * 4, out_specs=[spec] * 3, out_shape=[sd] * 3,
        compiler_params=_params(("parallel", "parallel")),
    )(w2, g2, m2, v2)
    return tuple(o.reshape(shape) for o in outs)


def _sum_blocks(gathered, n, rows):
    width = gathered.shape[1]
    tw = _tile(width, 8192)

    def body(g_ref, o_ref):
        acc = g_ref[0:rows, :]
        for b in range(1, n):
            acc = acc + g_ref[b * rows:(b + 1) * rows, :]
        o_ref[...] = acc

    return pl.pallas_call(
        body, name="sum_blocks", grid=(width // tw,),
        in_specs=[pl.BlockSpec((n * rows, tw), lambda i: (0, i))],
        out_specs=pl.BlockSpec((rows, tw), lambda i: (0, i)),
        out_shape=jax.ShapeDtypeStruct((rows, width), F32),
        compiler_params=_params(("parallel",)),
    )(gathered)


def _silu_bf16(c_rows):
    def body(c_ref, o_ref):
        cv = c_ref[...]
        o_ref[...] = (cv * _sigmoid(cv)).astype(BF16)

    return pl.pallas_call(
        body, name="silu_c", out_shape=jax.ShapeDtypeStruct(c_rows.shape, BF16),
        in_specs=[pl.BlockSpec(memory_space=pltpu.VMEM)], out_specs=pl.BlockSpec(memory_space=pltpu.VMEM),
    )(c_rows)


def _place():
    x, y, c = lax.axis_index("x"), lax.axis_index("y"), lax.axis_index("c")
    chips = [(1 - x, y), (x, 1 - y), (1 - x, 1 - y)]
    return x, y, c, chips


def _all_gather_small(block, name):
    m_per, n = block.shape

    def body(x_ref, out_ref, send_sems, recv_sems, local_sem):
        x, y, c, chips = _place()
        me, sibling = (x, y, c), (x, y, 1 - c)

        def rows(px, py, pc):
            return out_ref.at[pl.ds((4 * px + 2 * py + pc) * m_per, m_per), :]

        def copy(k, blk, to, src=None):
            return pltpu.make_async_remote_copy(
                src_ref=rows(*blk) if src is None else src, dst_ref=rows(*blk),
                send_sem=send_sems.at[k], recv_sem=recv_sems.at[k], device_id=to, device_id_type=MESH)

        mine = pltpu.make_async_copy(x_ref, rows(*me), local_sem)
        mine.start()
        first = [copy(0, me, sibling, src=x_ref)]
        first += [copy(1 + j, me, (*chip, c), src=x_ref) for j, chip in enumerate(chips)]
        for cp in first:
            cp.start()
        passed = [copy(4 + j, (*chip, c), sibling) for j, chip in enumerate(chips)]
        for j, chip in enumerate(chips):
            copy(1 + j, (*chip, c), me).wait_recv()
            passed[j].start()
        copy(0, sibling, me).wait_recv()
        for j, chip in enumerate(chips):
            copy(4 + j, (*chip, 1 - c), me).wait_recv()
        for cp in first + passed:
            cp.wait_send()
        mine.wait()

    return pl.pallas_call(
        body, name=name, out_shape=jax.ShapeDtypeStruct((N_DEV * m_per, n), block.dtype),
        in_specs=[pl.BlockSpec(memory_space=pltpu.VMEM)], out_specs=pl.BlockSpec(memory_space=pltpu.VMEM),
        scratch_shapes=[pltpu.SemaphoreType.DMA((7,)), pltpu.SemaphoreType.DMA((7,)), pltpu.SemaphoreType.DMA],
        compiler_params=pltpu.CompilerParams(vmem_limit_bytes=V7X_VMEM_LIMIT),
    )(block)


_ANY = pl.BlockSpec(memory_space=pl.ANY)


def _place_shard(w, chip_idx):
    layers, r, cols = w.shape
    tr, tc = _tile(r, 512, 16), _tile(cols, 1024)

    def body(j_ref, w_ref, o_ref):
        o_ref[...] = w_ref[...].astype(BF16)

    return pl.pallas_call(
        body, name="place_shard",
        grid_spec=pltpu.PrefetchScalarGridSpec(
            num_scalar_prefetch=1, grid=(layers, r // tr, cols // tc),
            in_specs=[pl.BlockSpec((None, tr, tc), lambda l, i, n, j: (l, i, n))],
            out_specs=pl.BlockSpec((None, None, tr, tc), lambda l, i, n, j: (j[0], l, i, n))),
        out_shape=jax.ShapeDtypeStruct((N_CHIPS, layers, r, cols), BF16),
        compiler_params=_params(("parallel", "parallel", "parallel")),
    )(chip_idx, w)


def _gather_weights(bufs):
    n = len(bufs)

    def body(*refs):
        outs = refs[n:2 * n]
        send_sems, recv_sems = refs[2 * n:]
        x, y, c, chips = _place()
        my_chip = 2 * x + y
        sibling = (x, y, 1 - c)

        def half(t, chip_idx, hc):
            h = outs[t].shape[2] // 2
            return outs[t].at[chip_idx, :, pl.ds(hc * h, h), :]

        def copy(t, k, ref, to):
            return pltpu.make_async_remote_copy(src_ref=ref, dst_ref=ref, send_sem=send_sems.at[6 * t + k],
                                                recv_sem=recv_sems.at[6 * t + k], device_id=to, device_id_type=MESH)

        sends = []
        for t in range(n):
            for k, chip in enumerate(chips):
                sends.append(copy(t, k, half(t, my_chip, c), (*chip, c)))
                sends[-1].start()
        for t in range(n):
            for k, (cx, cy) in enumerate(chips):
                landed = half(t, 2 * cx + cy, c)
                copy(t, k, landed, (cx, cy, c)).wait_recv()
                sends.append(copy(t, 3 + k, landed, sibling))
                sends[-1].start()
        for t in range(n):
            for k, (cx, cy) in enumerate(chips):
                copy(t, 3 + k, half(t, 2 * cx + cy, 1 - c), sibling).wait_recv()
        for cp in sends:
            cp.wait_send()

    return pl.pallas_call(
        body, name="gather_weights",
        out_shape=[jax.ShapeDtypeStruct(b.shape, b.dtype) for b in bufs],
        in_specs=[_ANY] * n, out_specs=[_ANY] * n, input_output_aliases={t: t for t in range(n)},
        scratch_shapes=[pltpu.SemaphoreType.DMA((6 * n,)), pltpu.SemaphoreType.DMA((6 * n,))],
    )(*bufs)


def _swap_halves(grads):
    n = len(grads)

    def body(*refs):
        ins, outs = refs[:n], refs[n:2 * n]
        send_sems, recv_sems = refs[2 * n:]
        x, y, c, _ = _place()
        copies = []
        for t in range(n):
            h = ins[t].shape[1] // 2
            copies.append(pltpu.make_async_remote_copy(
                src_ref=ins[t].at[:, pl.ds((1 - c) * h, h), :], dst_ref=outs[t],
                send_sem=send_sems.at[t], recv_sem=recv_sems.at[t], device_id=(x, y, 1 - c), device_id_type=MESH))
            copies[-1].start()
        for cp in copies:
            cp.wait()

    return pl.pallas_call(
        body, name="swap_halves",
        out_shape=[jax.ShapeDtypeStruct((g.shape[0], g.shape[1] // 2, g.shape[2]), g.dtype) for g in grads],
        in_specs=[_ANY] * n, out_specs=[_ANY] * n,
        scratch_shapes=[pltpu.SemaphoreType.DMA((n,)), pltpu.SemaphoreType.DMA((n,))],
    )(*grads)


def _send_to_owners(parts):
    n = len(parts)

    def body(*refs):
        ins, outs = refs[:n], refs[n:2 * n]
        send_sems, recv_sems = refs[2 * n:]
        x, y, c, chips = _place()
        copies = []
        for t in range(n):
            for k, (cx, cy) in enumerate(chips):
                copies.append(pltpu.make_async_remote_copy(
                    src_ref=ins[t].at[2 * cx + cy], dst_ref=outs[t].at[k],
                    send_sem=send_sems.at[3 * t + k], recv_sem=recv_sems.at[3 * t + k],
                    device_id=(cx, cy, c), device_id_type=MESH))
                copies[-1].start()
        for cp in copies:
            cp.wait()

    return pl.pallas_call(
        body, name="send_to_owners",
        out_shape=[jax.ShapeDtypeStruct((3,) + p.shape[1:], p.dtype) for p in parts],
        in_specs=[_ANY] * n, out_specs=[_ANY] * n,
        scratch_shapes=[pltpu.SemaphoreType.DMA((3 * n,)), pltpu.SemaphoreType.DMA((3 * n,))],
    )(*parts)


def _share_halves(bufs):
    n = len(bufs)

    def body(*refs):
        outs = refs[n:2 * n]
        send_sems, recv_sems = refs[2 * n:]
        x, y, c, _ = _place()

        def half(t, hc):
            h = outs[t].shape[1] // 2
            return outs[t].at[:, pl.ds(hc * h, h), :]

        def copy(t, ref):
            return pltpu.make_async_remote_copy(src_ref=ref, dst_ref=ref, send_sem=send_sems.at[t],
                                                recv_sem=recv_sems.at[t], device_id=(x, y, 1 - c),
                                                device_id_type=MESH)

        sends = [copy(t, half(t, c)) for t in range(n)]
        for cp in sends:
            cp.start()
        for t in range(n):
            copy(t, half(t, 1 - c)).wait_recv()
        for cp in sends:
            cp.wait_send()

    return pl.pallas_call(
        body, name="share_halves",
        out_shape=[jax.ShapeDtypeStruct(b.shape, b.dtype) for b in bufs],
        in_specs=[_ANY] * n, out_specs=[_ANY] * n, input_output_aliases={t: t for t in range(n)},
        scratch_shapes=[pltpu.SemaphoreType.DMA((n,)), pltpu.SemaphoreType.DMA((n,))],
    )(*bufs)


def _sum_sibling(grad, recv, c_idx):
    _, r, cols = grad.shape
    h = r // 2
    tr, tc = _tile(h, 512, 16), _tile(cols, 1024)
    per = h // tr

    def body(c_ref, g_ref, r_ref, o_ref):
        o_ref[...] = (g_ref[...] + r_ref[...]).astype(BF16)

    return pl.pallas_call(
        body, name="sum_sibling",
        grid_spec=pltpu.PrefetchScalarGridSpec(
            num_scalar_prefetch=1, grid=(N_CHIPS, per, cols // tc),
            in_specs=[pl.BlockSpec((None, tr, tc), lambda j, i, n, c: (j, c[0] * per + i, n)),
                      pl.BlockSpec((None, tr, tc), lambda j, i, n, c: (j, i, n))],
            out_specs=pl.BlockSpec((None, tr, tc), lambda j, i, n, c: (j, i, n))),
        out_shape=jax.ShapeDtypeStruct((N_CHIPS, h, cols), BF16),
        compiler_params=_params(("parallel", "parallel", "parallel")),
    )(c_idx, grad, recv)


def _sum_owner(parts, recv, place_idx, layer, layers, prev=None):
    _, h, cols = parts.shape
    tr, tc = _tile(h, 512, 16), _tile(cols, 1024)
    per = h // tr

    def body(idx_ref, p_ref, r_ref, *rest):
        o_ref = rest[-1]
        o_ref[...] = (p_ref[...].astype(F32) + r_ref[0].astype(F32) + r_ref[1].astype(F32)
                      + r_ref[2].astype(F32))

    in_specs = [pl.BlockSpec((None, tr, tc), lambda i, n, idx: (idx[0], i, n)),
                pl.BlockSpec((3, tr, tc), lambda i, n, idx: (0, i, n))]
    args = [place_idx, parts, recv]
    aliases = {}
    if prev is not None:
        in_specs.append(_ANY)
        args.append(prev)
        aliases = {3: 0}
    return pl.pallas_call(
        body, name="sum_owner",
        grid_spec=pltpu.PrefetchScalarGridSpec(
            num_scalar_prefetch=1, grid=(per, cols // tc), in_specs=in_specs,
            out_specs=pl.BlockSpec((None, tr, tc), lambda i, n, idx: (layer, idx[1] * per + i, n))),
        out_shape=jax.ShapeDtypeStruct((layers, 2 * h, cols), F32),
        input_output_aliases=aliases,
        compiler_params=_params(("parallel", "parallel")),
    )(*args)


def _layer_fwd(x, mod, p, cfg):
    d, tr, tm = cfg["d"], cfg["tr"], cfg["tm"]
    shift, scale, rg = mod[:, 0:d], mod[:, d:2 * d], mod[:, 2 * d:3 * d]
    h = _norm_fwd(x, p["norm_g"], scale, shift, tr)
    proj, = _mm_nn("proj", h, p["w_in"], tm=tm, tn=cfg["tn_in"], tk=d)
    gates, = _mm_nn("gates", h, p["w_gate"], tm=tm, tn=cfg["tn_gate"], tk=d,
                    extras=[(p["b_gate"], pl.BlockSpec((1, cfg["tn_gate"]), lambda i, j, k: (0, j)))],
                    epilogue=lambda acc, b: (_sigmoid(acc + b),))
    ya = _pool_fwd(proj, p["pool_w"], p["pool_scale"], d, tr)
    att, yb, rsave = _sb_fwd(proj, d, cfg["blk"])
    yc = _conv_fwd(proj, p["conv_w"], d, tr)
    pa, = _mm_nn("branch_a", ya, p["w_br_a"], tm=tm, tn=cfg["tn_d"], tk=d // 2)
    pb, = _mm_nn("branch_b", yb, p["w_br_b"], tm=tm, tn=cfg["tn_d"], tk=cfg["tk_row"])
    pc, = _mm_nn("branch_c", yc, p["w_br_c"], tm=tm, tn=cfg["tn_d"], tk=d // 2)
    merged = _merge_fwd(gates, pa, pb, pc, cfg["tr_small"])
    tn = cfg["tn_d"]
    s = x.shape[0]
    blk = pl.BlockSpec((tm, tn), lambda i, j, k: (i, j))
    sd = jax.ShapeDtypeStruct((s, d), F32)
    out, x_next = _mm_nn("out_proj", merged, p["w_out"], tm=tm, tn=tn, tk=cfg["tk_row"],
                         outs=[(sd, blk), (sd, blk)],
                         extras=[(x, blk), (rg, pl.BlockSpec((1, tn), lambda i, j, k: (0, j)))],
                         epilogue=lambda acc, xv, g: (acc, xv + g * acc))
    saved = dict(x=x, h=h, proj=proj, gates=gates, ya=ya, yb=yb, yc=yc, att=att, rsave=rsave,
                 pa=pa, pb=pb, pc=pc, merged=merged, out=out, scale=scale, rg=rg)
    return x_next, saved


def _layer_bwd(dxo, sv, p, cfg):
    d, tr, tm, tk_s = cfg["d"], cfg["tr"], cfg["tm"], cfg["tk_s"]
    dout, drg = _resid_bwd(dxo, sv["out"], sv["rg"], tr)
    dmerged = _mm_nt("d_merged", dout, p["w_out"], tm=tm, tn=cfg["tn_row"], tk=d)
    g_out = _mm_tn("g_w_out", sv["merged"], dout, "row", tm=cfg["tn_row"], tn=cfg["tn_d"], tk=tk_s)
    dpa, dpb, dpc, dlogit, g_bgate = _merge_bwd(dmerged, sv["gates"], sv["pa"], sv["pb"], sv["pc"], cfg["tr_small"])
    dya = _mm_nt("d_ya", dpa, p["w_br_a"], tm=tm, tn=cfg["tn_half"], tk=cfg["tn_d"])
    dyb = _mm_nt("d_yb", dpb, p["w_br_b"], tm=tm, tn=cfg["tn_row"], tk=d)
    dyc = _mm_nt("d_yc", dpc, p["w_br_c"], tm=tm, tn=cfg["tn_half"], tk=cfg["tn_d"])
    g_a = _mm_tn("g_w_br_a", sv["ya"], dpa, "col", tm=cfg["tn_half"], tn=cfg["tn_d"], tk=tk_s)
    g_b = _mm_tn("g_w_br_b", sv["yb"], dpb, "row", tm=cfg["tn_row"], tn=cfg["tn_d"], tk=tk_s)
    g_c = _mm_tn("g_w_br_c", sv["yc"], dpc, "col", tm=cfg["tn_half"], tn=cfg["tn_d"], tk=tk_s)
    d_a, g_pool_w, g_pool_scale = _pool_bwd(sv["proj"], dya, p["pool_w"], p["pool_scale"], d, tr)
    d_c, g_conv_w = _conv_bwd(sv["proj"], dyc, p["conv_w"], d, tr)
    dq, dk, dv, dzb = _sb_bwd(sv["proj"], sv["att"], dyb, sv["rsave"], d, cfg["blk"])
    dproj = jnp.concatenate([d_a, dq.astype(BF16), dk.astype(BF16), dv.astype(BF16), dzb.astype(BF16), d_c], axis=1)
    dh_in = _mm_nt("d_h_in", dproj, p["w_in"], tm=tm, tn=cfg["tn_d"], tk=cfg["tk_in"])
    tn = cfg["tn_d"]
    dh = _mm_nt("d_h_gate", dlogit, p["w_gate"], tm=tm, tn=tn, tk=cfg["tn_gate"],
                extras=[(dh_in, pl.BlockSpec((tm, tn), lambda i, j, k: (i, j)))],
                epilogue=lambda acc, prev: (prev + acc,))
    g_in = _mm_tn("g_w_in", sv["h"], dproj, "col", tm=cfg["tm_g"], tn=cfg["tn_in"], tk=tk_s)
    g_gate = _mm_tn("g_w_gate", sv["h"], dlogit, "col", tm=cfg["tm_g"], tn=cfg["tn_gate"], tk=tk_s)
    dx, dshift, dscale, g_norm = _norm_bwd(dh, sv["x"], p["norm_g"], sv["scale"], dxo, tr)
    big = dict(w_in=g_in, w_gate=g_gate, w_br_a=g_a, w_br_b=g_b, w_br_c=g_c, w_out=g_out)
    small = dict(dmod=jnp.concatenate([dshift, dscale, drg], axis=1), norm_g=g_norm, pool_scale=g_pool_scale,
                 b_gate=g_bgate, conv_w=g_conv_w, pool_w=g_pool_w)
    return dx, big, small


BIG = ("w_in", "w_gate", "w_br_a", "w_br_b", "w_br_c", "w_out")
BIG_KIND = dict(w_in="col", w_gate="col", w_br_a="col", w_br_b="row", w_br_c="col", w_out="row")


def _pad_to(v, n):
    return jnp.pad(v, (0, n - v.shape[0]))


def kernel(x, c, norm_g, w_ada, b_ada, w_in, pool_w, pool_scale, conv_w, w_br_a, w_br_b, w_br_c, w_gate, b_gate, w_out, final_g, loss_target, m_norm_g, m_w_ada, m_b_ada, m_w_in, m_pool_w, m_pool_scale, m_conv_w, m_w_br_a, m_w_br_b, m_w_br_c, m_w_gate, m_b_gate, m_w_out, m_final_g, v_norm_g, v_w_ada, v_b_ada, v_w_in, v_pool_w, v_pool_scale, v_conv_w, v_w_br_a, v_w_br_b, v_w_br_c, v_w_gate, v_b_gate, v_w_out, v_final_g):
    weights = dict(norm_g=norm_g, w_ada=w_ada, b_ada=b_ada, w_in=w_in, pool_w=pool_w, pool_scale=pool_scale,
                   conv_w=conv_w, w_br_a=w_br_a, w_br_b=w_br_b, w_br_c=w_br_c, w_gate=w_gate, b_gate=b_gate,
                   w_out=w_out, final_g=final_g)
    mom_m = dict(norm_g=m_norm_g, w_ada=m_w_ada, b_ada=m_b_ada, w_in=m_w_in, pool_w=m_pool_w,
                 pool_scale=m_pool_scale, conv_w=m_conv_w, w_br_a=m_w_br_a, w_br_b=m_w_br_b, w_br_c=m_w_br_c,
                 w_gate=m_w_gate, b_gate=m_b_gate, w_out=m_w_out, final_g=m_final_g)
    mom_v = dict(norm_g=v_norm_g, w_ada=v_w_ada, b_ada=v_b_ada, w_in=v_w_in, pool_w=v_pool_w,
                 pool_scale=v_pool_scale, conv_w=v_conv_w, w_br_a=v_w_br_a, w_br_b=v_w_br_b, w_br_c=v_w_br_c,
                 w_gate=v_w_gate, b_gate=v_b_gate, w_out=v_w_out, final_g=v_final_g)
    names = list(weights)

    _, s, d = x.shape
    layers = norm_g.shape[0]
    pw, gd, cc = d // 2, d // 8, d // 2
    ada_cols = 3 * d // N_CHIPS
    xi, yi, ci = lax.axis_index("x"), lax.axis_index("y"), lax.axis_index("c")
    me = 4 * xi + 2 * yi + ci
    my_chip = 2 * xi + yi
    c_idx = jnp.reshape(ci, (1,)).astype(jnp.int32)
    chip_idx = jnp.reshape(my_chip, (1,)).astype(jnp.int32)

    cfg = dict(
        d=d, tr=_tile(s, 256, 16), tr_small=_tile(s, 128, 16), tm=_tile(s, 1024, 16), tk_s=_tile(s, 1024, 16),
        blk=_tile(s, min(256, max(s // 4, 16)), 16),
        tn_in=_tile(7 * d // N_CHIPS, 512), tk_in=_tile(7 * d // N_CHIPS, 1792),
        tn_gate=_tile(3 * d // N_CHIPS, 512), tn_d=_tile(d // N_CHIPS, 512),
        tn_row=_tile(d // N_CHIPS, 512, 16), tk_row=_tile(d // N_CHIPS, 512, 16),
        tn_half=_tile(d // 2, 512), tm_g=_tile(d, 1024),
    )

    conv_flat = conv_w.reshape(-1)
    conv_len = -(-conv_flat.shape[0] // 1024) * 1024
    pack0 = jnp.concatenate([c.reshape(-1), _pad_to(conv_flat, conv_len), pool_w.reshape(-1)])
    w0 = -(-pack0.shape[0] // 1024) * 1024
    g0 = _all_gather_small(_pad_to(pack0, w0).reshape(8, w0 // 8), "gather_small").reshape(N_DEV, w0)
    c_all = g0[:, 0:d]
    chip_rows = g0[0::2]
    conv_full = jnp.concatenate(
        [chip_rows[j, d:d + conv_flat.shape[0]].reshape(conv_w.shape) for j in range(N_CHIPS)], axis=2)
    pool_full = jnp.concatenate(
        [chip_rows[j, d + conv_len:d + conv_len + pool_w.size].reshape(pool_w.shape) for j in range(N_CHIPS)],
        axis=2)
    pool_bf = pool_full.astype(BF16)

    sc16 = _silu_bf16(jnp.pad(c_all, ((0, 16 - N_DEV), (0, 0))))
    mods = []
    for l in range(layers):
        bias = lax.dynamic_slice(b_ada[l], (my_chip * ada_cols,), (ada_cols,)).reshape(1, ada_cols)
        tn = _tile(ada_cols, 512)
        mod_l, = _mm("mod", sc16, w_ada[l], dims=NN, grid=(1, ada_cols // tn, 1),
                     a_spec=pl.BlockSpec((16, d), lambda i, j, k: (0, 0)),
                     b_spec=pl.BlockSpec((d, tn), lambda i, j, k: (0, j)),
                     acc_shape=(16, tn),
                     outs=[(jax.ShapeDtypeStruct((16, ada_cols), F32), pl.BlockSpec((16, tn), lambda i, j, k: (0, j)))],
                     extras=[(bias, pl.BlockSpec((1, tn), lambda i, j, k: (0, j)))],
                     epilogue=lambda acc, b: (acc + b,))
        mods.append(mod_l[0:N_DEV])
    g1 = _all_gather_small(jnp.concatenate(mods, axis=1), "gather_mod")
    g1 = g1.reshape(N_CHIPS, 2, N_DEV, layers, ada_cols)[:, 0]
    mod_all = jnp.transpose(g1, (1, 2, 0, 3)).reshape(N_DEV, layers, 3 * d)
    mod_me = lax.dynamic_slice(mod_all, (me, 0, 0), (1, layers, 3 * d))[0]

    gathered = _gather_weights([_place_shard(weights[n], chip_idx) for n in BIG])
    params = []
    for l in range(layers):
        p = {n: _Sharded(g, BIG_KIND[n], layer=l) for n, g in zip(BIG, gathered)}
        p.update(norm_g=norm_g[l:l + 1], pool_scale=pool_scale[l:l + 1], b_gate=b_gate[l:l + 1],
                 conv_w=conv_full[l], pool_w=pool_bf[l])
        params.append(p)

    act = x[0]
    saved = []
    for l in range(layers):
        act, sv = _layer_fwd(act, mod_me[l:l + 1], params[l], cfg)
        saved.append(sv)
    loss_part, dact, g_final = _final_loss(act, final_g.reshape(1, d), loss_target[0], cfg["tr"])
    loss = lax.psum(loss_part[0, 0], ("x", "y", "c"))
    big_grads, small_grads = [None] * layers, [None] * layers
    for l in reversed(range(layers)):
        dact, big_grads[l], small_grads[l] = _layer_bwd(dact, saved[l], params[l], cfg)
    grad_x = dact.reshape(x.shape)

    flat = [big_grads[l][n] for n in BIG for l in range(layers)]
    recv = _swap_halves(flat)
    parts = [_sum_sibling(g, r, c_idx) for g, r in zip(flat, recv)]
    theirs = _send_to_owners(parts)
    place_idx = jnp.stack([my_chip, ci]).astype(jnp.int32)
    owned = []
    for w in range(len(BIG)):
        buf = None
        for l in range(layers):
            t = w * layers + l
            buf = _sum_owner(parts[t], theirs[t], place_idx, l, layers, prev=buf)
        owned.append(buf)
    full = _share_halves(owned)
    grads = {n: f.reshape(weights[n].shape) for n, f in zip(BIG, full)}

    small_names = ("dmod", "norm_g", "pool_scale", "b_gate", "conv_w", "pool_w")
    pieces = [small_grads[l][n].reshape(-1) for n in small_names for l in range(layers)] + [g_final.reshape(-1)]
    pack1 = jnp.concatenate(pieces)
    w1 = -(-pack1.shape[0] // 1024) * 1024
    g2 = _all_gather_small(_pad_to(pack1, w1).reshape(8, w1 // 8), "gather_grads")
    total = _sum_blocks(g2, N_DEV, 8).reshape(-1)
    off = 0
    summed = {}
    for n in small_names:
        per = small_grads[0][n].size
        summed[n] = jnp.stack([total[off + l * per:off + (l + 1) * per].reshape(small_grads[0][n].shape)
                               for l in range(layers)])
        off += layers * per
    grads["final_g"] = total[off:off + d]
    grads["norm_g"] = summed["norm_g"].reshape(layers, d)
    grads["pool_scale"] = summed["pool_scale"].reshape(layers, pw)
    grads["b_gate"] = summed["b_gate"].reshape(layers, 3 * d)
    grads["b_ada"] = summed["dmod"].reshape(layers, 3 * d)
    cs = cc // N_CHIPS
    grads["conv_w"] = lax.dynamic_slice(summed["conv_w"], (0, 0, my_chip * cs), (layers, 3, cs))
    rs_ = gd // N_CHIPS
    grads["pool_w"] = lax.dynamic_slice(summed["pool_w"], (0, 0, my_chip * rs_, 0), (layers, N_GROUPS, rs_, gd))
    dmod_all = g2.reshape(N_DEV, w1)[:, 0:layers * 3 * d].reshape(N_DEV, layers, 3 * d)
    g_ada = []
    for l in range(layers):
        cols = lax.dynamic_slice(dmod_all[:, l], (0, my_chip * ada_cols), (N_DEV, ada_cols))
        cols16 = jnp.pad(cols, ((0, 16 - N_DEV), (0, 0)))
        tn = _tile(ada_cols, 512)
        tm = _tile(d, 1024)
        ga, = _mm("g_w_ada", sc16, cols16, dims=TN, grid=(d // tm, ada_cols // tn, 1),
                  a_spec=pl.BlockSpec((16, tm), lambda i, j, k: (0, i)),
                  b_spec=pl.BlockSpec((16, tn), lambda i, j, k: (0, j)),
                  acc_shape=(tm, tn),
                  outs=[(jax.ShapeDtypeStruct((d, ada_cols), F32), pl.BlockSpec((tm, tn), lambda i, j, k: (i, j)))])
        g_ada.append(ga)
    grads["w_ada"] = jnp.stack(g_ada)

    deltas, new_m, new_v = {}, {}, {}
    for n in names:
        deltas[n], new_m[n], new_v[n] = _adamw(weights[n], grads[n], mom_m[n], mom_v[n])
    return (loss, grad_x, *[grads[n] for n in names], *[deltas[n] for n in names],
            *[new_m[n] for n in names], *[new_v[n] for n in names])
```

```python
import functools

import jax
import jax.numpy as jnp
from jax import lax
from jax.experimental import pallas as pl
from jax.experimental.pallas import tpu as pltpu

F32 = jnp.float32
BF16 = jnp.bfloat16
MESH = pl.DeviceIdType.MESH

N_CHIPS = 4
N_DEV = 8
N_GROUPS = 4
POOL_WINDOWS = (2, 4, 8, 16)
POOL_HALO = 16
CONV_HALO = 8
HEAD_DIM = 128
SB_FWD_HEADS = 4
SB_BWD_HEADS = 4
RMS_EPS = 1e-6
ADAM_LR = 0.001
ADAM_B1 = 0.9
ADAM_B2 = 0.999
ADAM_EPS = 1e-08
ADAM_WD = 0.01
ADAM_STEP = 10
V7X_VMEM_LIMIT = 56 * 1024 * 1024


def _tile(n, pref, mult=128):
    best = None
    t = mult
    while t <= min(n, pref):
        if n % t == 0:
            best = t
        t += mult
    return n if best is None else best


def _params(sem=None):
    return pltpu.CompilerParams(dimension_semantics=sem, vmem_limit_bytes=V7X_VMEM_LIMIT)


def _sigmoid(z):
    return jax.nn.sigmoid(z)


def _dsilu(z, sg):
    return sg * (1.0 + z * (1.0 - sg))


NN = ((1,), (0,))
NT = ((1,), (1,))
TN = ((0,), (0,))


def _mm(name, a, b, *, dims, grid, a_spec, b_spec, acc_shape, outs, extras=(), epilogue=None, with_col=False):
    nk = grid[2]
    ne, no = len(extras), len(outs)
    if epilogue is None:
        epilogue = lambda acc: (acc,)

    def body(a_ref, b_ref, *rest):
        ex, orefs, acc = rest[:ne], rest[ne:ne + no], rest[ne + no]
        k = pl.program_id(2)
        lead = (pl.program_id(1),) if with_col else ()

        @pl.when(k == 0)
        def _():
            acc[...] = jnp.zeros_like(acc)

        acc[...] += lax.dot_general(a_ref[...].astype(BF16), b_ref[...].astype(BF16), (dims, ((), ())),
                                    preferred_element_type=F32)

        @pl.when(k == nk - 1)
        def _():
            vals = epilogue(*lead, acc[...], *[e[...] for e in ex])
            for o, v in zip(orefs, vals):
                o[...] = v.astype(o.dtype)

    res = pl.pallas_call(
        body, name=name, grid=grid,
        in_specs=[a_spec, b_spec] + [s for _, s in extras],
        out_specs=[s for _, s in outs], out_shape=[sh for sh, _ in outs],
        scratch_shapes=[pltpu.VMEM(acc_shape, F32)],
        compiler_params=_params(("parallel", "parallel", "arbitrary")),
    )(a, b, *[e for e, _ in extras])
    return res


class _Sharded:
    def __init__(self, arr, kind, layer=None):
        self.arr, self.kind, self.layer = arr, kind, layer
        r, c = arr.shape[-2:]
        self.rows = r * (N_CHIPS if kind == "row" else 1)
        self.cols = c * (N_CHIPS if kind == "col" else 1)
        self.sr, self.sc = r, c

    def spec(self, br, bc, f):
        lead = (None,) if self.layer is None else (None, None)
        layer = self.layer
        if self.kind == "col":
            per = self.sc // bc
            assert per * bc == self.sc and self.sr % br == 0, (self.arr.shape, br, bc)

            def idx(*g):
                rb, cb = f(*g)
                return ((cb // per,) + (() if layer is None else (layer,)) + (rb, cb % per))
        else:
            per = self.sr // br
            assert per * br == self.sr and self.sc % bc == 0, (self.arr.shape, br, bc)

            def idx(*g):
                rb, cb = f(*g)
                return ((rb // per,) + (() if layer is None else (layer,)) + (rb % per, cb))
        return pl.BlockSpec(lead + (br, bc), idx)


def _grad_buffer(rows, cols, kind):
    if kind == "col":
        return jax.ShapeDtypeStruct((N_CHIPS, rows, cols // N_CHIPS), F32)
    return jax.ShapeDtypeStruct((N_CHIPS, rows // N_CHIPS, cols), F32)


def _mm_nn(name, a, w, *, tm, tn, tk, outs=None, extras=(), epilogue=None, out_dtype=F32, with_col=False):
    m, kdim = a.shape
    n = w.cols
    grid = (m // tm, n // tn, kdim // tk)
    if outs is None:
        outs = [(jax.ShapeDtypeStruct((m, n), out_dtype), pl.BlockSpec((tm, tn), lambda i, j, k: (i, j)))]
    return _mm(name, a, w.arr, dims=NN, grid=grid,
               a_spec=pl.BlockSpec((tm, tk), lambda i, j, k: (i, k)),
               b_spec=w.spec(tk, tn, lambda i, j, k: (k, j)),
               acc_shape=(tm, tn), outs=outs, extras=extras, epilogue=epilogue, with_col=with_col)


def _mm_nt(name, a, w, *, tm, tn, tk, extras=(), epilogue=None, out_dtype=F32):
    m, kdim = a.shape
    n = w.rows
    grid = (m // tm, n // tn, kdim // tk)
    outs = [(jax.ShapeDtypeStruct((m, n), out_dtype), pl.BlockSpec((tm, tn), lambda i, j, k: (i, j)))]
    return _mm(name, a, w.arr, dims=NT, grid=grid,
               a_spec=pl.BlockSpec((tm, tk), lambda i, j, k: (i, k)),
               b_spec=w.spec(tn, tk, lambda i, j, k: (j, k)),
               acc_shape=(tm, tn), outs=outs, extras=extras, epilogue=epilogue)[0]


def _mm_tn(name, a, b, kind, *, tm, tn, tk):
    kdim, m = a.shape
    n = b.shape[1]
    grid = (m // tm, n // tn, kdim // tk)
    out = _Sharded(_grad_buffer(m, n, kind), kind)
    outs = [(out.arr, out.spec(tm, tn, lambda i, j, k: (i, j)))]
    return _mm(name, a, b, dims=TN, grid=grid,
               a_spec=pl.BlockSpec((tk, tm), lambda i, j, k: (k, i)),
               b_spec=pl.BlockSpec((tk, tn), lambda i, j, k: (k, j)),
               acc_shape=(tm, tn), outs=outs)[0]


def _row_spec(tr, w, col=0):
    return pl.BlockSpec((tr, w), lambda i: (i, col))


def _vec_spec(w, col=0, rows=1):
    return pl.BlockSpec((rows, w), lambda i: (0, col))


def _norm_fwd(x, g, scale, shift, tr):
    s, d = x.shape

    def body(x_ref, g_ref, sc_ref, sh_ref, h_ref):
        xv = x_ref[...]
        r = lax.rsqrt(jnp.mean(xv * xv, axis=-1, keepdims=True) + RMS_EPS)
        y = xv * r * g_ref[...]
        h_ref[...] = (y * (1.0 + sc_ref[...]) + sh_ref[...]).astype(BF16)

    return pl.pallas_call(
        body, name="norm_fwd", grid=(s // tr,),
        in_specs=[_row_spec(tr, d), _vec_spec(d), _vec_spec(d), _vec_spec(d)],
        out_specs=_row_spec(tr, d), out_shape=jax.ShapeDtypeStruct((s, d), BF16),
        compiler_params=_params(("parallel",)),
    )(x, g, scale, shift)


def _norm_bwd(dh, x, g, scale, dxo, tr):
    s, d = x.shape

    def body(dh_ref, x_ref, g_ref, sc_ref, dxo_ref, dx_ref, dsh_ref, dsc_ref, dg_ref):
        @pl.when(pl.program_id(0) == 0)
        def _():
            dsh_ref[...] = jnp.zeros_like(dsh_ref)
            dsc_ref[...] = jnp.zeros_like(dsc_ref)
            dg_ref[...] = jnp.zeros_like(dg_ref)

        xv, dhv, gv = x_ref[...], dh_ref[...], g_ref[...]
        r = lax.rsqrt(jnp.mean(xv * xv, axis=-1, keepdims=True) + RMS_EPS)
        xn = xv * r
        dsh_ref[...] += jnp.sum(dhv, axis=0, keepdims=True)
        dsc_ref[...] += jnp.sum(dhv * (xn * gv), axis=0, keepdims=True)
        dyg = dhv * (1.0 + sc_ref[...])
        dg_ref[...] += jnp.sum(dyg * xn, axis=0, keepdims=True)
        dxn = dyg * gv
        dx_ref[...] = dxo_ref[...] + r * (dxn - xn * jnp.mean(dxn * xn, axis=-1, keepdims=True))

    vec = jax.ShapeDtypeStruct((1, d), F32)
    return pl.pallas_call(
        body, name="norm_bwd", grid=(s // tr,),
        in_specs=[_row_spec(tr, d), _row_spec(tr, d), _vec_spec(d), _vec_spec(d), _row_spec(tr, d)],
        out_specs=[_row_spec(tr, d), _vec_spec(d), _vec_spec(d), _vec_spec(d)],
        out_shape=[jax.ShapeDtypeStruct((s, d), F32), vec, vec, vec],
        compiler_params=_params(("arbitrary",)),
    )(dh, x, g, scale, dxo)


def _final_loss(x, g, target, tr):
    s, d = x.shape

    def body(x_ref, g_ref, t_ref, loss_ref, dx_ref, dg_ref):
        @pl.when(pl.program_id(0) == 0)
        def _():
            loss_ref[...] = jnp.zeros_like(loss_ref)
            dg_ref[...] = jnp.zeros_like(dg_ref)

        xv, gv = x_ref[...], g_ref[...]
        r = lax.rsqrt(jnp.mean(xv * xv, axis=-1, keepdims=True) + RMS_EPS)
        xn = xv * r
        err = xn * gv - t_ref[...]
        per_row = jnp.mean(err * err, axis=-1, keepdims=True)
        loss_ref[...] += 0.5 * jnp.sum(per_row, axis=0, keepdims=True)
        dy = err * (1.0 / d)
        dg_ref[...] += jnp.sum(dy * xn, axis=0, keepdims=True)
        dxn = dy * gv
        dx_ref[...] = r * (dxn - xn * jnp.mean(dxn * xn, axis=-1, keepdims=True))

    return pl.pallas_call(
        body, name="final_loss", grid=(s // tr,),
        in_specs=[_row_spec(tr, d), _vec_spec(d), _row_spec(tr, d)],
        out_specs=[pl.BlockSpec((8, 128), lambda i: (0, 0)), _row_spec(tr, d), _vec_spec(d)],
        out_shape=[jax.ShapeDtypeStruct((8, 128), F32), jax.ShapeDtypeStruct((s, d), F32),
                   jax.ShapeDtypeStruct((1, d), F32)],
        compiler_params=_params(("arbitrary",)),
    )(x, g, target)


def _halo_before(tr, halo, w, col):
    per = tr // halo
    return pl.BlockSpec((halo, w), lambda i: (jnp.maximum(i * per - 1, 0), col))


def _halo_after(tr, halo, w, col, n_tiles):
    per = tr // halo
    return pl.BlockSpec((halo, w), lambda i: (jnp.minimum((i + 1) * per, n_tiles * per - 1), col))


def _pool_fwd(proj, pool_w, pool_scale, d, tr):
    s = proj.shape[0]
    pw, gd, hl = d // 2, d // 8, POOL_HALO

    def body(xa_ref, xh_ref, za_ref, w_ref, ps_ref, ya_ref, buf):
        i = pl.program_id(0)
        buf[0:hl, :] = jnp.where(i > 0, xh_ref[...], 0.0)
        buf[hl:hl + tr, :] = xa_ref[...]
        row = i * tr + lax.broadcasted_iota(jnp.int32, (tr, 1), 0)
        za = za_ref[...]
        gate = za * _sigmoid(za)
        for g, win in enumerate(POOL_WINDOWS):
            cs = slice(g * gd, (g + 1) * gd)
            xg = buf[hl:hl + tr, cs]
            acc = xg
            for j in range(1, win):
                acc = acc + buf[hl - j:hl - j + tr, cs]
            cnt = jnp.minimum(row + 1, win).astype(F32)
            mixed = acc / cnt - xg
            y = jnp.dot(mixed.astype(BF16), w_ref[g], preferred_element_type=F32)
            ya_ref[:, cs] = ((y * ps_ref[:, cs]) * gate[:, cs]).astype(BF16)

    return pl.pallas_call(
        body, name="pool_fwd", grid=(s // tr,),
        in_specs=[_row_spec(tr, pw, 0), _halo_before(tr, hl, pw, 0), _row_spec(tr, pw, 1),
                  pl.BlockSpec((N_GROUPS, gd, gd), lambda i: (0, 0, 0)), _vec_spec(pw)],
        out_specs=_row_spec(tr, pw), out_shape=jax.ShapeDtypeStruct((s, pw), BF16),
        scratch_shapes=[pltpu.VMEM((tr + hl, pw), F32)],
        compiler_params=_params(("parallel",)),
    )(proj, proj, proj, pool_w, pool_scale)


def _pool_bwd(proj, dya, pool_w, pool_scale, d, tr):
    s = proj.shape[0]
    pw, gd, hl = d // 2, d // 8, POOL_HALO
    n_tiles = s // tr

    def body(xa_ref, xh_ref, za_ref, zh_ref, dya_ref, dyh_ref, w_ref, ps_ref, da_ref, gw_ref, gs_ref, buf, dbuf):
        i = pl.program_id(0)

        @pl.when(i == 0)
        def _():
            gw_ref[...] = jnp.zeros_like(gw_ref)
            gs_ref[...] = jnp.zeros_like(gs_ref)

        buf[0:hl, :] = jnp.where(i > 0, xh_ref[...], 0.0)
        buf[hl:hl + tr, :] = xa_ref[...]
        row = i * tr + lax.broadcasted_iota(jnp.int32, (tr, 1), 0)
        row_h = (i + 1) * tr + lax.broadcasted_iota(jnp.int32, (hl, 1), 0)
        za, dya_t = za_ref[...], dya_ref[...]
        sg = _sigmoid(za)
        gate = za * sg
        dpre = dya_t * gate
        zh = zh_ref[...]
        dpre_h = jnp.where(i < n_tiles - 1, dyh_ref[...], 0.0) * (zh * _sigmoid(zh))
        ps = ps_ref[...]
        for g, win in enumerate(POOL_WINDOWS):
            cs = slice(g * gd, (g + 1) * gd)
            wg = w_ref[g]
            xg = buf[hl:hl + tr, cs]
            acc = xg
            for j in range(1, win):
                acc = acc + buf[hl - j:hl - j + tr, cs]
            cnt = jnp.minimum(row + 1, win).astype(F32)
            mixed = (acc / cnt - xg).astype(BF16)
            ylin = jnp.dot(mixed, wg, preferred_element_type=F32)
            gs_ref[:, cs] += jnp.sum(dpre[:, cs] * ylin, axis=0, keepdims=True)
            da_ref[:, pw + g * gd:pw + (g + 1) * gd] = (
                dya_t[:, cs] * (ylin * ps[:, cs]) * _dsilu(za[:, cs], sg[:, cs])).astype(BF16)
            dyl = (dpre[:, cs] * ps[:, cs]).astype(BF16)
            gw_ref[g] += lax.dot_general(mixed, dyl, (TN, ((), ())), preferred_element_type=F32)
            dmix = lax.dot_general(dyl, wg, (NT, ((), ())), preferred_element_type=F32)
            dyl_h = (dpre_h[:, cs] * ps[:, cs]).astype(BF16)
            dmix_h = lax.dot_general(dyl_h, wg, (NT, ((), ())), preferred_element_type=F32)
            cnt_h = jnp.minimum(row_h + 1, win).astype(F32)
            dbuf[0:tr, cs] = dmix / cnt
            dbuf[tr:tr + hl, cs] = dmix_h / cnt_h
            dx = dbuf[0:tr, cs] - dmix
            for j in range(1, win):
                dx = dx + dbuf[j:j + tr, cs]
            da_ref[:, cs] = dx.astype(BF16)

    return pl.pallas_call(
        body, name="pool_bwd", grid=(n_tiles,),
        in_specs=[_row_spec(tr, pw, 0), _halo_before(tr, hl, pw, 0),
                  _row_spec(tr, pw, 1), _halo_after(tr, hl, pw, 1, n_tiles),
                  _row_spec(tr, pw, 0), _halo_after(tr, hl, pw, 0, n_tiles),
                  pl.BlockSpec((N_GROUPS, gd, gd), lambda i: (0, 0, 0)), _vec_spec(pw)],
        out_specs=[_row_spec(tr, 2 * pw), pl.BlockSpec((N_GROUPS, gd, gd), lambda i: (0, 0, 0)), _vec_spec(pw)],
        out_shape=[jax.ShapeDtypeStruct((s, 2 * pw), BF16), jax.ShapeDtypeStruct((N_GROUPS, gd, gd), F32),
                   jax.ShapeDtypeStruct((1, pw), F32)],
        scratch_shapes=[pltpu.VMEM((tr + hl, pw), F32), pltpu.VMEM((tr + hl, pw), F32)],
        compiler_params=_params(("arbitrary",)),
    )(proj, proj, proj, proj, dya, dya, pool_w, pool_scale)


def _conv_fwd(proj, conv_w, d, tr):
    s = proj.shape[0]
    cc, hl = d // 2, CONV_HALO
    cu, cb, cg_, cz = 10, 11, 12, 13

    def body(u_ref, uh_ref, bg_ref, cg_ref, ch_ref, zc_ref, w_ref, yc_ref, buf):
        i = pl.program_id(0)
        buf[0:hl, :] = jnp.where(i > 0, ch_ref[...] * uh_ref[...], 0.0)
        buf[hl:hl + tr, :] = cg_ref[...] * u_ref[...]
        y = w_ref[0:1, :] * buf[hl - 2:hl - 2 + tr, :]
        y = y + w_ref[1:2, :] * buf[hl - 1:hl - 1 + tr, :]
        y = y + w_ref[2:3, :] * buf[hl:hl + tr, :]
        zc = zc_ref[...]
        yc_ref[...] = ((bg_ref[...] * y) * (zc * _sigmoid(zc))).astype(BF16)

    return pl.pallas_call(
        body, name="conv_fwd", grid=(s // tr,),
        in_specs=[_row_spec(tr, cc, cu), _halo_before(tr, hl, cc, cu), _row_spec(tr, cc, cb),
                  _row_spec(tr, cc, cg_), _halo_before(tr, hl, cc, cg_), _row_spec(tr, cc, cz),
                  _vec_spec(cc, rows=3)],
        out_specs=_row_spec(tr, cc), out_shape=jax.ShapeDtypeStruct((s, cc), BF16),
        scratch_shapes=[pltpu.VMEM((tr + hl, cc), F32)],
        compiler_params=_params(("parallel",)),
    )(proj, proj, proj, proj, proj, proj, conv_w)


def _conv_bwd(proj, dyc, conv_w, d, tr):
    s = proj.shape[0]
    cc, hl = d // 2, CONV_HALO
    cu, cb, cg_, cz = 10, 11, 12, 13
    n_tiles = s // tr

    def body(u_ref, uh_ref, bg_ref, bh_ref, cg_ref, ch_ref, zc_ref, zh_ref, dy_ref, dyh_ref, w_ref,
             dc_ref, gw_ref, buf, dbuf):
        i = pl.program_id(0)

        @pl.when(i == 0)
        def _():
            gw_ref[...] = jnp.zeros_like(gw_ref)

        u, bg, cg, zc, dyc_t = u_ref[...], bg_ref[...], cg_ref[...], zc_ref[...], dy_ref[...]
        buf[0:hl, :] = jnp.where(i > 0, ch_ref[...] * uh_ref[...], 0.0)
        buf[hl:hl + tr, :] = cg * u
        v2, v1, v0 = buf[hl - 2:hl - 2 + tr, :], buf[hl - 1:hl - 1 + tr, :], buf[hl:hl + tr, :]
        w0, w1, w2 = w_ref[0:1, :], w_ref[1:2, :], w_ref[2:3, :]
        y = w0 * v2 + w1 * v1 + w2 * v0
        sg = _sigmoid(zc)
        gate = zc * sg
        dc_ref[:, cc:2 * cc] = (dyc_t * y * gate).astype(BF16)
        dc_ref[:, 3 * cc:4 * cc] = (dyc_t * bg * y * _dsilu(zc, sg)).astype(BF16)
        dy = dyc_t * bg * gate
        zh = zh_ref[...]
        dy_h = jnp.where(i < n_tiles - 1, dyh_ref[...], 0.0) * bh_ref[...] * (zh * _sigmoid(zh))
        gw_ref[0:1, :] += jnp.sum(dy * v2, axis=0, keepdims=True)
        gw_ref[1:2, :] += jnp.sum(dy * v1, axis=0, keepdims=True)
        gw_ref[2:3, :] += jnp.sum(dy * v0, axis=0, keepdims=True)
        dbuf[0:tr, :] = dy
        dbuf[tr:tr + hl, :] = dy_h
        dv = w2 * dy + w1 * dbuf[1:1 + tr, :] + w0 * dbuf[2:2 + tr, :]
        dc_ref[:, 0:cc] = (dv * cg).astype(BF16)
        dc_ref[:, 2 * cc:3 * cc] = (dv * u).astype(BF16)

    return pl.pallas_call(
        body, name="conv_bwd", grid=(n_tiles,),
        in_specs=[_row_spec(tr, cc, cu), _halo_before(tr, hl, cc, cu),
                  _row_spec(tr, cc, cb), _halo_after(tr, hl, cc, cb, n_tiles),
                  _row_spec(tr, cc, cg_), _halo_before(tr, hl, cc, cg_),
                  _row_spec(tr, cc, cz), _halo_after(tr, hl, cc, cz, n_tiles),
                  _row_spec(tr, cc, 0), _halo_after(tr, hl, cc, 0, n_tiles),
                  _vec_spec(cc, rows=3)],
        out_specs=[_row_spec(tr, 4 * cc), _vec_spec(cc, rows=3)],
        out_shape=[jax.ShapeDtypeStruct((s, 4 * cc), BF16), jax.ShapeDtypeStruct((3, cc), F32)],
        scratch_shapes=[pltpu.VMEM((tr + hl, cc), F32), pltpu.VMEM((tr + hl, cc), F32)],
        compiler_params=_params(("arbitrary",)),
    )(proj, proj, proj, proj, proj, proj, proj, proj, dyc, dyc, conv_w)


def _merge_fwd(gates, pa, pb, pc, tr):
    s, d = pa.shape

    def body(g_ref, a_ref, b_ref, c_ref, m_ref):
        m = g_ref[:, 0:d] * a_ref[...] + g_ref[:, d:2 * d] * b_ref[...] + g_ref[:, 2 * d:3 * d] * c_ref[...]
        m_ref[...] = m.astype(BF16)

    return pl.pallas_call(
        body, name="merge_fwd", grid=(s // tr,),
        in_specs=[_row_spec(tr, 3 * d), _row_spec(tr, d), _row_spec(tr, d), _row_spec(tr, d)],
        out_specs=_row_spec(tr, d), out_shape=jax.ShapeDtypeStruct((s, d), BF16),
        compiler_params=_params(("parallel",)),
    )(gates, pa, pb, pc)


def _merge_bwd(dm, gates, pa, pb, pc, tr):
    s, d = pa.shape

    def body(dm_ref, g_ref, a_ref, b_ref, c_ref, da_ref, db_ref, dc_ref, dg_ref, gb_ref):
        @pl.when(pl.program_id(0) == 0)
        def _():
            gb_ref[...] = jnp.zeros_like(gb_ref)

        dmv = dm_ref[...]
        for n, (p_ref, o_ref) in enumerate(((a_ref, da_ref), (b_ref, db_ref), (c_ref, dc_ref))):
            gv = g_ref[:, n * d:(n + 1) * d]
            o_ref[...] = (dmv * gv).astype(BF16)
            dlogit = (dmv * p_ref[...]) * (gv * (1.0 - gv))
            dg_ref[:, n * d:(n + 1) * d] = dlogit.astype(BF16)
            gb_ref[:, n * d:(n + 1) * d] += jnp.sum(dlogit, axis=0, keepdims=True)

    act = jax.ShapeDtypeStruct((s, d), BF16)
    return pl.pallas_call(
        body, name="merge_bwd", grid=(s // tr,),
        in_specs=[_row_spec(tr, d), _row_spec(tr, 3 * d), _row_spec(tr, d), _row_spec(tr, d), _row_spec(tr, d)],
        out_specs=[_row_spec(tr, d), _row_spec(tr, d), _row_spec(tr, d), _row_spec(tr, 3 * d), _vec_spec(3 * d)],
        out_shape=[act, act, act, jax.ShapeDtypeStruct((s, 3 * d), BF16), jax.ShapeDtypeStruct((1, 3 * d), F32)],
        compiler_params=_params(("arbitrary",)),
    )(dm, gates, pa, pb, pc)


def _resid_bwd(dxo, out, rg, tr):
    s, d = dxo.shape

    def body(dx_ref, o_ref, rg_ref, do_ref, drg_ref):
        @pl.when(pl.program_id(0) == 0)
        def _():
            drg_ref[...] = jnp.zeros_like(drg_ref)

        dxv = dx_ref[...]
        do_ref[...] = (dxv * rg_ref[...]).astype(BF16)
        drg_ref[...] += jnp.sum(dxv * o_ref[...], axis=0, keepdims=True)

    return pl.pallas_call(
        body, name="resid_bwd", grid=(s // tr,),
        in_specs=[_row_spec(tr, d), _row_spec(tr, d), _vec_spec(d)],
        out_specs=[_row_spec(tr, d), _vec_spec(d)],
        out_shape=[jax.ShapeDtypeStruct((s, d), BF16), jax.ShapeDtypeStruct((1, d), F32)],
        compiler_params=_params(("arbitrary",)),
    )(dxo, out, rg)


def _split_bf16(v):
    hi = v.astype(BF16)
    return hi, (v - hi.astype(F32)).astype(BF16)


def _log_keep(z):
    e = jnp.exp(-jnp.abs(z))
    return -(jnp.maximum(z, 0.0) + jnp.log(1.0 + e)), e


class _Job:
    def __init__(self, ins, out_shapes, aliases, n_sems, start, finish, mid=None):
        self.ins, self.out_shapes, self.aliases, self.n_sems = list(ins), list(out_shapes), list(aliases), n_sems
        self.start, self.mid, self.finish = start, mid, finish


def _host(job, n_in, n_out):
    if job is None:
        return dict(ins=[], in_specs=[], out_specs=[], out_shape=[], scratch=[], aliases={})
    return dict(ins=job.ins, in_specs=[_ANY] * len(job.ins), out_specs=[_ANY] * len(job.out_shapes),
                out_shape=job.out_shapes,
                scratch=[pltpu.SemaphoreType.DMA((job.n_sems,)), pltpu.SemaphoreType.DMA((job.n_sems,))],
                aliases={n_in + a: n_out + b for a, b in job.aliases})


def _job_refs(job, refs, n_in, n_out):
    if job is None:
        return refs[:n_in], refs[n_in:n_in + n_out], None
    ji, jo = len(job.ins), len(job.out_shapes)
    own_in, job_in = refs[:n_in], refs[n_in:n_in + ji]
    own_out, job_out = refs[n_in + ji:n_in + ji + n_out], refs[n_in + ji + n_out:n_in + ji + n_out + jo]
    send_sems, recv_sems = refs[n_in + ji + n_out + jo:]
    return own_in, own_out, (job_in, job_out, send_sems, recv_sems)


def _sb_fwd(proj, projb, d, blk, job=None):
    s = proj.shape[0]
    hps, wid = SB_FWD_HEADS, SB_FWD_HEADS * HEAD_DIM
    groups, nq = d // wid, s // blk
    qc, kc, vc, zc = d // wid, 2 * d // wid, 3 * d // wid, 4 * d // wid
    assert nq <= HEAD_DIM
    mid_step = (groups - 1, nq // 3)

    def body(*refs):
        (q_ref, k_ref, v_ref, zb_ref), (att_ref, yb_ref, rs_ref), jargs = _job_refs(job, refs, 4, 3)
        grp, i = pl.program_id(0), pl.program_id(1)
        if job is not None:
            @pl.when((grp == 0) & (i == 0))
            def _():
                job.start(*jargs)

            if job.mid is not None:
                @pl.when((grp == mid_step[0]) & (i == mid_step[1]))
                def _():
                    job.mid(*jargs)

        r_io = lax.broadcasted_iota(jnp.int32, (blk, blk), 0)
        c_io = lax.broadcasted_iota(jnp.int32, (blk, blk), 1)
        tri = (r_io >= c_io).astype(BF16)
        strict = c_io < r_io
        lane = lax.broadcasted_iota(jnp.int32, (blk, HEAD_DIM), 1)
        heads = [slice(n * HEAD_DIM, (n + 1) * HEAD_DIM) for n in range(hps)]
        qs = [q_ref[:, hd] for hd in heads]

        def block(j, carry, masked):
            ks = pl.multiple_of(j * blk, blk)
            rng = range(hps)
            zs = [lax.dot_general(qs[n], k_ref[pl.ds(ks, blk), heads[n]], (NT, ((), ())),
                                  preferred_element_type=F32) for n in rng]
            splits = []
            for n in rng:
                lk, _ = _log_keep(zs[n])
                if masked:
                    lk = jnp.where(strict, lk, 0.0)
                splits.append(_split_bf16(lk))
            csums = [carry[n][0] + jnp.dot(splits[n][0], tri, preferred_element_type=F32)
                     + jnp.dot(splits[n][1], tri, preferred_element_type=F32) for n in rng]
            probs = []
            for n in rng:
                a = jnp.exp(zs[n] + csums[n])
                if masked:
                    a = jnp.where(strict, a, 0.0)
                probs.append(a.astype(BF16))
            out = []
            for n in rng:
                run, acc, rs = carry[n]
                acc = acc + jnp.dot(probs[n], v_ref[pl.ds(ks, blk), heads[n]], preferred_element_type=F32)
                out.append((csums[n][:, 0:1], acc, jnp.where(lane == j, run, rs)))
            return tuple(out)

        zero = (jnp.zeros((blk, 1), F32), jnp.zeros((blk, HEAD_DIM), F32), jnp.zeros((blk, HEAD_DIM), F32))
        carry = block(i, (zero,) * hps, True)
        carry = lax.fori_loop(0, i, lambda jj, cr: block(i - 1 - jj, cr, False), carry)
        for hd, (_, acc, rs) in zip(heads, carry):
            att_ref[:, hd] = acc
            rs_ref[:, hd] = rs
            zb = zb_ref[:, hd]
            yb_ref[:, hd] = (acc * (zb * _sigmoid(zb))).astype(BF16)

        if job is not None:
            @pl.when((grp == groups - 1) & (i == nq - 1))
            def _():
                job.finish(*jargs)

    blk_spec = pl.BlockSpec((blk, wid), lambda h, i: (i, h))
    hosted = _host(job, 4, 3)
    res = pl.pallas_call(
        body, name="sb_fwd", grid=(groups, nq),
        in_specs=[pl.BlockSpec((blk, wid), lambda h, i: (i, qc + h)),
                  pl.BlockSpec((s, wid), lambda h, i: (0, kc + h)),
                  pl.BlockSpec((s, wid), lambda h, i: (0, vc + h)),
                  pl.BlockSpec((blk, wid), lambda h, i: (i, zc + h))] + hosted["in_specs"],
        out_specs=[blk_spec, blk_spec, blk_spec] + hosted["out_specs"],
        out_shape=[jax.ShapeDtypeStruct((s, d), F32), jax.ShapeDtypeStruct((s, d), BF16),
                   jax.ShapeDtypeStruct((s, d), F32)] + hosted["out_shape"],
        scratch_shapes=hosted["scratch"], input_output_aliases=hosted["aliases"],
        compiler_params=_params(("arbitrary", "arbitrary")),
    )(projb, projb, projb, proj, *hosted["ins"])
    return res[0], res[1], res[2], list(res[3:])


def _sb_bwd(proj, projb, att, dyb, rsave, d, blk, job=None):
    s = proj.shape[0]
    hps, wid = SB_BWD_HEADS, SB_BWD_HEADS * HEAD_DIM
    groups, nq = d // wid, s // blk
    qc, kc, vc, zc = d // wid, 2 * d // wid, 3 * d // wid, 4 * d // wid
    scale = HEAD_DIM ** -0.5

    def body(*refs):
        ((q_ref, k_ref, v_ref, zb_ref, att_ref, dyb_ref, rs_ref), (dq_ref, dk_ref, dv_ref, dzb_ref),
         jargs) = _job_refs(job, refs, 7, 4)
        grp, i = pl.program_id(0), pl.program_id(1)
        if job is not None:
            @pl.when((grp == 0) & (i == 0))
            def _():
                job.start(*jargs)

        @pl.when(i == 0)
        def _():
            dk_ref[...] = jnp.zeros_like(dk_ref)
            dv_ref[...] = jnp.zeros_like(dv_ref)

        r_io = lax.broadcasted_iota(jnp.int32, (blk, blk), 0)
        c_io = lax.broadcasted_iota(jnp.int32, (blk, blk), 1)
        tri = (r_io >= c_io).astype(BF16)
        tri_up = (r_io <= c_io).astype(BF16)
        strict = c_io < r_io
        lane = lax.broadcasted_iota(jnp.int32, (blk, HEAD_DIM), 1)
        heads = [slice(n * HEAD_DIM, (n + 1) * HEAD_DIM) for n in range(hps)]
        qs, dos, rss = [], [], []
        for hd in heads:
            qs.append(q_ref[:, hd])
            zb, dyb_t = zb_ref[:, hd], dyb_ref[:, hd]
            sg = _sigmoid(zb)
            dzb_ref[:, hd] = dyb_t * att_ref[:, hd] * _dsilu(zb, sg)
            dos.append((dyb_t * (zb * sg)).astype(BF16))
            rss.append(rs_ref[:, hd])

        def block(j, carry, masked):
            ks = pl.multiple_of(j * blk, blk)
            rng = range(hps)
            kbs = [k_ref[pl.ds(ks, blk), heads[n]] for n in rng]
            vbs = [v_ref[pl.ds(ks, blk), heads[n]] for n in rng]
            zs = [lax.dot_general(qs[n], kbs[n], (NT, ((), ())), preferred_element_type=F32) for n in rng]
            das = [lax.dot_general(dos[n], vbs[n], (NT, ((), ())), preferred_element_type=F32) for n in rng]
            splits, betas = [], []
            for n in rng:
                lk, e = _log_keep(zs[n])
                betas.append(jnp.where(zs[n] >= 0, 1.0, e) / (1.0 + e))
                if masked:
                    lk = jnp.where(strict, lk, 0.0)
                splits.append(_split_bf16(lk))
            csums = []
            for n in rng:
                run = jnp.sum(jnp.where(lane == j, rss[n], 0.0), axis=1, keepdims=True)
                csums.append(run + jnp.dot(splits[n][0], tri, preferred_element_type=F32)
                             + jnp.dot(splits[n][1], tri, preferred_element_type=F32))
            probs, gs = [], []
            for n in rng:
                a = jnp.exp(zs[n] + csums[n])
                if masked:
                    a = jnp.where(strict, a, 0.0)
                probs.append(a.astype(BF16))
                gs.append(a * das[n])
            gcums = [carry[n][0] + jnp.dot(gs[n].astype(BF16), tri_up, preferred_element_type=F32) for n in rng]
            for n in rng:
                dv_ref[pl.ds(ks, blk), heads[n]] += lax.dot_general(probs[n], dos[n], (TN, ((), ())),
                                                                   preferred_element_type=F32)
            dzs = []
            for n in rng:
                dz = gs[n] - betas[n] * gcums[n]
                if masked:
                    dz = jnp.where(strict, dz, 0.0)
                dzs.append(dz.astype(BF16))
            out = []
            for n in rng:
                dq = carry[n][1] + jnp.dot(dzs[n], kbs[n], preferred_element_type=F32)
                dk_ref[pl.ds(ks, blk), heads[n]] += lax.dot_general(dzs[n], qs[n], (TN, ((), ())),
                                                                   preferred_element_type=F32)
                out.append((gcums[n][:, blk - 1:blk], dq))
            return tuple(out)

        zero = (jnp.zeros((blk, 1), F32), jnp.zeros((blk, HEAD_DIM), F32))
        carry = lax.fori_loop(0, i, lambda j, cr: block(j, cr, False), (zero,) * hps)
        carry = block(i, carry, True)
        for hd, (_, dq) in zip(heads, carry):
            dq_ref[:, hd] = dq * scale

        if job is not None:
            @pl.when((grp == groups - 1) & (i == nq - 1))
            def _():
                job.finish(*jargs)

    blk_spec = pl.BlockSpec((blk, wid), lambda h, i: (i, h))
    full_spec = pl.BlockSpec((s, wid), lambda h, i: (0, h), pipeline_mode=pl.Buffered(1))
    act = jax.ShapeDtypeStruct((s, d), F32)
    hosted = _host(job, 7, 4)
    res = pl.pallas_call(
        body, name="sb_bwd", grid=(groups, nq),
        in_specs=[pl.BlockSpec((blk, wid), lambda h, i: (i, qc + h)),
                  pl.BlockSpec((s, wid), lambda h, i: (0, kc + h)),
                  pl.BlockSpec((s, wid), lambda h, i: (0, vc + h)),
                  pl.BlockSpec((blk, wid), lambda h, i: (i, zc + h)),
                  blk_spec, blk_spec, blk_spec] + hosted["in_specs"],
        out_specs=[blk_spec, full_spec, full_spec, blk_spec] + hosted["out_specs"],
        out_shape=[act, act, act, act] + hosted["out_shape"],
        scratch_shapes=hosted["scratch"], input_output_aliases=hosted["aliases"],
        compiler_params=_params(("arbitrary", "arbitrary")),
    )(projb, projb, projb, proj, att, dyb, rsave, *hosted["ins"])
    return res[0], res[1], res[2], res[3], list(res[4:])


def _adamw(w, g, m, v):
    shape = w.shape
    cols = shape[-1]
    rows = w.size // cols
    w2, g2, m2, v2 = (t.reshape(rows, cols) for t in (w, g, m, v))
    tr, tc = _tile(rows, 512, 8), _tile(cols, 1024, 128)
    c1 = 1.0 - ADAM_B1 ** ADAM_STEP
    c2 = 1.0 - ADAM_B2 ** ADAM_STEP

    def body(w_ref, g_ref, m_ref, v_ref, d_ref, nm_ref, nv_ref):
        gv = g_ref[...]
        nm = ADAM_B1 * m_ref[...] + (1.0 - ADAM_B1) * gv
        nv = ADAM_B2 * v_ref[...] + (1.0 - ADAM_B2) * (gv * gv)
        d_ref[...] = -ADAM_LR * ((nm / c1) / (jnp.sqrt(nv / c2) + ADAM_EPS) + ADAM_WD * w_ref[...])
        nm_ref[...] = nm
        nv_ref[...] = nv

    spec = pl.BlockSpec((tr, tc), lambda i, j: (i, j))
    sd = jax.ShapeDtypeStruct((rows, cols), F32)
    outs = pl.pallas_call(
        body, name="adamw", grid=(rows // tr, cols // tc),
        in_specs=[spec] * 4, out_specs=[spec] * 3, out_shape=[sd] * 3,
        compiler_params=_params(("parallel", "parallel")),
    )(w2, g2, m2, v2)
    return tuple(o.reshape(shape) for o in outs)


def _sum_blocks(gathered, n, rows):
    width = gathered.shape[1]
    tw = _tile(width, 8192)

    def body(g_ref, o_ref):
        acc = g_ref[0:rows, :]
        for b in range(1, n):
            acc = acc + g_ref[b * rows:(b + 1) * rows, :]
        o_ref[...] = acc

    return pl.pallas_call(
        body, name="sum_blocks", grid=(width // tw,),
        in_specs=[pl.BlockSpec((n * rows, tw), lambda i: (0, i))],
        out_specs=pl.BlockSpec((rows, tw), lambda i: (0, i)),
        out_shape=jax.ShapeDtypeStruct((rows, width), F32),
        compiler_params=_params(("parallel",)),
    )(gathered)


def _silu_bf16(c_rows):
    def body(c_ref, o_ref):
        cv = c_ref[...]
        o_ref[...] = (cv * _sigmoid(cv)).astype(BF16)

    return pl.pallas_call(
        body, name="silu_c", out_shape=jax.ShapeDtypeStruct(c_rows.shape, BF16),
        in_specs=[pl.BlockSpec(memory_space=pltpu.VMEM)], out_specs=pl.BlockSpec(memory_space=pltpu.VMEM),
    )(c_rows)


def _place():
    x, y, c = lax.axis_index("x"), lax.axis_index("y"), lax.axis_index("c")
    chips = [(1 - x, y), (x, 1 - y), (1 - x, 1 - y)]
    return x, y, c, chips


def _all_gather_small(block, name):
    m_per, n = block.shape

    def body(x_ref, out_ref, send_sems, recv_sems, local_sem):
        x, y, c, chips = _place()
        me, sibling = (x, y, c), (x, y, 1 - c)

        def rows(px, py, pc):
            return out_ref.at[pl.ds((4 * px + 2 * py + pc) * m_per, m_per), :]

        def copy(k, blk, to, src=None):
            return pltpu.make_async_remote_copy(
                src_ref=rows(*blk) if src is None else src, dst_ref=rows(*blk),
                send_sem=send_sems.at[k], recv_sem=recv_sems.at[k], device_id=to, device_id_type=MESH)

        mine = pltpu.make_async_copy(x_ref, rows(*me), local_sem)
        mine.start()
        first = [copy(0, me, sibling, src=x_ref)]
        first += [copy(1 + j, me, (*chip, c), src=x_ref) for j, chip in enumerate(chips)]
        for cp in first:
            cp.start()
        passed = [copy(4 + j, (*chip, c), sibling) for j, chip in enumerate(chips)]
        for j, chip in enumerate(chips):
            copy(1 + j, (*chip, c), me).wait_recv()
            passed[j].start()
        copy(0, sibling, me).wait_recv()
        for j, chip in enumerate(chips):
            copy(4 + j, (*chip, 1 - c), me).wait_recv()
        for cp in first + passed:
            cp.wait_send()
        mine.wait()

    return pl.pallas_call(
        body, name=name, out_shape=jax.ShapeDtypeStruct((N_DEV * m_per, n), block.dtype),
        in_specs=[pl.BlockSpec(memory_space=pltpu.VMEM)], out_specs=pl.BlockSpec(memory_space=pltpu.VMEM),
        scratch_shapes=[pltpu.SemaphoreType.DMA((7,)), pltpu.SemaphoreType.DMA((7,)), pltpu.SemaphoreType.DMA],
        compiler_params=pltpu.CompilerParams(vmem_limit_bytes=V7X_VMEM_LIMIT),
    )(block)


_ANY = pl.BlockSpec(memory_space=pl.ANY)


def _place_shard(w, chip_idx):
    layers, r, cols = w.shape
    tr, tc = _tile(r, 512, 16), _tile(cols, 1024)

    def body(j_ref, w_ref, o_ref):
        o_ref[...] = w_ref[...].astype(BF16)

    return pl.pallas_call(
        body, name="place_shard",
        grid_spec=pltpu.PrefetchScalarGridSpec(
            num_scalar_prefetch=1, grid=(layers, r // tr, cols // tc),
            in_specs=[pl.BlockSpec((None, tr, tc), lambda l, i, n, j: (l, i, n))],
            out_specs=pl.BlockSpec((None, None, tr, tc), lambda l, i, n, j: (j[0], l, i, n))),
        out_shape=jax.ShapeDtypeStruct((N_CHIPS, layers, r, cols), BF16),
        compiler_params=_params(("parallel", "parallel", "parallel")),
    )(chip_idx, w)


def _gather_job(bufs, layer):
    n = len(bufs)

    def tools(outs, send_sems, recv_sems):
        x, y, c, chips = _place()

        def half(t, chip_idx, hc):
            h = outs[t].shape[2] // 2
            return outs[t].at[chip_idx, layer, pl.ds(hc * h, h), :]

        def copy(t, k, ref, to):
            return pltpu.make_async_remote_copy(src_ref=ref, dst_ref=ref, send_sem=send_sems.at[6 * t + k],
                                                recv_sem=recv_sems.at[6 * t + k], device_id=to, device_id_type=MESH)

        return x, y, c, chips, half, copy

    def start(ins, outs, send_sems, recv_sems):
        x, y, c, chips, half, copy = tools(outs, send_sems, recv_sems)
        for t in range(n):
            for k, chip in enumerate(chips):
                copy(t, k, half(t, 2 * x + y, c), (*chip, c)).start()

    def mid(ins, outs, send_sems, recv_sems):
        x, y, c, chips, half, copy = tools(outs, send_sems, recv_sems)
        for t in range(n):
            for k, (cx, cy) in enumerate(chips):
                landed = half(t, 2 * cx + cy, c)
                copy(t, k, landed, (cx, cy, c)).wait_recv()
                copy(t, 3 + k, landed, (x, y, 1 - c)).start()

    def finish(ins, outs, send_sems, recv_sems):
        x, y, c, chips, half, copy = tools(outs, send_sems, recv_sems)
        for t in range(n):
            for k, (cx, cy) in enumerate(chips):
                copy(t, 3 + k, half(t, 2 * cx + cy, 1 - c), (x, y, 1 - c)).wait_recv()
        for t in range(n):
            for k, (cx, cy) in enumerate(chips):
                copy(t, k, half(t, 2 * x + y, c), (cx, cy, c)).wait_send()
                copy(t, 3 + k, half(t, 2 * cx + cy, c), (x, y, 1 - c)).wait_send()

    return _Job(ins=bufs, out_shapes=[jax.ShapeDtypeStruct(b.shape, b.dtype) for b in bufs],
                aliases=[(t, t) for t in range(n)], n_sems=6 * n, start=start, mid=mid, finish=finish)


def _send_job(parts):
    n = len(parts)

    def copies(ins, outs, send_sems, recv_sems):
        x, y, c, chips = _place()
        return [pltpu.make_async_remote_copy(
            src_ref=ins[t].at[2 * cx + cy], dst_ref=outs[t].at[k], send_sem=send_sems.at[3 * t + k],
            recv_sem=recv_sems.at[3 * t + k], device_id=(cx, cy, c), device_id_type=MESH)
            for t in range(n) for k, (cx, cy) in enumerate(chips)]

    def start(*args):
        for cp in copies(*args):
            cp.start()

    def finish(*args):
        for cp in copies(*args):
            cp.wait()

    return _Job(ins=parts, out_shapes=[jax.ShapeDtypeStruct((3,) + p.shape[1:], p.dtype) for p in parts],
                aliases=[], n_sems=3 * n, start=start, finish=finish)


def _run_job(job, name):
    def body(*refs):
        _, _, jargs = _job_refs(job, refs, 0, 0)
        job.start(*jargs)
        if job.mid is not None:
            job.mid(*jargs)
        job.finish(*jargs)

    hosted = _host(job, 0, 0)
    return list(pl.pallas_call(
        body, name=name, out_shape=hosted["out_shape"], in_specs=hosted["in_specs"], out_specs=hosted["out_specs"],
        scratch_shapes=hosted["scratch"], input_output_aliases=hosted["aliases"],
    )(*hosted["ins"]))


def _swap_halves(grads):
    n = len(grads)

    def body(*refs):
        ins, outs = refs[:n], refs[n:2 * n]
        send_sems, recv_sems = refs[2 * n:]
        x, y, c, _ = _place()
        copies = []
        for t in range(n):
            h = ins[t].shape[1] // 2
            copies.append(pltpu.make_async_remote_copy(
                src_ref=ins[t].at[:, pl.ds((1 - c) * h, h), :], dst_ref=outs[t],
                send_sem=send_sems.at[t], recv_sem=recv_sems.at[t], device_id=(x, y, 1 - c), device_id_type=MESH))
            copies[-1].start()
        for cp in copies:
            cp.wait()

    return pl.pallas_call(
        body, name="swap_halves",
        out_shape=[jax.ShapeDtypeStruct((g.shape[0], g.shape[1] // 2, g.shape[2]), g.dtype) for g in grads],
        in_specs=[_ANY] * n, out_specs=[_ANY] * n,
        scratch_shapes=[pltpu.SemaphoreType.DMA((n,)), pltpu.SemaphoreType.DMA((n,))],
    )(*grads)


def _share_halves(bufs):
    n = len(bufs)

    def body(*refs):
        outs = refs[n:2 * n]
        send_sems, recv_sems = refs[2 * n:]
        x, y, c, _ = _place()

        def half(t, hc):
            h = outs[t].shape[1] // 2
            return outs[t].at[:, pl.ds(hc * h, h), :]

        def copy(t, ref):
            return pltpu.make_async_remote_copy(src_ref=ref, dst_ref=ref, send_sem=send_sems.at[t],
                                                recv_sem=recv_sems.at[t], device_id=(x, y, 1 - c),
                                                device_id_type=MESH)

        sends = [copy(t, half(t, c)) for t in range(n)]
        for cp in sends:
            cp.start()
        for t in range(n):
            copy(t, half(t, 1 - c)).wait_recv()
        for cp in sends:
            cp.wait_send()

    return pl.pallas_call(
        body, name="share_halves",
        out_shape=[jax.ShapeDtypeStruct(b.shape, b.dtype) for b in bufs],
        in_specs=[_ANY] * n, out_specs=[_ANY] * n, input_output_aliases={t: t for t in range(n)},
        scratch_shapes=[pltpu.SemaphoreType.DMA((n,)), pltpu.SemaphoreType.DMA((n,))],
    )(*bufs)


def _sum_sibling(grad, recv, c_idx):
    _, r, cols = grad.shape
    h = r // 2
    tr, tc = _tile(h, 512, 16), _tile(cols, 1024)
    per = h // tr

    def body(c_ref, g_ref, r_ref, o_ref):
        o_ref[...] = (g_ref[...] + r_ref[...]).astype(BF16)

    return pl.pallas_call(
        body, name="sum_sibling",
        grid_spec=pltpu.PrefetchScalarGridSpec(
            num_scalar_prefetch=1, grid=(N_CHIPS, per, cols // tc),
            in_specs=[pl.BlockSpec((None, tr, tc), lambda j, i, n, c: (j, c[0] * per + i, n)),
                      pl.BlockSpec((None, tr, tc), lambda j, i, n, c: (j, i, n))],
            out_specs=pl.BlockSpec((None, tr, tc), lambda j, i, n, c: (j, i, n))),
        out_shape=jax.ShapeDtypeStruct((N_CHIPS, h, cols), BF16),
        compiler_params=_params(("parallel", "parallel", "parallel")),
    )(c_idx, grad, recv)


def _sum_owner(parts, recv, place_idx, layer, layers, prev=None):
    _, h, cols = parts.shape
    tr, tc = _tile(h, 512, 16), _tile(cols, 1024)
    per = h // tr

    def body(idx_ref, p_ref, r_ref, *rest):
        o_ref = rest[-1]
        o_ref[...] = (p_ref[...].astype(F32) + r_ref[0].astype(F32) + r_ref[1].astype(F32)
                      + r_ref[2].astype(F32))

    in_specs = [pl.BlockSpec((None, tr, tc), lambda i, n, idx: (idx[0], i, n)),
                pl.BlockSpec((3, tr, tc), lambda i, n, idx: (0, i, n))]
    args = [place_idx, parts, recv]
    aliases = {}
    if prev is not None:
        in_specs.append(_ANY)
        args.append(prev)
        aliases = {3: 0}
    return pl.pallas_call(
        body, name="sum_owner",
        grid_spec=pltpu.PrefetchScalarGridSpec(
            num_scalar_prefetch=1, grid=(per, cols // tc), in_specs=in_specs,
            out_specs=pl.BlockSpec((None, tr, tc), lambda i, n, idx: (layer, idx[1] * per + i, n))),
        out_shape=jax.ShapeDtypeStruct((layers, 2 * h, cols), F32),
        input_output_aliases=aliases,
        compiler_params=_params(("parallel", "parallel")),
    )(*args)


def _layer_fwd(x, mod, wbuf, p, layer, cfg, make_job=None):
    d, tr, tm = cfg["d"], cfg["tr"], cfg["tm"]
    s = x.shape[0]
    weight = lambda n: _Sharded(wbuf[n], BIG_KIND[n], layer=layer)
    shift, scale, rg = mod[:, 0:d], mod[:, d:2 * d], mod[:, 2 * d:3 * d]
    h = _norm_fwd(x, p["norm_g"], scale, shift, tr)
    tn_in = cfg["tn_in"]
    q_lo, q_hi, q_scale = d // tn_in, 2 * d // tn_in, HEAD_DIM ** -0.5
    blk_in = pl.BlockSpec((tm, tn_in), lambda i, j, k: (i, j))

    def proj_out(col, acc):
        return acc, acc * jnp.where((col >= q_lo) & (col < q_hi), q_scale, 1.0)

    proj, projb = _mm_nn("proj", h, weight("w_in"), tm=tm, tn=tn_in, tk=d, with_col=True, epilogue=proj_out,
                         outs=[(jax.ShapeDtypeStruct((s, 7 * d), F32), blk_in),
                               (jax.ShapeDtypeStruct((s, 7 * d), BF16), blk_in)])
    gates, = _mm_nn("gates", h, weight("w_gate"), tm=tm, tn=cfg["tn_gate"], tk=d,
                    extras=[(p["b_gate"], pl.BlockSpec((1, cfg["tn_gate"]), lambda i, j, k: (0, j)))],
                    epilogue=lambda acc, b: (_sigmoid(acc + b),))
    ya = _pool_fwd(proj, p["pool_w"], p["pool_scale"], d, tr)
    yc = _conv_fwd(proj, p["conv_w"], d, tr)
    job = None if make_job is None else make_job([wbuf[n] for n in BIG])
    att, yb, rsave, job_out = _sb_fwd(proj, projb, d, cfg["blk"], job)
    if job is not None:
        wbuf.update(zip(BIG, job_out))
    pa, = _mm_nn("branch_a", ya, weight("w_br_a"), tm=tm, tn=cfg["tn_d"], tk=d // 2)
    pb, = _mm_nn("branch_b", yb, weight("w_br_b"), tm=tm, tn=cfg["tn_d"], tk=cfg["tk_row"])
    pc, = _mm_nn("branch_c", yc, weight("w_br_c"), tm=tm, tn=cfg["tn_d"], tk=d // 2)
    merged = _merge_fwd(gates, pa, pb, pc, cfg["tr_small"])
    tn = cfg["tn_d"]
    blk = pl.BlockSpec((tm, tn), lambda i, j, k: (i, j))
    sd = jax.ShapeDtypeStruct((s, d), F32)
    out, x_next = _mm_nn("out_proj", merged, weight("w_out"), tm=tm, tn=tn, tk=cfg["tk_row"],
                         outs=[(sd, blk), (sd, blk)],
                         extras=[(x, blk), (rg, pl.BlockSpec((1, tn), lambda i, j, k: (0, j)))],
                         epilogue=lambda acc, xv, g: (acc, xv + g * acc))
    saved = dict(x=x, h=h, proj=proj, projb=projb, gates=gates, ya=ya, yb=yb, yc=yc, att=att, rsave=rsave,
                 pa=pa, pb=pb, pc=pc, merged=merged, out=out, scale=scale, rg=rg)
    return x_next, saved


def _layer_bwd(dxo, sv, wbuf, p, layer, cfg, job=None):
    d, tr, tm, tk_s = cfg["d"], cfg["tr"], cfg["tm"], cfg["tk_s"]
    weight = lambda n: _Sharded(wbuf[n], BIG_KIND[n], layer=layer)
    dout, drg = _resid_bwd(dxo, sv["out"], sv["rg"], tr)
    dmerged = _mm_nt("d_merged", dout, weight("w_out"), tm=tm, tn=cfg["tn_row"], tk=d)
    g_out = _mm_tn("g_w_out", sv["merged"], dout, "row", tm=cfg["tn_row"], tn=cfg["tn_d"], tk=tk_s)
    dpa, dpb, dpc, dlogit, g_bgate = _merge_bwd(dmerged, sv["gates"], sv["pa"], sv["pb"], sv["pc"], cfg["tr_small"])
    dya = _mm_nt("d_ya", dpa, weight("w_br_a"), tm=tm, tn=cfg["tn_half"], tk=cfg["tn_d"])
    dyb = _mm_nt("d_yb", dpb, weight("w_br_b"), tm=tm, tn=cfg["tn_row"], tk=d)
    dyc = _mm_nt("d_yc", dpc, weight("w_br_c"), tm=tm, tn=cfg["tn_half"], tk=cfg["tn_d"])
    g_a = _mm_tn("g_w_br_a", sv["ya"], dpa, "col", tm=cfg["tn_half"], tn=cfg["tn_d"], tk=tk_s)
    g_b = _mm_tn("g_w_br_b", sv["yb"], dpb, "row", tm=cfg["tn_row"], tn=cfg["tn_d"], tk=tk_s)
    g_c = _mm_tn("g_w_br_c", sv["yc"], dpc, "col", tm=cfg["tn_half"], tn=cfg["tn_d"], tk=tk_s)
    d_a, g_pool_w, g_pool_scale = _pool_bwd(sv["proj"], dya, p["pool_w"], p["pool_scale"], d, tr)
    d_c, g_conv_w = _conv_bwd(sv["proj"], dyc, p["conv_w"], d, tr)
    dq, dk, dv, dzb, job_out = _sb_bwd(sv["proj"], sv["projb"], sv["att"], dyb, sv["rsave"], d, cfg["blk"], job)
    dproj = jnp.concatenate([d_a, dq.astype(BF16), dk.astype(BF16), dv.astype(BF16), dzb.astype(BF16), d_c], axis=1)
    dh_in = _mm_nt("d_h_in", dproj, weight("w_in"), tm=tm, tn=cfg["tn_d"], tk=cfg["tk_in"])
    tn = cfg["tn_d"]
    dh = _mm_nt("d_h_gate", dlogit, weight("w_gate"), tm=tm, tn=tn, tk=cfg["tn_gate"],
                extras=[(dh_in, pl.BlockSpec((tm, tn), lambda i, j, k: (i, j)))],
                epilogue=lambda acc, prev: (prev + acc,))
    g_in = _mm_tn("g_w_in", sv["h"], dproj, "col", tm=cfg["tm_g"], tn=cfg["tn_in"], tk=tk_s)
    g_gate = _mm_tn("g_w_gate", sv["h"], dlogit, "col", tm=cfg["tm_g"], tn=cfg["tn_gate"], tk=tk_s)
    dx, dshift, dscale, g_norm = _norm_bwd(dh, sv["x"], p["norm_g"], sv["scale"], dxo, tr)
    big = dict(w_in=g_in, w_gate=g_gate, w_br_a=g_a, w_br_b=g_b, w_br_c=g_c, w_out=g_out)
    small = dict(dmod=jnp.concatenate([dshift, dscale, drg], axis=1), norm_g=g_norm, pool_scale=g_pool_scale,
                 b_gate=g_bgate, conv_w=g_conv_w, pool_w=g_pool_w)
    return dx, big, small, job_out


BIG = ("w_in", "w_gate", "w_br_a", "w_br_b", "w_br_c", "w_out")
BIG_KIND = dict(w_in="col", w_gate="col", w_br_a="col", w_br_b="row", w_br_c="col", w_out="row")


def _pad_to(v, n):
    return jnp.pad(v, (0, n - v.shape[0]))


def kernel(x, c, norm_g, w_ada, b_ada, w_in, pool_w, pool_scale, conv_w, w_br_a, w_br_b, w_br_c, w_gate, b_gate, w_out, final_g, loss_target, m_norm_g, m_w_ada, m_b_ada, m_w_in, m_pool_w, m_pool_scale, m_conv_w, m_w_br_a, m_w_br_b, m_w_br_c, m_w_gate, m_b_gate, m_w_out, m_final_g, v_norm_g, v_w_ada, v_b_ada, v_w_in, v_pool_w, v_pool_scale, v_conv_w, v_w_br_a, v_w_br_b, v_w_br_c, v_w_gate, v_b_gate, v_w_out, v_final_g):
    weights = dict(norm_g=norm_g, w_ada=w_ada, b_ada=b_ada, w_in=w_in, pool_w=pool_w, pool_scale=pool_scale,
                   conv_w=conv_w, w_br_a=w_br_a, w_br_b=w_br_b, w_br_c=w_br_c, w_gate=w_gate, b_gate=b_gate,
                   w_out=w_out, final_g=final_g)
    mom_m = dict(norm_g=m_norm_g, w_ada=m_w_ada, b_ada=m_b_ada, w_in=m_w_in, pool_w=m_pool_w,
                 pool_scale=m_pool_scale, conv_w=m_conv_w, w_br_a=m_w_br_a, w_br_b=m_w_br_b, w_br_c=m_w_br_c,
                 w_gate=m_w_gate, b_gate=m_b_gate, w_out=m_w_out, final_g=m_final_g)
    mom_v = dict(norm_g=v_norm_g, w_ada=v_w_ada, b_ada=v_b_ada, w_in=v_w_in, pool_w=v_pool_w,
                 pool_scale=v_pool_scale, conv_w=v_conv_w, w_br_a=v_w_br_a, w_br_b=v_w_br_b, w_br_c=v_w_br_c,
                 w_gate=v_w_gate, b_gate=v_b_gate, w_out=v_w_out, final_g=v_final_g)
    names = list(weights)

    _, s, d = x.shape
    layers = norm_g.shape[0]
    pw, gd, cc = d // 2, d // 8, d // 2
    ada_cols = 3 * d // N_CHIPS
    xi, yi, ci = lax.axis_index("x"), lax.axis_index("y"), lax.axis_index("c")
    me = 4 * xi + 2 * yi + ci
    my_chip = 2 * xi + yi
    c_idx = jnp.reshape(ci, (1,)).astype(jnp.int32)
    chip_idx = jnp.reshape(my_chip, (1,)).astype(jnp.int32)

    cfg = dict(
        d=d, tr=_tile(s, 256, 16), tr_small=_tile(s, 128, 16), tm=_tile(s, 1024, 16), tk_s=_tile(s, 1024, 16),
        blk=_tile(s, min(256, max(s // 4, 16)), 16),
        tn_in=_tile(7 * d // N_CHIPS, 512), tk_in=_tile(7 * d // N_CHIPS, 1792),
        tn_gate=_tile(3 * d // N_CHIPS, 512), tn_d=_tile(d // N_CHIPS, 512),
        tn_row=_tile(d // N_CHIPS, 512, 16), tk_row=_tile(d // N_CHIPS, 512, 16),
        tn_half=_tile(d // 2, 512), tm_g=_tile(d, 1024),
    )

    conv_flat = conv_w.reshape(-1)
    conv_len = -(-conv_flat.shape[0] // 1024) * 1024
    pack0 = jnp.concatenate([c.reshape(-1), _pad_to(conv_flat, conv_len), pool_w.reshape(-1)])
    w0 = -(-pack0.shape[0] // 1024) * 1024
    g0 = _all_gather_small(_pad_to(pack0, w0).reshape(8, w0 // 8), "gather_small").reshape(N_DEV, w0)
    c_all = g0[:, 0:d]
    chip_rows = g0[0::2]
    conv_full = jnp.concatenate(
        [chip_rows[j, d:d + conv_flat.shape[0]].reshape(conv_w.shape) for j in range(N_CHIPS)], axis=2)
    pool_full = jnp.concatenate(
        [chip_rows[j, d + conv_len:d + conv_len + pool_w.size].reshape(pool_w.shape) for j in range(N_CHIPS)],
        axis=2)
    pool_bf = pool_full.astype(BF16)

    sc16 = _silu_bf16(jnp.pad(c_all, ((0, 16 - N_DEV), (0, 0))))
    mods = []
    for l in range(layers):
        bias = lax.dynamic_slice(b_ada[l], (my_chip * ada_cols,), (ada_cols,)).reshape(1, ada_cols)
        tn = _tile(ada_cols, 512)
        mod_l, = _mm("mod", sc16, w_ada[l], dims=NN, grid=(1, ada_cols // tn, 1),
                     a_spec=pl.BlockSpec((16, d), lambda i, j, k: (0, 0)),
                     b_spec=pl.BlockSpec((d, tn), lambda i, j, k: (0, j)),
                     acc_shape=(16, tn),
                     outs=[(jax.ShapeDtypeStruct((16, ada_cols), F32), pl.BlockSpec((16, tn), lambda i, j, k: (0, j)))],
                     extras=[(bias, pl.BlockSpec((1, tn), lambda i, j, k: (0, j)))],
                     epilogue=lambda acc, b: (acc + b,))
        mods.append(mod_l[0:N_DEV])
    g1 = _all_gather_small(jnp.concatenate(mods, axis=1), "gather_mod")
    g1 = g1.reshape(N_CHIPS, 2, N_DEV, layers, ada_cols)[:, 0]
    mod_all = jnp.transpose(g1, (1, 2, 0, 3)).reshape(N_DEV, layers, 3 * d)
    mod_me = lax.dynamic_slice(mod_all, (me, 0, 0), (1, layers, 3 * d))[0]

    wbuf = {n: _place_shard(weights[n], chip_idx) for n in BIG}
    wbuf.update(zip(BIG, _run_job(_gather_job([wbuf[n] for n in BIG], 0), "gather_weights")))
    params = [dict(norm_g=norm_g[l:l + 1], pool_scale=pool_scale[l:l + 1], b_gate=b_gate[l:l + 1],
                   conv_w=conv_full[l], pool_w=pool_bf[l]) for l in range(layers)]

    act = x[0]
    saved = []
    for l in range(layers):
        make_job = (lambda bufs, nxt=l + 1: _gather_job(bufs, nxt)) if l + 1 < layers else None
        act, sv = _layer_fwd(act, mod_me[l:l + 1], wbuf, params[l], l, cfg, make_job)
        saved.append(sv)
    loss_part, dact, g_final = _final_loss(act, final_g.reshape(1, d), loss_target[0], cfg["tr"])
    loss = lax.psum(loss_part[0, 0], ("x", "y", "c"))

    small_grads, parts, theirs = [None] * layers, [None] * layers, [None] * layers
    for l in reversed(range(layers)):
        job = _send_job(parts[l + 1]) if l + 1 < layers else None
        dact, big, small_grads[l], job_out = _layer_bwd(dact, saved[l], wbuf, params[l], l, cfg, job)
        if job is not None:
            theirs[l + 1] = job_out
        flat = [big[n] for n in BIG]
        parts[l] = [_sum_sibling(g, r, c_idx) for g, r in zip(flat, _swap_halves(flat))]
    theirs[0] = _run_job(_send_job(parts[0]), "send_to_owners")
    grad_x = dact.reshape(x.shape)
    place_idx = jnp.stack([my_chip, ci]).astype(jnp.int32)
    owned = []
    for w in range(len(BIG)):
        buf = None
        for l in range(layers):
            buf = _sum_owner(parts[l][w], theirs[l][w], place_idx, l, layers, prev=buf)
        owned.append(buf)
    full = _share_halves(owned)
    grads = {n: f.reshape(weights[n].shape) for n, f in zip(BIG, full)}

    small_names = ("dmod", "norm_g", "pool_scale", "b_gate", "conv_w", "pool_w")
    pieces = [small_grads[l][n].reshape(-1) for n in small_names for l in range(layers)] + [g_final.reshape(-1)]
    pack1 = jnp.concatenate(pieces)
    w1 = -(-pack1.shape[0] // 1024) * 1024
    g2 = _all_gather_small(_pad_to(pack1, w1).reshape(8, w1 // 8), "gather_grads")
    total = _sum_blocks(g2, N_DEV, 8).reshape(-1)
    off = 0
    summed = {}
    for n in small_names:
        per = small_grads[0][n].size
        summed[n] = jnp.stack([total[off + l * per:off + (l + 1) * per].reshape(small_grads[0][n].shape)
                               for l in range(layers)])
        off += layers * per
    grads["final_g"] = total[off:off + d]
    grads["norm_g"] = summed["norm_g"].reshape(layers, d)
    grads["pool_scale"] = summed["pool_scale"].reshape(layers, pw)
    grads["b_gate"] = summed["b_gate"].reshape(layers, 3 * d)
    grads["b_ada"] = summed["dmod"].reshape(layers, 3 * d)
    cs = cc // N_CHIPS
    grads["conv_w"] = lax.dynamic_slice(summed["conv_w"], (0, 0, my_chip * cs), (layers, 3, cs))
    rs_ = gd // N_CHIPS
    grads["pool_w"] = lax.dynamic_slice(summed["pool_w"], (0, 0, my_chip * rs_, 0), (layers, N_GROUPS, rs_, gd))
    dmod_all = g2.reshape(N_DEV, w1)[:, 0:layers * 3 * d].reshape(N_DEV, layers, 3 * d)
    g_ada = []
    for l in range(layers):
        cols = lax.dynamic_slice(dmod_all[:, l], (0, my_chip * ada_cols), (N_DEV, ada_cols))
        cols16 = jnp.pad(cols, ((0, 16 - N_DEV), (0, 0)))
        tn = _tile(ada_cols, 512)
        tm = _tile(d, 1024)
        ga, = _mm("g_w_ada", sc16, cols16, dims=TN, grid=(d // tm, ada_cols // tn, 1),
                  a_spec=pl.BlockSpec((16, tm), lambda i, j, k: (0, i)),
                  b_spec=pl.BlockSpec((16, tn), lambda i, j, k: (0, j)),
                  acc_shape=(tm, tn),
                  outs=[(jax.ShapeDtypeStruct((d, ada_cols), F32), pl.BlockSpec((tm, tn), lambda i, j, k: (i, j)))])
        g_ada.append(ga)
    grads["w_ada"] = jnp.stack(g_ada)

    deltas, new_m, new_v = {}, {}, {}
    for n in names:
        deltas[n], new_m[n], new_v[n] = _adamw(weights[n], grads[n], mom_m[n], mom_v[n])
    return (loss, grad_x, *[grads[n] for n in names], *[deltas[n] for n in names],
            *[new_m[n] for n in names], *[new_v[n] for n in names])
```

```python
import functools

import jax
import jax.numpy as jnp
from jax import lax
from jax.experimental import pallas as pl
from jax.experimental.pallas import tpu as pltpu

F32 = jnp.float32
BF16 = jnp.bfloat16
MESH = pl.DeviceIdType.MESH

N_CHIPS = 4
N_DEV = 8
N_GROUPS = 4
POOL_WINDOWS = (2, 4, 8, 16)
POOL_HALO = 16
CONV_HALO = 8
HEAD_DIM = 128
SB_FWD_HEADS = 4
SB_BWD_HEADS = 4
RMS_EPS = 1e-6
ADAM_LR = 0.001
ADAM_B1 = 0.9
ADAM_B2 = 0.999
ADAM_EPS = 1e-08
ADAM_WD = 0.01
ADAM_STEP = 10
V7X_VMEM_LIMIT = 56 * 1024 * 1024


def _tile(n, pref, mult=128):
    best = None
    t = mult
    while t <= min(n, pref):
        if n % t == 0:
            best = t
        t += mult
    return n if best is None else best


def _params(sem=None):
    return pltpu.CompilerParams(dimension_semantics=sem, vmem_limit_bytes=V7X_VMEM_LIMIT)


def _sigmoid(z):
    return jax.nn.sigmoid(z)


def _dsilu(z, sg):
    return sg * (1.0 + z * (1.0 - sg))


NN = ((1,), (0,))
NT = ((1,), (1,))
TN = ((0,), (0,))


def _mm(name, a, b, *, dims, grid, a_spec, b_spec, acc_shape, outs, extras=(), epilogue=None, with_col=False,
        job=None):
    gi, gj, nk = grid
    ne, no = len(extras), len(outs)
    if epilogue is None:
        epilogue = lambda acc: (acc,)

    def body(*refs):
        own_in, orefs, jargs = _job_refs(job, refs[:-1], 2 + ne, no)
        a_ref, b_ref, ex, acc = own_in[0], own_in[1], own_in[2:], refs[-1]
        i, j, k = pl.program_id(0), pl.program_id(1), pl.program_id(2)
        lead = (j,) if with_col else ()
        if job is not None:
            @pl.when((i == 0) & (j == 0) & (k == 0))
            def _():
                job.start(*jargs)

            if job.mid is not None:
                @pl.when((i == gi - 1) & (j == gj // 2) & (k == 0))
                def _():
                    job.mid(*jargs)

        @pl.when(k == 0)
        def _():
            acc[...] = jnp.zeros_like(acc)

        acc[...] += lax.dot_general(a_ref[...].astype(BF16), b_ref[...].astype(BF16), (dims, ((), ())),
                                    preferred_element_type=F32)

        @pl.when(k == nk - 1)
        def _():
            vals = epilogue(*lead, acc[...], *[e[...] for e in ex])
            for o, v in zip(orefs, vals):
                o[...] = v.astype(o.dtype)

        if job is not None:
            @pl.when((i == gi - 1) & (j == gj - 1) & (k == nk - 1))
            def _():
                job.finish(*jargs)

    hosted = _host(job, 2 + ne, no)
    sem = ("parallel", "parallel", "arbitrary") if job is None else ("arbitrary",) * 3
    res = pl.pallas_call(
        body, name=name, grid=grid,
        in_specs=[a_spec, b_spec] + [s for _, s in extras] + hosted["in_specs"],
        out_specs=[s for _, s in outs] + hosted["out_specs"],
        out_shape=[sh for sh, _ in outs] + hosted["out_shape"],
        scratch_shapes=hosted["scratch"] + [pltpu.VMEM(acc_shape, F32)],
        input_output_aliases=hosted["aliases"],
        compiler_params=_params(sem),
    )(a, b, *[e for e, _ in extras], *hosted["ins"])
    return res


class _Sharded:
    def __init__(self, arr, kind, layer=None):
        self.arr, self.kind, self.layer = arr, kind, layer
        r, c = arr.shape[-2:]
        self.rows = r * (N_CHIPS if kind == "row" else 1)
        self.cols = c * (N_CHIPS if kind == "col" else 1)
        self.sr, self.sc = r, c

    def spec(self, br, bc, f):
        lead = (None,) if self.layer is None else (None, None)
        layer = self.layer
        if self.kind == "col":
            per = self.sc // bc
            assert per * bc == self.sc and self.sr % br == 0, (self.arr.shape, br, bc)

            def idx(*g):
                rb, cb = f(*g)
                return ((cb // per,) + (() if layer is None else (layer,)) + (rb, cb % per))
        else:
            per = self.sr // br
            assert per * br == self.sr and self.sc % bc == 0, (self.arr.shape, br, bc)

            def idx(*g):
                rb, cb = f(*g)
                return ((rb // per,) + (() if layer is None else (layer,)) + (rb % per, cb))
        return pl.BlockSpec(lead + (br, bc), idx)


def _grad_buffer(rows, cols, kind):
    if kind == "col":
        return jax.ShapeDtypeStruct((N_CHIPS, rows, cols // N_CHIPS), BF16)
    return jax.ShapeDtypeStruct((N_CHIPS, rows // N_CHIPS, cols), BF16)


def _mm_nn(name, a, w, *, tm, tn, tk, outs=None, extras=(), epilogue=None, out_dtype=F32, with_col=False,
           job=None):
    m, kdim = a.shape
    n = w.cols
    grid = (m // tm, n // tn, kdim // tk)
    if outs is None:
        outs = [(jax.ShapeDtypeStruct((m, n), out_dtype), pl.BlockSpec((tm, tn), lambda i, j, k: (i, j)))]
    return _mm(name, a, w.arr, dims=NN, grid=grid,
               a_spec=pl.BlockSpec((tm, tk), lambda i, j, k: (i, k)),
               b_spec=w.spec(tk, tn, lambda i, j, k: (k, j)),
               acc_shape=(tm, tn), outs=outs, extras=extras, epilogue=epilogue, with_col=with_col, job=job)


def _mm_grad(name, a_t, b, kind, *, tm, tn, tk):
    m, kdim = a_t.shape
    n = b.shape[1]
    out = _Sharded(_grad_buffer(m, n, kind), kind)
    return _mm(name, a_t, b, dims=NN, grid=(m // tm, n // tn, kdim // tk),
               a_spec=pl.BlockSpec((tm, tk), lambda i, j, k: (i, k)),
               b_spec=pl.BlockSpec((tk, tn), lambda i, j, k: (k, j)),
               acc_shape=(tm, tn), outs=[(out.arr, out.spec(tm, tn, lambda i, j, k: (i, j)))])[0]


def _mm_nt(name, a, w, *, tm, tn, tk, extras=(), epilogue=None, out_dtype=F32):
    m, kdim = a.shape
    n = w.rows
    grid = (m // tm, n // tn, kdim // tk)
    outs = [(jax.ShapeDtypeStruct((m, n), out_dtype), pl.BlockSpec((tm, tn), lambda i, j, k: (i, j)))]
    return _mm(name, a, w.arr, dims=NT, grid=grid,
               a_spec=pl.BlockSpec((tm, tk), lambda i, j, k: (i, k)),
               b_spec=w.spec(tn, tk, lambda i, j, k: (j, k)),
               acc_shape=(tm, tn), outs=outs, extras=extras, epilogue=epilogue)[0]


def _mm_tn(name, a, b, kind, *, tm, tn, tk):
    kdim, m = a.shape
    n = b.shape[1]
    grid = (m // tm, n // tn, kdim // tk)
    out = _Sharded(_grad_buffer(m, n, kind), kind)
    outs = [(out.arr, out.spec(tm, tn, lambda i, j, k: (i, j)))]
    return _mm(name, a, b, dims=TN, grid=grid,
               a_spec=pl.BlockSpec((tk, tm), lambda i, j, k: (k, i)),
               b_spec=pl.BlockSpec((tk, tn), lambda i, j, k: (k, j)),
               acc_shape=(tm, tn), outs=outs)[0]


def _row_spec(tr, w, col=0):
    return pl.BlockSpec((tr, w), lambda i: (i, col))


def _vec_spec(w, col=0, rows=1):
    return pl.BlockSpec((rows, w), lambda i: (0, col))


def _norm_fwd(x, g, scale, shift, tr):
    s, d = x.shape

    def body(x_ref, g_ref, sc_ref, sh_ref, h_ref, ht_ref):
        xv = x_ref[...]
        r = lax.rsqrt(jnp.mean(xv * xv, axis=-1, keepdims=True) + RMS_EPS)
        y = xv * r * g_ref[...]
        hv = y * (1.0 + sc_ref[...]) + sh_ref[...]
        h_ref[...] = hv.astype(BF16)
        ht_ref[...] = hv.T.astype(BF16)

    return pl.pallas_call(
        body, name="norm_fwd", grid=(s // tr,),
        in_specs=[_row_spec(tr, d), _vec_spec(d), _vec_spec(d), _vec_spec(d)],
        out_specs=[_row_spec(tr, d), pl.BlockSpec((d, tr), lambda i: (0, i))],
        out_shape=[jax.ShapeDtypeStruct((s, d), BF16), jax.ShapeDtypeStruct((d, s), BF16)],
        compiler_params=_params(("parallel",)),
    )(x, g, scale, shift)


def _norm_bwd(dh, x, g, scale, dxo, tr):
    s, d = x.shape

    def body(dh_ref, x_ref, g_ref, sc_ref, dxo_ref, dx_ref, dsh_ref, dsc_ref, dg_ref):
        @pl.when(pl.program_id(0) == 0)
        def _():
            dsh_ref[...] = jnp.zeros_like(dsh_ref)
            dsc_ref[...] = jnp.zeros_like(dsc_ref)
            dg_ref[...] = jnp.zeros_like(dg_ref)

        xv, dhv, gv = x_ref[...], dh_ref[...], g_ref[...]
        r = lax.rsqrt(jnp.mean(xv * xv, axis=-1, keepdims=True) + RMS_EPS)
        xn = xv * r
        dsh_ref[...] += jnp.sum(dhv, axis=0, keepdims=True)
        dsc_ref[...] += jnp.sum(dhv * (xn * gv), axis=0, keepdims=True)
        dyg = dhv * (1.0 + sc_ref[...])
        dg_ref[...] += jnp.sum(dyg * xn, axis=0, keepdims=True)
        dxn = dyg * gv
        dx_ref[...] = dxo_ref[...] + r * (dxn - xn * jnp.mean(dxn * xn, axis=-1, keepdims=True))

    vec = jax.ShapeDtypeStruct((1, d), F32)
    return pl.pallas_call(
        body, name="norm_bwd", grid=(s // tr,),
        in_specs=[_row_spec(tr, d), _row_spec(tr, d), _vec_spec(d), _vec_spec(d), _row_spec(tr, d)],
        out_specs=[_row_spec(tr, d), _vec_spec(d), _vec_spec(d), _vec_spec(d)],
        out_shape=[jax.ShapeDtypeStruct((s, d), F32), vec, vec, vec],
        compiler_params=_params(("arbitrary",)),
    )(dh, x, g, scale, dxo)


def _final_loss(x, g, target, tr):
    s, d = x.shape

    def body(x_ref, g_ref, t_ref, loss_ref, dx_ref, dg_ref):
        @pl.when(pl.program_id(0) == 0)
        def _():
            loss_ref[...] = jnp.zeros_like(loss_ref)
            dg_ref[...] = jnp.zeros_like(dg_ref)

        xv, gv = x_ref[...], g_ref[...]
        r = lax.rsqrt(jnp.mean(xv * xv, axis=-1, keepdims=True) + RMS_EPS)
        xn = xv * r
        err = xn * gv - t_ref[...]
        per_row = jnp.mean(err * err, axis=-1, keepdims=True)
        loss_ref[...] += 0.5 * jnp.sum(per_row, axis=0, keepdims=True)
        dy = err * (1.0 / d)
        dg_ref[...] += jnp.sum(dy * xn, axis=0, keepdims=True)
        dxn = dy * gv
        dx_ref[...] = r * (dxn - xn * jnp.mean(dxn * xn, axis=-1, keepdims=True))

    return pl.pallas_call(
        body, name="final_loss", grid=(s // tr,),
        in_specs=[_row_spec(tr, d), _vec_spec(d), _row_spec(tr, d)],
        out_specs=[pl.BlockSpec((8, 128), lambda i: (0, 0)), _row_spec(tr, d), _vec_spec(d)],
        out_shape=[jax.ShapeDtypeStruct((8, 128), F32), jax.ShapeDtypeStruct((s, d), F32),
                   jax.ShapeDtypeStruct((1, d), F32)],
        compiler_params=_params(("arbitrary",)),
    )(x, g, target)


def _halo_before(tr, halo, w, col):
    per = tr // halo
    return pl.BlockSpec((halo, w), lambda i: (jnp.maximum(i * per - 1, 0), col))


def _halo_after(tr, halo, w, col, n_tiles):
    per = tr // halo
    return pl.BlockSpec((halo, w), lambda i: (jnp.minimum((i + 1) * per, n_tiles * per - 1), col))


def _pool_fwd(proj, pool_w, pool_scale, d, tr):
    s = proj.shape[0]
    pw, gd, hl = d // 2, d // 8, POOL_HALO

    def body(xa_ref, xh_ref, za_ref, w_ref, ps_ref, ya_ref, buf):
        i = pl.program_id(0)
        buf[0:hl, :] = jnp.where(i > 0, xh_ref[...], 0.0)
        buf[hl:hl + tr, :] = xa_ref[...]
        row = i * tr + lax.broadcasted_iota(jnp.int32, (tr, 1), 0)
        za = za_ref[...]
        gate = za * _sigmoid(za)
        for g, win in enumerate(POOL_WINDOWS):
            cs = slice(g * gd, (g + 1) * gd)
            xg = buf[hl:hl + tr, cs]
            acc = xg
            for j in range(1, win):
                acc = acc + buf[hl - j:hl - j + tr, cs]
            cnt = jnp.minimum(row + 1, win).astype(F32)
            mixed = acc / cnt - xg
            y = jnp.dot(mixed.astype(BF16), w_ref[g], preferred_element_type=F32)
            ya_ref[:, cs] = ((y * ps_ref[:, cs]) * gate[:, cs]).astype(BF16)

    return pl.pallas_call(
        body, name="pool_fwd", grid=(s // tr,),
        in_specs=[_row_spec(tr, pw, 0), _halo_before(tr, hl, pw, 0), _row_spec(tr, pw, 1),
                  pl.BlockSpec((N_GROUPS, gd, gd), lambda i: (0, 0, 0)), _vec_spec(pw)],
        out_specs=_row_spec(tr, pw), out_shape=jax.ShapeDtypeStruct((s, pw), BF16),
        scratch_shapes=[pltpu.VMEM((tr + hl, pw), F32)],
        compiler_params=_params(("parallel",)),
    )(proj, proj, proj, pool_w, pool_scale)


def _pool_bwd(proj, dya, pool_w, pool_scale, d, tr):
    s = proj.shape[0]
    pw, gd, hl = d // 2, d // 8, POOL_HALO
    n_tiles = s // tr

    def body(xa_ref, xh_ref, za_ref, zh_ref, dya_ref, dyh_ref, w_ref, ps_ref, da_ref, gw_ref, gs_ref, buf, dbuf):
        i = pl.program_id(0)

        @pl.when(i == 0)
        def _():
            gw_ref[...] = jnp.zeros_like(gw_ref)
            gs_ref[...] = jnp.zeros_like(gs_ref)

        buf[0:hl, :] = jnp.where(i > 0, xh_ref[...], 0.0)
        buf[hl:hl + tr, :] = xa_ref[...]
        row = i * tr + lax.broadcasted_iota(jnp.int32, (tr, 1), 0)
        row_h = (i + 1) * tr + lax.broadcasted_iota(jnp.int32, (hl, 1), 0)
        za, dya_t = za_ref[...], dya_ref[...]
        sg = _sigmoid(za)
        gate = za * sg
        dpre = dya_t * gate
        zh = zh_ref[...]
        dpre_h = jnp.where(i < n_tiles - 1, dyh_ref[...], 0.0) * (zh * _sigmoid(zh))
        ps = ps_ref[...]
        for g, win in enumerate(POOL_WINDOWS):
            cs = slice(g * gd, (g + 1) * gd)
            wg = w_ref[g]
            xg = buf[hl:hl + tr, cs]
            acc = xg
            for j in range(1, win):
                acc = acc + buf[hl - j:hl - j + tr, cs]
            cnt = jnp.minimum(row + 1, win).astype(F32)
            mixed = (acc / cnt - xg).astype(BF16)
            ylin = jnp.dot(mixed, wg, preferred_element_type=F32)
            gs_ref[:, cs] += jnp.sum(dpre[:, cs] * ylin, axis=0, keepdims=True)
            da_ref[:, pw + g * gd:pw + (g + 1) * gd] = (
                dya_t[:, cs] * (ylin * ps[:, cs]) * _dsilu(za[:, cs], sg[:, cs])).astype(BF16)
            dyl = (dpre[:, cs] * ps[:, cs]).astype(BF16)
            gw_ref[g] += lax.dot_general(mixed, dyl, (TN, ((), ())), preferred_element_type=F32)
            dmix = lax.dot_general(dyl, wg, (NT, ((), ())), preferred_element_type=F32)
            dyl_h = (dpre_h[:, cs] * ps[:, cs]).astype(BF16)
            dmix_h = lax.dot_general(dyl_h, wg, (NT, ((), ())), preferred_element_type=F32)
            cnt_h = jnp.minimum(row_h + 1, win).astype(F32)
            dbuf[0:tr, cs] = dmix / cnt
            dbuf[tr:tr + hl, cs] = dmix_h / cnt_h
            dx = dbuf[0:tr, cs] - dmix
            for j in range(1, win):
                dx = dx + dbuf[j:j + tr, cs]
            da_ref[:, cs] = dx.astype(BF16)

    return pl.pallas_call(
        body, name="pool_bwd", grid=(n_tiles,),
        in_specs=[_row_spec(tr, pw, 0), _halo_before(tr, hl, pw, 0),
                  _row_spec(tr, pw, 1), _halo_after(tr, hl, pw, 1, n_tiles),
                  _row_spec(tr, pw, 0), _halo_after(tr, hl, pw, 0, n_tiles),
                  pl.BlockSpec((N_GROUPS, gd, gd), lambda i: (0, 0, 0)), _vec_spec(pw)],
        out_specs=[_row_spec(tr, 2 * pw), pl.BlockSpec((N_GROUPS, gd, gd), lambda i: (0, 0, 0)), _vec_spec(pw)],
        out_shape=[jax.ShapeDtypeStruct((s, 2 * pw), BF16), jax.ShapeDtypeStruct((N_GROUPS, gd, gd), F32),
                   jax.ShapeDtypeStruct((1, pw), F32)],
        scratch_shapes=[pltpu.VMEM((tr + hl, pw), F32), pltpu.VMEM((tr + hl, pw), F32)],
        compiler_params=_params(("arbitrary",)),
    )(proj, proj, proj, proj, dya, dya, pool_w, pool_scale)


def _conv_fwd(proj, conv_w, d, tr):
    s = proj.shape[0]
    cc, hl = d // 2, CONV_HALO
    cu, cb, cg_, cz = 10, 11, 12, 13

    def body(u_ref, uh_ref, bg_ref, cg_ref, ch_ref, zc_ref, w_ref, yc_ref, buf):
        i = pl.program_id(0)
        buf[0:hl, :] = jnp.where(i > 0, ch_ref[...] * uh_ref[...], 0.0)
        buf[hl:hl + tr, :] = cg_ref[...] * u_ref[...]
        y = w_ref[0:1, :] * buf[hl - 2:hl - 2 + tr, :]
        y = y + w_ref[1:2, :] * buf[hl - 1:hl - 1 + tr, :]
        y = y + w_ref[2:3, :] * buf[hl:hl + tr, :]
        zc = zc_ref[...]
        yc_ref[...] = ((bg_ref[...] * y) * (zc * _sigmoid(zc))).astype(BF16)

    return pl.pallas_call(
        body, name="conv_fwd", grid=(s // tr,),
        in_specs=[_row_spec(tr, cc, cu), _halo_before(tr, hl, cc, cu), _row_spec(tr, cc, cb),
                  _row_spec(tr, cc, cg_), _halo_before(tr, hl, cc, cg_), _row_spec(tr, cc, cz),
                  _vec_spec(cc, rows=3)],
        out_specs=_row_spec(tr, cc), out_shape=jax.ShapeDtypeStruct((s, cc), BF16),
        scratch_shapes=[pltpu.VMEM((tr + hl, cc), F32)],
        compiler_params=_params(("parallel",)),
    )(proj, proj, proj, proj, proj, proj, conv_w)


def _conv_bwd(proj, dyc, conv_w, d, tr):
    s = proj.shape[0]
    cc, hl = d // 2, CONV_HALO
    cu, cb, cg_, cz = 10, 11, 12, 13
    n_tiles = s // tr

    def body(u_ref, uh_ref, bg_ref, bh_ref, cg_ref, ch_ref, zc_ref, zh_ref, dy_ref, dyh_ref, w_ref,
             dc_ref, gw_ref, buf, dbuf):
        i = pl.program_id(0)

        @pl.when(i == 0)
        def _():
            gw_ref[...] = jnp.zeros_like(gw_ref)

        u, bg, cg, zc, dyc_t = u_ref[...], bg_ref[...], cg_ref[...], zc_ref[...], dy_ref[...]
        buf[0:hl, :] = jnp.where(i > 0, ch_ref[...] * uh_ref[...], 0.0)
        buf[hl:hl + tr, :] = cg * u
        v2, v1, v0 = buf[hl - 2:hl - 2 + tr, :], buf[hl - 1:hl - 1 + tr, :], buf[hl:hl + tr, :]
        w0, w1, w2 = w_ref[0:1, :], w_ref[1:2, :], w_ref[2:3, :]
        y = w0 * v2 + w1 * v1 + w2 * v0
        sg = _sigmoid(zc)
        gate = zc * sg
        dc_ref[:, cc:2 * cc] = (dyc_t * y * gate).astype(BF16)
        dc_ref[:, 3 * cc:4 * cc] = (dyc_t * bg * y * _dsilu(zc, sg)).astype(BF16)
        dy = dyc_t * bg * gate
        zh = zh_ref[...]
        dy_h = jnp.where(i < n_tiles - 1, dyh_ref[...], 0.0) * bh_ref[...] * (zh * _sigmoid(zh))
        gw_ref[0:1, :] += jnp.sum(dy * v2, axis=0, keepdims=True)
        gw_ref[1:2, :] += jnp.sum(dy * v1, axis=0, keepdims=True)
        gw_ref[2:3, :] += jnp.sum(dy * v0, axis=0, keepdims=True)
        dbuf[0:tr, :] = dy
        dbuf[tr:tr + hl, :] = dy_h
        dv = w2 * dy + w1 * dbuf[1:1 + tr, :] + w0 * dbuf[2:2 + tr, :]
        dc_ref[:, 0:cc] = (dv * cg).astype(BF16)
        dc_ref[:, 2 * cc:3 * cc] = (dv * u).astype(BF16)

    return pl.pallas_call(
        body, name="conv_bwd", grid=(n_tiles,),
        in_specs=[_row_spec(tr, cc, cu), _halo_before(tr, hl, cc, cu),
                  _row_spec(tr, cc, cb), _halo_after(tr, hl, cc, cb, n_tiles),
                  _row_spec(tr, cc, cg_), _halo_before(tr, hl, cc, cg_),
                  _row_spec(tr, cc, cz), _halo_after(tr, hl, cc, cz, n_tiles),
                  _row_spec(tr, cc, 0), _halo_after(tr, hl, cc, 0, n_tiles),
                  _vec_spec(cc, rows=3)],
        out_specs=[_row_spec(tr, 4 * cc), _vec_spec(cc, rows=3)],
        out_shape=[jax.ShapeDtypeStruct((s, 4 * cc), BF16), jax.ShapeDtypeStruct((3, cc), F32)],
        scratch_shapes=[pltpu.VMEM((tr + hl, cc), F32), pltpu.VMEM((tr + hl, cc), F32)],
        compiler_params=_params(("arbitrary",)),
    )(proj, proj, proj, proj, proj, proj, proj, proj, dyc, dyc, conv_w)


def _merge_fwd(gates, pa, pb, pc, tr):
    s, d = pa.shape

    def body(g_ref, a_ref, b_ref, c_ref, m_ref):
        m = g_ref[:, 0:d] * a_ref[...] + g_ref[:, d:2 * d] * b_ref[...] + g_ref[:, 2 * d:3 * d] * c_ref[...]
        m_ref[...] = m.astype(BF16)

    return pl.pallas_call(
        body, name="merge_fwd", grid=(s // tr,),
        in_specs=[_row_spec(tr, 3 * d), _row_spec(tr, d), _row_spec(tr, d), _row_spec(tr, d)],
        out_specs=_row_spec(tr, d), out_shape=jax.ShapeDtypeStruct((s, d), BF16),
        compiler_params=_params(("parallel",)),
    )(gates, pa, pb, pc)


def _merge_bwd(dm, gates, pa, pb, pc, tr):
    s, d = pa.shape

    def body(dm_ref, g_ref, a_ref, b_ref, c_ref, da_ref, db_ref, dc_ref, dg_ref, gb_ref):
        @pl.when(pl.program_id(0) == 0)
        def _():
            gb_ref[...] = jnp.zeros_like(gb_ref)

        dmv = dm_ref[...]
        for n, (p_ref, o_ref) in enumerate(((a_ref, da_ref), (b_ref, db_ref), (c_ref, dc_ref))):
            gv = g_ref[:, n * d:(n + 1) * d]
            o_ref[...] = (dmv * gv).astype(BF16)
            dlogit = (dmv * p_ref[...]) * (gv * (1.0 - gv))
            dg_ref[:, n * d:(n + 1) * d] = dlogit.astype(BF16)
            gb_ref[:, n * d:(n + 1) * d] += jnp.sum(dlogit, axis=0, keepdims=True)

    act = jax.ShapeDtypeStruct((s, d), BF16)
    return pl.pallas_call(
        body, name="merge_bwd", grid=(s // tr,),
        in_specs=[_row_spec(tr, d), _row_spec(tr, 3 * d), _row_spec(tr, d), _row_spec(tr, d), _row_spec(tr, d)],
        out_specs=[_row_spec(tr, d), _row_spec(tr, d), _row_spec(tr, d), _row_spec(tr, 3 * d), _vec_spec(3 * d)],
        out_shape=[act, act, act, jax.ShapeDtypeStruct((s, 3 * d), BF16), jax.ShapeDtypeStruct((1, 3 * d), F32)],
        compiler_params=_params(("arbitrary",)),
    )(dm, gates, pa, pb, pc)


def _resid_bwd(dxo, out, rg, tr):
    s, d = dxo.shape

    def body(dx_ref, o_ref, rg_ref, do_ref, drg_ref):
        @pl.when(pl.program_id(0) == 0)
        def _():
            drg_ref[...] = jnp.zeros_like(drg_ref)

        dxv = dx_ref[...]
        do_ref[...] = (dxv * rg_ref[...]).astype(BF16)
        drg_ref[...] += jnp.sum(dxv * o_ref[...], axis=0, keepdims=True)

    return pl.pallas_call(
        body, name="resid_bwd", grid=(s // tr,),
        in_specs=[_row_spec(tr, d), _row_spec(tr, d), _vec_spec(d)],
        out_specs=[_row_spec(tr, d), _vec_spec(d)],
        out_shape=[jax.ShapeDtypeStruct((s, d), BF16), jax.ShapeDtypeStruct((1, d), F32)],
        compiler_params=_params(("arbitrary",)),
    )(dxo, out, rg)


def _split_bf16(v):
    hi = v.astype(BF16)
    return hi, (v - hi.astype(F32)).astype(BF16)


def _log_keep(z):
    e = jnp.exp(-jnp.abs(z))
    return -(jnp.maximum(z, 0.0) + jnp.log(1.0 + e)), e


class _Job:
    def __init__(self, ins, out_shapes, aliases, n_sems, start, finish, mid=None):
        self.ins, self.out_shapes, self.aliases, self.n_sems = list(ins), list(out_shapes), list(aliases), n_sems
        self.start, self.mid, self.finish = start, mid, finish


def _host(job, n_in, n_out):
    if job is None:
        return dict(ins=[], in_specs=[], out_specs=[], out_shape=[], scratch=[], aliases={})
    return dict(ins=job.ins, in_specs=[_ANY] * len(job.ins), out_specs=[_ANY] * len(job.out_shapes),
                out_shape=job.out_shapes,
                scratch=[pltpu.SemaphoreType.DMA((job.n_sems,)), pltpu.SemaphoreType.DMA((job.n_sems,))],
                aliases={n_in + a: n_out + b for a, b in job.aliases})


def _job_refs(job, refs, n_in, n_out):
    if job is None:
        return refs[:n_in], refs[n_in:n_in + n_out], None
    ji, jo = len(job.ins), len(job.out_shapes)
    own_in, job_in = refs[:n_in], refs[n_in:n_in + ji]
    own_out, job_out = refs[n_in + ji:n_in + ji + n_out], refs[n_in + ji + n_out:n_in + ji + n_out + jo]
    send_sems, recv_sems = refs[n_in + ji + n_out + jo:]
    return own_in, own_out, (job_in, job_out, send_sems, recv_sems)


def _sb_fwd(proj, projb, d, blk, job=None):
    s = proj.shape[0]
    hps, wid = SB_FWD_HEADS, SB_FWD_HEADS * HEAD_DIM
    groups, nq = d // wid, s // blk
    qc, kc, vc, zc = d // wid, 2 * d // wid, 3 * d // wid, 4 * d // wid
    assert nq <= HEAD_DIM
    mid_step = (groups - 1, nq // 3)

    def body(*refs):
        (q_ref, k_ref, v_ref, zb_ref), (att_ref, yb_ref, rs_ref), jargs = _job_refs(job, refs, 4, 3)
        grp, i = pl.program_id(0), pl.program_id(1)
        if job is not None:
            @pl.when((grp == 0) & (i == 0))
            def _():
                job.start(*jargs)

            if job.mid is not None:
                @pl.when((grp == mid_step[0]) & (i == mid_step[1]))
                def _():
                    job.mid(*jargs)

        r_io = lax.broadcasted_iota(jnp.int32, (blk, blk), 0)
        c_io = lax.broadcasted_iota(jnp.int32, (blk, blk), 1)
        tri = (r_io >= c_io).astype(BF16)
        strict = c_io < r_io
        lane = lax.broadcasted_iota(jnp.int32, (blk, HEAD_DIM), 1)
        heads = [slice(n * HEAD_DIM, (n + 1) * HEAD_DIM) for n in range(hps)]
        qs = [q_ref[:, hd] for hd in heads]

        def block(j, carry, masked):
            ks = pl.multiple_of(j * blk, blk)
            rng = range(hps)
            zs = [lax.dot_general(qs[n], k_ref[pl.ds(ks, blk), heads[n]], (NT, ((), ())),
                                  preferred_element_type=F32) for n in rng]
            splits = []
            for n in rng:
                lk, _ = _log_keep(zs[n])
                if masked:
                    lk = jnp.where(strict, lk, 0.0)
                splits.append(_split_bf16(lk))
            csums = [carry[n][0] + jnp.dot(splits[n][0], tri, preferred_element_type=F32)
                     + jnp.dot(splits[n][1], tri, preferred_element_type=F32) for n in rng]
            probs = []
            for n in rng:
                a = jnp.exp(zs[n] + csums[n])
                if masked:
                    a = jnp.where(strict, a, 0.0)
                probs.append(a.astype(BF16))
            out = []
            for n in rng:
                run, acc, rs = carry[n]
                acc = acc + jnp.dot(probs[n], v_ref[pl.ds(ks, blk), heads[n]], preferred_element_type=F32)
                out.append((csums[n][:, 0:1], acc, jnp.where(lane == j, run, rs)))
            return tuple(out)

        zero = (jnp.zeros((blk, 1), F32), jnp.zeros((blk, HEAD_DIM), F32), jnp.zeros((blk, HEAD_DIM), F32))
        carry = block(i, (zero,) * hps, True)
        carry = lax.fori_loop(0, i, lambda jj, cr: block(i - 1 - jj, cr, False), carry)
        for hd, (_, acc, rs) in zip(heads, carry):
            att_ref[:, hd] = acc
            rs_ref[:, hd] = rs
            zb = zb_ref[:, hd]
            yb_ref[:, hd] = (acc * (zb * _sigmoid(zb))).astype(BF16)

        if job is not None:
            @pl.when((grp == groups - 1) & (i == nq - 1))
            def _():
                job.finish(*jargs)

    blk_spec = pl.BlockSpec((blk, wid), lambda h, i: (i, h))
    hosted = _host(job, 4, 3)
    res = pl.pallas_call(
        body, name="sb_fwd", grid=(groups, nq),
        in_specs=[pl.BlockSpec((blk, wid), lambda h, i: (i, qc + h)),
                  pl.BlockSpec((s, wid), lambda h, i: (0, kc + h)),
                  pl.BlockSpec((s, wid), lambda h, i: (0, vc + h)),
                  pl.BlockSpec((blk, wid), lambda h, i: (i, zc + h))] + hosted["in_specs"],
        out_specs=[blk_spec, blk_spec, blk_spec] + hosted["out_specs"],
        out_shape=[jax.ShapeDtypeStruct((s, d), F32), jax.ShapeDtypeStruct((s, d), BF16),
                   jax.ShapeDtypeStruct((s, d), F32)] + hosted["out_shape"],
        scratch_shapes=hosted["scratch"], input_output_aliases=hosted["aliases"],
        compiler_params=_params(("arbitrary", "arbitrary")),
    )(projb, projb, projb, proj, *hosted["ins"])
    return res[0], res[1], res[2], list(res[3:])


def _sb_bwd(proj, projb, att, dyb, rsave, d, blk, job=None):
    s = proj.shape[0]
    hps, wid = SB_BWD_HEADS, SB_BWD_HEADS * HEAD_DIM
    groups, nq = d // wid, s // blk
    qc, kc, vc, zc = d // wid, 2 * d // wid, 3 * d // wid, 4 * d // wid
    scale = HEAD_DIM ** -0.5

    def body(*refs):
        ((q_ref, k_ref, v_ref, zb_ref, att_ref, dyb_ref, rs_ref), (dq_ref, dk_ref, dv_ref, dzb_ref),
         jargs) = _job_refs(job, refs, 7, 4)
        grp, i = pl.program_id(0), pl.program_id(1)
        if job is not None:
            @pl.when((grp == 0) & (i == 0))
            def _():
                job.start(*jargs)

        @pl.when(i == 0)
        def _():
            dk_ref[...] = jnp.zeros_like(dk_ref)
            dv_ref[...] = jnp.zeros_like(dv_ref)

        r_io = lax.broadcasted_iota(jnp.int32, (blk, blk), 0)
        c_io = lax.broadcasted_iota(jnp.int32, (blk, blk), 1)
        tri = (r_io >= c_io).astype(BF16)
        tri_up = (r_io <= c_io).astype(BF16)
        strict = c_io < r_io
        lane = lax.broadcasted_iota(jnp.int32, (blk, HEAD_DIM), 1)
        heads = [slice(n * HEAD_DIM, (n + 1) * HEAD_DIM) for n in range(hps)]
        qs, dos, rss = [], [], []
        for hd in heads:
            qs.append(q_ref[:, hd])
            zb, dyb_t = zb_ref[:, hd], dyb_ref[:, hd]
            sg = _sigmoid(zb)
            dzb_ref[:, hd] = dyb_t * att_ref[:, hd] * _dsilu(zb, sg)
            dos.append((dyb_t * (zb * sg)).astype(BF16))
            rss.append(rs_ref[:, hd])

        def block(j, carry, masked):
            ks = pl.multiple_of(j * blk, blk)
            rng = range(hps)
            kbs = [k_ref[pl.ds(ks, blk), heads[n]] for n in rng]
            vbs = [v_ref[pl.ds(ks, blk), heads[n]] for n in rng]
            zs = [lax.dot_general(qs[n], kbs[n], (NT, ((), ())), preferred_element_type=F32) for n in rng]
            das = [lax.dot_general(dos[n], vbs[n], (NT, ((), ())), preferred_element_type=F32) for n in rng]
            splits, betas = [], []
            for n in rng:
                lk, e = _log_keep(zs[n])
                betas.append(jnp.where(zs[n] >= 0, 1.0, e) / (1.0 + e))
                if masked:
                    lk = jnp.where(strict, lk, 0.0)
                splits.append(_split_bf16(lk))
            csums = []
            for n in rng:
                run = jnp.sum(jnp.where(lane == j, rss[n], 0.0), axis=1, keepdims=True)
                csums.append(run + jnp.dot(splits[n][0], tri, preferred_element_type=F32)
                             + jnp.dot(splits[n][1], tri, preferred_element_type=F32))
            probs, gs = [], []
            for n in rng:
                a = jnp.exp(zs[n] + csums[n])
                if masked:
                    a = jnp.where(strict, a, 0.0)
                probs.append(a.astype(BF16))
                gs.append(a * das[n])
            gcums = [carry[n][0] + jnp.dot(gs[n].astype(BF16), tri_up, preferred_element_type=F32) for n in rng]
            for n in rng:
                dv_ref[pl.ds(ks, blk), heads[n]] += lax.dot_general(probs[n], dos[n], (TN, ((), ())),
                                                                   preferred_element_type=F32)
            dzs = []
            for n in rng:
                dz = gs[n] - betas[n] * gcums[n]
                if masked:
                    dz = jnp.where(strict, dz, 0.0)
                dzs.append(dz.astype(BF16))
            out = []
            for n in rng:
                dq = carry[n][1] + jnp.dot(dzs[n], kbs[n], preferred_element_type=F32)
                dk_ref[pl.ds(ks, blk), heads[n]] += lax.dot_general(dzs[n], qs[n], (TN, ((), ())),
                                                                   preferred_element_type=F32)
                out.append((gcums[n][:, blk - 1:blk], dq))
            return tuple(out)

        zero = (jnp.zeros((blk, 1), F32), jnp.zeros((blk, HEAD_DIM), F32))
        carry = lax.fori_loop(0, i, lambda j, cr: block(j, cr, False), (zero,) * hps)
        carry = block(i, carry, True)
        for hd, (_, dq) in zip(heads, carry):
            dq_ref[:, hd] = dq * scale

        if job is not None:
            @pl.when((grp == groups - 1) & (i == nq - 1))
            def _():
                job.finish(*jargs)

    blk_spec = pl.BlockSpec((blk, wid), lambda h, i: (i, h))
    full_spec = pl.BlockSpec((s, wid), lambda h, i: (0, h), pipeline_mode=pl.Buffered(1))
    act = jax.ShapeDtypeStruct((s, d), F32)
    hosted = _host(job, 7, 4)
    res = pl.pallas_call(
        body, name="sb_bwd", grid=(groups, nq),
        in_specs=[pl.BlockSpec((blk, wid), lambda h, i: (i, qc + h)),
                  pl.BlockSpec((s, wid), lambda h, i: (0, kc + h)),
                  pl.BlockSpec((s, wid), lambda h, i: (0, vc + h)),
                  pl.BlockSpec((blk, wid), lambda h, i: (i, zc + h)),
                  blk_spec, blk_spec, blk_spec] + hosted["in_specs"],
        out_specs=[blk_spec, full_spec, full_spec, blk_spec] + hosted["out_specs"],
        out_shape=[act, act, act, act] + hosted["out_shape"],
        scratch_shapes=hosted["scratch"], input_output_aliases=hosted["aliases"],
        compiler_params=_params(("arbitrary", "arbitrary")),
    )(projb, projb, projb, proj, att, dyb, rsave, *hosted["ins"])
    return res[0], res[1], res[2], res[3], list(res[4:])


def _adamw(w, g, m, v):
    shape = w.shape
    cols = shape[-1]
    rows = w.size // cols
    w2, g2, m2, v2 = (t.reshape(rows, cols) for t in (w, g, m, v))
    tr, tc = _tile(rows, 512, 8), _tile(cols, 1024, 128)
    c1 = 1.0 - ADAM_B1 ** ADAM_STEP
    c2 = 1.0 - ADAM_B2 ** ADAM_STEP

    def body(w_ref, g_ref, m_ref, v_ref, d_ref, nm_ref, nv_ref):
        gv = g_ref[...]
        nm = ADAM_B1 * m_ref[...] + (1.0 - ADAM_B1) * gv
        nv = ADAM_B2 * v_ref[...] + (1.0 - ADAM_B2) * (gv * gv)
        d_ref[...] = -ADAM_LR * ((nm / c1) / (jnp.sqrt(nv / c2) + ADAM_EPS) + ADAM_WD * w_ref[...])
        nm_ref[...] = nm
        nv_ref[...] = nv

    spec = pl.BlockSpec((tr, tc), lambda i, j: (i, j))
    sd = jax.ShapeDtypeStruct((rows, cols), F32)
    outs = pl.pallas_call(
        body, name="adamw", grid=(rows // tr, cols // tc),
        in_specs=[spec] * 4, out_specs=[spec] * 3, out_shape=[sd] * 3,
        compiler_params=_params(("parallel", "parallel")),
    )(w2, g2, m2, v2)
    return tuple(o.reshape(shape) for o in outs)


def _sum_blocks(gathered, n, rows):
    width = gathered.shape[1]
    tw = _tile(width, 8192)

    def body(g_ref, o_ref):
        acc = g_ref[0:rows, :]
        for b in range(1, n):
            acc = acc + g_ref[b * rows:(b + 1) * rows, :]
        o_ref[...] = acc

    return pl.pallas_call(
        body, name="sum_blocks", grid=(width // tw,),
        in_specs=[pl.BlockSpec((n * rows, tw), lambda i: (0, i))],
        out_specs=pl.BlockSpec((rows, tw), lambda i: (0, i)),
        out_shape=jax.ShapeDtypeStruct((rows, width), F32),
        compiler_params=_params(("parallel",)),
    )(gathered)


def _silu_bf16(c_rows):
    def body(c_ref, o_ref):
        cv = c_ref[...]
        o_ref[...] = (cv * _sigmoid(cv)).astype(BF16)

    return pl.pallas_call(
        body, name="silu_c", out_shape=jax.ShapeDtypeStruct(c_rows.shape, BF16),
        in_specs=[pl.BlockSpec(memory_space=pltpu.VMEM)], out_specs=pl.BlockSpec(memory_space=pltpu.VMEM),
    )(c_rows)


def _place():
    x, y, c = lax.axis_index("x"), lax.axis_index("y"), lax.axis_index("c")
    chips = [(1 - x, y), (x, 1 - y), (1 - x, 1 - y)]
    return x, y, c, chips


def _all_gather_small(block, name):
    m_per, n = block.shape

    def body(x_ref, out_ref, send_sems, recv_sems, local_sem):
        x, y, c, chips = _place()
        me, sibling = (x, y, c), (x, y, 1 - c)

        def rows(px, py, pc):
            return out_ref.at[pl.ds((4 * px + 2 * py + pc) * m_per, m_per), :]

        def copy(k, blk, to, src=None):
            return pltpu.make_async_remote_copy(
                src_ref=rows(*blk) if src is None else src, dst_ref=rows(*blk),
                send_sem=send_sems.at[k], recv_sem=recv_sems.at[k], device_id=to, device_id_type=MESH)

        mine = pltpu.make_async_copy(x_ref, rows(*me), local_sem)
        mine.start()
        first = [copy(0, me, sibling, src=x_ref)]
        first += [copy(1 + j, me, (*chip, c), src=x_ref) for j, chip in enumerate(chips)]
        for cp in first:
            cp.start()
        passed = [copy(4 + j, (*chip, c), sibling) for j, chip in enumerate(chips)]
        for j, chip in enumerate(chips):
            copy(1 + j, (*chip, c), me).wait_recv()
            passed[j].start()
        copy(0, sibling, me).wait_recv()
        for j, chip in enumerate(chips):
            copy(4 + j, (*chip, 1 - c), me).wait_recv()
        for cp in first + passed:
            cp.wait_send()
        mine.wait()

    return pl.pallas_call(
        body, name=name, out_shape=jax.ShapeDtypeStruct((N_DEV * m_per, n), block.dtype),
        in_specs=[pl.BlockSpec(memory_space=pltpu.VMEM)], out_specs=pl.BlockSpec(memory_space=pltpu.VMEM),
        scratch_shapes=[pltpu.SemaphoreType.DMA((7,)), pltpu.SemaphoreType.DMA((7,)), pltpu.SemaphoreType.DMA],
        compiler_params=pltpu.CompilerParams(vmem_limit_bytes=V7X_VMEM_LIMIT),
    )(block)


_ANY = pl.BlockSpec(memory_space=pl.ANY)


def _place_shard(w, chip_idx):
    layers, r, cols = w.shape
    tr, tc = _tile(r, 512, 16), _tile(cols, 1024)

    def body(j_ref, w_ref, o_ref):
        o_ref[...] = w_ref[...].astype(BF16)

    return pl.pallas_call(
        body, name="place_shard",
        grid_spec=pltpu.PrefetchScalarGridSpec(
            num_scalar_prefetch=1, grid=(layers, r // tr, cols // tc),
            in_specs=[pl.BlockSpec((None, tr, tc), lambda l, i, n, j: (l, i, n))],
            out_specs=pl.BlockSpec((None, None, tr, tc), lambda l, i, n, j: (j[0], l, i, n))),
        out_shape=jax.ShapeDtypeStruct((N_CHIPS, layers, r, cols), BF16),
        compiler_params=_params(("parallel", "parallel", "parallel")),
    )(chip_idx, w)


def _gather_job(bufs, layer):
    n = len(bufs)

    def tools(outs, send_sems, recv_sems):
        x, y, c, chips = _place()

        def half(t, chip_idx, hc):
            h = outs[t].shape[2] // 2
            return outs[t].at[chip_idx, layer, pl.ds(hc * h, h), :]

        def copy(t, k, ref, to):
            return pltpu.make_async_remote_copy(src_ref=ref, dst_ref=ref, send_sem=send_sems.at[6 * t + k],
                                                recv_sem=recv_sems.at[6 * t + k], device_id=to, device_id_type=MESH)

        return x, y, c, chips, half, copy

    def start(ins, outs, send_sems, recv_sems):
        x, y, c, chips, half, copy = tools(outs, send_sems, recv_sems)
        for t in range(n):
            for k, chip in enumerate(chips):
                copy(t, k, half(t, 2 * x + y, c), (*chip, c)).start()

    def mid(ins, outs, send_sems, recv_sems):
        x, y, c, chips, half, copy = tools(outs, send_sems, recv_sems)
        for t in range(n):
            for k, (cx, cy) in enumerate(chips):
                landed = half(t, 2 * cx + cy, c)
                copy(t, k, landed, (cx, cy, c)).wait_recv()
                copy(t, 3 + k, landed, (x, y, 1 - c)).start()

    def finish(ins, outs, send_sems, recv_sems):
        x, y, c, chips, half, copy = tools(outs, send_sems, recv_sems)
        for t in range(n):
            for k, (cx, cy) in enumerate(chips):
                copy(t, 3 + k, half(t, 2 * cx + cy, 1 - c), (x, y, 1 - c)).wait_recv()
        for t in range(n):
            for k, (cx, cy) in enumerate(chips):
                copy(t, k, half(t, 2 * x + y, c), (cx, cy, c)).wait_send()
                copy(t, 3 + k, half(t, 2 * cx + cy, c), (x, y, 1 - c)).wait_send()

    return _Job(ins=bufs, out_shapes=[jax.ShapeDtypeStruct(b.shape, b.dtype) for b in bufs],
                aliases=[(t, t) for t in range(n)], n_sems=6 * n, start=start, mid=mid, finish=finish)


def _send_job(parts):
    n = len(parts)

    def copies(ins, outs, send_sems, recv_sems):
        x, y, c, chips = _place()
        return [pltpu.make_async_remote_copy(
            src_ref=ins[t].at[2 * cx + cy], dst_ref=outs[t].at[k], send_sem=send_sems.at[3 * t + k],
            recv_sem=recv_sems.at[3 * t + k], device_id=(cx, cy, c), device_id_type=MESH)
            for t in range(n) for k, (cx, cy) in enumerate(chips)]

    def start(*args):
        for cp in copies(*args):
            cp.start()

    def finish(*args):
        for cp in copies(*args):
            cp.wait()

    return _Job(ins=parts, out_shapes=[jax.ShapeDtypeStruct((3,) + p.shape[1:], p.dtype) for p in parts],
                aliases=[], n_sems=3 * n, start=start, finish=finish)


def _run_job(job, name):
    def body(*refs):
        _, _, jargs = _job_refs(job, refs, 0, 0)
        job.start(*jargs)
        if job.mid is not None:
            job.mid(*jargs)
        job.finish(*jargs)

    hosted = _host(job, 0, 0)
    return list(pl.pallas_call(
        body, name=name, out_shape=hosted["out_shape"], in_specs=hosted["in_specs"], out_specs=hosted["out_specs"],
        scratch_shapes=hosted["scratch"], input_output_aliases=hosted["aliases"],
    )(*hosted["ins"]))


def _swap_halves(grads):
    n = len(grads)

    def body(*refs):
        ins, outs = refs[:n], refs[n:2 * n]
        send_sems, recv_sems = refs[2 * n:]
        x, y, c, _ = _place()
        copies = []
        for t in range(n):
            h = ins[t].shape[1] // 2
            copies.append(pltpu.make_async_remote_copy(
                src_ref=ins[t].at[:, pl.ds((1 - c) * h, h), :], dst_ref=outs[t],
                send_sem=send_sems.at[t], recv_sem=recv_sems.at[t], device_id=(x, y, 1 - c), device_id_type=MESH))
            copies[-1].start()
        for cp in copies:
            cp.wait()

    return pl.pallas_call(
        body, name="swap_halves",
        out_shape=[jax.ShapeDtypeStruct((g.shape[0], g.shape[1] // 2, g.shape[2]), g.dtype) for g in grads],
        in_specs=[_ANY] * n, out_specs=[_ANY] * n,
        scratch_shapes=[pltpu.SemaphoreType.DMA((n,)), pltpu.SemaphoreType.DMA((n,))],
    )(*grads)


def _share_halves(bufs):
    n = len(bufs)

    def body(*refs):
        outs = refs[n:2 * n]
        send_sems, recv_sems = refs[2 * n:]
        x, y, c, _ = _place()

        def half(t, hc):
            h = outs[t].shape[1] // 2
            return outs[t].at[:, pl.ds(hc * h, h), :]

        def copy(t, ref):
            return pltpu.make_async_remote_copy(src_ref=ref, dst_ref=ref, send_sem=send_sems.at[t],
                                                recv_sem=recv_sems.at[t], device_id=(x, y, 1 - c),
                                                device_id_type=MESH)

        sends = [copy(t, half(t, c)) for t in range(n)]
        for cp in sends:
            cp.start()
        for t in range(n):
            copy(t, half(t, 1 - c)).wait_recv()
        for cp in sends:
            cp.wait_send()

    return pl.pallas_call(
        body, name="share_halves",
        out_shape=[jax.ShapeDtypeStruct(b.shape, b.dtype) for b in bufs],
        in_specs=[_ANY] * n, out_specs=[_ANY] * n, input_output_aliases={t: t for t in range(n)},
        scratch_shapes=[pltpu.SemaphoreType.DMA((n,)), pltpu.SemaphoreType.DMA((n,))],
    )(*bufs)


def _sum_sibling(grad, recv, c_idx):
    _, r, cols = grad.shape
    h = r // 2
    tr, tc = _tile(h, 512, 16), _tile(cols, 1024)
    per = h // tr

    def body(c_ref, g_ref, r_ref, o_ref):
        o_ref[...] = (g_ref[...].astype(F32) + r_ref[...].astype(F32)).astype(BF16)

    return pl.pallas_call(
        body, name="sum_sibling",
        grid_spec=pltpu.PrefetchScalarGridSpec(
            num_scalar_prefetch=1, grid=(N_CHIPS, per, cols // tc),
            in_specs=[pl.BlockSpec((None, tr, tc), lambda j, i, n, c: (j, c[0] * per + i, n)),
                      pl.BlockSpec((None, tr, tc), lambda j, i, n, c: (j, i, n))],
            out_specs=pl.BlockSpec((None, tr, tc), lambda j, i, n, c: (j, i, n))),
        out_shape=jax.ShapeDtypeStruct((N_CHIPS, h, cols), BF16),
        compiler_params=_params(("parallel", "parallel", "parallel")),
    )(c_idx, grad, recv)


def _sum_owner(parts, recv, place_idx, layer, layers, prev=None):
    _, h, cols = parts.shape
    tr, tc = _tile(h, 512, 16), _tile(cols, 1024)
    per = h // tr

    def body(idx_ref, p_ref, r_ref, *rest):
        o_ref = rest[-1]
        o_ref[...] = (p_ref[...].astype(F32) + r_ref[0].astype(F32) + r_ref[1].astype(F32)
                      + r_ref[2].astype(F32))

    in_specs = [pl.BlockSpec((None, tr, tc), lambda i, n, idx: (idx[0], i, n)),
                pl.BlockSpec((3, tr, tc), lambda i, n, idx: (0, i, n))]
    args = [place_idx, parts, recv]
    aliases = {}
    if prev is not None:
        in_specs.append(_ANY)
        args.append(prev)
        aliases = {3: 0}
    return pl.pallas_call(
        body, name="sum_owner",
        grid_spec=pltpu.PrefetchScalarGridSpec(
            num_scalar_prefetch=1, grid=(per, cols // tc), in_specs=in_specs,
            out_specs=pl.BlockSpec((None, tr, tc), lambda i, n, idx: (layer, idx[1] * per + i, n))),
        out_shape=jax.ShapeDtypeStruct((layers, 2 * h, cols), F32),
        input_output_aliases=aliases,
        compiler_params=_params(("parallel", "parallel")),
    )(*args)


def _layer_fwd(x, mod, wbuf, p, layer, cfg, make_jobs):
    d, tr, tm = cfg["d"], cfg["tr"], cfg["tm"]
    s = x.shape[0]
    weight = lambda n: _Sharded(wbuf[n], BIG_KIND[n], layer=layer)
    hosted = lambda key: make_jobs[key](wbuf) if key in make_jobs else (None, ())
    shift, scale, rg = mod[:, 0:d], mod[:, d:2 * d], mod[:, 2 * d:3 * d]
    h, h_t = _norm_fwd(x, p["norm_g"], scale, shift, tr)
    tn_in = cfg["tn_in"]
    q_lo, q_hi, q_scale = d // tn_in, 2 * d // tn_in, HEAD_DIM ** -0.5
    blk_in = pl.BlockSpec((tm, tn_in), lambda i, j, k: (i, j))

    def proj_out(col, acc):
        return acc, acc * jnp.where((col >= q_lo) & (col < q_hi), q_scale, 1.0)

    job, names = hosted("proj")
    proj, projb, *job_out = _mm_nn("proj", h, weight("w_in"), tm=tm, tn=tn_in, tk=d, with_col=True,
                                   epilogue=proj_out, job=job,
                                   outs=[(jax.ShapeDtypeStruct((s, 7 * d), F32), blk_in),
                                         (jax.ShapeDtypeStruct((s, 7 * d), BF16), blk_in)])
    wbuf.update(zip(names, job_out))
    gates, = _mm_nn("gates", h, weight("w_gate"), tm=tm, tn=cfg["tn_gate"], tk=d,
                    extras=[(p["b_gate"], pl.BlockSpec((1, cfg["tn_gate"]), lambda i, j, k: (0, j)))],
                    epilogue=lambda acc, b: (_sigmoid(acc + b),))
    ya = _pool_fwd(proj, p["pool_w"], p["pool_scale"], d, tr)
    yc = _conv_fwd(proj, p["conv_w"], d, tr)
    job, names = hosted("attn")
    att, yb, rsave, job_out = _sb_fwd(proj, projb, d, cfg["blk"], job)
    wbuf.update(zip(names, job_out))
    pa, = _mm_nn("branch_a", ya, weight("w_br_a"), tm=tm, tn=cfg["tn_d"], tk=d // 2)
    pb, = _mm_nn("branch_b", yb, weight("w_br_b"), tm=tm, tn=cfg["tn_d"], tk=cfg["tk_row"])
    pc, = _mm_nn("branch_c", yc, weight("w_br_c"), tm=tm, tn=cfg["tn_d"], tk=d // 2)
    merged = _merge_fwd(gates, pa, pb, pc, cfg["tr_small"])
    tn = cfg["tn_d"]
    blk = pl.BlockSpec((tm, tn), lambda i, j, k: (i, j))
    sd = jax.ShapeDtypeStruct((s, d), F32)
    out, x_next = _mm_nn("out_proj", merged, weight("w_out"), tm=tm, tn=tn, tk=cfg["tk_row"],
                         outs=[(sd, blk), (sd, blk)],
                         extras=[(x, blk), (rg, pl.BlockSpec((1, tn), lambda i, j, k: (0, j)))],
                         epilogue=lambda acc, xv, g: (acc, xv + g * acc))
    saved = dict(x=x, h_t=h_t, proj=proj, projb=projb, gates=gates, ya=ya, yb=yb, yc=yc, att=att, rsave=rsave,
                 pa=pa, pb=pb, pc=pc, merged=merged, out=out, scale=scale, rg=rg)
    return x_next, saved


def _layer_bwd(dxo, sv, wbuf, p, layer, cfg, job=None):
    d, tr, tm, tk_s = cfg["d"], cfg["tr"], cfg["tm"], cfg["tk_s"]
    weight = lambda n: _Sharded(wbuf[n], BIG_KIND[n], layer=layer)
    dout, drg = _resid_bwd(dxo, sv["out"], sv["rg"], tr)
    dmerged = _mm_nt("d_merged", dout, weight("w_out"), tm=tm, tn=cfg["tn_row"], tk=d)
    g_out = _mm_tn("g_w_out", sv["merged"], dout, "row", tm=cfg["tn_row"], tn=cfg["tn_d"], tk=tk_s)
    dpa, dpb, dpc, dlogit, g_bgate = _merge_bwd(dmerged, sv["gates"], sv["pa"], sv["pb"], sv["pc"], cfg["tr_small"])
    dya = _mm_nt("d_ya", dpa, weight("w_br_a"), tm=tm, tn=cfg["tn_half"], tk=cfg["tn_d"])
    dyb = _mm_nt("d_yb", dpb, weight("w_br_b"), tm=tm, tn=cfg["tn_row"], tk=d)
    dyc = _mm_nt("d_yc", dpc, weight("w_br_c"), tm=tm, tn=cfg["tn_half"], tk=cfg["tn_d"])
    g_a = _mm_tn("g_w_br_a", sv["ya"], dpa, "col", tm=cfg["tn_half"], tn=cfg["tn_d"], tk=tk_s)
    g_b = _mm_tn("g_w_br_b", sv["yb"], dpb, "row", tm=cfg["tn_row"], tn=cfg["tn_d"], tk=tk_s)
    g_c = _mm_tn("g_w_br_c", sv["yc"], dpc, "col", tm=cfg["tn_half"], tn=cfg["tn_d"], tk=tk_s)
    d_a, g_pool_w, g_pool_scale = _pool_bwd(sv["proj"], dya, p["pool_w"], p["pool_scale"], d, tr)
    d_c, g_conv_w = _conv_bwd(sv["proj"], dyc, p["conv_w"], d, tr)
    dq, dk, dv, dzb, job_out = _sb_bwd(sv["proj"], sv["projb"], sv["att"], dyb, sv["rsave"], d, cfg["blk"], job)
    dproj = jnp.concatenate([d_a, dq.astype(BF16), dk.astype(BF16), dv.astype(BF16), dzb.astype(BF16), d_c], axis=1)
    dh_in = _mm_nt("d_h_in", dproj, weight("w_in"), tm=tm, tn=cfg["tn_d"], tk=cfg["tk_in"])
    tn = cfg["tn_d"]
    dh = _mm_nt("d_h_gate", dlogit, weight("w_gate"), tm=tm, tn=tn, tk=cfg["tk_gate"],
                extras=[(dh_in, pl.BlockSpec((tm, tn), lambda i, j, k: (i, j)))],
                epilogue=lambda acc, prev: (prev + acc,))
    g_in = _mm_grad("g_w_in", sv["h_t"], dproj, "col", tm=cfg["tm_g"], tn=cfg["tn_in"], tk=tk_s)
    g_gate = _mm_grad("g_w_gate", sv["h_t"], dlogit, "col", tm=cfg["tm_g"], tn=cfg["tn_gate"], tk=tk_s)
    dx, dshift, dscale, g_norm = _norm_bwd(dh, sv["x"], p["norm_g"], sv["scale"], dxo, tr)
    big = dict(w_in=g_in, w_gate=g_gate, w_br_a=g_a, w_br_b=g_b, w_br_c=g_c, w_out=g_out)
    small = dict(dmod=jnp.concatenate([dshift, dscale, drg], axis=1), norm_g=g_norm, pool_scale=g_pool_scale,
                 b_gate=g_bgate, conv_w=g_conv_w, pool_w=g_pool_w)
    return dx, big, small, job_out


BIG = ("w_in", "w_gate", "w_br_a", "w_br_b", "w_br_c", "w_out")
BIG_KIND = dict(w_in="col", w_gate="col", w_br_a="col", w_br_b="row", w_br_c="col", w_out="row")


def _pad_to(v, n):
    return jnp.pad(v, (0, n - v.shape[0]))


def kernel(x, c, norm_g, w_ada, b_ada, w_in, pool_w, pool_scale, conv_w, w_br_a, w_br_b, w_br_c, w_gate, b_gate, w_out, final_g, loss_target, m_norm_g, m_w_ada, m_b_ada, m_w_in, m_pool_w, m_pool_scale, m_conv_w, m_w_br_a, m_w_br_b, m_w_br_c, m_w_gate, m_b_gate, m_w_out, m_final_g, v_norm_g, v_w_ada, v_b_ada, v_w_in, v_pool_w, v_pool_scale, v_conv_w, v_w_br_a, v_w_br_b, v_w_br_c, v_w_gate, v_b_gate, v_w_out, v_final_g):
    weights = dict(norm_g=norm_g, w_ada=w_ada, b_ada=b_ada, w_in=w_in, pool_w=pool_w, pool_scale=pool_scale,
                   conv_w=conv_w, w_br_a=w_br_a, w_br_b=w_br_b, w_br_c=w_br_c, w_gate=w_gate, b_gate=b_gate,
                   w_out=w_out, final_g=final_g)
    mom_m = dict(norm_g=m_norm_g, w_ada=m_w_ada, b_ada=m_b_ada, w_in=m_w_in, pool_w=m_pool_w,
                 pool_scale=m_pool_scale, conv_w=m_conv_w, w_br_a=m_w_br_a, w_br_b=m_w_br_b, w_br_c=m_w_br_c,
                 w_gate=m_w_gate, b_gate=m_b_gate, w_out=m_w_out, final_g=m_final_g)
    mom_v = dict(norm_g=v_norm_g, w_ada=v_w_ada, b_ada=v_b_ada, w_in=v_w_in, pool_w=v_pool_w,
                 pool_scale=v_pool_scale, conv_w=v_conv_w, w_br_a=v_w_br_a, w_br_b=v_w_br_b, w_br_c=v_w_br_c,
                 w_gate=v_w_gate, b_gate=v_b_gate, w_out=v_w_out, final_g=v_final_g)
    names = list(weights)

    _, s, d = x.shape
    layers = norm_g.shape[0]
    pw, gd, cc = d // 2, d // 8, d // 2
    ada_cols = 3 * d // N_CHIPS
    xi, yi, ci = lax.axis_index("x"), lax.axis_index("y"), lax.axis_index("c")
    me = 4 * xi + 2 * yi + ci
    my_chip = 2 * xi + yi
    c_idx = jnp.reshape(ci, (1,)).astype(jnp.int32)
    chip_idx = jnp.reshape(my_chip, (1,)).astype(jnp.int32)

    cfg = dict(
        d=d, tr=_tile(s, 256, 16), tr_small=_tile(s, 128, 16), tm=_tile(s, 1024, 16), tk_s=_tile(s, 1024, 16),
        blk=_tile(s, min(256, max(s // 4, 16)), 16),
        tn_in=_tile(7 * d // N_CHIPS, 512), tk_in=_tile(7 * d // N_CHIPS, 1792),
        tn_gate=_tile(3 * d // N_CHIPS, 512), tk_gate=_tile(3 * d // N_CHIPS, 1536), tn_d=_tile(d // N_CHIPS, 512),
        tn_row=_tile(d // N_CHIPS, 512, 16), tk_row=_tile(d // N_CHIPS, 512, 16),
        tn_half=_tile(d // 2, 512), tm_g=_tile(d, 1024),
    )

    conv_flat = conv_w.reshape(-1)
    conv_len = -(-conv_flat.shape[0] // 1024) * 1024
    pack0 = jnp.concatenate([c.reshape(-1), _pad_to(conv_flat, conv_len), pool_w.reshape(-1)])
    w0 = -(-pack0.shape[0] // 1024) * 1024
    g0 = _all_gather_small(_pad_to(pack0, w0).reshape(8, w0 // 8), "gather_small").reshape(N_DEV, w0)
    c_all = g0[:, 0:d]
    chip_rows = g0[0::2]
    conv_full = jnp.concatenate(
        [chip_rows[j, d:d + conv_flat.shape[0]].reshape(conv_w.shape) for j in range(N_CHIPS)], axis=2)
    pool_full = jnp.concatenate(
        [chip_rows[j, d + conv_len:d + conv_len + pool_w.size].reshape(pool_w.shape) for j in range(N_CHIPS)],
        axis=2)
    pool_bf = pool_full.astype(BF16)

    sc16 = _silu_bf16(jnp.pad(c_all, ((0, 16 - N_DEV), (0, 0))))
    mods = []
    for l in range(layers):
        bias = lax.dynamic_slice(b_ada[l], (my_chip * ada_cols,), (ada_cols,)).reshape(1, ada_cols)
        tn = _tile(ada_cols, 512)
        mod_l, = _mm("mod", sc16, w_ada[l], dims=NN, grid=(1, ada_cols // tn, 1),
                     a_spec=pl.BlockSpec((16, d), lambda i, j, k: (0, 0)),
                     b_spec=pl.BlockSpec((d, tn), lambda i, j, k: (0, j)),
                     acc_shape=(16, tn),
                     outs=[(jax.ShapeDtypeStruct((16, ada_cols), F32), pl.BlockSpec((16, tn), lambda i, j, k: (0, j)))],
                     extras=[(bias, pl.BlockSpec((1, tn), lambda i, j, k: (0, j)))],
                     epilogue=lambda acc, b: (acc + b,))
        mods.append(mod_l[0:N_DEV])
    g1 = _all_gather_small(jnp.concatenate(mods, axis=1), "gather_mod")
    g1 = g1.reshape(N_CHIPS, 2, N_DEV, layers, ada_cols)[:, 0]
    mod_all = jnp.transpose(g1, (1, 2, 0, 3)).reshape(N_DEV, layers, 3 * d)
    mod_me = lax.dynamic_slice(mod_all, (me, 0, 0), (1, layers, 3 * d))[0]

    wbuf = {n: _place_shard(weights[n], chip_idx) for n in BIG}
    wbuf["w_in"], = _run_job(_gather_job([wbuf["w_in"]], 0), "gather_w_in")
    params = [dict(norm_g=norm_g[l:l + 1], pool_scale=pool_scale[l:l + 1], b_gate=b_gate[l:l + 1],
                   conv_w=conv_full[l], pool_w=pool_bf[l]) for l in range(layers)]

    def gather(names, layer):
        return lambda wb: (_gather_job([wb[n] for n in names], layer), names)

    act = x[0]
    saved = []
    for l in range(layers):
        make_jobs = {}
        if l == 0:
            make_jobs["proj"] = gather([n for n in BIG if n != "w_in"], 0)
        if l + 1 < layers:
            make_jobs["attn"] = gather(BIG, l + 1)
        act, sv = _layer_fwd(act, mod_me[l:l + 1], wbuf, params[l], l, cfg, make_jobs)
        saved.append(sv)
    loss_part, dact, g_final = _final_loss(act, final_g.reshape(1, d), loss_target[0], cfg["tr"])
    loss = lax.psum(loss_part[0, 0], ("x", "y", "c"))

    small_grads, parts, theirs = [None] * layers, [None] * layers, [None] * layers
    for l in reversed(range(layers)):
        job = _send_job(parts[l + 1]) if l + 1 < layers else None
        dact, big, small_grads[l], job_out = _layer_bwd(dact, saved[l], wbuf, params[l], l, cfg, job)
        if job is not None:
            theirs[l + 1] = job_out
        flat = [big[n] for n in BIG]
        parts[l] = [_sum_sibling(g, r, c_idx) for g, r in zip(flat, _swap_halves(flat))]
    theirs[0] = _run_job(_send_job(parts[0]), "send_to_owners")
    grad_x = dact.reshape(x.shape)
    place_idx = jnp.stack([my_chip, ci]).astype(jnp.int32)
    owned = []
    for w in range(len(BIG)):
        buf = None
        for l in range(layers):
            buf = _sum_owner(parts[l][w], theirs[l][w], place_idx, l, layers, prev=buf)
        owned.append(buf)
    full = _share_halves(owned)
    grads = {n: f.reshape(weights[n].shape) for n, f in zip(BIG, full)}

    small_names = ("dmod", "norm_g", "pool_scale", "b_gate", "conv_w", "pool_w")
    pieces = [small_grads[l][n].reshape(-1) for n in small_names for l in range(layers)] + [g_final.reshape(-1)]
    pack1 = jnp.concatenate(pieces)
    w1 = -(-pack1.shape[0] // 1024) * 1024
    g2 = _all_gather_small(_pad_to(pack1, w1).reshape(8, w1 // 8), "gather_grads")
    total = _sum_blocks(g2, N_DEV, 8).reshape(-1)
    off = 0
    summed = {}
    for n in small_names:
        per = small_grads[0][n].size
        summed[n] = jnp.stack([total[off + l * per:off + (l + 1) * per].reshape(small_grads[0][n].shape)
                               for l in range(layers)])
        off += layers * per
    grads["final_g"] = total[off:off + d]
    grads["norm_g"] = summed["norm_g"].reshape(layers, d)
    grads["pool_scale"] = summed["pool_scale"].reshape(layers, pw)
    grads["b_gate"] = summed["b_gate"].reshape(layers, 3 * d)
    grads["b_ada"] = summed["dmod"].reshape(layers, 3 * d)
    cs = cc // N_CHIPS
    grads["conv_w"] = lax.dynamic_slice(summed["conv_w"], (0, 0, my_chip * cs), (layers, 3, cs))
    rs_ = gd // N_CHIPS
    grads["pool_w"] = lax.dynamic_slice(summed["pool_w"], (0, 0, my_chip * rs_, 0), (layers, N_GROUPS, rs_, gd))
    dmod_all = g2.reshape(N_DEV, w1)[:, 0:layers * 3 * d].reshape(N_DEV, layers, 3 * d)
    g_ada = []
    for l in range(layers):
        cols = lax.dynamic_slice(dmod_all[:, l], (0, my_chip * ada_cols), (N_DEV, ada_cols))
        cols16 = jnp.pad(cols, ((0, 16 - N_DEV), (0, 0)))
        tn = _tile(ada_cols, 512)
        tm = _tile(d, 1024)
        ga, = _mm("g_w_ada", sc16, cols16, dims=TN, grid=(d // tm, ada_cols // tn, 1),
                  a_spec=pl.BlockSpec((16, tm), lambda i, j, k: (0, i)),
                  b_spec=pl.BlockSpec((16, tn), lambda i, j, k: (0, j)),
                  acc_shape=(tm, tn),
                  outs=[(jax.ShapeDtypeStruct((d, ada_cols), F32), pl.BlockSpec((tm, tn), lambda i, j, k: (i, j)))])
        g_ada.append(ga)
    grads["w_ada"] = jnp.stack(g_ada)

    deltas, new_m, new_v = {}, {}, {}
    for n in names:
        deltas[n], new_m[n], new_v[n] = _adamw(weights[n], grads[n], mom_m[n], mom_v[n])
    return (loss, grad_x, *[grads[n] for n in names], *[deltas[n] for n in names],
            *[new_m[n] for n in names], *[new_v[n] for n in names])
```

```python
import functools

import jax
import jax.numpy as jnp
from jax import lax
from jax.experimental import pallas as pl
from jax.experimental.pallas import tpu as pltpu

F32 = jnp.float32
BF16 = jnp.bfloat16
MESH = pl.DeviceIdType.MESH

N_CHIPS = 4
N_DEV = 8
N_GROUPS = 4
POOL_WINDOWS = (2, 4, 8, 16)
POOL_HALO = 16
CONV_HALO = 8
HEAD_DIM = 128
SB_FWD_HEADS = 4
SB_BWD_HEADS = 4
RMS_EPS = 1e-6
ADAM_LR = 0.001
ADAM_B1 = 0.9
ADAM_B2 = 0.999
ADAM_EPS = 1e-08
ADAM_WD = 0.01
ADAM_STEP = 10
V7X_VMEM_LIMIT = 56 * 1024 * 1024


def _tile(n, pref, mult=128):
    best = None
    t = mult
    while t <= min(n, pref):
        if n % t == 0:
            best = t
        t += mult
    return n if best is None else best


def _params(sem=None):
    return pltpu.CompilerParams(dimension_semantics=sem, vmem_limit_bytes=V7X_VMEM_LIMIT)


def _sigmoid(z):
    return jax.nn.sigmoid(z)


def _dsilu(z, sg):
    return sg * (1.0 + z * (1.0 - sg))


NN = ((1,), (0,))
NT = ((1,), (1,))
TN = ((0,), (0,))


def _mm(name, a, b, *, dims, grid, a_spec, b_spec, acc_shape, outs, extras=(), epilogue=None, with_col=False,
        job=None):
    gi, gj, nk = grid
    ne, no = len(extras), len(outs)
    if epilogue is None:
        epilogue = lambda acc: (acc,)

    def body(*refs):
        own_in, orefs, jargs = _job_refs(job, refs[:-1], 2 + ne, no)
        a_ref, b_ref, ex, acc = own_in[0], own_in[1], own_in[2:], refs[-1]
        i, j, k = pl.program_id(0), pl.program_id(1), pl.program_id(2)
        lead = (j,) if with_col else ()
        if job is not None:
            @pl.when((i == 0) & (j == 0) & (k == 0))
            def _():
                job.start(*jargs)

            if job.mid is not None:
                @pl.when((i == gi - 1) & (j == gj // 2) & (k == 0))
                def _():
                    job.mid(*jargs)

        @pl.when(k == 0)
        def _():
            acc[...] = jnp.zeros_like(acc)

        acc[...] += lax.dot_general(a_ref[...].astype(BF16), b_ref[...].astype(BF16), (dims, ((), ())),
                                    preferred_element_type=F32)

        @pl.when(k == nk - 1)
        def _():
            vals = epilogue(*lead, acc[...], *[e[...] for e in ex])
            for o, v in zip(orefs, vals):
                o[...] = v.astype(o.dtype)

        if job is not None:
            @pl.when((i == gi - 1) & (j == gj - 1) & (k == nk - 1))
            def _():
                job.finish(*jargs)

    hosted = _host(job, 2 + ne, no)
    sem = ("parallel", "parallel", "arbitrary") if job is None else ("arbitrary",) * 3
    res = pl.pallas_call(
        body, name=name, grid=grid,
        in_specs=[a_spec, b_spec] + [s for _, s in extras] + hosted["in_specs"],
        out_specs=[s for _, s in outs] + hosted["out_specs"],
        out_shape=[sh for sh, _ in outs] + hosted["out_shape"],
        scratch_shapes=hosted["scratch"] + [pltpu.VMEM(acc_shape, F32)],
        input_output_aliases=hosted["aliases"],
        compiler_params=_params(sem),
    )(a, b, *[e for e, _ in extras], *hosted["ins"])
    return res


class _Sharded:
    def __init__(self, arr, kind, layer=None):
        self.arr, self.kind, self.layer = arr, kind, layer
        r, c = arr.shape[-2:]
        self.rows = r * (N_CHIPS if kind == "row" else 1)
        self.cols = c * (N_CHIPS if kind == "col" else 1)
        self.sr, self.sc = r, c

    def spec(self, br, bc, f):
        lead = (None,) if self.layer is None else (None, None)
        layer = self.layer
        if self.kind == "col":
            per = self.sc // bc
            assert per * bc == self.sc and self.sr % br == 0, (self.arr.shape, br, bc)

            def idx(*g):
                rb, cb = f(*g)
                return ((cb // per,) + (() if layer is None else (layer,)) + (rb, cb % per))
        else:
            per = self.sr // br
            assert per * br == self.sr and self.sc % bc == 0, (self.arr.shape, br, bc)

            def idx(*g):
                rb, cb = f(*g)
                return ((rb // per,) + (() if layer is None else (layer,)) + (rb % per, cb))
        return pl.BlockSpec(lead + (br, bc), idx)


def _grad_buffer(rows, cols, kind):
    if kind == "col":
        return jax.ShapeDtypeStruct((N_CHIPS, rows, cols // N_CHIPS), BF16)
    return jax.ShapeDtypeStruct((N_CHIPS, rows // N_CHIPS, cols), BF16)


def _mm_nn(name, a, w, *, tm, tn, tk, outs=None, extras=(), epilogue=None, out_dtype=F32, with_col=False,
           job=None):
    m, kdim = a.shape
    n = w.cols
    grid = (m // tm, n // tn, kdim // tk)
    if outs is None:
        outs = [(jax.ShapeDtypeStruct((m, n), out_dtype), pl.BlockSpec((tm, tn), lambda i, j, k: (i, j)))]
    return _mm(name, a, w.arr, dims=NN, grid=grid,
               a_spec=pl.BlockSpec((tm, tk), lambda i, j, k: (i, k)),
               b_spec=w.spec(tk, tn, lambda i, j, k: (k, j)),
               acc_shape=(tm, tn), outs=outs, extras=extras, epilogue=epilogue, with_col=with_col, job=job)


def _mm_grad(name, a_t, b, kind, *, tm, tn, tk):
    m, kdim = a_t.shape
    n = b.shape[1]
    out = _Sharded(_grad_buffer(m, n, kind), kind)
    return _mm(name, a_t, b, dims=NN, grid=(m // tm, n // tn, kdim // tk),
               a_spec=pl.BlockSpec((tm, tk), lambda i, j, k: (i, k)),
               b_spec=pl.BlockSpec((tk, tn), lambda i, j, k: (k, j)),
               acc_shape=(tm, tn), outs=[(out.arr, out.spec(tm, tn, lambda i, j, k: (i, j)))])[0]


def _mm_nt(name, a, w, *, tm, tn, tk, extras=(), epilogue=None, out_dtype=F32, job=None):
    m, kdim = a.shape
    n = w.rows
    grid = (m // tm, n // tn, kdim // tk)
    outs = [(jax.ShapeDtypeStruct((m, n), out_dtype), pl.BlockSpec((tm, tn), lambda i, j, k: (i, j)))]
    res = _mm(name, a, w.arr, dims=NT, grid=grid,
              a_spec=pl.BlockSpec((tm, tk), lambda i, j, k: (i, k)),
              b_spec=w.spec(tn, tk, lambda i, j, k: (j, k)),
              acc_shape=(tm, tn), outs=outs, extras=extras, epilogue=epilogue, job=job)
    return res[0] if job is None else (res[0], list(res[1:]))


def _mm_tn(name, a, b, kind, *, tm, tn, tk):
    kdim, m = a.shape
    n = b.shape[1]
    grid = (m // tm, n // tn, kdim // tk)
    out = _Sharded(_grad_buffer(m, n, kind), kind)
    outs = [(out.arr, out.spec(tm, tn, lambda i, j, k: (i, j)))]
    return _mm(name, a, b, dims=TN, grid=grid,
               a_spec=pl.BlockSpec((tk, tm), lambda i, j, k: (k, i)),
               b_spec=pl.BlockSpec((tk, tn), lambda i, j, k: (k, j)),
               acc_shape=(tm, tn), outs=outs)[0]


def _row_spec(tr, w, col=0):
    return pl.BlockSpec((tr, w), lambda i: (i, col))


def _vec_spec(w, col=0, rows=1):
    return pl.BlockSpec((rows, w), lambda i: (0, col))


def _norm_fwd(x, g, scale, shift, tr):
    s, d = x.shape

    def body(x_ref, g_ref, sc_ref, sh_ref, h_ref, ht_ref):
        xv = x_ref[...]
        r = lax.rsqrt(jnp.mean(xv * xv, axis=-1, keepdims=True) + RMS_EPS)
        y = xv * r * g_ref[...]
        hv = y * (1.0 + sc_ref[...]) + sh_ref[...]
        h_ref[...] = hv.astype(BF16)
        ht_ref[...] = hv.T.astype(BF16)

    return pl.pallas_call(
        body, name="norm_fwd", grid=(s // tr,),
        in_specs=[_row_spec(tr, d), _vec_spec(d), _vec_spec(d), _vec_spec(d)],
        out_specs=[_row_spec(tr, d), pl.BlockSpec((d, tr), lambda i: (0, i))],
        out_shape=[jax.ShapeDtypeStruct((s, d), BF16), jax.ShapeDtypeStruct((d, s), BF16)],
        compiler_params=_params(("parallel",)),
    )(x, g, scale, shift)


def _norm_bwd(dh, x, g, scale, dxo, tr):
    s, d = x.shape

    def body(dh_ref, x_ref, g_ref, sc_ref, dxo_ref, dx_ref, dsh_ref, dsc_ref, dg_ref):
        @pl.when(pl.program_id(0) == 0)
        def _():
            dsh_ref[...] = jnp.zeros_like(dsh_ref)
            dsc_ref[...] = jnp.zeros_like(dsc_ref)
            dg_ref[...] = jnp.zeros_like(dg_ref)

        xv, dhv, gv = x_ref[...], dh_ref[...], g_ref[...]
        r = lax.rsqrt(jnp.mean(xv * xv, axis=-1, keepdims=True) + RMS_EPS)
        xn = xv * r
        dsh_ref[...] += jnp.sum(dhv, axis=0, keepdims=True)
        dsc_ref[...] += jnp.sum(dhv * (xn * gv), axis=0, keepdims=True)
        dyg = dhv * (1.0 + sc_ref[...])
        dg_ref[...] += jnp.sum(dyg * xn, axis=0, keepdims=True)
        dxn = dyg * gv
        dx_ref[...] = dxo_ref[...] + r * (dxn - xn * jnp.mean(dxn * xn, axis=-1, keepdims=True))

    vec = jax.ShapeDtypeStruct((1, d), F32)
    return pl.pallas_call(
        body, name="norm_bwd", grid=(s // tr,),
        in_specs=[_row_spec(tr, d), _row_spec(tr, d), _vec_spec(d), _vec_spec(d), _row_spec(tr, d)],
        out_specs=[_row_spec(tr, d), _vec_spec(d), _vec_spec(d), _vec_spec(d)],
        out_shape=[jax.ShapeDtypeStruct((s, d), F32), vec, vec, vec],
        compiler_params=_params(("arbitrary",)),
    )(dh, x, g, scale, dxo)


def _final_loss(x, g, target, tr):
    s, d = x.shape

    def body(x_ref, g_ref, t_ref, loss_ref, dx_ref, dg_ref):
        @pl.when(pl.program_id(0) == 0)
        def _():
            loss_ref[...] = jnp.zeros_like(loss_ref)
            dg_ref[...] = jnp.zeros_like(dg_ref)

        xv, gv = x_ref[...], g_ref[...]
        r = lax.rsqrt(jnp.mean(xv * xv, axis=-1, keepdims=True) + RMS_EPS)
        xn = xv * r
        err = xn * gv - t_ref[...]
        per_row = jnp.mean(err * err, axis=-1, keepdims=True)
        loss_ref[...] += 0.5 * jnp.sum(per_row, axis=0, keepdims=True)
        dy = err * (1.0 / d)
        dg_ref[...] += jnp.sum(dy * xn, axis=0, keepdims=True)
        dxn = dy * gv
        dx_ref[...] = r * (dxn - xn * jnp.mean(dxn * xn, axis=-1, keepdims=True))

    return pl.pallas_call(
        body, name="final_loss", grid=(s // tr,),
        in_specs=[_row_spec(tr, d), _vec_spec(d), _row_spec(tr, d)],
        out_specs=[pl.BlockSpec((8, 128), lambda i: (0, 0)), _row_spec(tr, d), _vec_spec(d)],
        out_shape=[jax.ShapeDtypeStruct((8, 128), F32), jax.ShapeDtypeStruct((s, d), F32),
                   jax.ShapeDtypeStruct((1, d), F32)],
        compiler_params=_params(("arbitrary",)),
    )(x, g, target)


def _halo_before(tr, halo, w, col):
    per = tr // halo
    return pl.BlockSpec((halo, w), lambda i: (jnp.maximum(i * per - 1, 0), col))


def _halo_after(tr, halo, w, col, n_tiles):
    per = tr // halo
    return pl.BlockSpec((halo, w), lambda i: (jnp.minimum((i + 1) * per, n_tiles * per - 1), col))


def _pool_fwd(proj, pool_w, pool_scale, d, tr):
    s = proj.shape[0]
    pw, gd, hl = d // 2, d // 8, POOL_HALO

    def body(xa_ref, xh_ref, za_ref, w_ref, ps_ref, ya_ref, buf):
        i = pl.program_id(0)
        buf[0:hl, :] = jnp.where(i > 0, xh_ref[...], 0.0)
        buf[hl:hl + tr, :] = xa_ref[...]
        row = i * tr + lax.broadcasted_iota(jnp.int32, (tr, 1), 0)
        za = za_ref[...]
        gate = za * _sigmoid(za)
        for g, win in enumerate(POOL_WINDOWS):
            cs = slice(g * gd, (g + 1) * gd)
            xg = buf[hl:hl + tr, cs]
            acc = xg
            for j in range(1, win):
                acc = acc + buf[hl - j:hl - j + tr, cs]
            cnt = jnp.minimum(row + 1, win).astype(F32)
            mixed = acc / cnt - xg
            y = jnp.dot(mixed.astype(BF16), w_ref[g], preferred_element_type=F32)
            ya_ref[:, cs] = ((y * ps_ref[:, cs]) * gate[:, cs]).astype(BF16)

    return pl.pallas_call(
        body, name="pool_fwd", grid=(s // tr,),
        in_specs=[_row_spec(tr, pw, 0), _halo_before(tr, hl, pw, 0), _row_spec(tr, pw, 1),
                  pl.BlockSpec((N_GROUPS, gd, gd), lambda i: (0, 0, 0)), _vec_spec(pw)],
        out_specs=_row_spec(tr, pw), out_shape=jax.ShapeDtypeStruct((s, pw), BF16),
        scratch_shapes=[pltpu.VMEM((tr + hl, pw), F32)],
        compiler_params=_params(("parallel",)),
    )(proj, proj, proj, pool_w, pool_scale)


def _pool_bwd(proj, dya, pool_w, pool_scale, d, tr):
    s = proj.shape[0]
    pw, gd, hl = d // 2, d // 8, POOL_HALO
    n_tiles = s // tr

    def body(xa_ref, xh_ref, za_ref, zh_ref, dya_ref, dyh_ref, w_ref, ps_ref, da_ref, gw_ref, gs_ref, buf, dbuf):
        i = pl.program_id(0)

        @pl.when(i == 0)
        def _():
            gw_ref[...] = jnp.zeros_like(gw_ref)
            gs_ref[...] = jnp.zeros_like(gs_ref)

        buf[0:hl, :] = jnp.where(i > 0, xh_ref[...], 0.0)
        buf[hl:hl + tr, :] = xa_ref[...]
        row = i * tr + lax.broadcasted_iota(jnp.int32, (tr, 1), 0)
        row_h = (i + 1) * tr + lax.broadcasted_iota(jnp.int32, (hl, 1), 0)
        za, dya_t = za_ref[...], dya_ref[...]
        sg = _sigmoid(za)
        gate = za * sg
        dpre = dya_t * gate
        zh = zh_ref[...]
        dpre_h = jnp.where(i < n_tiles - 1, dyh_ref[...], 0.0) * (zh * _sigmoid(zh))
        ps = ps_ref[...]
        for g, win in enumerate(POOL_WINDOWS):
            cs = slice(g * gd, (g + 1) * gd)
            wg = w_ref[g]
            xg = buf[hl:hl + tr, cs]
            acc = xg
            for j in range(1, win):
                acc = acc + buf[hl - j:hl - j + tr, cs]
            cnt = jnp.minimum(row + 1, win).astype(F32)
            mixed = (acc / cnt - xg).astype(BF16)
            ylin = jnp.dot(mixed, wg, preferred_element_type=F32)
            gs_ref[:, cs] += jnp.sum(dpre[:, cs] * ylin, axis=0, keepdims=True)
            da_ref[:, pw + g * gd:pw + (g + 1) * gd] = (
                dya_t[:, cs] * (ylin * ps[:, cs]) * _dsilu(za[:, cs], sg[:, cs])).astype(BF16)
            dyl = (dpre[:, cs] * ps[:, cs]).astype(BF16)
            gw_ref[g] += lax.dot_general(mixed, dyl, (TN, ((), ())), preferred_element_type=F32)
            dmix = lax.dot_general(dyl, wg, (NT, ((), ())), preferred_element_type=F32)
            dyl_h = (dpre_h[:, cs] * ps[:, cs]).astype(BF16)
            dmix_h = lax.dot_general(dyl_h, wg, (NT, ((), ())), preferred_element_type=F32)
            cnt_h = jnp.minimum(row_h + 1, win).astype(F32)
            dbuf[0:tr, cs] = dmix / cnt
            dbuf[tr:tr + hl, cs] = dmix_h / cnt_h
            dx = dbuf[0:tr, cs] - dmix
            for j in range(1, win):
                dx = dx + dbuf[j:j + tr, cs]
            da_ref[:, cs] = dx.astype(BF16)

    return pl.pallas_call(
        body, name="pool_bwd", grid=(n_tiles,),
        in_specs=[_row_spec(tr, pw, 0), _halo_before(tr, hl, pw, 0),
                  _row_spec(tr, pw, 1), _halo_after(tr, hl, pw, 1, n_tiles),
                  _row_spec(tr, pw, 0), _halo_after(tr, hl, pw, 0, n_tiles),
                  pl.BlockSpec((N_GROUPS, gd, gd), lambda i: (0, 0, 0)), _vec_spec(pw)],
        out_specs=[_row_spec(tr, 2 * pw), pl.BlockSpec((N_GROUPS, gd, gd), lambda i: (0, 0, 0)), _vec_spec(pw)],
        out_shape=[jax.ShapeDtypeStruct((s, 2 * pw), BF16), jax.ShapeDtypeStruct((N_GROUPS, gd, gd), F32),
                   jax.ShapeDtypeStruct((1, pw), F32)],
        scratch_shapes=[pltpu.VMEM((tr + hl, pw), F32), pltpu.VMEM((tr + hl, pw), F32)],
        compiler_params=_params(("arbitrary",)),
    )(proj, proj, proj, proj, dya, dya, pool_w, pool_scale)


def _conv_fwd(proj, conv_w, d, tr):
    s = proj.shape[0]
    cc, hl = d // 2, CONV_HALO
    cu, cb, cg_, cz = 10, 11, 12, 13

    def body(u_ref, uh_ref, bg_ref, cg_ref, ch_ref, zc_ref, w_ref, yc_ref, buf):
        i = pl.program_id(0)
        buf[0:hl, :] = jnp.where(i > 0, ch_ref[...] * uh_ref[...], 0.0)
        buf[hl:hl + tr, :] = cg_ref[...] * u_ref[...]
        y = w_ref[0:1, :] * buf[hl - 2:hl - 2 + tr, :]
        y = y + w_ref[1:2, :] * buf[hl - 1:hl - 1 + tr, :]
        y = y + w_ref[2:3, :] * buf[hl:hl + tr, :]
        zc = zc_ref[...]
        yc_ref[...] = ((bg_ref[...] * y) * (zc * _sigmoid(zc))).astype(BF16)

    return pl.pallas_call(
        body, name="conv_fwd", grid=(s // tr,),
        in_specs=[_row_spec(tr, cc, cu), _halo_before(tr, hl, cc, cu), _row_spec(tr, cc, cb),
                  _row_spec(tr, cc, cg_), _halo_before(tr, hl, cc, cg_), _row_spec(tr, cc, cz),
                  _vec_spec(cc, rows=3)],
        out_specs=_row_spec(tr, cc), out_shape=jax.ShapeDtypeStruct((s, cc), BF16),
        scratch_shapes=[pltpu.VMEM((tr + hl, cc), F32)],
        compiler_params=_params(("parallel",)),
    )(proj, proj, proj, proj, proj, proj, conv_w)


def _conv_bwd(proj, dyc, conv_w, d, tr):
    s = proj.shape[0]
    cc, hl = d // 2, CONV_HALO
    cu, cb, cg_, cz = 10, 11, 12, 13
    n_tiles = s // tr

    def body(u_ref, uh_ref, bg_ref, bh_ref, cg_ref, ch_ref, zc_ref, zh_ref, dy_ref, dyh_ref, w_ref,
             dc_ref, gw_ref, buf, dbuf):
        i = pl.program_id(0)

        @pl.when(i == 0)
        def _():
            gw_ref[...] = jnp.zeros_like(gw_ref)

        u, bg, cg, zc, dyc_t = u_ref[...], bg_ref[...], cg_ref[...], zc_ref[...], dy_ref[...]
        buf[0:hl, :] = jnp.where(i > 0, ch_ref[...] * uh_ref[...], 0.0)
        buf[hl:hl + tr, :] = cg * u
        v2, v1, v0 = buf[hl - 2:hl - 2 + tr, :], buf[hl - 1:hl - 1 + tr, :], buf[hl:hl + tr, :]
        w0, w1, w2 = w_ref[0:1, :], w_ref[1:2, :], w_ref[2:3, :]
        y = w0 * v2 + w1 * v1 + w2 * v0
        sg = _sigmoid(zc)
        gate = zc * sg
        dc_ref[:, cc:2 * cc] = (dyc_t * y * gate).astype(BF16)
        dc_ref[:, 3 * cc:4 * cc] = (dyc_t * bg * y * _dsilu(zc, sg)).astype(BF16)
        dy = dyc_t * bg * gate
        zh = zh_ref[...]
        dy_h = jnp.where(i < n_tiles - 1, dyh_ref[...], 0.0) * bh_ref[...] * (zh * _sigmoid(zh))
        gw_ref[0:1, :] += jnp.sum(dy * v2, axis=0, keepdims=True)
        gw_ref[1:2, :] += jnp.sum(dy * v1, axis=0, keepdims=True)
        gw_ref[2:3, :] += jnp.sum(dy * v0, axis=0, keepdims=True)
        dbuf[0:tr, :] = dy
        dbuf[tr:tr + hl, :] = dy_h
        dv = w2 * dy + w1 * dbuf[1:1 + tr, :] + w0 * dbuf[2:2 + tr, :]
        dc_ref[:, 0:cc] = (dv * cg).astype(BF16)
        dc_ref[:, 2 * cc:3 * cc] = (dv * u).astype(BF16)

    return pl.pallas_call(
        body, name="conv_bwd", grid=(n_tiles,),
        in_specs=[_row_spec(tr, cc, cu), _halo_before(tr, hl, cc, cu),
                  _row_spec(tr, cc, cb), _halo_after(tr, hl, cc, cb, n_tiles),
                  _row_spec(tr, cc, cg_), _halo_before(tr, hl, cc, cg_),
                  _row_spec(tr, cc, cz), _halo_after(tr, hl, cc, cz, n_tiles),
                  _row_spec(tr, cc, 0), _halo_after(tr, hl, cc, 0, n_tiles),
                  _vec_spec(cc, rows=3)],
        out_specs=[_row_spec(tr, 4 * cc), _vec_spec(cc, rows=3)],
        out_shape=[jax.ShapeDtypeStruct((s, 4 * cc), BF16), jax.ShapeDtypeStruct((3, cc), F32)],
        scratch_shapes=[pltpu.VMEM((tr + hl, cc), F32), pltpu.VMEM((tr + hl, cc), F32)],
        compiler_params=_params(("arbitrary",)),
    )(proj, proj, proj, proj, proj, proj, proj, proj, dyc, dyc, conv_w)


def _merge_fwd(gates, pa, pb, pc, tr):
    s, d = pa.shape

    def body(g_ref, a_ref, b_ref, c_ref, m_ref):
        m = g_ref[:, 0:d] * a_ref[...] + g_ref[:, d:2 * d] * b_ref[...] + g_ref[:, 2 * d:3 * d] * c_ref[...]
        m_ref[...] = m.astype(BF16)

    return pl.pallas_call(
        body, name="merge_fwd", grid=(s // tr,),
        in_specs=[_row_spec(tr, 3 * d), _row_spec(tr, d), _row_spec(tr, d), _row_spec(tr, d)],
        out_specs=_row_spec(tr, d), out_shape=jax.ShapeDtypeStruct((s, d), BF16),
        compiler_params=_params(("parallel",)),
    )(gates, pa, pb, pc)


def _merge_bwd(dm, gates, pa, pb, pc, tr):
    s, d = pa.shape

    def body(dm_ref, g_ref, a_ref, b_ref, c_ref, da_ref, db_ref, dc_ref, dg_ref, gb_ref):
        @pl.when(pl.program_id(0) == 0)
        def _():
            gb_ref[...] = jnp.zeros_like(gb_ref)

        dmv = dm_ref[...]
        for n, (p_ref, o_ref) in enumerate(((a_ref, da_ref), (b_ref, db_ref), (c_ref, dc_ref))):
            gv = g_ref[:, n * d:(n + 1) * d]
            o_ref[...] = (dmv * gv).astype(BF16)
            dlogit = (dmv * p_ref[...]) * (gv * (1.0 - gv))
            dg_ref[:, n * d:(n + 1) * d] = dlogit.astype(BF16)
            gb_ref[:, n * d:(n + 1) * d] += jnp.sum(dlogit, axis=0, keepdims=True)

    act = jax.ShapeDtypeStruct((s, d), BF16)
    return pl.pallas_call(
        body, name="merge_bwd", grid=(s // tr,),
        in_specs=[_row_spec(tr, d), _row_spec(tr, 3 * d), _row_spec(tr, d), _row_spec(tr, d), _row_spec(tr, d)],
        out_specs=[_row_spec(tr, d), _row_spec(tr, d), _row_spec(tr, d), _row_spec(tr, 3 * d), _vec_spec(3 * d)],
        out_shape=[act, act, act, jax.ShapeDtypeStruct((s, 3 * d), BF16), jax.ShapeDtypeStruct((1, 3 * d), F32)],
        compiler_params=_params(("arbitrary",)),
    )(dm, gates, pa, pb, pc)


def _resid_bwd(dxo, out, rg, tr):
    s, d = dxo.shape

    def body(dx_ref, o_ref, rg_ref, do_ref, drg_ref):
        @pl.when(pl.program_id(0) == 0)
        def _():
            drg_ref[...] = jnp.zeros_like(drg_ref)

        dxv = dx_ref[...]
        do_ref[...] = (dxv * rg_ref[...]).astype(BF16)
        drg_ref[...] += jnp.sum(dxv * o_ref[...], axis=0, keepdims=True)

    return pl.pallas_call(
        body, name="resid_bwd", grid=(s // tr,),
        in_specs=[_row_spec(tr, d), _row_spec(tr, d), _vec_spec(d)],
        out_specs=[_row_spec(tr, d), _vec_spec(d)],
        out_shape=[jax.ShapeDtypeStruct((s, d), BF16), jax.ShapeDtypeStruct((1, d), F32)],
        compiler_params=_params(("arbitrary",)),
    )(dxo, out, rg)


def _log_keep(z):
    e = jnp.exp(-jnp.abs(z))
    return -(jnp.maximum(z, 0.0) + jnp.log(1.0 + e)), e


class _Job:
    def __init__(self, ins, out_shapes, aliases, n_sems, start, finish, mid=None):
        self.ins, self.out_shapes, self.aliases, self.n_sems = list(ins), list(out_shapes), list(aliases), n_sems
        self.start, self.mid, self.finish = start, mid, finish


def _host(job, n_in, n_out):
    if job is None:
        return dict(ins=[], in_specs=[], out_specs=[], out_shape=[], scratch=[], aliases={})
    return dict(ins=job.ins, in_specs=[_ANY] * len(job.ins), out_specs=[_ANY] * len(job.out_shapes),
                out_shape=job.out_shapes,
                scratch=[pltpu.SemaphoreType.DMA((job.n_sems,)), pltpu.SemaphoreType.DMA((job.n_sems,))],
                aliases={n_in + a: n_out + b for a, b in job.aliases})


def _job_refs(job, refs, n_in, n_out):
    if job is None:
        return refs[:n_in], refs[n_in:n_in + n_out], None
    ji, jo = len(job.ins), len(job.out_shapes)
    own_in, job_in = refs[:n_in], refs[n_in:n_in + ji]
    own_out, job_out = refs[n_in + ji:n_in + ji + n_out], refs[n_in + ji + n_out:n_in + ji + n_out + jo]
    send_sems, recv_sems = refs[n_in + ji + n_out + jo:]
    return own_in, own_out, (job_in, job_out, send_sems, recv_sems)


def _sb_fwd(proj, projb, d, blk, job=None):
    s = proj.shape[0]
    hps, wid = SB_FWD_HEADS, SB_FWD_HEADS * HEAD_DIM
    groups, nq = d // wid, s // blk
    qc, kc, vc, zc = d // wid, 2 * d // wid, 3 * d // wid, 4 * d // wid
    assert nq <= HEAD_DIM
    mid_step = (groups - 1, nq // 3)

    def body(*refs):
        (q_ref, k_ref, v_ref, zb_ref), (att_ref, yb_ref, rs_ref), jargs = _job_refs(job, refs, 4, 3)
        grp, i = pl.program_id(0), pl.program_id(1)
        if job is not None:
            @pl.when((grp == 0) & (i == 0))
            def _():
                job.start(*jargs)

            if job.mid is not None:
                @pl.when((grp == mid_step[0]) & (i == mid_step[1]))
                def _():
                    job.mid(*jargs)

        r_io = lax.broadcasted_iota(jnp.int32, (blk, blk), 0)
        c_io = lax.broadcasted_iota(jnp.int32, (blk, blk), 1)
        tri = (r_io >= c_io).astype(BF16)
        strict = c_io < r_io
        lane = lax.broadcasted_iota(jnp.int32, (blk, HEAD_DIM), 1)
        heads = [slice(n * HEAD_DIM, (n + 1) * HEAD_DIM) for n in range(hps)]
        qs = [q_ref[:, hd] for hd in heads]

        def block(j, carry, masked):
            ks = pl.multiple_of(j * blk, blk)
            rng = range(hps)
            zs = [lax.dot_general(qs[n], k_ref[pl.ds(ks, blk), heads[n]], (NT, ((), ())),
                                  preferred_element_type=F32) for n in rng]
            lks = []
            for n in rng:
                lk, _ = _log_keep(zs[n])
                if masked:
                    lk = jnp.where(strict, lk, 0.0)
                lks.append(lk.astype(BF16))
            csums = [carry[n][0] + jnp.dot(lks[n], tri, preferred_element_type=F32) for n in rng]
            probs = []
            for n in rng:
                a = jnp.exp(zs[n] + csums[n])
                if masked:
                    a = jnp.where(strict, a, 0.0)
                probs.append(a.astype(BF16))
            out = []
            for n in rng:
                run, acc, rs = carry[n]
                acc = acc + jnp.dot(probs[n], v_ref[pl.ds(ks, blk), heads[n]], preferred_element_type=F32)
                out.append((csums[n][:, 0:1], acc, jnp.where(lane == j, run, rs)))
            return tuple(out)

        zero = (jnp.zeros((blk, 1), F32), jnp.zeros((blk, HEAD_DIM), F32), jnp.zeros((blk, HEAD_DIM), F32))
        carry = block(i, (zero,) * hps, True)
        carry = lax.fori_loop(0, i, lambda jj, cr: block(i - 1 - jj, cr, False), carry)
        for hd, (_, acc, rs) in zip(heads, carry):
            att_ref[:, hd] = acc
            rs_ref[:, hd] = rs
            zb = zb_ref[:, hd]
            yb_ref[:, hd] = (acc * (zb * _sigmoid(zb))).astype(BF16)

        if job is not None:
            @pl.when((grp == groups - 1) & (i == nq - 1))
            def _():
                job.finish(*jargs)

    blk_spec = pl.BlockSpec((blk, wid), lambda h, i: (i, h))
    hosted = _host(job, 4, 3)
    res = pl.pallas_call(
        body, name="sb_fwd", grid=(groups, nq),
        in_specs=[pl.BlockSpec((blk, wid), lambda h, i: (i, qc + h)),
                  pl.BlockSpec((s, wid), lambda h, i: (0, kc + h)),
                  pl.BlockSpec((s, wid), lambda h, i: (0, vc + h)),
                  pl.BlockSpec((blk, wid), lambda h, i: (i, zc + h))] + hosted["in_specs"],
        out_specs=[blk_spec, blk_spec, blk_spec] + hosted["out_specs"],
        out_shape=[jax.ShapeDtypeStruct((s, d), F32), jax.ShapeDtypeStruct((s, d), BF16),
                   jax.ShapeDtypeStruct((s, d), F32)] + hosted["out_shape"],
        scratch_shapes=hosted["scratch"], input_output_aliases=hosted["aliases"],
        compiler_params=_params(("arbitrary", "arbitrary")),
    )(projb, projb, projb, proj, *hosted["ins"])
    return res[0], res[1], res[2], list(res[3:])


def _sb_bwd(proj, projb, att, dyb, rsave, d, blk, job=None):
    s = proj.shape[0]
    hps, wid = SB_BWD_HEADS, SB_BWD_HEADS * HEAD_DIM
    groups, nq = d // wid, s // blk
    qc, kc, vc, zc = d // wid, 2 * d // wid, 3 * d // wid, 4 * d // wid
    scale = HEAD_DIM ** -0.5

    def body(*refs):
        ((q_ref, k_ref, v_ref, zb_ref, att_ref, dyb_ref, rs_ref), (dq_ref, dk_ref, dv_ref, dzb_ref),
         jargs) = _job_refs(job, refs, 7, 4)
        grp, i = pl.program_id(0), pl.program_id(1)
        if job is not None:
            @pl.when((grp == 0) & (i == 0))
            def _():
                job.start(*jargs)

        @pl.when(i == 0)
        def _():
            dk_ref[...] = jnp.zeros_like(dk_ref)
            dv_ref[...] = jnp.zeros_like(dv_ref)

        r_io = lax.broadcasted_iota(jnp.int32, (blk, blk), 0)
        c_io = lax.broadcasted_iota(jnp.int32, (blk, blk), 1)
        tri = (r_io >= c_io).astype(BF16)
        tri_up = (r_io <= c_io).astype(BF16)
        strict = c_io < r_io
        lane = lax.broadcasted_iota(jnp.int32, (blk, HEAD_DIM), 1)
        heads = [slice(n * HEAD_DIM, (n + 1) * HEAD_DIM) for n in range(hps)]
        qs, dos, rss = [], [], []
        for hd in heads:
            qs.append(q_ref[:, hd])
            zb, dyb_t = zb_ref[:, hd], dyb_ref[:, hd]
            sg = _sigmoid(zb)
            dzb_ref[:, hd] = dyb_t * att_ref[:, hd] * _dsilu(zb, sg)
            dos.append((dyb_t * (zb * sg)).astype(BF16))
            rss.append(rs_ref[:, hd])

        def block(j, carry, masked):
            ks = pl.multiple_of(j * blk, blk)
            rng = range(hps)
            kbs = [k_ref[pl.ds(ks, blk), heads[n]] for n in rng]
            vbs = [v_ref[pl.ds(ks, blk), heads[n]] for n in rng]
            zs = [lax.dot_general(qs[n], kbs[n], (NT, ((), ())), preferred_element_type=F32) for n in rng]
            das = [lax.dot_general(dos[n], vbs[n], (NT, ((), ())), preferred_element_type=F32) for n in rng]
            lks, betas = [], []
            for n in rng:
                lk, e = _log_keep(zs[n])
                betas.append(jnp.where(zs[n] >= 0, 1.0, e) / (1.0 + e))
                if masked:
                    lk = jnp.where(strict, lk, 0.0)
                lks.append(lk.astype(BF16))
            csums = []
            for n in rng:
                run = jnp.sum(jnp.where(lane == j, rss[n], 0.0), axis=1, keepdims=True)
                csums.append(run + jnp.dot(lks[n], tri, preferred_element_type=F32))
            probs, gs = [], []
            for n in rng:
                a = jnp.exp(zs[n] + csums[n])
                if masked:
                    a = jnp.where(strict, a, 0.0)
                probs.append(a.astype(BF16))
                gs.append(a * das[n])
            gcums = [carry[n][0] + jnp.dot(gs[n].astype(BF16), tri_up, preferred_element_type=F32) for n in rng]
            for n in rng:
                dv_ref[pl.ds(ks, blk), heads[n]] += lax.dot_general(probs[n], dos[n], (TN, ((), ())),
                                                                   preferred_element_type=F32)
            dzs = []
            for n in rng:
                dz = gs[n] - betas[n] * gcums[n]
                if masked:
                    dz = jnp.where(strict, dz, 0.0)
                dzs.append(dz.astype(BF16))
            out = []
            for n in rng:
                dq = carry[n][1] + jnp.dot(dzs[n], kbs[n], preferred_element_type=F32)
                dk_ref[pl.ds(ks, blk), heads[n]] += lax.dot_general(dzs[n], qs[n], (TN, ((), ())),
                                                                   preferred_element_type=F32)
                out.append((gcums[n][:, blk - 1:blk], dq))
            return tuple(out)

        zero = (jnp.zeros((blk, 1), F32), jnp.zeros((blk, HEAD_DIM), F32))
        carry = lax.fori_loop(0, i, lambda j, cr: block(j, cr, False), (zero,) * hps)
        carry = block(i, carry, True)
        for hd, (_, dq) in zip(heads, carry):
            dq_ref[:, hd] = dq * scale

        if job is not None:
            @pl.when((grp == groups - 1) & (i == nq - 1))
            def _():
                job.finish(*jargs)

    blk_spec = pl.BlockSpec((blk, wid), lambda h, i: (i, h))
    full_spec = pl.BlockSpec((s, wid), lambda h, i: (0, h), pipeline_mode=pl.Buffered(1))
    act = jax.ShapeDtypeStruct((s, d), F32)
    hosted = _host(job, 7, 4)
    res = pl.pallas_call(
        body, name="sb_bwd", grid=(groups, nq),
        in_specs=[pl.BlockSpec((blk, wid), lambda h, i: (i, qc + h)),
                  pl.BlockSpec((s, wid), lambda h, i: (0, kc + h)),
                  pl.BlockSpec((s, wid), lambda h, i: (0, vc + h)),
                  pl.BlockSpec((blk, wid), lambda h, i: (i, zc + h)),
                  blk_spec, blk_spec, blk_spec] + hosted["in_specs"],
        out_specs=[blk_spec, full_spec, full_spec, blk_spec] + hosted["out_specs"],
        out_shape=[act, act, act, act] + hosted["out_shape"],
        scratch_shapes=hosted["scratch"], input_output_aliases=hosted["aliases"],
        compiler_params=_params(("arbitrary", "arbitrary")),
    )(projb, projb, projb, proj, att, dyb, rsave, *hosted["ins"])
    return res[0], res[1], res[2], res[3], list(res[4:])


def _adamw(w, g, m, v):
    shape = w.shape
    cols = shape[-1]
    rows = w.size // cols
    w2, g2, m2, v2 = (t.reshape(rows, cols) for t in (w, g, m, v))
    tr, tc = _tile(rows, 512, 8), _tile(cols, 1024, 128)
    c1 = 1.0 - ADAM_B1 ** ADAM_STEP
    c2 = 1.0 - ADAM_B2 ** ADAM_STEP

    def body(w_ref, g_ref, m_ref, v_ref, d_ref, nm_ref, nv_ref):
        gv = g_ref[...]
        nm = ADAM_B1 * m_ref[...] + (1.0 - ADAM_B1) * gv
        nv = ADAM_B2 * v_ref[...] + (1.0 - ADAM_B2) * (gv * gv)
        d_ref[...] = -ADAM_LR * ((nm / c1) / (jnp.sqrt(nv / c2) + ADAM_EPS) + ADAM_WD * w_ref[...])
        nm_ref[...] = nm
        nv_ref[...] = nv

    spec = pl.BlockSpec((tr, tc), lambda i, j: (i, j))
    sd = jax.ShapeDtypeStruct((rows, cols), F32)
    outs = pl.pallas_call(
        body, name="adamw", grid=(rows // tr, cols // tc),
        in_specs=[spec] * 4, out_specs=[spec] * 3, out_shape=[sd] * 3,
        compiler_params=_params(("parallel", "parallel")),
    )(w2, g2, m2, v2)
    return tuple(o.reshape(shape) for o in outs)


def _sum_blocks(gathered, n, rows):
    width = gathered.shape[1]
    tw = _tile(width, 8192)

    def body(g_ref, o_ref):
        acc = g_ref[0:rows, :]
        for b in range(1, n):
            acc = acc + g_ref[b * rows:(b + 1) * rows, :]
        o_ref[...] = acc

    return pl.pallas_call(
        body, name="sum_blocks", grid=(width // tw,),
        in_specs=[pl.BlockSpec((n * rows, tw), lambda i: (0, i))],
        out_specs=pl.BlockSpec((rows, tw), lambda i: (0, i)),
        out_shape=jax.ShapeDtypeStruct((rows, width), F32),
        compiler_params=_params(("parallel",)),
    )(gathered)


def _silu_bf16(c_rows):
    def body(c_ref, o_ref):
        cv = c_ref[...]
        o_ref[...] = (cv * _sigmoid(cv)).astype(BF16)

    return pl.pallas_call(
        body, name="silu_c", out_shape=jax.ShapeDtypeStruct(c_rows.shape, BF16),
        in_specs=[pl.BlockSpec(memory_space=pltpu.VMEM)], out_specs=pl.BlockSpec(memory_space=pltpu.VMEM),
    )(c_rows)


def _place():
    x, y, c = lax.axis_index("x"), lax.axis_index("y"), lax.axis_index("c")
    chips = [(1 - x, y), (x, 1 - y), (1 - x, 1 - y)]
    return x, y, c, chips


def _all_gather_small(block, name):
    m_per, n = block.shape

    def body(x_ref, out_ref, send_sems, recv_sems, local_sem):
        x, y, c, chips = _place()
        me, sibling = (x, y, c), (x, y, 1 - c)

        def rows(px, py, pc):
            return out_ref.at[pl.ds((4 * px + 2 * py + pc) * m_per, m_per), :]

        def copy(k, blk, to, src=None):
            return pltpu.make_async_remote_copy(
                src_ref=rows(*blk) if src is None else src, dst_ref=rows(*blk),
                send_sem=send_sems.at[k], recv_sem=recv_sems.at[k], device_id=to, device_id_type=MESH)

        mine = pltpu.make_async_copy(x_ref, rows(*me), local_sem)
        mine.start()
        first = [copy(0, me, sibling, src=x_ref)]
        first += [copy(1 + j, me, (*chip, c), src=x_ref) for j, chip in enumerate(chips)]
        for cp in first:
            cp.start()
        passed = [copy(4 + j, (*chip, c), sibling) for j, chip in enumerate(chips)]
        for j, chip in enumerate(chips):
            copy(1 + j, (*chip, c), me).wait_recv()
            passed[j].start()
        copy(0, sibling, me).wait_recv()
        for j, chip in enumerate(chips):
            copy(4 + j, (*chip, 1 - c), me).wait_recv()
        for cp in first + passed:
            cp.wait_send()
        mine.wait()

    return pl.pallas_call(
        body, name=name, out_shape=jax.ShapeDtypeStruct((N_DEV * m_per, n), block.dtype),
        in_specs=[pl.BlockSpec(memory_space=pltpu.VMEM)], out_specs=pl.BlockSpec(memory_space=pltpu.VMEM),
        scratch_shapes=[pltpu.SemaphoreType.DMA((7,)), pltpu.SemaphoreType.DMA((7,)), pltpu.SemaphoreType.DMA],
        compiler_params=pltpu.CompilerParams(vmem_limit_bytes=V7X_VMEM_LIMIT),
    )(block)


_ANY = pl.BlockSpec(memory_space=pl.ANY)


def _place_shard(w, chip_idx):
    layers, r, cols = w.shape
    tr, tc = _tile(r, 512, 16), _tile(cols, 1024)

    def body(j_ref, w_ref, o_ref):
        o_ref[...] = w_ref[...].astype(BF16)

    return pl.pallas_call(
        body, name="place_shard",
        grid_spec=pltpu.PrefetchScalarGridSpec(
            num_scalar_prefetch=1, grid=(layers, r // tr, cols // tc),
            in_specs=[pl.BlockSpec((None, tr, tc), lambda l, i, n, j: (l, i, n))],
            out_specs=pl.BlockSpec((None, None, tr, tc), lambda l, i, n, j: (j[0], l, i, n))),
        out_shape=jax.ShapeDtypeStruct((N_CHIPS, layers, r, cols), BF16),
        compiler_params=_params(("parallel", "parallel", "parallel")),
    )(chip_idx, w)


def _gather_job(bufs, layer):
    n = len(bufs)

    def tools(outs, send_sems, recv_sems):
        x, y, c, chips = _place()

        def half(t, chip_idx, hc):
            h = outs[t].shape[2] // 2
            return outs[t].at[chip_idx, layer, pl.ds(hc * h, h), :]

        def copy(t, k, ref, to):
            return pltpu.make_async_remote_copy(src_ref=ref, dst_ref=ref, send_sem=send_sems.at[6 * t + k],
                                                recv_sem=recv_sems.at[6 * t + k], device_id=to, device_id_type=MESH)

        return x, y, c, chips, half, copy

    def start(ins, outs, send_sems, recv_sems):
        x, y, c, chips, half, copy = tools(outs, send_sems, recv_sems)
        for t in range(n):
            for k, chip in enumerate(chips):
                copy(t, k, half(t, 2 * x + y, c), (*chip, c)).start()

    def mid(ins, outs, send_sems, recv_sems):
        x, y, c, chips, half, copy = tools(outs, send_sems, recv_sems)
        for t in range(n):
            for k, (cx, cy) in enumerate(chips):
                landed = half(t, 2 * cx + cy, c)
                copy(t, k, landed, (cx, cy, c)).wait_recv()
                copy(t, 3 + k, landed, (x, y, 1 - c)).start()

    def finish(ins, outs, send_sems, recv_sems):
        x, y, c, chips, half, copy = tools(outs, send_sems, recv_sems)
        for t in range(n):
            for k, (cx, cy) in enumerate(chips):
                copy(t, 3 + k, half(t, 2 * cx + cy, 1 - c), (x, y, 1 - c)).wait_recv()
        for t in range(n):
            for k, (cx, cy) in enumerate(chips):
                copy(t, k, half(t, 2 * x + y, c), (cx, cy, c)).wait_send()
                copy(t, 3 + k, half(t, 2 * cx + cy, c), (x, y, 1 - c)).wait_send()

    return _Job(ins=bufs, out_shapes=[jax.ShapeDtypeStruct(b.shape, b.dtype) for b in bufs],
                aliases=[(t, t) for t in range(n)], n_sems=6 * n, start=start, mid=mid, finish=finish)


def _send_job(parts):
    n = len(parts)

    def copies(ins, outs, send_sems, recv_sems):
        x, y, c, chips = _place()
        return [pltpu.make_async_remote_copy(
            src_ref=ins[t].at[2 * cx + cy], dst_ref=outs[t].at[k], send_sem=send_sems.at[3 * t + k],
            recv_sem=recv_sems.at[3 * t + k], device_id=(cx, cy, c), device_id_type=MESH)
            for t in range(n) for k, (cx, cy) in enumerate(chips)]

    def start(*args):
        for cp in copies(*args):
            cp.start()

    def finish(*args):
        for cp in copies(*args):
            cp.wait()

    return _Job(ins=parts, out_shapes=[jax.ShapeDtypeStruct((3,) + p.shape[1:], p.dtype) for p in parts],
                aliases=[], n_sems=3 * n, start=start, finish=finish)


def _run_job(job, name):
    def body(*refs):
        _, _, jargs = _job_refs(job, refs, 0, 0)
        job.start(*jargs)
        if job.mid is not None:
            job.mid(*jargs)
        job.finish(*jargs)

    hosted = _host(job, 0, 0)
    return list(pl.pallas_call(
        body, name=name, out_shape=hosted["out_shape"], in_specs=hosted["in_specs"], out_specs=hosted["out_specs"],
        scratch_shapes=hosted["scratch"], input_output_aliases=hosted["aliases"],
    )(*hosted["ins"]))


def _swap_halves(grads):
    n = len(grads)

    def body(*refs):
        ins, outs = refs[:n], refs[n:2 * n]
        send_sems, recv_sems = refs[2 * n:]
        x, y, c, _ = _place()
        copies = []
        for t in range(n):
            h = ins[t].shape[1] // 2
            copies.append(pltpu.make_async_remote_copy(
                src_ref=ins[t].at[:, pl.ds((1 - c) * h, h), :], dst_ref=outs[t],
                send_sem=send_sems.at[t], recv_sem=recv_sems.at[t], device_id=(x, y, 1 - c), device_id_type=MESH))
            copies[-1].start()
        for cp in copies:
            cp.wait()

    return pl.pallas_call(
        body, name="swap_halves",
        out_shape=[jax.ShapeDtypeStruct((g.shape[0], g.shape[1] // 2, g.shape[2]), g.dtype) for g in grads],
        in_specs=[_ANY] * n, out_specs=[_ANY] * n,
        scratch_shapes=[pltpu.SemaphoreType.DMA((n,)), pltpu.SemaphoreType.DMA((n,))],
    )(*grads)


def _share_halves(bufs):
    n = len(bufs)

    def body(*refs):
        outs = refs[n:2 * n]
        send_sems, recv_sems = refs[2 * n:]
        x, y, c, _ = _place()

        def half(t, hc):
            h = outs[t].shape[1] // 2
            return outs[t].at[:, pl.ds(hc * h, h), :]

        def copy(t, ref):
            return pltpu.make_async_remote_copy(src_ref=ref, dst_ref=ref, send_sem=send_sems.at[t],
                                                recv_sem=recv_sems.at[t], device_id=(x, y, 1 - c),
                                                device_id_type=MESH)

        sends = [copy(t, half(t, c)) for t in range(n)]
        for cp in sends:
            cp.start()
        for t in range(n):
            copy(t, half(t, 1 - c)).wait_recv()
        for cp in sends:
            cp.wait_send()

    return pl.pallas_call(
        body, name="share_halves",
        out_shape=[jax.ShapeDtypeStruct(b.shape, b.dtype) for b in bufs],
        in_specs=[_ANY] * n, out_specs=[_ANY] * n, input_output_aliases={t: t for t in range(n)},
        scratch_shapes=[pltpu.SemaphoreType.DMA((n,)), pltpu.SemaphoreType.DMA((n,))],
    )(*bufs)


def _sum_sibling(grad, recv, c_idx):
    _, r, cols = grad.shape
    h = r // 2
    tr, tc = _tile(h, 512, 16), _tile(cols, 1024)
    per = h // tr

    def body(c_ref, g_ref, r_ref, o_ref):
        o_ref[...] = (g_ref[...].astype(F32) + r_ref[...].astype(F32)).astype(BF16)

    return pl.pallas_call(
        body, name="sum_sibling",
        grid_spec=pltpu.PrefetchScalarGridSpec(
            num_scalar_prefetch=1, grid=(N_CHIPS, per, cols // tc),
            in_specs=[pl.BlockSpec((None, tr, tc), lambda j, i, n, c: (j, c[0] * per + i, n)),
                      pl.BlockSpec((None, tr, tc), lambda j, i, n, c: (j, i, n))],
            out_specs=pl.BlockSpec((None, tr, tc), lambda j, i, n, c: (j, i, n))),
        out_shape=jax.ShapeDtypeStruct((N_CHIPS, h, cols), BF16),
        compiler_params=_params(("parallel", "parallel", "parallel")),
    )(c_idx, grad, recv)


def _sum_owner(parts, recv, place_idx, layer, layers, prev=None):
    _, h, cols = parts.shape
    tr, tc = _tile(h, 512, 16), _tile(cols, 1024)
    per = h // tr

    def body(idx_ref, p_ref, r_ref, *rest):
        o_ref = rest[-1]
        o_ref[...] = (p_ref[...].astype(F32) + r_ref[0].astype(F32) + r_ref[1].astype(F32)
                      + r_ref[2].astype(F32))

    in_specs = [pl.BlockSpec((None, tr, tc), lambda i, n, idx: (idx[0], i, n)),
                pl.BlockSpec((3, tr, tc), lambda i, n, idx: (0, i, n))]
    args = [place_idx, parts, recv]
    aliases = {}
    if prev is not None:
        in_specs.append(_ANY)
        args.append(prev)
        aliases = {3: 0}
    return pl.pallas_call(
        body, name="sum_owner",
        grid_spec=pltpu.PrefetchScalarGridSpec(
            num_scalar_prefetch=1, grid=(per, cols // tc), in_specs=in_specs,
            out_specs=pl.BlockSpec((None, tr, tc), lambda i, n, idx: (layer, idx[1] * per + i, n))),
        out_shape=jax.ShapeDtypeStruct((layers, 2 * h, cols), F32),
        input_output_aliases=aliases,
        compiler_params=_params(("parallel", "parallel")),
    )(*args)


def _layer_fwd(x, mod, wbuf, p, layer, cfg, make_jobs):
    d, tr, tm = cfg["d"], cfg["tr"], cfg["tm"]
    s = x.shape[0]
    weight = lambda n: _Sharded(wbuf[n], BIG_KIND[n], layer=layer)
    hosted = lambda key: make_jobs[key](wbuf) if key in make_jobs else (None, ())
    shift, scale, rg = mod[:, 0:d], mod[:, d:2 * d], mod[:, 2 * d:3 * d]
    h, h_t = _norm_fwd(x, p["norm_g"], scale, shift, tr)
    tn_in = cfg["tn_in"]
    q_lo, q_hi, q_scale = d // tn_in, 2 * d // tn_in, HEAD_DIM ** -0.5
    blk_in = pl.BlockSpec((tm, tn_in), lambda i, j, k: (i, j))

    def proj_out(col, acc):
        return acc, acc * jnp.where((col >= q_lo) & (col < q_hi), q_scale, 1.0)

    job, names = hosted("proj")
    proj, projb, *job_out = _mm_nn("proj", h, weight("w_in"), tm=tm, tn=tn_in, tk=d, with_col=True,
                                   epilogue=proj_out, job=job,
                                   outs=[(jax.ShapeDtypeStruct((s, 7 * d), F32), blk_in),
                                         (jax.ShapeDtypeStruct((s, 7 * d), BF16), blk_in)])
    wbuf.update(zip(names, job_out))
    gates, = _mm_nn("gates", h, weight("w_gate"), tm=tm, tn=cfg["tn_gate"], tk=d,
                    extras=[(p["b_gate"], pl.BlockSpec((1, cfg["tn_gate"]), lambda i, j, k: (0, j)))],
                    epilogue=lambda acc, b: (_sigmoid(acc + b),))
    ya = _pool_fwd(proj, p["pool_w"], p["pool_scale"], d, tr)
    yc = _conv_fwd(proj, p["conv_w"], d, tr)
    job, names = hosted("attn")
    att, yb, rsave, job_out = _sb_fwd(proj, projb, d, cfg["blk"], job)
    wbuf.update(zip(names, job_out))
    pa, = _mm_nn("branch_a", ya, weight("w_br_a"), tm=tm, tn=cfg["tn_d"], tk=d // 2)
    pb, = _mm_nn("branch_b", yb, weight("w_br_b"), tm=tm, tn=cfg["tn_d"], tk=cfg["tk_row"])
    pc, = _mm_nn("branch_c", yc, weight("w_br_c"), tm=tm, tn=cfg["tn_d"], tk=d // 2)
    merged = _merge_fwd(gates, pa, pb, pc, cfg["tr_small"])
    tn = cfg["tn_d"]
    blk = pl.BlockSpec((tm, tn), lambda i, j, k: (i, j))
    sd = jax.ShapeDtypeStruct((s, d), F32)
    out, x_next = _mm_nn("out_proj", merged, weight("w_out"), tm=tm, tn=tn, tk=cfg["tk_row"],
                         outs=[(sd, blk), (sd, blk)],
                         extras=[(x, blk), (rg, pl.BlockSpec((1, tn), lambda i, j, k: (0, j)))],
                         epilogue=lambda acc, xv, g: (acc, xv + g * acc))
    saved = dict(x=x, h_t=h_t, proj=proj, projb=projb, gates=gates, ya=ya, yb=yb, yc=yc, att=att, rsave=rsave,
                 pa=pa, pb=pb, pc=pc, merged=merged, out=out, scale=scale, rg=rg)
    return x_next, saved


def _reduce_prepare(grads, c_idx):
    return [_sum_sibling(g, r, c_idx) for g, r in zip(grads, _swap_halves(grads))]


def _layer_bwd(dxo, sv, wbuf, p, layer, cfg, c_idx):
    d, tr, tm, tk_s = cfg["d"], cfg["tr"], cfg["tm"], cfg["tk_s"]
    weight = lambda n: _Sharded(wbuf[n], BIG_KIND[n], layer=layer)
    dout, drg = _resid_bwd(dxo, sv["out"], sv["rg"], tr)
    dmerged = _mm_nt("d_merged", dout, weight("w_out"), tm=tm, tn=cfg["tn_row"], tk=d)
    g_out = _mm_tn("g_w_out", sv["merged"], dout, "row", tm=cfg["tn_row"], tn=cfg["tn_d"], tk=tk_s)
    dpa, dpb, dpc, dlogit, g_bgate = _merge_bwd(dmerged, sv["gates"], sv["pa"], sv["pb"], sv["pc"], cfg["tr_small"])
    dya = _mm_nt("d_ya", dpa, weight("w_br_a"), tm=tm, tn=cfg["tn_half"], tk=cfg["tn_d"])
    dyb = _mm_nt("d_yb", dpb, weight("w_br_b"), tm=tm, tn=cfg["tn_row"], tk=d)
    dyc = _mm_nt("d_yc", dpc, weight("w_br_c"), tm=tm, tn=cfg["tn_half"], tk=cfg["tn_d"])
    g_a = _mm_tn("g_w_br_a", sv["ya"], dpa, "col", tm=cfg["tn_half"], tn=cfg["tn_d"], tk=tk_s)
    g_b = _mm_tn("g_w_br_b", sv["yb"], dpb, "row", tm=cfg["tn_row"], tn=cfg["tn_d"], tk=tk_s)
    g_c = _mm_tn("g_w_br_c", sv["yc"], dpc, "col", tm=cfg["tn_half"], tn=cfg["tn_d"], tk=tk_s)
    g_gate = _mm_grad("g_w_gate", sv["h_t"], dlogit, "col", tm=cfg["tm_g"], tn=cfg["tn_gate"], tk=tk_s)
    early = ("w_gate", "w_br_a", "w_br_b", "w_br_c", "w_out")
    parts = dict(zip(early, _reduce_prepare([g_gate, g_a, g_b, g_c, g_out], c_idx)))
    d_a, g_pool_w, g_pool_scale = _pool_bwd(sv["proj"], dya, p["pool_w"], p["pool_scale"], d, tr)
    d_c, g_conv_w = _conv_bwd(sv["proj"], dyc, p["conv_w"], d, tr)
    dq, dk, dv, dzb, sent = _sb_bwd(sv["proj"], sv["projb"], sv["att"], dyb, sv["rsave"], d, cfg["blk"],
                                    _send_job([parts[n] for n in early]))
    theirs = dict(zip(early, sent))
    dproj = jnp.concatenate([d_a, dq.astype(BF16), dk.astype(BF16), dv.astype(BF16), dzb.astype(BF16), d_c], axis=1)
    g_in = _mm_grad("g_w_in", sv["h_t"], dproj, "col", tm=cfg["tm_g"], tn=cfg["tn_in"], tk=tk_s)
    parts["w_in"], = _reduce_prepare([g_in], c_idx)
    dh_in, (theirs["w_in"],) = _mm_nt("d_h_in", dproj, weight("w_in"), tm=tm, tn=cfg["tn_d"], tk=cfg["tk_in"],
                                      job=_send_job([parts["w_in"]]))
    tn = cfg["tn_d"]
    dh = _mm_nt("d_h_gate", dlogit, weight("w_gate"), tm=tm, tn=tn, tk=cfg["tk_gate"],
                extras=[(dh_in, pl.BlockSpec((tm, tn), lambda i, j, k: (i, j)))],
                epilogue=lambda acc, prev: (prev + acc,))
    dx, dshift, dscale, g_norm = _norm_bwd(dh, sv["x"], p["norm_g"], sv["scale"], dxo, tr)
    small = dict(dmod=jnp.concatenate([dshift, dscale, drg], axis=1), norm_g=g_norm, pool_scale=g_pool_scale,
                 b_gate=g_bgate, conv_w=g_conv_w, pool_w=g_pool_w)
    return dx, parts, theirs, small


BIG = ("w_in", "w_gate", "w_br_a", "w_br_b", "w_br_c", "w_out")
BIG_KIND = dict(w_in="col", w_gate="col", w_br_a="col", w_br_b="row", w_br_c="col", w_out="row")


def _pad_to(v, n):
    return jnp.pad(v, (0, n - v.shape[0]))


def kernel(x, c, norm_g, w_ada, b_ada, w_in, pool_w, pool_scale, conv_w, w_br_a, w_br_b, w_br_c, w_gate, b_gate, w_out, final_g, loss_target, m_norm_g, m_w_ada, m_b_ada, m_w_in, m_pool_w, m_pool_scale, m_conv_w, m_w_br_a, m_w_br_b, m_w_br_c, m_w_gate, m_b_gate, m_w_out, m_final_g, v_norm_g, v_w_ada, v_b_ada, v_w_in, v_pool_w, v_pool_scale, v_conv_w, v_w_br_a, v_w_br_b, v_w_br_c, v_w_gate, v_b_gate, v_w_out, v_final_g):
    weights = dict(norm_g=norm_g, w_ada=w_ada, b_ada=b_ada, w_in=w_in, pool_w=pool_w, pool_scale=pool_scale,
                   conv_w=conv_w, w_br_a=w_br_a, w_br_b=w_br_b, w_br_c=w_br_c, w_gate=w_gate, b_gate=b_gate,
                   w_out=w_out, final_g=final_g)
    mom_m = dict(norm_g=m_norm_g, w_ada=m_w_ada, b_ada=m_b_ada, w_in=m_w_in, pool_w=m_pool_w,
                 pool_scale=m_pool_scale, conv_w=m_conv_w, w_br_a=m_w_br_a, w_br_b=m_w_br_b, w_br_c=m_w_br_c,
                 w_gate=m_w_gate, b_gate=m_b_gate, w_out=m_w_out, final_g=m_final_g)
    mom_v = dict(norm_g=v_norm_g, w_ada=v_w_ada, b_ada=v_b_ada, w_in=v_w_in, pool_w=v_pool_w,
                 pool_scale=v_pool_scale, conv_w=v_conv_w, w_br_a=v_w_br_a, w_br_b=v_w_br_b, w_br_c=v_w_br_c,
                 w_gate=v_w_gate, b_gate=v_b_gate, w_out=v_w_out, final_g=v_final_g)
    names = list(weights)

    _, s, d = x.shape
    layers = norm_g.shape[0]
    pw, gd, cc = d // 2, d // 8, d // 2
    ada_cols = 3 * d // N_CHIPS
    xi, yi, ci = lax.axis_index("x"), lax.axis_index("y"), lax.axis_index("c")
    me = 4 * xi + 2 * yi + ci
    my_chip = 2 * xi + yi
    c_idx = jnp.reshape(ci, (1,)).astype(jnp.int32)
    chip_idx = jnp.reshape(my_chip, (1,)).astype(jnp.int32)

    cfg = dict(
        d=d, tr=_tile(s, 256, 16), tr_small=_tile(s, 128, 16), tm=_tile(s, 1024, 16), tk_s=_tile(s, 4096, 16),
        blk=_tile(s, min(256, max(s // 4, 16)), 16),
        tn_in=_tile(7 * d // N_CHIPS, 512), tk_in=_tile(7 * d // N_CHIPS, 1792),
        tn_gate=_tile(3 * d // N_CHIPS, 512), tk_gate=_tile(3 * d // N_CHIPS, 1536), tn_d=_tile(d // N_CHIPS, 512),
        tn_row=_tile(d // N_CHIPS, 512, 16), tk_row=_tile(d // N_CHIPS, 512, 16),
        tn_half=_tile(d // 2, 512), tm_g=_tile(d, 1024),
    )

    conv_flat = conv_w.reshape(-1)
    conv_len = -(-conv_flat.shape[0] // 1024) * 1024
    pack0 = jnp.concatenate([c.reshape(-1), _pad_to(conv_flat, conv_len), pool_w.reshape(-1)])
    w0 = -(-pack0.shape[0] // 1024) * 1024
    g0 = _all_gather_small(_pad_to(pack0, w0).reshape(8, w0 // 8), "gather_small").reshape(N_DEV, w0)
    c_all = g0[:, 0:d]
    chip_rows = g0[0::2]
    conv_full = jnp.concatenate(
        [chip_rows[j, d:d + conv_flat.shape[0]].reshape(conv_w.shape) for j in range(N_CHIPS)], axis=2)
    pool_full = jnp.concatenate(
        [chip_rows[j, d + conv_len:d + conv_len + pool_w.size].reshape(pool_w.shape) for j in range(N_CHIPS)],
        axis=2)
    pool_bf = pool_full.astype(BF16)

    sc16 = _silu_bf16(jnp.pad(c_all, ((0, 16 - N_DEV), (0, 0))))
    mods = []
    for l in range(layers):
        bias = lax.dynamic_slice(b_ada[l], (my_chip * ada_cols,), (ada_cols,)).reshape(1, ada_cols)
        tn = _tile(ada_cols, 512)
        mod_l, = _mm("mod", sc16, w_ada[l], dims=NN, grid=(1, ada_cols // tn, 1),
                     a_spec=pl.BlockSpec((16, d), lambda i, j, k: (0, 0)),
                     b_spec=pl.BlockSpec((d, tn), lambda i, j, k: (0, j)),
                     acc_shape=(16, tn),
                     outs=[(jax.ShapeDtypeStruct((16, ada_cols), F32), pl.BlockSpec((16, tn), lambda i, j, k: (0, j)))],
                     extras=[(bias, pl.BlockSpec((1, tn), lambda i, j, k: (0, j)))],
                     epilogue=lambda acc, b: (acc + b,))
        mods.append(mod_l[0:N_DEV])
    g1 = _all_gather_small(jnp.concatenate(mods, axis=1), "gather_mod")
    g1 = g1.reshape(N_CHIPS, 2, N_DEV, layers, ada_cols)[:, 0]
    mod_all = jnp.transpose(g1, (1, 2, 0, 3)).reshape(N_DEV, layers, 3 * d)
    mod_me = lax.dynamic_slice(mod_all, (me, 0, 0), (1, layers, 3 * d))[0]

    wbuf = {n: _place_shard(weights[n], chip_idx) for n in BIG}
    wbuf["w_in"], = _run_job(_gather_job([wbuf["w_in"]], 0), "gather_w_in")
    params = [dict(norm_g=norm_g[l:l + 1], pool_scale=pool_scale[l:l + 1], b_gate=b_gate[l:l + 1],
                   conv_w=conv_full[l], pool_w=pool_bf[l]) for l in range(layers)]

    def gather(names, layer):
        return lambda wb: (_gather_job([wb[n] for n in names], layer), names)

    act = x[0]
    saved = []
    for l in range(layers):
        make_jobs = {}
        if l == 0:
            make_jobs["proj"] = gather([n for n in BIG if n != "w_in"], 0)
        if l + 1 < layers:
            make_jobs["attn"] = gather(BIG, l + 1)
        act, sv = _layer_fwd(act, mod_me[l:l + 1], wbuf, params[l], l, cfg, make_jobs)
        saved.append(sv)
    loss_part, dact, g_final = _final_loss(act, final_g.reshape(1, d), loss_target[0], cfg["tr"])
    loss = lax.psum(loss_part[0, 0], ("x", "y", "c"))

    small_grads, parts, theirs = [None] * layers, [None] * layers, [None] * layers
    for l in reversed(range(layers)):
        dact, parts[l], theirs[l], small_grads[l] = _layer_bwd(dact, saved[l], wbuf, params[l], l, cfg, c_idx)
    grad_x = dact.reshape(x.shape)
    place_idx = jnp.stack([my_chip, ci]).astype(jnp.int32)
    owned = []
    for n in BIG:
        buf = None
        for l in range(layers):
            buf = _sum_owner(parts[l][n], theirs[l][n], place_idx, l, layers, prev=buf)
        owned.append(buf)
    full = _share_halves(owned)
    grads = {n: f.reshape(weights[n].shape) for n, f in zip(BIG, full)}

    small_names = ("dmod", "norm_g", "pool_scale", "b_gate", "conv_w", "pool_w")
    pieces = [small_grads[l][n].reshape(-1) for n in small_names for l in range(layers)] + [g_final.reshape(-1)]
    pack1 = jnp.concatenate(pieces)
    w1 = -(-pack1.shape[0] // 1024) * 1024
    g2 = _all_gather_small(_pad_to(pack1, w1).reshape(8, w1 // 8), "gather_grads")
    total = _sum_blocks(g2, N_DEV, 8).reshape(-1)
    off = 0
    summed = {}
    for n in small_names:
        per = small_grads[0][n].size
        summed[n] = jnp.stack([total[off + l * per:off + (l + 1) * per].reshape(small_grads[0][n].shape)
                               for l in range(layers)])
        off += layers * per
    grads["final_g"] = total[off:off + d]
    grads["norm_g"] = summed["norm_g"].reshape(layers, d)
    grads["pool_scale"] = summed["pool_scale"].reshape(layers, pw)
    grads["b_gate"] = summed["b_gate"].reshape(layers, 3 * d)
    grads["b_ada"] = summed["dmod"].reshape(layers, 3 * d)
    cs = cc // N_CHIPS
    grads["conv_w"] = lax.dynamic_slice(summed["conv_w"], (0, 0, my_chip * cs), (layers, 3, cs))
    rs_ = gd // N_CHIPS
    grads["pool_w"] = lax.dynamic_slice(summed["pool_w"], (0, 0, my_chip * rs_, 0), (layers, N_GROUPS, rs_, gd))
    dmod_all = g2.reshape(N_DEV, w1)[:, 0:layers * 3 * d].reshape(N_DEV, layers, 3 * d)
    g_ada = []
    for l in range(layers):
        cols = lax.dynamic_slice(dmod_all[:, l], (0, my_chip * ada_cols), (N_DEV, ada_cols))
        cols16 = jnp.pad(cols, ((0, 16 - N_DEV), (0, 0)))
        tn = _tile(ada_cols, 512)
        tm = _tile(d, 1024)
        ga, = _mm("g_w_ada", sc16, cols16, dims=TN, grid=(d // tm, ada_cols // tn, 1),
                  a_spec=pl.BlockSpec((16, tm), lambda i, j, k: (0, i)),
                  b_spec=pl.BlockSpec((16, tn), lambda i, j, k: (0, j)),
                  acc_shape=(tm, tn),
                  outs=[(jax.ShapeDtypeStruct((d, ada_cols), F32), pl.BlockSpec((tm, tn), lambda i, j, k: (i, j)))])
        g_ada.append(ga)
    grads["w_ada"] = jnp.stack(g_ada)

    deltas, new_m, new_v = {}, {}, {}
    for n in names:
        deltas[n], new_m[n], new_v[n] = _adamw(weights[n], grads[n], mom_m[n], mom_v[n])
    return (loss, grad_x, *[grads[n] for n in names], *[deltas[n] for n in names],
            *[new_m[n] for n in names], *[new_v[n] for n in names])
```

```python
import functools

import jax
import jax.numpy as jnp
from jax import lax
from jax.experimental import pallas as pl
from jax.experimental.pallas import tpu as pltpu

F32 = jnp.float32
BF16 = jnp.bfloat16
MESH = pl.DeviceIdType.MESH

N_CHIPS = 4
N_DEV = 8
N_GROUPS = 4
POOL_WINDOWS = (2, 4, 8, 16)
POOL_HALO = 16
CONV_HALO = 8
HEAD_DIM = 128
SB_FWD_HEADS = 4
SB_BWD_HEADS = 4
RMS_EPS = 1e-6
ADAM_LR = 0.001
ADAM_B1 = 0.9
ADAM_B2 = 0.999
ADAM_EPS = 1e-08
ADAM_WD = 0.01
ADAM_STEP = 10
V7X_VMEM_LIMIT = 56 * 1024 * 1024


def _tile(n, pref, mult=128):
    best = None
    t = mult
    while t <= min(n, pref):
        if n % t == 0:
            best = t
        t += mult
    return n if best is None else best


def _params(sem=None):
    return pltpu.CompilerParams(dimension_semantics=sem, vmem_limit_bytes=V7X_VMEM_LIMIT)


def _sigmoid(z):
    return jax.nn.sigmoid(z)


def _dsilu(z, sg):
    return sg * (1.0 + z * (1.0 - sg))


NN = ((1,), (0,))
NT = ((1,), (1,))
TN = ((0,), (0,))


def _mm(name, a, b, *, dims, grid, a_spec, b_spec, acc_shape, outs, extras=(), epilogue=None, with_col=False,
        job=None):
    gi, gj, nk = grid
    ne, no = len(extras), len(outs)
    if epilogue is None:
        epilogue = lambda acc: (acc,)

    def body(*refs):
        own_in, orefs, jargs = _job_refs(job, refs[:-1], 2 + ne, no)
        a_ref, b_ref, ex, acc = own_in[0], own_in[1], own_in[2:], refs[-1]
        i, j, k = pl.program_id(0), pl.program_id(1), pl.program_id(2)
        lead = (j,) if with_col else ()
        if job is not None:
            @pl.when((i == 0) & (j == 0) & (k == 0))
            def _():
                job.start(*jargs)

            if job.mid is not None:
                @pl.when((i == gi - 1) & (j == gj // 2) & (k == 0))
                def _():
                    job.mid(*jargs)

        @pl.when(k == 0)
        def _():
            acc[...] = jnp.zeros_like(acc)

        acc[...] += lax.dot_general(a_ref[...].astype(BF16), b_ref[...].astype(BF16), (dims, ((), ())),
                                    preferred_element_type=F32)

        @pl.when(k == nk - 1)
        def _():
            vals = epilogue(*lead, acc[...], *[e[...] for e in ex])
            for o, v in zip(orefs, vals):
                o[...] = v.astype(o.dtype)

        if job is not None:
            @pl.when((i == gi - 1) & (j == gj - 1) & (k == nk - 1))
            def _():
                job.finish(*jargs)

    hosted = _host(job, 2 + ne, no)
    sem = ("parallel", "parallel", "arbitrary") if job is None else ("arbitrary",) * 3
    res = pl.pallas_call(
        body, name=name, grid=grid,
        in_specs=[a_spec, b_spec] + [s for _, s in extras] + hosted["in_specs"],
        out_specs=[s for _, s in outs] + hosted["out_specs"],
        out_shape=[sh for sh, _ in outs] + hosted["out_shape"],
        scratch_shapes=hosted["scratch"] + [pltpu.VMEM(acc_shape, F32)],
        input_output_aliases=hosted["aliases"],
        compiler_params=_params(sem),
    )(a, b, *[e for e, _ in extras], *hosted["ins"])
    return res


class _Sharded:
    def __init__(self, arr, kind, layer=None):
        self.arr, self.kind, self.layer = arr, kind, layer
        r, c = arr.shape[-2:]
        self.rows = r * (N_CHIPS if kind == "row" else 1)
        self.cols = c * (N_CHIPS if kind == "col" else 1)
        self.sr, self.sc = r, c

    def spec(self, br, bc, f):
        lead = (None,) if self.layer is None else (None, None)
        layer = self.layer
        if self.kind == "col":
            per = self.sc // bc
            assert per * bc == self.sc and self.sr % br == 0, (self.arr.shape, br, bc)

            def idx(*g):
                rb, cb = f(*g)
                return ((cb // per,) + (() if layer is None else (layer,)) + (rb, cb % per))
        else:
            per = self.sr // br
            assert per * br == self.sr and self.sc % bc == 0, (self.arr.shape, br, bc)

            def idx(*g):
                rb, cb = f(*g)
                return ((rb // per,) + (() if layer is None else (layer,)) + (rb % per, cb))
        return pl.BlockSpec(lead + (br, bc), idx)


def _grad_buffer(rows, cols, kind):
    if kind == "col":
        return jax.ShapeDtypeStruct((N_CHIPS, rows, cols // N_CHIPS), BF16)
    return jax.ShapeDtypeStruct((N_CHIPS, rows // N_CHIPS, cols), BF16)


def _mm_nn(name, a, w, *, tm, tn, tk, outs=None, extras=(), epilogue=None, out_dtype=F32, with_col=False,
           job=None):
    m, kdim = a.shape
    n = w.cols
    grid = (m // tm, n // tn, kdim // tk)
    if outs is None:
        outs = [(jax.ShapeDtypeStruct((m, n), out_dtype), pl.BlockSpec((tm, tn), lambda i, j, k: (i, j)))]
    return _mm(name, a, w.arr, dims=NN, grid=grid,
               a_spec=pl.BlockSpec((tm, tk), lambda i, j, k: (i, k)),
               b_spec=w.spec(tk, tn, lambda i, j, k: (k, j)),
               acc_shape=(tm, tn), outs=outs, extras=extras, epilogue=epilogue, with_col=with_col, job=job)


def _mm_grad(name, a_t, b, kind, *, tm, tn, tk):
    m, kdim = a_t.shape
    n = b.shape[1]
    out = _Sharded(_grad_buffer(m, n, kind), kind)
    return _mm(name, a_t, b, dims=NN, grid=(m // tm, n // tn, kdim // tk),
               a_spec=pl.BlockSpec((tm, tk), lambda i, j, k: (i, k)),
               b_spec=pl.BlockSpec((tk, tn), lambda i, j, k: (k, j)),
               acc_shape=(tm, tn), outs=[(out.arr, out.spec(tm, tn, lambda i, j, k: (i, j)))])[0]


def _mm_nt(name, a, w, *, tm, tn, tk, extras=(), epilogue=None, out_dtype=F32, job=None):
    m, kdim = a.shape
    n = w.rows
    grid = (m // tm, n // tn, kdim // tk)
    outs = [(jax.ShapeDtypeStruct((m, n), out_dtype), pl.BlockSpec((tm, tn), lambda i, j, k: (i, j)))]
    res = _mm(name, a, w.arr, dims=NT, grid=grid,
              a_spec=pl.BlockSpec((tm, tk), lambda i, j, k: (i, k)),
              b_spec=w.spec(tn, tk, lambda i, j, k: (j, k)),
              acc_shape=(tm, tn), outs=outs, extras=extras, epilogue=epilogue, job=job)
    return res[0] if job is None else (res[0], list(res[1:]))


def _mm_tn(name, a, b, kind, *, tm, tn, tk):
    kdim, m = a.shape
    n = b.shape[1]
    grid = (m // tm, n // tn, kdim // tk)
    out = _Sharded(_grad_buffer(m, n, kind), kind)
    outs = [(out.arr, out.spec(tm, tn, lambda i, j, k: (i, j)))]
    return _mm(name, a, b, dims=TN, grid=grid,
               a_spec=pl.BlockSpec((tk, tm), lambda i, j, k: (k, i)),
               b_spec=pl.BlockSpec((tk, tn), lambda i, j, k: (k, j)),
               acc_shape=(tm, tn), outs=outs)[0]


def _row_spec(tr, w, col=0):
    return pl.BlockSpec((tr, w), lambda i: (i, col))


def _vec_spec(w, col=0, rows=1):
    return pl.BlockSpec((rows, w), lambda i: (0, col))


def _norm_fwd(x, g, scale, shift, tr):
    s, d = x.shape

    def body(x_ref, g_ref, sc_ref, sh_ref, h_ref, ht_ref):
        xv = x_ref[...]
        r = lax.rsqrt(jnp.mean(xv * xv, axis=-1, keepdims=True) + RMS_EPS)
        y = xv * r * g_ref[...]
        hv = y * (1.0 + sc_ref[...]) + sh_ref[...]
        h_ref[...] = hv.astype(BF16)
        ht_ref[...] = hv.T.astype(BF16)

    return pl.pallas_call(
        body, name="norm_fwd", grid=(s // tr,),
        in_specs=[_row_spec(tr, d), _vec_spec(d), _vec_spec(d), _vec_spec(d)],
        out_specs=[_row_spec(tr, d), pl.BlockSpec((d, tr), lambda i: (0, i))],
        out_shape=[jax.ShapeDtypeStruct((s, d), BF16), jax.ShapeDtypeStruct((d, s), BF16)],
        compiler_params=_params(("parallel",)),
    )(x, g, scale, shift)


def _norm_bwd(dh, x, g, scale, dxo, tr):
    s, d = x.shape

    def body(dh_ref, x_ref, g_ref, sc_ref, dxo_ref, dx_ref, dsh_ref, dsc_ref, dg_ref):
        @pl.when(pl.program_id(0) == 0)
        def _():
            dsh_ref[...] = jnp.zeros_like(dsh_ref)
            dsc_ref[...] = jnp.zeros_like(dsc_ref)
            dg_ref[...] = jnp.zeros_like(dg_ref)

        xv, dhv, gv = x_ref[...], dh_ref[...], g_ref[...]
        r = lax.rsqrt(jnp.mean(xv * xv, axis=-1, keepdims=True) + RMS_EPS)
        xn = xv * r
        dsh_ref[...] += jnp.sum(dhv, axis=0, keepdims=True)
        dsc_ref[...] += jnp.sum(dhv * (xn * gv), axis=0, keepdims=True)
        dyg = dhv * (1.0 + sc_ref[...])
        dg_ref[...] += jnp.sum(dyg * xn, axis=0, keepdims=True)
        dxn = dyg * gv
        dx_ref[...] = dxo_ref[...] + r * (dxn - xn * jnp.mean(dxn * xn, axis=-1, keepdims=True))

    vec = jax.ShapeDtypeStruct((1, d), F32)
    return pl.pallas_call(
        body, name="norm_bwd", grid=(s // tr,),
        in_specs=[_row_spec(tr, d), _row_spec(tr, d), _vec_spec(d), _vec_spec(d), _row_spec(tr, d)],
        out_specs=[_row_spec(tr, d), _vec_spec(d), _vec_spec(d), _vec_spec(d)],
        out_shape=[jax.ShapeDtypeStruct((s, d), F32), vec, vec, vec],
        compiler_params=_params(("arbitrary",)),
    )(dh, x, g, scale, dxo)


def _final_loss(x, g, target, tr):
    s, d = x.shape

    def body(x_ref, g_ref, t_ref, loss_ref, dx_ref, dg_ref):
        @pl.when(pl.program_id(0) == 0)
        def _():
            loss_ref[...] = jnp.zeros_like(loss_ref)
            dg_ref[...] = jnp.zeros_like(dg_ref)

        xv, gv = x_ref[...], g_ref[...]
        r = lax.rsqrt(jnp.mean(xv * xv, axis=-1, keepdims=True) + RMS_EPS)
        xn = xv * r
        err = xn * gv - t_ref[...]
        per_row = jnp.mean(err * err, axis=-1, keepdims=True)
        loss_ref[...] += 0.5 * jnp.sum(per_row, axis=0, keepdims=True)
        dy = err * (1.0 / d)
        dg_ref[...] += jnp.sum(dy * xn, axis=0, keepdims=True)
        dxn = dy * gv
        dx_ref[...] = r * (dxn - xn * jnp.mean(dxn * xn, axis=-1, keepdims=True))

    return pl.pallas_call(
        body, name="final_loss", grid=(s // tr,),
        in_specs=[_row_spec(tr, d), _vec_spec(d), _row_spec(tr, d)],
        out_specs=[pl.BlockSpec((8, 128), lambda i: (0, 0)), _row_spec(tr, d), _vec_spec(d)],
        out_shape=[jax.ShapeDtypeStruct((8, 128), F32), jax.ShapeDtypeStruct((s, d), F32),
                   jax.ShapeDtypeStruct((1, d), F32)],
        compiler_params=_params(("arbitrary",)),
    )(x, g, target)


def _halo_before(tr, halo, w, col):
    per = tr // halo
    return pl.BlockSpec((halo, w), lambda i: (jnp.maximum(i * per - 1, 0), col))


def _halo_after(tr, halo, w, col, n_tiles):
    per = tr // halo
    return pl.BlockSpec((halo, w), lambda i: (jnp.minimum((i + 1) * per, n_tiles * per - 1), col))


def _pool_fwd(proj, pool_w, pool_scale, d, tr):
    s = proj.shape[0]
    pw, gd, hl = d // 2, d // 8, POOL_HALO

    def body(xa_ref, xh_ref, za_ref, w_ref, ps_ref, ya_ref, buf):
        i = pl.program_id(0)
        buf[0:hl, :] = jnp.where(i > 0, xh_ref[...], 0.0)
        buf[hl:hl + tr, :] = xa_ref[...]
        row = i * tr + lax.broadcasted_iota(jnp.int32, (tr, 1), 0)
        za = za_ref[...]
        gate = za * _sigmoid(za)
        for g, win in enumerate(POOL_WINDOWS):
            cs = slice(g * gd, (g + 1) * gd)
            xg = buf[hl:hl + tr, cs]
            acc = xg
            for j in range(1, win):
                acc = acc + buf[hl - j:hl - j + tr, cs]
            cnt = jnp.minimum(row + 1, win).astype(F32)
            mixed = acc / cnt - xg
            y = jnp.dot(mixed.astype(BF16), w_ref[g], preferred_element_type=F32)
            ya_ref[:, cs] = ((y * ps_ref[:, cs]) * gate[:, cs]).astype(BF16)

    return pl.pallas_call(
        body, name="pool_fwd", grid=(s // tr,),
        in_specs=[_row_spec(tr, pw, 0), _halo_before(tr, hl, pw, 0), _row_spec(tr, pw, 1),
                  pl.BlockSpec((N_GROUPS, gd, gd), lambda i: (0, 0, 0)), _vec_spec(pw)],
        out_specs=_row_spec(tr, pw), out_shape=jax.ShapeDtypeStruct((s, pw), BF16),
        scratch_shapes=[pltpu.VMEM((tr + hl, pw), F32)],
        compiler_params=_params(("parallel",)),
    )(proj, proj, proj, pool_w, pool_scale)


def _pool_bwd(proj, dya, pool_w, pool_scale, d, tr):
    s = proj.shape[0]
    pw, gd, hl = d // 2, d // 8, POOL_HALO
    n_tiles = s // tr

    def body(xa_ref, xh_ref, za_ref, zh_ref, dya_ref, dyh_ref, w_ref, ps_ref, da_ref, gw_ref, gs_ref, buf, dbuf):
        i = pl.program_id(0)

        @pl.when(i == 0)
        def _():
            gw_ref[...] = jnp.zeros_like(gw_ref)
            gs_ref[...] = jnp.zeros_like(gs_ref)

        buf[0:hl, :] = jnp.where(i > 0, xh_ref[...], 0.0)
        buf[hl:hl + tr, :] = xa_ref[...]
        row = i * tr + lax.broadcasted_iota(jnp.int32, (tr, 1), 0)
        row_h = (i + 1) * tr + lax.broadcasted_iota(jnp.int32, (hl, 1), 0)
        za, dya_t = za_ref[...], dya_ref[...]
        sg = _sigmoid(za)
        gate = za * sg
        dpre = dya_t * gate
        zh = zh_ref[...]
        dpre_h = jnp.where(i < n_tiles - 1, dyh_ref[...], 0.0) * (zh * _sigmoid(zh))
        ps = ps_ref[...]
        for g, win in enumerate(POOL_WINDOWS):
            cs = slice(g * gd, (g + 1) * gd)
            wg = w_ref[g]
            xg = buf[hl:hl + tr, cs]
            acc = xg
            for j in range(1, win):
                acc = acc + buf[hl - j:hl - j + tr, cs]
            cnt = jnp.minimum(row + 1, win).astype(F32)
            mixed = (acc / cnt - xg).astype(BF16)
            ylin = jnp.dot(mixed, wg, preferred_element_type=F32)
            gs_ref[:, cs] += jnp.sum(dpre[:, cs] * ylin, axis=0, keepdims=True)
            da_ref[:, pw + g * gd:pw + (g + 1) * gd] = (
                dya_t[:, cs] * (ylin * ps[:, cs]) * _dsilu(za[:, cs], sg[:, cs])).astype(BF16)
            dyl = (dpre[:, cs] * ps[:, cs]).astype(BF16)
            gw_ref[g] += lax.dot_general(mixed, dyl, (TN, ((), ())), preferred_element_type=F32)
            dmix = lax.dot_general(dyl, wg, (NT, ((), ())), preferred_element_type=F32)
            dyl_h = (dpre_h[:, cs] * ps[:, cs]).astype(BF16)
            dmix_h = lax.dot_general(dyl_h, wg, (NT, ((), ())), preferred_element_type=F32)
            cnt_h = jnp.minimum(row_h + 1, win).astype(F32)
            dbuf[0:tr, cs] = dmix / cnt
            dbuf[tr:tr + hl, cs] = dmix_h / cnt_h
            dx = dbuf[0:tr, cs] - dmix
            for j in range(1, win):
                dx = dx + dbuf[j:j + tr, cs]
            da_ref[:, cs] = dx.astype(BF16)

    return pl.pallas_call(
        body, name="pool_bwd", grid=(n_tiles,),
        in_specs=[_row_spec(tr, pw, 0), _halo_before(tr, hl, pw, 0),
                  _row_spec(tr, pw, 1), _halo_after(tr, hl, pw, 1, n_tiles),
                  _row_spec(tr, pw, 0), _halo_after(tr, hl, pw, 0, n_tiles),
                  pl.BlockSpec((N_GROUPS, gd, gd), lambda i: (0, 0, 0)), _vec_spec(pw)],
        out_specs=[_row_spec(tr, 2 * pw), pl.BlockSpec((N_GROUPS, gd, gd), lambda i: (0, 0, 0)), _vec_spec(pw)],
        out_shape=[jax.ShapeDtypeStruct((s, 2 * pw), BF16), jax.ShapeDtypeStruct((N_GROUPS, gd, gd), F32),
                   jax.ShapeDtypeStruct((1, pw), F32)],
        scratch_shapes=[pltpu.VMEM((tr + hl, pw), F32), pltpu.VMEM((tr + hl, pw), F32)],
        compiler_params=_params(("arbitrary",)),
    )(proj, proj, proj, proj, dya, dya, pool_w, pool_scale)


def _conv_fwd(proj, conv_w, d, tr):
    s = proj.shape[0]
    cc, hl = d // 2, CONV_HALO
    cu, cb, cg_, cz = 10, 11, 12, 13

    def body(u_ref, uh_ref, bg_ref, cg_ref, ch_ref, zc_ref, w_ref, yc_ref, buf):
        i = pl.program_id(0)
        buf[0:hl, :] = jnp.where(i > 0, ch_ref[...] * uh_ref[...], 0.0)
        buf[hl:hl + tr, :] = cg_ref[...] * u_ref[...]
        y = w_ref[0:1, :] * buf[hl - 2:hl - 2 + tr, :]
        y = y + w_ref[1:2, :] * buf[hl - 1:hl - 1 + tr, :]
        y = y + w_ref[2:3, :] * buf[hl:hl + tr, :]
        zc = zc_ref[...]
        yc_ref[...] = ((bg_ref[...] * y) * (zc * _sigmoid(zc))).astype(BF16)

    return pl.pallas_call(
        body, name="conv_fwd", grid=(s // tr,),
        in_specs=[_row_spec(tr, cc, cu), _halo_before(tr, hl, cc, cu), _row_spec(tr, cc, cb),
                  _row_spec(tr, cc, cg_), _halo_before(tr, hl, cc, cg_), _row_spec(tr, cc, cz),
                  _vec_spec(cc, rows=3)],
        out_specs=_row_spec(tr, cc), out_shape=jax.ShapeDtypeStruct((s, cc), BF16),
        scratch_shapes=[pltpu.VMEM((tr + hl, cc), F32)],
        compiler_params=_params(("parallel",)),
    )(proj, proj, proj, proj, proj, proj, conv_w)


def _conv_bwd(proj, dyc, conv_w, d, tr):
    s = proj.shape[0]
    cc, hl = d // 2, CONV_HALO
    cu, cb, cg_, cz = 10, 11, 12, 13
    n_tiles = s // tr

    def body(u_ref, uh_ref, bg_ref, bh_ref, cg_ref, ch_ref, zc_ref, zh_ref, dy_ref, dyh_ref, w_ref,
             dc_ref, gw_ref, buf, dbuf):
        i = pl.program_id(0)

        @pl.when(i == 0)
        def _():
            gw_ref[...] = jnp.zeros_like(gw_ref)

        u, bg, cg, zc, dyc_t = u_ref[...], bg_ref[...], cg_ref[...], zc_ref[...], dy_ref[...]
        buf[0:hl, :] = jnp.where(i > 0, ch_ref[...] * uh_ref[...], 0.0)
        buf[hl:hl + tr, :] = cg * u
        v2, v1, v0 = buf[hl - 2:hl - 2 + tr, :], buf[hl - 1:hl - 1 + tr, :], buf[hl:hl + tr, :]
        w0, w1, w2 = w_ref[0:1, :], w_ref[1:2, :], w_ref[2:3, :]
        y = w0 * v2 + w1 * v1 + w2 * v0
        sg = _sigmoid(zc)
        gate = zc * sg
        dc_ref[:, cc:2 * cc] = (dyc_t * y * gate).astype(BF16)
        dc_ref[:, 3 * cc:4 * cc] = (dyc_t * bg * y * _dsilu(zc, sg)).astype(BF16)
        dy = dyc_t * bg * gate
        zh = zh_ref[...]
        dy_h = jnp.where(i < n_tiles - 1, dyh_ref[...], 0.0) * bh_ref[...] * (zh * _sigmoid(zh))
        gw_ref[0:1, :] += jnp.sum(dy * v2, axis=0, keepdims=True)
        gw_ref[1:2, :] += jnp.sum(dy * v1, axis=0, keepdims=True)
        gw_ref[2:3, :] += jnp.sum(dy * v0, axis=0, keepdims=True)
        dbuf[0:tr, :] = dy
        dbuf[tr:tr + hl, :] = dy_h
        dv = w2 * dy + w1 * dbuf[1:1 + tr, :] + w0 * dbuf[2:2 + tr, :]
        dc_ref[:, 0:cc] = (dv * cg).astype(BF16)
        dc_ref[:, 2 * cc:3 * cc] = (dv * u).astype(BF16)

    return pl.pallas_call(
        body, name="conv_bwd", grid=(n_tiles,),
        in_specs=[_row_spec(tr, cc, cu), _halo_before(tr, hl, cc, cu),
                  _row_spec(tr, cc, cb), _halo_after(tr, hl, cc, cb, n_tiles),
                  _row_spec(tr, cc, cg_), _halo_before(tr, hl, cc, cg_),
                  _row_spec(tr, cc, cz), _halo_after(tr, hl, cc, cz, n_tiles),
                  _row_spec(tr, cc, 0), _halo_after(tr, hl, cc, 0, n_tiles),
                  _vec_spec(cc, rows=3)],
        out_specs=[_row_spec(tr, 4 * cc), _vec_spec(cc, rows=3)],
        out_shape=[jax.ShapeDtypeStruct((s, 4 * cc), BF16), jax.ShapeDtypeStruct((3, cc), F32)],
        scratch_shapes=[pltpu.VMEM((tr + hl, cc), F32), pltpu.VMEM((tr + hl, cc), F32)],
        compiler_params=_params(("arbitrary",)),
    )(proj, proj, proj, proj, proj, proj, proj, proj, dyc, dyc, conv_w)


def _merge_bwd(dm, gates, pa, pb, pc, tr):
    s, d = pa.shape

    def body(dm_ref, g_ref, a_ref, b_ref, c_ref, da_ref, db_ref, dc_ref, dg_ref, gb_ref):
        @pl.when(pl.program_id(0) == 0)
        def _():
            gb_ref[...] = jnp.zeros_like(gb_ref)

        dmv = dm_ref[...]
        for n, (p_ref, o_ref) in enumerate(((a_ref, da_ref), (b_ref, db_ref), (c_ref, dc_ref))):
            gv = g_ref[:, n * d:(n + 1) * d]
            o_ref[...] = (dmv * gv).astype(BF16)
            dlogit = (dmv * p_ref[...]) * (gv * (1.0 - gv))
            dg_ref[:, n * d:(n + 1) * d] = dlogit.astype(BF16)
            gb_ref[:, n * d:(n + 1) * d] += jnp.sum(dlogit, axis=0, keepdims=True)

    act = jax.ShapeDtypeStruct((s, d), BF16)
    return pl.pallas_call(
        body, name="merge_bwd", grid=(s // tr,),
        in_specs=[_row_spec(tr, d), _row_spec(tr, 3 * d), _row_spec(tr, d), _row_spec(tr, d), _row_spec(tr, d)],
        out_specs=[_row_spec(tr, d), _row_spec(tr, d), _row_spec(tr, d), _row_spec(tr, 3 * d), _vec_spec(3 * d)],
        out_shape=[act, act, act, jax.ShapeDtypeStruct((s, 3 * d), BF16), jax.ShapeDtypeStruct((1, 3 * d), F32)],
        compiler_params=_params(("arbitrary",)),
    )(dm, gates, pa, pb, pc)


def _resid_bwd(dxo, out, rg, tr):
    s, d = dxo.shape

    def body(dx_ref, o_ref, rg_ref, do_ref, drg_ref):
        @pl.when(pl.program_id(0) == 0)
        def _():
            drg_ref[...] = jnp.zeros_like(drg_ref)

        dxv = dx_ref[...]
        do_ref[...] = (dxv * rg_ref[...]).astype(BF16)
        drg_ref[...] += jnp.sum(dxv * o_ref[...], axis=0, keepdims=True)

    return pl.pallas_call(
        body, name="resid_bwd", grid=(s // tr,),
        in_specs=[_row_spec(tr, d), _row_spec(tr, d), _vec_spec(d)],
        out_specs=[_row_spec(tr, d), _vec_spec(d)],
        out_shape=[jax.ShapeDtypeStruct((s, d), BF16), jax.ShapeDtypeStruct((1, d), F32)],
        compiler_params=_params(("arbitrary",)),
    )(dxo, out, rg)


def _log_keep(z):
    e = jnp.exp(-jnp.abs(z))
    return -(jnp.maximum(z, 0.0) + jnp.log(1.0 + e)), e


class _Job:
    def __init__(self, ins, out_shapes, aliases, n_sems, start, finish, mid=None):
        self.ins, self.out_shapes, self.aliases, self.n_sems = list(ins), list(out_shapes), list(aliases), n_sems
        self.start, self.mid, self.finish = start, mid, finish


def _host(job, n_in, n_out):
    if job is None:
        return dict(ins=[], in_specs=[], out_specs=[], out_shape=[], scratch=[], aliases={})
    return dict(ins=job.ins, in_specs=[_ANY] * len(job.ins), out_specs=[_ANY] * len(job.out_shapes),
                out_shape=job.out_shapes,
                scratch=[pltpu.SemaphoreType.DMA((job.n_sems,)), pltpu.SemaphoreType.DMA((job.n_sems,))],
                aliases={n_in + a: n_out + b for a, b in job.aliases})


def _job_refs(job, refs, n_in, n_out):
    if job is None:
        return refs[:n_in], refs[n_in:n_in + n_out], None
    ji, jo = len(job.ins), len(job.out_shapes)
    own_in, job_in = refs[:n_in], refs[n_in:n_in + ji]
    own_out, job_out = refs[n_in + ji:n_in + ji + n_out], refs[n_in + ji + n_out:n_in + ji + n_out + jo]
    send_sems, recv_sems = refs[n_in + ji + n_out + jo:]
    return own_in, own_out, (job_in, job_out, send_sems, recv_sems)


def _sb_fwd(proj, projb, d, blk, job=None):
    s = proj.shape[0]
    hps, wid = SB_FWD_HEADS, SB_FWD_HEADS * HEAD_DIM
    groups, nq = d // wid, s // blk
    qc, kc, vc, zc = d // wid, 2 * d // wid, 3 * d // wid, 4 * d // wid
    assert nq <= HEAD_DIM
    mid_step = (groups - 1, nq // 3)

    def body(*refs):
        (q_ref, k_ref, v_ref, zb_ref), (att_ref, yb_ref, rs_ref), jargs = _job_refs(job, refs, 4, 3)
        grp, i = pl.program_id(0), pl.program_id(1)
        if job is not None:
            @pl.when((grp == 0) & (i == 0))
            def _():
                job.start(*jargs)

            if job.mid is not None:
                @pl.when((grp == mid_step[0]) & (i == mid_step[1]))
                def _():
                    job.mid(*jargs)

        r_io = lax.broadcasted_iota(jnp.int32, (blk, blk), 0)
        c_io = lax.broadcasted_iota(jnp.int32, (blk, blk), 1)
        tri = (r_io >= c_io).astype(BF16)
        strict = c_io < r_io
        lane = lax.broadcasted_iota(jnp.int32, (blk, HEAD_DIM), 1)
        heads = [slice(n * HEAD_DIM, (n + 1) * HEAD_DIM) for n in range(hps)]
        qs = [q_ref[:, hd] for hd in heads]

        def block(j, carry, masked):
            ks = pl.multiple_of(j * blk, blk)
            rng = range(hps)
            zs = [lax.dot_general(qs[n], k_ref[pl.ds(ks, blk), heads[n]], (NT, ((), ())),
                                  preferred_element_type=F32) for n in rng]
            lks = []
            for n in rng:
                lk, _ = _log_keep(zs[n])
                if masked:
                    lk = jnp.where(strict, lk, 0.0)
                lks.append(lk.astype(BF16))
            csums = [carry[n][0] + jnp.dot(lks[n], tri, preferred_element_type=F32) for n in rng]
            probs = []
            for n in rng:
                a = jnp.exp(zs[n] + csums[n])
                if masked:
                    a = jnp.where(strict, a, 0.0)
                probs.append(a.astype(BF16))
            out = []
            for n in rng:
                run, acc, rs = carry[n]
                acc = acc + jnp.dot(probs[n], v_ref[pl.ds(ks, blk), heads[n]], preferred_element_type=F32)
                out.append((csums[n][:, 0:1], acc, jnp.where(lane == j, run, rs)))
            return tuple(out)

        zero = (jnp.zeros((blk, 1), F32), jnp.zeros((blk, HEAD_DIM), F32), jnp.zeros((blk, HEAD_DIM), F32))
        carry = block(i, (zero,) * hps, True)
        carry = lax.fori_loop(0, i, lambda jj, cr: block(i - 1 - jj, cr, False), carry)
        for hd, (_, acc, rs) in zip(heads, carry):
            att_ref[:, hd] = acc
            rs_ref[:, hd] = rs
            zb = zb_ref[:, hd]
            yb_ref[:, hd] = (acc * (zb * _sigmoid(zb))).astype(BF16)

        if job is not None:
            @pl.when((grp == groups - 1) & (i == nq - 1))
            def _():
                job.finish(*jargs)

    blk_spec = pl.BlockSpec((blk, wid), lambda h, i: (i, h))
    hosted = _host(job, 4, 3)
    res = pl.pallas_call(
        body, name="sb_fwd", grid=(groups, nq),
        in_specs=[pl.BlockSpec((blk, wid), lambda h, i: (i, qc + h)),
                  pl.BlockSpec((s, wid), lambda h, i: (0, kc + h)),
                  pl.BlockSpec((s, wid), lambda h, i: (0, vc + h)),
                  pl.BlockSpec((blk, wid), lambda h, i: (i, zc + h))] + hosted["in_specs"],
        out_specs=[blk_spec, blk_spec, blk_spec] + hosted["out_specs"],
        out_shape=[jax.ShapeDtypeStruct((s, d), F32), jax.ShapeDtypeStruct((s, d), BF16),
                   jax.ShapeDtypeStruct((s, d), F32)] + hosted["out_shape"],
        scratch_shapes=hosted["scratch"], input_output_aliases=hosted["aliases"],
        compiler_params=_params(("arbitrary", "arbitrary")),
    )(projb, projb, projb, proj, *hosted["ins"])
    return res[0], res[1], res[2], list(res[3:])


def _sb_bwd(proj, projb, att, dyb, rsave, d, blk, job=None):
    s = proj.shape[0]
    hps, wid = SB_BWD_HEADS, SB_BWD_HEADS * HEAD_DIM
    groups, nq = d // wid, s // blk
    qc, kc, vc, zc = d // wid, 2 * d // wid, 3 * d // wid, 4 * d // wid
    scale = HEAD_DIM ** -0.5

    def body(*refs):
        ((q_ref, k_ref, v_ref, zb_ref, att_ref, dyb_ref, rs_ref), (dq_ref, dk_ref, dv_ref, dzb_ref),
         jargs) = _job_refs(job, refs, 7, 4)
        grp, i = pl.program_id(0), pl.program_id(1)
        if job is not None:
            @pl.when((grp == 0) & (i == 0))
            def _():
                job.start(*jargs)

        @pl.when(i == 0)
        def _():
            dk_ref[...] = jnp.zeros_like(dk_ref)
            dv_ref[...] = jnp.zeros_like(dv_ref)

        r_io = lax.broadcasted_iota(jnp.int32, (blk, blk), 0)
        c_io = lax.broadcasted_iota(jnp.int32, (blk, blk), 1)
        tri = (r_io >= c_io).astype(BF16)
        tri_up = (r_io <= c_io).astype(BF16)
        strict = c_io < r_io
        lane = lax.broadcasted_iota(jnp.int32, (blk, HEAD_DIM), 1)
        heads = [slice(n * HEAD_DIM, (n + 1) * HEAD_DIM) for n in range(hps)]
        qs, dos, rss = [], [], []
        for hd in heads:
            qs.append(q_ref[:, hd])
            zb, dyb_t = zb_ref[:, hd], dyb_ref[:, hd]
            sg = _sigmoid(zb)
            dzb_ref[:, hd] = dyb_t * att_ref[:, hd] * _dsilu(zb, sg)
            dos.append((dyb_t * (zb * sg)).astype(BF16))
            rss.append(rs_ref[:, hd])

        def block(j, carry, masked):
            ks = pl.multiple_of(j * blk, blk)
            rng = range(hps)
            kbs = [k_ref[pl.ds(ks, blk), heads[n]] for n in rng]
            vbs = [v_ref[pl.ds(ks, blk), heads[n]] for n in rng]
            zs = [lax.dot_general(qs[n], kbs[n], (NT, ((), ())), preferred_element_type=F32) for n in rng]
            das = [lax.dot_general(dos[n], vbs[n], (NT, ((), ())), preferred_element_type=F32) for n in rng]
            lks, betas = [], []
            for n in rng:
                lk, e = _log_keep(zs[n])
                betas.append(jnp.where(zs[n] >= 0, 1.0, e) / (1.0 + e))
                if masked:
                    lk = jnp.where(strict, lk, 0.0)
                lks.append(lk.astype(BF16))
            csums = []
            for n in rng:
                run = jnp.sum(jnp.where(lane == j, rss[n], 0.0), axis=1, keepdims=True)
                csums.append(run + jnp.dot(lks[n], tri, preferred_element_type=F32))
            probs, gs = [], []
            for n in rng:
                a = jnp.exp(zs[n] + csums[n])
                if masked:
                    a = jnp.where(strict, a, 0.0)
                probs.append(a.astype(BF16))
                gs.append(a * das[n])
            gcums = [carry[n][0] + jnp.dot(gs[n].astype(BF16), tri_up, preferred_element_type=F32) for n in rng]
            for n in rng:
                dv_ref[pl.ds(ks, blk), heads[n]] += lax.dot_general(probs[n], dos[n], (TN, ((), ())),
                                                                   preferred_element_type=F32)
            dzs = []
            for n in rng:
                dz = gs[n] - betas[n] * gcums[n]
                if masked:
                    dz = jnp.where(strict, dz, 0.0)
                dzs.append(dz.astype(BF16))
            out = []
            for n in rng:
                dq = carry[n][1] + jnp.dot(dzs[n], kbs[n], preferred_element_type=F32)
                dk_ref[pl.ds(ks, blk), heads[n]] += lax.dot_general(dzs[n], qs[n], (TN, ((), ())),
                                                                   preferred_element_type=F32)
                out.append((gcums[n][:, blk - 1:blk], dq))
            return tuple(out)

        zero = (jnp.zeros((blk, 1), F32), jnp.zeros((blk, HEAD_DIM), F32))
        carry = lax.fori_loop(0, i, lambda j, cr: block(j, cr, False), (zero,) * hps)
        carry = block(i, carry, True)
        for hd, (_, dq) in zip(heads, carry):
            dq_ref[:, hd] = dq * scale

        if job is not None:
            @pl.when((grp == groups - 1) & (i == nq - 1))
            def _():
                job.finish(*jargs)

    blk_spec = pl.BlockSpec((blk, wid), lambda h, i: (i, h))
    full_spec = pl.BlockSpec((s, wid), lambda h, i: (0, h), pipeline_mode=pl.Buffered(1))
    act = jax.ShapeDtypeStruct((s, d), F32)
    hosted = _host(job, 7, 4)
    res = pl.pallas_call(
        body, name="sb_bwd", grid=(groups, nq),
        in_specs=[pl.BlockSpec((blk, wid), lambda h, i: (i, qc + h)),
                  pl.BlockSpec((s, wid), lambda h, i: (0, kc + h)),
                  pl.BlockSpec((s, wid), lambda h, i: (0, vc + h)),
                  pl.BlockSpec((blk, wid), lambda h, i: (i, zc + h)),
                  blk_spec, blk_spec, blk_spec] + hosted["in_specs"],
        out_specs=[blk_spec, full_spec, full_spec, blk_spec] + hosted["out_specs"],
        out_shape=[act, act, act, act] + hosted["out_shape"],
        scratch_shapes=hosted["scratch"], input_output_aliases=hosted["aliases"],
        compiler_params=_params(("arbitrary", "arbitrary")),
    )(projb, projb, projb, proj, att, dyb, rsave, *hosted["ins"])
    return res[0], res[1], res[2], res[3], list(res[4:])


def _adamw(w, g, m, v):
    shape = w.shape
    cols = shape[-1]
    rows = w.size // cols
    w2, g2, m2, v2 = (t.reshape(rows, cols) for t in (w, g, m, v))
    tr, tc = _tile(rows, 512, 8), _tile(cols, 1024, 128)
    c1 = 1.0 - ADAM_B1 ** ADAM_STEP
    c2 = 1.0 - ADAM_B2 ** ADAM_STEP

    def body(w_ref, g_ref, m_ref, v_ref, d_ref, nm_ref, nv_ref):
        gv = g_ref[...]
        nm = ADAM_B1 * m_ref[...] + (1.0 - ADAM_B1) * gv
        nv = ADAM_B2 * v_ref[...] + (1.0 - ADAM_B2) * (gv * gv)
        d_ref[...] = -ADAM_LR * ((nm / c1) / (jnp.sqrt(nv / c2) + ADAM_EPS) + ADAM_WD * w_ref[...])
        nm_ref[...] = nm
        nv_ref[...] = nv

    spec = pl.BlockSpec((tr, tc), lambda i, j: (i, j))
    sd = jax.ShapeDtypeStruct((rows, cols), F32)
    outs = pl.pallas_call(
        body, name="adamw", grid=(rows // tr, cols // tc),
        in_specs=[spec] * 4, out_specs=[spec] * 3, out_shape=[sd] * 3,
        compiler_params=_params(("parallel", "parallel")),
    )(w2, g2, m2, v2)
    return tuple(o.reshape(shape) for o in outs)


def _sum_blocks(gathered, n, rows):
    width = gathered.shape[1]
    tw = _tile(width, 8192)

    def body(g_ref, o_ref):
        acc = g_ref[0:rows, :]
        for b in range(1, n):
            acc = acc + g_ref[b * rows:(b + 1) * rows, :]
        o_ref[...] = acc

    return pl.pallas_call(
        body, name="sum_blocks", grid=(width // tw,),
        in_specs=[pl.BlockSpec((n * rows, tw), lambda i: (0, i))],
        out_specs=pl.BlockSpec((rows, tw), lambda i: (0, i)),
        out_shape=jax.ShapeDtypeStruct((rows, width), F32),
        compiler_params=_params(("parallel",)),
    )(gathered)


def _silu_bf16(c_rows):
    def body(c_ref, o_ref):
        cv = c_ref[...]
        o_ref[...] = (cv * _sigmoid(cv)).astype(BF16)

    return pl.pallas_call(
        body, name="silu_c", out_shape=jax.ShapeDtypeStruct(c_rows.shape, BF16),
        in_specs=[pl.BlockSpec(memory_space=pltpu.VMEM)], out_specs=pl.BlockSpec(memory_space=pltpu.VMEM),
    )(c_rows)


def _place():
    x, y, c = lax.axis_index("x"), lax.axis_index("y"), lax.axis_index("c")
    chips = [(1 - x, y), (x, 1 - y), (1 - x, 1 - y)]
    return x, y, c, chips


def _all_gather_small(block, name):
    m_per, n = block.shape

    def body(x_ref, out_ref, send_sems, recv_sems, local_sem):
        x, y, c, chips = _place()
        me, sibling = (x, y, c), (x, y, 1 - c)

        def rows(px, py, pc):
            return out_ref.at[pl.ds((4 * px + 2 * py + pc) * m_per, m_per), :]

        def copy(k, blk, to, src=None):
            return pltpu.make_async_remote_copy(
                src_ref=rows(*blk) if src is None else src, dst_ref=rows(*blk),
                send_sem=send_sems.at[k], recv_sem=recv_sems.at[k], device_id=to, device_id_type=MESH)

        mine = pltpu.make_async_copy(x_ref, rows(*me), local_sem)
        mine.start()
        first = [copy(0, me, sibling, src=x_ref)]
        first += [copy(1 + j, me, (*chip, c), src=x_ref) for j, chip in enumerate(chips)]
        for cp in first:
            cp.start()
        passed = [copy(4 + j, (*chip, c), sibling) for j, chip in enumerate(chips)]
        for j, chip in enumerate(chips):
            copy(1 + j, (*chip, c), me).wait_recv()
            passed[j].start()
        copy(0, sibling, me).wait_recv()
        for j, chip in enumerate(chips):
            copy(4 + j, (*chip, 1 - c), me).wait_recv()
        for cp in first + passed:
            cp.wait_send()
        mine.wait()

    return pl.pallas_call(
        body, name=name, out_shape=jax.ShapeDtypeStruct((N_DEV * m_per, n), block.dtype),
        in_specs=[pl.BlockSpec(memory_space=pltpu.VMEM)], out_specs=pl.BlockSpec(memory_space=pltpu.VMEM),
        scratch_shapes=[pltpu.SemaphoreType.DMA((7,)), pltpu.SemaphoreType.DMA((7,)), pltpu.SemaphoreType.DMA],
        compiler_params=pltpu.CompilerParams(vmem_limit_bytes=V7X_VMEM_LIMIT),
    )(block)


_ANY = pl.BlockSpec(memory_space=pl.ANY)


def _my_chip():
    return 2 * lax.axis_index("x") + lax.axis_index("y")


def _place_shard(w):
    layers, r, cols = w.shape
    tr, tc = _tile(r, 512, 16), _tile(cols, 1024)

    def body(w_ref, o_ref):
        o_ref[...] = w_ref[...].astype(BF16)

    return pl.pallas_call(
        body, name="place_shard", grid=(layers, r // tr, cols // tc),
        in_specs=[pl.BlockSpec((None, tr, tc), lambda l, i, n: (l, i, n))],
        out_specs=pl.BlockSpec((None, None, tr, tc), lambda l, i, n: (_my_chip(), l, i, n)),
        out_shape=jax.ShapeDtypeStruct((N_CHIPS, layers, r, cols), BF16),
        compiler_params=_params(("parallel", "parallel", "parallel")),
    )(w)


def _gather_job(bufs, layer):
    n = len(bufs)

    def tools(outs, send_sems, recv_sems):
        x, y, c, chips = _place()

        def half(t, chip_idx, hc):
            h = outs[t].shape[2] // 2
            return outs[t].at[chip_idx, layer, pl.ds(hc * h, h), :]

        def copy(t, k, ref, to):
            return pltpu.make_async_remote_copy(src_ref=ref, dst_ref=ref, send_sem=send_sems.at[6 * t + k],
                                                recv_sem=recv_sems.at[6 * t + k], device_id=to, device_id_type=MESH)

        return x, y, c, chips, half, copy

    def start(ins, outs, send_sems, recv_sems):
        x, y, c, chips, half, copy = tools(outs, send_sems, recv_sems)
        for t in range(n):
            for k, chip in enumerate(chips):
                copy(t, k, half(t, 2 * x + y, c), (*chip, c)).start()

    def mid(ins, outs, send_sems, recv_sems):
        x, y, c, chips, half, copy = tools(outs, send_sems, recv_sems)
        for t in range(n):
            for k, (cx, cy) in enumerate(chips):
                landed = half(t, 2 * cx + cy, c)
                copy(t, k, landed, (cx, cy, c)).wait_recv()
                copy(t, 3 + k, landed, (x, y, 1 - c)).start()

    def finish(ins, outs, send_sems, recv_sems):
        x, y, c, chips, half, copy = tools(outs, send_sems, recv_sems)
        for t in range(n):
            for k, (cx, cy) in enumerate(chips):
                copy(t, 3 + k, half(t, 2 * cx + cy, 1 - c), (x, y, 1 - c)).wait_recv()
        for t in range(n):
            for k, (cx, cy) in enumerate(chips):
                copy(t, k, half(t, 2 * x + y, c), (cx, cy, c)).wait_send()
                copy(t, 3 + k, half(t, 2 * cx + cy, c), (x, y, 1 - c)).wait_send()

    return _Job(ins=bufs, out_shapes=[jax.ShapeDtypeStruct(b.shape, b.dtype) for b in bufs],
                aliases=[(t, t) for t in range(n)], n_sems=6 * n, start=start, mid=mid, finish=finish)


def _send_job(parts):
    n = len(parts)

    def copies(ins, outs, send_sems, recv_sems):
        x, y, c, chips = _place()
        return [pltpu.make_async_remote_copy(
            src_ref=ins[t].at[2 * cx + cy], dst_ref=outs[t].at[k], send_sem=send_sems.at[3 * t + k],
            recv_sem=recv_sems.at[3 * t + k], device_id=(cx, cy, c), device_id_type=MESH)
            for t in range(n) for k, (cx, cy) in enumerate(chips)]

    def start(*args):
        for cp in copies(*args):
            cp.start()

    def finish(*args):
        for cp in copies(*args):
            cp.wait()

    return _Job(ins=parts, out_shapes=[jax.ShapeDtypeStruct((3,) + p.shape[1:], p.dtype) for p in parts],
                aliases=[], n_sems=3 * n, start=start, finish=finish)


def _run_job(job, name):
    def body(*refs):
        _, _, jargs = _job_refs(job, refs, 0, 0)
        job.start(*jargs)
        if job.mid is not None:
            job.mid(*jargs)
        job.finish(*jargs)

    hosted = _host(job, 0, 0)
    return list(pl.pallas_call(
        body, name=name, out_shape=hosted["out_shape"], in_specs=hosted["in_specs"], out_specs=hosted["out_specs"],
        scratch_shapes=hosted["scratch"], input_output_aliases=hosted["aliases"],
    )(*hosted["ins"]))


def _swap_halves(grads):
    n = len(grads)

    def body(*refs):
        ins, outs = refs[:n], refs[n:2 * n]
        send_sems, recv_sems = refs[2 * n:]
        x, y, c, _ = _place()
        copies = []
        for t in range(n):
            h = ins[t].shape[1] // 2
            copies.append(pltpu.make_async_remote_copy(
                src_ref=ins[t].at[:, pl.ds((1 - c) * h, h), :], dst_ref=outs[t],
                send_sem=send_sems.at[t], recv_sem=recv_sems.at[t], device_id=(x, y, 1 - c), device_id_type=MESH))
            copies[-1].start()
        for cp in copies:
            cp.wait()

    return pl.pallas_call(
        body, name="swap_halves",
        out_shape=[jax.ShapeDtypeStruct((g.shape[0], g.shape[1] // 2, g.shape[2]), g.dtype) for g in grads],
        in_specs=[_ANY] * n, out_specs=[_ANY] * n,
        scratch_shapes=[pltpu.SemaphoreType.DMA((n,)), pltpu.SemaphoreType.DMA((n,))],
    )(*grads)


def _share_halves(bufs):
    n = len(bufs)

    def body(*refs):
        outs = refs[n:2 * n]
        send_sems, recv_sems = refs[2 * n:]
        x, y, c, _ = _place()

        def half(t, hc):
            h = outs[t].shape[1] // 2
            return outs[t].at[:, pl.ds(hc * h, h), :]

        def copy(t, ref):
            return pltpu.make_async_remote_copy(src_ref=ref, dst_ref=ref, send_sem=send_sems.at[t],
                                                recv_sem=recv_sems.at[t], device_id=(x, y, 1 - c),
                                                device_id_type=MESH)

        sends = [copy(t, half(t, c)) for t in range(n)]
        for cp in sends:
            cp.start()
        for t in range(n):
            copy(t, half(t, 1 - c)).wait_recv()
        for cp in sends:
            cp.wait_send()

    return pl.pallas_call(
        body, name="share_halves",
        out_shape=[jax.ShapeDtypeStruct(b.shape, b.dtype) for b in bufs],
        in_specs=[_ANY] * n, out_specs=[_ANY] * n, input_output_aliases={t: t for t in range(n)},
        scratch_shapes=[pltpu.SemaphoreType.DMA((n,)), pltpu.SemaphoreType.DMA((n,))],
    )(*bufs)


def _sum_sibling(grad, recv):
    _, r, cols = grad.shape
    h = r // 2
    tr, tc = _tile(h, 512, 16), _tile(cols, 1024)
    per = h // tr

    def body(g_ref, r_ref, o_ref):
        o_ref[...] = (g_ref[...].astype(F32) + r_ref[...].astype(F32)).astype(BF16)

    return pl.pallas_call(
        body, name="sum_sibling", grid=(N_CHIPS, per, cols // tc),
        in_specs=[pl.BlockSpec((None, tr, tc), lambda j, i, n: (j, lax.axis_index("c") * per + i, n)),
                  pl.BlockSpec((None, tr, tc), lambda j, i, n: (j, i, n))],
        out_specs=pl.BlockSpec((None, tr, tc), lambda j, i, n: (j, i, n)),
        out_shape=jax.ShapeDtypeStruct((N_CHIPS, h, cols), BF16),
        compiler_params=_params(("parallel", "parallel", "parallel")),
    )(grad, recv)


def _sum_owner(parts, recv, layer, layers, prev=None):
    _, h, cols = parts.shape
    tr, tc = _tile(h, 512, 16), _tile(cols, 1024)
    per = h // tr

    def body(p_ref, r_ref, *rest):
        o_ref = rest[-1]
        o_ref[...] = (p_ref[...].astype(F32) + r_ref[0].astype(F32) + r_ref[1].astype(F32)
                      + r_ref[2].astype(F32))

    in_specs = [pl.BlockSpec((None, tr, tc), lambda i, n: (_my_chip(), i, n)),
                pl.BlockSpec((3, tr, tc), lambda i, n: (0, i, n))]
    args = [parts, recv]
    aliases = {}
    if prev is not None:
        in_specs.append(_ANY)
        args.append(prev)
        aliases = {2: 0}
    return pl.pallas_call(
        body, name="sum_owner", grid=(per, cols // tc), in_specs=in_specs,
        out_specs=pl.BlockSpec((None, tr, tc), lambda i, n: (layer, lax.axis_index("c") * per + i, n)),
        out_shape=jax.ShapeDtypeStruct((layers, 2 * h, cols), F32),
        input_output_aliases=aliases,
        compiler_params=_params(("parallel", "parallel")),
    )(*args)


def _layer_fwd(x, mod, wbuf, p, layer, cfg, make_jobs):
    d, tr, tm = cfg["d"], cfg["tr"], cfg["tm"]
    s = x.shape[0]
    weight = lambda n: _Sharded(wbuf[n], BIG_KIND[n], layer=layer)
    hosted = lambda key: make_jobs[key](wbuf) if key in make_jobs else (None, ())
    shift, scale, rg = mod[:, 0:d], mod[:, d:2 * d], mod[:, 2 * d:3 * d]
    h, h_t = _norm_fwd(x, p["norm_g"], scale, shift, tr)
    tn_in = cfg["tn_in"]
    q_lo, q_hi, q_scale = d // tn_in, 2 * d // tn_in, HEAD_DIM ** -0.5
    blk_in = pl.BlockSpec((tm, tn_in), lambda i, j, k: (i, j))

    def proj_out(col, acc):
        return acc, acc * jnp.where((col >= q_lo) & (col < q_hi), q_scale, 1.0)

    job, names = hosted("proj")
    proj, projb, *job_out = _mm_nn("proj", h, weight("w_in"), tm=tm, tn=tn_in, tk=d, with_col=True,
                                   epilogue=proj_out, job=job,
                                   outs=[(jax.ShapeDtypeStruct((s, 7 * d), F32), blk_in),
                                         (jax.ShapeDtypeStruct((s, 7 * d), BF16), blk_in)])
    wbuf.update(zip(names, job_out))
    gates, = _mm_nn("gates", h, weight("w_gate"), tm=tm, tn=cfg["tn_gate"], tk=d,
                    extras=[(p["b_gate"], pl.BlockSpec((1, cfg["tn_gate"]), lambda i, j, k: (0, j)))],
                    epilogue=lambda acc, b: (_sigmoid(acc + b),))
    ya = _pool_fwd(proj, p["pool_w"], p["pool_scale"], d, tr)
    yc = _conv_fwd(proj, p["conv_w"], d, tr)
    job, names = hosted("attn")
    att, yb, rsave, job_out = _sb_fwd(proj, projb, d, cfg["blk"], job)
    wbuf.update(zip(names, job_out))
    tn = cfg["tn_d"]
    blk = pl.BlockSpec((tm, tn), lambda i, j, k: (i, j))
    sd = jax.ShapeDtypeStruct((s, d), F32)
    pa, = _mm_nn("branch_a", ya, weight("w_br_a"), tm=tm, tn=tn, tk=d // 2)
    pb, = _mm_nn("branch_b", yb, weight("w_br_b"), tm=tm, tn=tn, tk=cfg["tk_row"])
    gate_blk = lambda n: (gates, pl.BlockSpec((tm, tn), lambda i, j, k: (i, n * (d // tn) + j)))
    pc, merged = _mm_nn("branch_c", yc, weight("w_br_c"), tm=tm, tn=tn, tk=d // 2,
                        outs=[(sd, blk), (jax.ShapeDtypeStruct((s, d), BF16), blk)],
                        extras=[gate_blk(0), gate_blk(1), gate_blk(2), (pa, blk), (pb, blk)],
                        epilogue=lambda acc, g0, g1, g2, av, bv: (acc, g0 * av + g1 * bv + g2 * acc))
    out, x_next = _mm_nn("out_proj", merged, weight("w_out"), tm=tm, tn=tn, tk=cfg["tk_row"],
                         outs=[(sd, blk), (sd, blk)],
                         extras=[(x, blk), (rg, pl.BlockSpec((1, tn), lambda i, j, k: (0, j)))],
                         epilogue=lambda acc, xv, g: (acc, xv + g * acc))
    saved = dict(x=x, h_t=h_t, proj=proj, projb=projb, gates=gates, ya=ya, yb=yb, yc=yc, att=att, rsave=rsave,
                 pa=pa, pb=pb, pc=pc, merged=merged, out=out, scale=scale, rg=rg)
    return x_next, saved


def _reduce_prepare(grads):
    return [_sum_sibling(g, r) for g, r in zip(grads, _swap_halves(grads))]


def _layer_bwd(dxo, sv, wbuf, p, layer, cfg):
    d, tr, tm, tk_s = cfg["d"], cfg["tr"], cfg["tm"], cfg["tk_s"]
    weight = lambda n: _Sharded(wbuf[n], BIG_KIND[n], layer=layer)
    dout, drg = _resid_bwd(dxo, sv["out"], sv["rg"], tr)
    dmerged = _mm_nt("d_merged", dout, weight("w_out"), tm=tm, tn=cfg["tn_row"], tk=d)
    g_out = _mm_tn("g_w_out", sv["merged"], dout, "row", tm=cfg["tn_row"], tn=cfg["tn_d"], tk=tk_s)
    dpa, dpb, dpc, dlogit, g_bgate = _merge_bwd(dmerged, sv["gates"], sv["pa"], sv["pb"], sv["pc"], cfg["tr_small"])
    dya = _mm_nt("d_ya", dpa, weight("w_br_a"), tm=tm, tn=cfg["tn_half"], tk=cfg["tn_d"])
    dyb = _mm_nt("d_yb", dpb, weight("w_br_b"), tm=tm, tn=cfg["tn_row"], tk=d)
    dyc = _mm_nt("d_yc", dpc, weight("w_br_c"), tm=tm, tn=cfg["tn_half"], tk=cfg["tn_d"])
    g_a = _mm_tn("g_w_br_a", sv["ya"], dpa, "col", tm=cfg["tn_half"], tn=cfg["tn_d"], tk=tk_s)
    g_b = _mm_tn("g_w_br_b", sv["yb"], dpb, "row", tm=cfg["tn_row"], tn=cfg["tn_d"], tk=tk_s)
    g_c = _mm_tn("g_w_br_c", sv["yc"], dpc, "col", tm=cfg["tn_half"], tn=cfg["tn_d"], tk=tk_s)
    g_gate = _mm_grad("g_w_gate", sv["h_t"], dlogit, "col", tm=cfg["tm_g"], tn=cfg["tn_gate"], tk=tk_s)
    early = ("w_gate", "w_br_a", "w_br_b", "w_br_c", "w_out")
    parts = dict(zip(early, _reduce_prepare([g_gate, g_a, g_b, g_c, g_out])))
    d_a, g_pool_w, g_pool_scale = _pool_bwd(sv["proj"], dya, p["pool_w"], p["pool_scale"], d, tr)
    d_c, g_conv_w = _conv_bwd(sv["proj"], dyc, p["conv_w"], d, tr)
    dq, dk, dv, dzb, sent = _sb_bwd(sv["proj"], sv["projb"], sv["att"], dyb, sv["rsave"], d, cfg["blk"],
                                    _send_job([parts[n] for n in early]))
    theirs = dict(zip(early, sent))
    dproj = jnp.concatenate([d_a, dq.astype(BF16), dk.astype(BF16), dv.astype(BF16), dzb.astype(BF16), d_c], axis=1)
    g_in = _mm_grad("g_w_in", sv["h_t"], dproj, "col", tm=cfg["tm_g"], tn=cfg["tn_in"], tk=tk_s)
    parts["w_in"], = _reduce_prepare([g_in])
    dh_in, (theirs["w_in"],) = _mm_nt("d_h_in", dproj, weight("w_in"), tm=tm, tn=cfg["tn_d"], tk=cfg["tk_in"],
                                      job=_send_job([parts["w_in"]]))
    tn = cfg["tn_d"]
    dh = _mm_nt("d_h_gate", dlogit, weight("w_gate"), tm=tm, tn=tn, tk=cfg["tk_gate"],
                extras=[(dh_in, pl.BlockSpec((tm, tn), lambda i, j, k: (i, j)))],
                epilogue=lambda acc, prev: (prev + acc,))
    dx, dshift, dscale, g_norm = _norm_bwd(dh, sv["x"], p["norm_g"], sv["scale"], dxo, tr)
    small = dict(dmod=jnp.concatenate([dshift, dscale, drg], axis=1), norm_g=g_norm, pool_scale=g_pool_scale,
                 b_gate=g_bgate, conv_w=g_conv_w, pool_w=g_pool_w)
    return dx, parts, theirs, small


BIG = ("w_in", "w_gate", "w_br_a", "w_br_b", "w_br_c", "w_out")
BIG_KIND = dict(w_in="col", w_gate="col", w_br_a="col", w_br_b="row", w_br_c="col", w_out="row")


def _pad_to(v, n):
    return jnp.pad(v, (0, n - v.shape[0]))


def kernel(x, c, norm_g, w_ada, b_ada, w_in, pool_w, pool_scale, conv_w, w_br_a, w_br_b, w_br_c, w_gate, b_gate, w_out, final_g, loss_target, m_norm_g, m_w_ada, m_b_ada, m_w_in, m_pool_w, m_pool_scale, m_conv_w, m_w_br_a, m_w_br_b, m_w_br_c, m_w_gate, m_b_gate, m_w_out, m_final_g, v_norm_g, v_w_ada, v_b_ada, v_w_in, v_pool_w, v_pool_scale, v_conv_w, v_w_br_a, v_w_br_b, v_w_br_c, v_w_gate, v_b_gate, v_w_out, v_final_g):
    weights = dict(norm_g=norm_g, w_ada=w_ada, b_ada=b_ada, w_in=w_in, pool_w=pool_w, pool_scale=pool_scale,
                   conv_w=conv_w, w_br_a=w_br_a, w_br_b=w_br_b, w_br_c=w_br_c, w_gate=w_gate, b_gate=b_gate,
                   w_out=w_out, final_g=final_g)
    mom_m = dict(norm_g=m_norm_g, w_ada=m_w_ada, b_ada=m_b_ada, w_in=m_w_in, pool_w=m_pool_w,
                 pool_scale=m_pool_scale, conv_w=m_conv_w, w_br_a=m_w_br_a, w_br_b=m_w_br_b, w_br_c=m_w_br_c,
                 w_gate=m_w_gate, b_gate=m_b_gate, w_out=m_w_out, final_g=m_final_g)
    mom_v = dict(norm_g=v_norm_g, w_ada=v_w_ada, b_ada=v_b_ada, w_in=v_w_in, pool_w=v_pool_w,
                 pool_scale=v_pool_scale, conv_w=v_conv_w, w_br_a=v_w_br_a, w_br_b=v_w_br_b, w_br_c=v_w_br_c,
                 w_gate=v_w_gate, b_gate=v_b_gate, w_out=v_w_out, final_g=v_final_g)
    names = list(weights)

    _, s, d = x.shape
    layers = norm_g.shape[0]
    pw, gd, cc = d // 2, d // 8, d // 2
    ada_cols = 3 * d // N_CHIPS
    xi, yi, ci = lax.axis_index("x"), lax.axis_index("y"), lax.axis_index("c")
    me = 4 * xi + 2 * yi + ci
    my_chip = 2 * xi + yi

    cfg = dict(
        d=d, tr=_tile(s, 256, 16), tr_small=_tile(s, 128, 16), tm=_tile(s, 1024, 16), tk_s=_tile(s, 4096, 16),
        blk=_tile(s, min(256, max(s // 4, 16)), 16),
        tn_in=_tile(7 * d // N_CHIPS, 512), tk_in=_tile(7 * d // N_CHIPS, 1792),
        tn_gate=_tile(3 * d // N_CHIPS, 512), tk_gate=_tile(3 * d // N_CHIPS, 1536), tn_d=_tile(d // N_CHIPS, 512),
        tn_row=_tile(d // N_CHIPS, 512, 16), tk_row=_tile(d // N_CHIPS, 512, 16),
        tn_half=_tile(d // 2, 512), tm_g=_tile(d, 1024),
    )

    conv_flat = conv_w.reshape(-1)
    conv_len = -(-conv_flat.shape[0] // 1024) * 1024
    pack0 = jnp.concatenate([c.reshape(-1), _pad_to(conv_flat, conv_len), pool_w.reshape(-1)])
    w0 = -(-pack0.shape[0] // 1024) * 1024
    g0 = _all_gather_small(_pad_to(pack0, w0).reshape(8, w0 // 8), "gather_small").reshape(N_DEV, w0)
    c_all = g0[:, 0:d]
    chip_rows = g0[0::2]
    conv_full = jnp.concatenate(
        [chip_rows[j, d:d + conv_flat.shape[0]].reshape(conv_w.shape) for j in range(N_CHIPS)], axis=2)
    pool_full = jnp.concatenate(
        [chip_rows[j, d + conv_len:d + conv_len + pool_w.size].reshape(pool_w.shape) for j in range(N_CHIPS)],
        axis=2)
    pool_bf = pool_full.astype(BF16)

    sc16 = _silu_bf16(jnp.pad(c_all, ((0, 16 - N_DEV), (0, 0))))
    mods = []
    for l in range(layers):
        bias = lax.dynamic_slice(b_ada[l], (my_chip * ada_cols,), (ada_cols,)).reshape(1, ada_cols)
        tn = _tile(ada_cols, 512)
        mod_l, = _mm("mod", sc16, w_ada[l], dims=NN, grid=(1, ada_cols // tn, 1),
                     a_spec=pl.BlockSpec((16, d), lambda i, j, k: (0, 0)),
                     b_spec=pl.BlockSpec((d, tn), lambda i, j, k: (0, j)),
                     acc_shape=(16, tn),
                     outs=[(jax.ShapeDtypeStruct((16, ada_cols), F32), pl.BlockSpec((16, tn), lambda i, j, k: (0, j)))],
                     extras=[(bias, pl.BlockSpec((1, tn), lambda i, j, k: (0, j)))],
                     epilogue=lambda acc, b: (acc + b,))
        mods.append(mod_l[0:N_DEV])
    g1 = _all_gather_small(jnp.concatenate(mods, axis=1), "gather_mod")
    g1 = g1.reshape(N_CHIPS, 2, N_DEV, layers, ada_cols)[:, 0]
    mod_all = jnp.transpose(g1, (1, 2, 0, 3)).reshape(N_DEV, layers, 3 * d)
    mod_me = lax.dynamic_slice(mod_all, (me, 0, 0), (1, layers, 3 * d))[0]

    wbuf = {n: _place_shard(weights[n]) for n in BIG}
    wbuf["w_in"], = _run_job(_gather_job([wbuf["w_in"]], 0), "gather_w_in")
    params = [dict(norm_g=norm_g[l:l + 1], pool_scale=pool_scale[l:l + 1], b_gate=b_gate[l:l + 1],
                   conv_w=conv_full[l], pool_w=pool_bf[l]) for l in range(layers)]

    def gather(names, layer):
        return lambda wb: (_gather_job([wb[n] for n in names], layer), names)

    act = x[0]
    saved = []
    for l in range(layers):
        make_jobs = {"proj": gather([n for n in BIG if n != "w_in"], l)}
        if l + 1 < layers:
            make_jobs["attn"] = gather(["w_in"], l + 1)
        act, sv = _layer_fwd(act, mod_me[l:l + 1], wbuf, params[l], l, cfg, make_jobs)
        saved.append(sv)
    loss_part, dact, g_final = _final_loss(act, final_g.reshape(1, d), loss_target[0], cfg["tr"])
    loss = lax.psum(loss_part[0, 0], ("x", "y", "c"))

    small_grads, parts, theirs = [None] * layers, [None] * layers, [None] * layers
    for l in reversed(range(layers)):
        dact, parts[l], theirs[l], small_grads[l] = _layer_bwd(dact, saved[l], wbuf, params[l], l, cfg)
    grad_x = dact.reshape(x.shape)
    owned = []
    for n in BIG:
        buf = None
        for l in range(layers):
            buf = _sum_owner(parts[l][n], theirs[l][n], l, layers, prev=buf)
        owned.append(buf)
    full = _share_halves(owned)
    grads = {n: f.reshape(weights[n].shape) for n, f in zip(BIG, full)}

    small_names = ("dmod", "norm_g", "pool_scale", "b_gate", "conv_w", "pool_w")
    pieces = [small_grads[l][n].reshape(-1) for n in small_names for l in range(layers)] + [g_final.reshape(-1)]
    pack1 = jnp.concatenate(pieces)
    w1 = -(-pack1.shape[0] // 1024) * 1024
    g2 = _all_gather_small(_pad_to(pack1, w1).reshape(8, w1 // 8), "gather_grads")
    total = _sum_blocks(g2, N_DEV, 8).reshape(-1)
    off = 0
    summed = {}
    for n in small_names:
        per = small_grads[0][n].size
        summed[n] = jnp.stack([total[off + l * per:off + (l + 1) * per].reshape(small_grads[0][n].shape)
                               for l in range(layers)])
        off += layers * per
    grads["final_g"] = total[off:off + d]
    grads["norm_g"] = summed["norm_g"].reshape(layers, d)
    grads["pool_scale"] = summed["pool_scale"].reshape(layers, pw)
    grads["b_gate"] = summed["b_gate"].reshape(layers, 3 * d)
    grads["b_ada"] = summed["dmod"].reshape(layers, 3 * d)
    cs = cc // N_CHIPS
    grads["conv_w"] = lax.dynamic_slice(summed["conv_w"], (0, 0, my_chip * cs), (layers, 3, cs))
    rs_ = gd // N_CHIPS
    grads["pool_w"] = lax.dynamic_slice(summed["pool_w"], (0, 0, my_chip * rs_, 0), (layers, N_GROUPS, rs_, gd))
    dmod_all = g2.reshape(N_DEV, w1)[:, 0:layers * 3 * d].reshape(N_DEV, layers, 3 * d)
    g_ada = []
    for l in range(layers):
        cols = lax.dynamic_slice(dmod_all[:, l], (0, my_chip * ada_cols), (N_DEV, ada_cols))
        cols16 = jnp.pad(cols, ((0, 16 - N_DEV), (0, 0)))
        tn = _tile(ada_cols, 512)
        tm = _tile(d, 1024)
        ga, = _mm("g_w_ada", sc16, cols16, dims=TN, grid=(d // tm, ada_cols // tn, 1),
                  a_spec=pl.BlockSpec((16, tm), lambda i, j, k: (0, i)),
                  b_spec=pl.BlockSpec((16, tn), lambda i, j, k: (0, j)),
                  acc_shape=(tm, tn),
                  outs=[(jax.ShapeDtypeStruct((d, ada_cols), F32), pl.BlockSpec((tm, tn), lambda i, j, k: (i, j)))])
        g_ada.append(ga)
    grads["w_ada"] = jnp.stack(g_ada)

    deltas, new_m, new_v = {}, {}, {}
    for n in names:
        deltas[n], new_m[n], new_v[n] = _adamw(weights[n], grads[n], mom_m[n], mom_v[n])
    return (loss, grad_x, *[grads[n] for n in names], *[deltas[n] for n in names],
            *[new_m[n] for n in names], *[new_v[n] for n in names])
```

```python
import functools

import jax
import jax.numpy as jnp
from jax import lax
from jax.experimental import pallas as pl
from jax.experimental.pallas import tpu as pltpu

F32 = jnp.float32
BF16 = jnp.bfloat16
MESH = pl.DeviceIdType.MESH

N_CHIPS = 4
N_DEV = 8
N_GROUPS = 4
POOL_WINDOWS = (2, 4, 8, 16)
POOL_HALO = 16
CONV_HALO = 8
HEAD_DIM = 128
SB_FWD_HEADS = 4
SB_BWD_HEADS = 4
RMS_EPS = 1e-6
ADAM_LR = 0.001
ADAM_B1 = 0.9
ADAM_B2 = 0.999
ADAM_EPS = 1e-08
ADAM_WD = 0.01
ADAM_STEP = 10
V7X_VMEM_LIMIT = 56 * 1024 * 1024


def _tile(n, pref, mult=128):
    best = None
    t = mult
    while t <= min(n, pref):
        if n % t == 0:
            best = t
        t += mult
    return n if best is None else best


def _params(sem=None):
    return pltpu.CompilerParams(dimension_semantics=sem, vmem_limit_bytes=V7X_VMEM_LIMIT)


def _sigmoid(z):
    return jax.nn.sigmoid(z)


def _dsilu(z, sg):
    return sg * (1.0 + z * (1.0 - sg))


NN = ((1,), (0,))
NT = ((1,), (1,))
TN = ((0,), (0,))


def _mm(name, a, b, *, dims, grid, a_spec, b_spec, acc_shape, outs, extras=(), epilogue=None, with_col=False,
        job=None):
    gi, gj, nk = grid
    ne, no = len(extras), len(outs)
    if epilogue is None:
        epilogue = lambda acc: (acc,)

    def body(*refs):
        own_in, orefs, jargs = _job_refs(job, refs[:-1], 2 + ne, no)
        a_ref, b_ref, ex, acc = own_in[0], own_in[1], own_in[2:], refs[-1]
        i, j, k = pl.program_id(0), pl.program_id(1), pl.program_id(2)
        lead = (j,) if with_col else ()
        if job is not None:
            @pl.when((i == 0) & (j == 0) & (k == 0))
            def _():
                job.start(*jargs)

            if job.mid is not None:
                @pl.when((i == gi - 1) & (j == gj // 2) & (k == 0))
                def _():
                    job.mid(*jargs)

        @pl.when(k == 0)
        def _():
            acc[...] = jnp.zeros_like(acc)

        acc[...] += lax.dot_general(a_ref[...].astype(BF16), b_ref[...].astype(BF16), (dims, ((), ())),
                                    preferred_element_type=F32)

        @pl.when(k == nk - 1)
        def _():
            vals = epilogue(*lead, acc[...], *[e[...] for e in ex])
            for o, v in zip(orefs, vals):
                o[...] = v.astype(o.dtype)

        if job is not None:
            @pl.when((i == gi - 1) & (j == gj - 1) & (k == nk - 1))
            def _():
                job.finish(*jargs)

    hosted = _host(job, 2 + ne, no)
    sem = ("parallel", "parallel", "arbitrary") if job is None else ("arbitrary",) * 3
    res = pl.pallas_call(
        body, name=name, grid=grid,
        in_specs=[a_spec, b_spec] + [s for _, s in extras] + hosted["in_specs"],
        out_specs=[s for _, s in outs] + hosted["out_specs"],
        out_shape=[sh for sh, _ in outs] + hosted["out_shape"],
        scratch_shapes=hosted["scratch"] + [pltpu.VMEM(acc_shape, F32)],
        input_output_aliases=hosted["aliases"],
        compiler_params=_params(sem),
    )(a, b, *[e for e, _ in extras], *hosted["ins"])
    return res


class _Sharded:
    def __init__(self, arr, kind, layer=None):
        self.arr, self.kind, self.layer = arr, kind, layer
        r, c = arr.shape[-2:]
        self.rows = r * (N_CHIPS if kind == "row" else 1)
        self.cols = c * (N_CHIPS if kind == "col" else 1)
        self.sr, self.sc = r, c

    def spec(self, br, bc, f):
        lead = (None,) if self.layer is None else (None, None)
        layer = self.layer
        if self.kind == "col":
            per = self.sc // bc
            assert per * bc == self.sc and self.sr % br == 0, (self.arr.shape, br, bc)

            def idx(*g):
                rb, cb = f(*g)
                return ((cb // per,) + (() if layer is None else (layer,)) + (rb, cb % per))
        else:
            per = self.sr // br
            assert per * br == self.sr and self.sc % bc == 0, (self.arr.shape, br, bc)

            def idx(*g):
                rb, cb = f(*g)
                return ((rb // per,) + (() if layer is None else (layer,)) + (rb % per, cb))
        return pl.BlockSpec(lead + (br, bc), idx)


def _grad_buffer(rows, cols, kind):
    if kind == "col":
        return jax.ShapeDtypeStruct((N_CHIPS, rows, cols // N_CHIPS), BF16)
    return jax.ShapeDtypeStruct((N_CHIPS, rows // N_CHIPS, cols), BF16)


def _mm_nn(name, a, w, *, tm, tn, tk, outs=None, extras=(), epilogue=None, out_dtype=F32, with_col=False,
           job=None):
    m, kdim = a.shape
    n = w.cols
    grid = (m // tm, n // tn, kdim // tk)
    if outs is None:
        outs = [(jax.ShapeDtypeStruct((m, n), out_dtype), pl.BlockSpec((tm, tn), lambda i, j, k: (i, j)))]
    return _mm(name, a, w.arr, dims=NN, grid=grid,
               a_spec=pl.BlockSpec((tm, tk), lambda i, j, k: (i, k)),
               b_spec=w.spec(tk, tn, lambda i, j, k: (k, j)),
               acc_shape=(tm, tn), outs=outs, extras=extras, epilogue=epilogue, with_col=with_col, job=job)


def _mm_grad(name, a_t, b, kind, *, tm, tn, tk):
    m, kdim = a_t.shape
    n = b.shape[1]
    out = _Sharded(_grad_buffer(m, n, kind), kind)
    return _mm(name, a_t, b, dims=NN, grid=(m // tm, n // tn, kdim // tk),
               a_spec=pl.BlockSpec((tm, tk), lambda i, j, k: (i, k)),
               b_spec=pl.BlockSpec((tk, tn), lambda i, j, k: (k, j)),
               acc_shape=(tm, tn), outs=[(out.arr, out.spec(tm, tn, lambda i, j, k: (i, j)))])[0]


def _mm_nt(name, a, w, *, tm, tn, tk, extras=(), epilogue=None, out_dtype=F32, job=None):
    m, kdim = a.shape
    n = w.rows
    grid = (m // tm, n // tn, kdim // tk)
    outs = [(jax.ShapeDtypeStruct((m, n), out_dtype), pl.BlockSpec((tm, tn), lambda i, j, k: (i, j)))]
    res = _mm(name, a, w.arr, dims=NT, grid=grid,
              a_spec=pl.BlockSpec((tm, tk), lambda i, j, k: (i, k)),
              b_spec=w.spec(tn, tk, lambda i, j, k: (j, k)),
              acc_shape=(tm, tn), outs=outs, extras=extras, epilogue=epilogue, job=job)
    return res[0] if job is None else (res[0], list(res[1:]))


def _mm_tn(name, a, b, kind, *, tm, tn, tk):
    kdim, m = a.shape
    n = b.shape[1]
    grid = (m // tm, n // tn, kdim // tk)
    out = _Sharded(_grad_buffer(m, n, kind), kind)
    outs = [(out.arr, out.spec(tm, tn, lambda i, j, k: (i, j)))]
    return _mm(name, a, b, dims=TN, grid=grid,
               a_spec=pl.BlockSpec((tk, tm), lambda i, j, k: (k, i)),
               b_spec=pl.BlockSpec((tk, tn), lambda i, j, k: (k, j)),
               acc_shape=(tm, tn), outs=outs)[0]


def _row_spec(tr, w, col=0):
    return pl.BlockSpec((tr, w), lambda i: (i, col))


def _vec_spec(w, col=0, rows=1):
    return pl.BlockSpec((rows, w), lambda i: (0, col))


def _norm_fwd(x, g, scale, shift, tr):
    s, d = x.shape

    def body(x_ref, g_ref, sc_ref, sh_ref, h_ref, ht_ref):
        xv = x_ref[...]
        r = lax.rsqrt(jnp.mean(xv * xv, axis=-1, keepdims=True) + RMS_EPS)
        y = xv * r * g_ref[...]
        hv = y * (1.0 + sc_ref[...]) + sh_ref[...]
        h_ref[...] = hv.astype(BF16)
        ht_ref[...] = hv.T.astype(BF16)

    return pl.pallas_call(
        body, name="norm_fwd", grid=(s // tr,),
        in_specs=[_row_spec(tr, d), _vec_spec(d), _vec_spec(d), _vec_spec(d)],
        out_specs=[_row_spec(tr, d), pl.BlockSpec((d, tr), lambda i: (0, i))],
        out_shape=[jax.ShapeDtypeStruct((s, d), BF16), jax.ShapeDtypeStruct((d, s), BF16)],
        compiler_params=_params(("parallel",)),
    )(x, g, scale, shift)


def _norm_bwd(dh, x, g, scale, dxo, tr):
    s, d = x.shape

    def body(dh_ref, x_ref, g_ref, sc_ref, dxo_ref, dx_ref, dsh_ref, dsc_ref, dg_ref):
        @pl.when(pl.program_id(0) == 0)
        def _():
            dsh_ref[...] = jnp.zeros_like(dsh_ref)
            dsc_ref[...] = jnp.zeros_like(dsc_ref)
            dg_ref[...] = jnp.zeros_like(dg_ref)

        xv, dhv, gv = x_ref[...], dh_ref[...], g_ref[...]
        r = lax.rsqrt(jnp.mean(xv * xv, axis=-1, keepdims=True) + RMS_EPS)
        xn = xv * r
        dsh_ref[...] += jnp.sum(dhv, axis=0, keepdims=True)
        dsc_ref[...] += jnp.sum(dhv * (xn * gv), axis=0, keepdims=True)
        dyg = dhv * (1.0 + sc_ref[...])
        dg_ref[...] += jnp.sum(dyg * xn, axis=0, keepdims=True)
        dxn = dyg * gv
        dx_ref[...] = dxo_ref[...] + r * (dxn - xn * jnp.mean(dxn * xn, axis=-1, keepdims=True))

    vec = jax.ShapeDtypeStruct((1, d), F32)
    return pl.pallas_call(
        body, name="norm_bwd", grid=(s // tr,),
        in_specs=[_row_spec(tr, d), _row_spec(tr, d), _vec_spec(d), _vec_spec(d), _row_spec(tr, d)],
        out_specs=[_row_spec(tr, d), _vec_spec(d), _vec_spec(d), _vec_spec(d)],
        out_shape=[jax.ShapeDtypeStruct((s, d), F32), vec, vec, vec],
        compiler_params=_params(("arbitrary",)),
    )(dh, x, g, scale, dxo)


def _final_loss(x, g, target, tr):
    s, d = x.shape

    def body(x_ref, g_ref, t_ref, loss_ref, dx_ref, dg_ref):
        @pl.when(pl.program_id(0) == 0)
        def _():
            loss_ref[...] = jnp.zeros_like(loss_ref)
            dg_ref[...] = jnp.zeros_like(dg_ref)

        xv, gv = x_ref[...], g_ref[...]
        r = lax.rsqrt(jnp.mean(xv * xv, axis=-1, keepdims=True) + RMS_EPS)
        xn = xv * r
        err = xn * gv - t_ref[...]
        per_row = jnp.mean(err * err, axis=-1, keepdims=True)
        loss_ref[...] += 0.5 * jnp.sum(per_row, axis=0, keepdims=True)
        dy = err * (1.0 / d)
        dg_ref[...] += jnp.sum(dy * xn, axis=0, keepdims=True)
        dxn = dy * gv
        dx_ref[...] = r * (dxn - xn * jnp.mean(dxn * xn, axis=-1, keepdims=True))

    return pl.pallas_call(
        body, name="final_loss", grid=(s // tr,),
        in_specs=[_row_spec(tr, d), _vec_spec(d), _row_spec(tr, d)],
        out_specs=[pl.BlockSpec((8, 128), lambda i: (0, 0)), _row_spec(tr, d), _vec_spec(d)],
        out_shape=[jax.ShapeDtypeStruct((8, 128), F32), jax.ShapeDtypeStruct((s, d), F32),
                   jax.ShapeDtypeStruct((1, d), F32)],
        compiler_params=_params(("arbitrary",)),
    )(x, g, target)


def _halo_before(tr, halo, w, col):
    per = tr // halo
    return pl.BlockSpec((halo, w), lambda i: (jnp.maximum(i * per - 1, 0), col))


def _halo_after(tr, halo, w, col, n_tiles):
    per = tr // halo
    return pl.BlockSpec((halo, w), lambda i: (jnp.minimum((i + 1) * per, n_tiles * per - 1), col))


def _pool_fwd(proj, pool_w, pool_scale, d, tr):
    s = proj.shape[0]
    pw, gd, hl = d // 2, d // 8, POOL_HALO

    def body(xa_ref, xh_ref, za_ref, w_ref, ps_ref, ya_ref, buf):
        i = pl.program_id(0)
        buf[0:hl, :] = jnp.where(i > 0, xh_ref[...], 0.0)
        buf[hl:hl + tr, :] = xa_ref[...]
        row = i * tr + lax.broadcasted_iota(jnp.int32, (tr, 1), 0)
        za = za_ref[...]
        gate = za * _sigmoid(za)
        for g, win in enumerate(POOL_WINDOWS):
            cs = slice(g * gd, (g + 1) * gd)
            xg = buf[hl:hl + tr, cs]
            acc = xg
            for j in range(1, win):
                acc = acc + buf[hl - j:hl - j + tr, cs]
            cnt = jnp.minimum(row + 1, win).astype(F32)
            mixed = acc / cnt - xg
            y = jnp.dot(mixed.astype(BF16), w_ref[g], preferred_element_type=F32)
            ya_ref[:, cs] = ((y * ps_ref[:, cs]) * gate[:, cs]).astype(BF16)

    return pl.pallas_call(
        body, name="pool_fwd", grid=(s // tr,),
        in_specs=[_row_spec(tr, pw, 0), _halo_before(tr, hl, pw, 0), _row_spec(tr, pw, 1),
                  pl.BlockSpec((N_GROUPS, gd, gd), lambda i: (0, 0, 0)), _vec_spec(pw)],
        out_specs=_row_spec(tr, pw), out_shape=jax.ShapeDtypeStruct((s, pw), BF16),
        scratch_shapes=[pltpu.VMEM((tr + hl, pw), F32)],
        compiler_params=_params(("parallel",)),
    )(proj, proj, proj, pool_w, pool_scale)


def _pool_bwd(proj, dya, pool_w, pool_scale, d, tr):
    s = proj.shape[0]
    pw, gd, hl = d // 2, d // 8, POOL_HALO
    n_tiles = s // tr

    def body(xa_ref, xh_ref, za_ref, zh_ref, dya_ref, dyh_ref, w_ref, ps_ref, da_ref, gw_ref, gs_ref, buf, dbuf):
        i = pl.program_id(0)

        @pl.when(i == 0)
        def _():
            gw_ref[...] = jnp.zeros_like(gw_ref)
            gs_ref[...] = jnp.zeros_like(gs_ref)

        buf[0:hl, :] = jnp.where(i > 0, xh_ref[...], 0.0)
        buf[hl:hl + tr, :] = xa_ref[...]
        row = i * tr + lax.broadcasted_iota(jnp.int32, (tr, 1), 0)
        row_h = (i + 1) * tr + lax.broadcasted_iota(jnp.int32, (hl, 1), 0)
        za, dya_t = za_ref[...], dya_ref[...]
        sg = _sigmoid(za)
        gate = za * sg
        dpre = dya_t * gate
        zh = zh_ref[...]
        dpre_h = jnp.where(i < n_tiles - 1, dyh_ref[...], 0.0) * (zh * _sigmoid(zh))
        ps = ps_ref[...]
        for g, win in enumerate(POOL_WINDOWS):
            cs = slice(g * gd, (g + 1) * gd)
            wg = w_ref[g]
            xg = buf[hl:hl + tr, cs]
            acc = xg
            for j in range(1, win):
                acc = acc + buf[hl - j:hl - j + tr, cs]
            cnt = jnp.minimum(row + 1, win).astype(F32)
            mixed = (acc / cnt - xg).astype(BF16)
            ylin = jnp.dot(mixed, wg, preferred_element_type=F32)
            gs_ref[:, cs] += jnp.sum(dpre[:, cs] * ylin, axis=0, keepdims=True)
            da_ref[:, pw + g * gd:pw + (g + 1) * gd] = (
                dya_t[:, cs] * (ylin * ps[:, cs]) * _dsilu(za[:, cs], sg[:, cs])).astype(BF16)
            dyl = (dpre[:, cs] * ps[:, cs]).astype(BF16)
            gw_ref[g] += lax.dot_general(mixed, dyl, (TN, ((), ())), preferred_element_type=F32)
            dmix = lax.dot_general(dyl, wg, (NT, ((), ())), preferred_element_type=F32)
            dyl_h = (dpre_h[:, cs] * ps[:, cs]).astype(BF16)
            dmix_h = lax.dot_general(dyl_h, wg, (NT, ((), ())), preferred_element_type=F32)
            cnt_h = jnp.minimum(row_h + 1, win).astype(F32)
            dbuf[0:tr, cs] = dmix / cnt
            dbuf[tr:tr + hl, cs] = dmix_h / cnt_h
            dx = dbuf[0:tr, cs] - dmix
            for j in range(1, win):
                dx = dx + dbuf[j:j + tr, cs]
            da_ref[:, cs] = dx.astype(BF16)

    return pl.pallas_call(
        body, name="pool_bwd", grid=(n_tiles,),
        in_specs=[_row_spec(tr, pw, 0), _halo_before(tr, hl, pw, 0),
                  _row_spec(tr, pw, 1), _halo_after(tr, hl, pw, 1, n_tiles),
                  _row_spec(tr, pw, 0), _halo_after(tr, hl, pw, 0, n_tiles),
                  pl.BlockSpec((N_GROUPS, gd, gd), lambda i: (0, 0, 0)), _vec_spec(pw)],
        out_specs=[_row_spec(tr, 2 * pw), pl.BlockSpec((N_GROUPS, gd, gd), lambda i: (0, 0, 0)), _vec_spec(pw)],
        out_shape=[jax.ShapeDtypeStruct((s, 2 * pw), BF16), jax.ShapeDtypeStruct((N_GROUPS, gd, gd), F32),
                   jax.ShapeDtypeStruct((1, pw), F32)],
        scratch_shapes=[pltpu.VMEM((tr + hl, pw), F32), pltpu.VMEM((tr + hl, pw), F32)],
        compiler_params=_params(("arbitrary",)),
    )(proj, proj, proj, proj, dya, dya, pool_w, pool_scale)


def _conv_fwd(proj, conv_w, d, tr):
    s = proj.shape[0]
    cc, hl = d // 2, CONV_HALO
    cu, cb, cg_, cz = 10, 11, 12, 13

    def body(u_ref, uh_ref, bg_ref, cg_ref, ch_ref, zc_ref, w_ref, yc_ref, buf):
        i = pl.program_id(0)
        buf[0:hl, :] = jnp.where(i > 0, ch_ref[...] * uh_ref[...], 0.0)
        buf[hl:hl + tr, :] = cg_ref[...] * u_ref[...]
        y = w_ref[0:1, :] * buf[hl - 2:hl - 2 + tr, :]
        y = y + w_ref[1:2, :] * buf[hl - 1:hl - 1 + tr, :]
        y = y + w_ref[2:3, :] * buf[hl:hl + tr, :]
        zc = zc_ref[...]
        yc_ref[...] = ((bg_ref[...] * y) * (zc * _sigmoid(zc))).astype(BF16)

    return pl.pallas_call(
        body, name="conv_fwd", grid=(s // tr,),
        in_specs=[_row_spec(tr, cc, cu), _halo_before(tr, hl, cc, cu), _row_spec(tr, cc, cb),
                  _row_spec(tr, cc, cg_), _halo_before(tr, hl, cc, cg_), _row_spec(tr, cc, cz),
                  _vec_spec(cc, rows=3)],
        out_specs=_row_spec(tr, cc), out_shape=jax.ShapeDtypeStruct((s, cc), BF16),
        scratch_shapes=[pltpu.VMEM((tr + hl, cc), F32)],
        compiler_params=_params(("parallel",)),
    )(proj, proj, proj, proj, proj, proj, conv_w)


def _conv_bwd(proj, dyc, conv_w, d, tr):
    s = proj.shape[0]
    cc, hl = d // 2, CONV_HALO
    cu, cb, cg_, cz = 10, 11, 12, 13
    n_tiles = s // tr

    def body(u_ref, uh_ref, bg_ref, bh_ref, cg_ref, ch_ref, zc_ref, zh_ref, dy_ref, dyh_ref, w_ref,
             dc_ref, gw_ref, buf, dbuf):
        i = pl.program_id(0)

        @pl.when(i == 0)
        def _():
            gw_ref[...] = jnp.zeros_like(gw_ref)

        u, bg, cg, zc, dyc_t = u_ref[...], bg_ref[...], cg_ref[...], zc_ref[...], dy_ref[...]
        buf[0:hl, :] = jnp.where(i > 0, ch_ref[...] * uh_ref[...], 0.0)
        buf[hl:hl + tr, :] = cg * u
        v2, v1, v0 = buf[hl - 2:hl - 2 + tr, :], buf[hl - 1:hl - 1 + tr, :], buf[hl:hl + tr, :]
        w0, w1, w2 = w_ref[0:1, :], w_ref[1:2, :], w_ref[2:3, :]
        y = w0 * v2 + w1 * v1 + w2 * v0
        sg = _sigmoid(zc)
        gate = zc * sg
        dc_ref[:, cc:2 * cc] = (dyc_t * y * gate).astype(BF16)
        dc_ref[:, 3 * cc:4 * cc] = (dyc_t * bg * y * _dsilu(zc, sg)).astype(BF16)
        dy = dyc_t * bg * gate
        zh = zh_ref[...]
        dy_h = jnp.where(i < n_tiles - 1, dyh_ref[...], 0.0) * bh_ref[...] * (zh * _sigmoid(zh))
        gw_ref[0:1, :] += jnp.sum(dy * v2, axis=0, keepdims=True)
        gw_ref[1:2, :] += jnp.sum(dy * v1, axis=0, keepdims=True)
        gw_ref[2:3, :] += jnp.sum(dy * v0, axis=0, keepdims=True)
        dbuf[0:tr, :] = dy
        dbuf[tr:tr + hl, :] = dy_h
        dv = w2 * dy + w1 * dbuf[1:1 + tr, :] + w0 * dbuf[2:2 + tr, :]
        dc_ref[:, 0:cc] = (dv * cg).astype(BF16)
        dc_ref[:, 2 * cc:3 * cc] = (dv * u).astype(BF16)

    return pl.pallas_call(
        body, name="conv_bwd", grid=(n_tiles,),
        in_specs=[_row_spec(tr, cc, cu), _halo_before(tr, hl, cc, cu),
                  _row_spec(tr, cc, cb), _halo_after(tr, hl, cc, cb, n_tiles),
                  _row_spec(tr, cc, cg_), _halo_before(tr, hl, cc, cg_),
                  _row_spec(tr, cc, cz), _halo_after(tr, hl, cc, cz, n_tiles),
                  _row_spec(tr, cc, 0), _halo_after(tr, hl, cc, 0, n_tiles),
                  _vec_spec(cc, rows=3)],
        out_specs=[_row_spec(tr, 4 * cc), _vec_spec(cc, rows=3)],
        out_shape=[jax.ShapeDtypeStruct((s, 4 * cc), BF16), jax.ShapeDtypeStruct((3, cc), F32)],
        scratch_shapes=[pltpu.VMEM((tr + hl, cc), F32), pltpu.VMEM((tr + hl, cc), F32)],
        compiler_params=_params(("arbitrary",)),
    )(proj, proj, proj, proj, proj, proj, proj, proj, dyc, dyc, conv_w)


def _merge_bwd(dm, gates, pa, pb, pc, tr):
    s, d = pa.shape

    def body(dm_ref, g_ref, a_ref, b_ref, c_ref, da_ref, db_ref, dc_ref, dg_ref, gb_ref):
        @pl.when(pl.program_id(0) == 0)
        def _():
            gb_ref[...] = jnp.zeros_like(gb_ref)

        dmv = dm_ref[...]
        for n, (p_ref, o_ref) in enumerate(((a_ref, da_ref), (b_ref, db_ref), (c_ref, dc_ref))):
            gv = g_ref[:, n * d:(n + 1) * d]
            o_ref[...] = (dmv * gv).astype(BF16)
            dlogit = (dmv * p_ref[...]) * (gv * (1.0 - gv))
            dg_ref[:, n * d:(n + 1) * d] = dlogit.astype(BF16)
            gb_ref[:, n * d:(n + 1) * d] += jnp.sum(dlogit, axis=0, keepdims=True)

    act = jax.ShapeDtypeStruct((s, d), BF16)
    return pl.pallas_call(
        body, name="merge_bwd", grid=(s // tr,),
        in_specs=[_row_spec(tr, d), _row_spec(tr, 3 * d), _row_spec(tr, d), _row_spec(tr, d), _row_spec(tr, d)],
        out_specs=[_row_spec(tr, d), _row_spec(tr, d), _row_spec(tr, d), _row_spec(tr, 3 * d), _vec_spec(3 * d)],
        out_shape=[act, act, act, jax.ShapeDtypeStruct((s, 3 * d), BF16), jax.ShapeDtypeStruct((1, 3 * d), F32)],
        compiler_params=_params(("arbitrary",)),
    )(dm, gates, pa, pb, pc)


def _resid_bwd(dxo, out, rg, tr):
    s, d = dxo.shape

    def body(dx_ref, o_ref, rg_ref, do_ref, drg_ref):
        @pl.when(pl.program_id(0) == 0)
        def _():
            drg_ref[...] = jnp.zeros_like(drg_ref)

        dxv = dx_ref[...]
        do_ref[...] = (dxv * rg_ref[...]).astype(BF16)
        drg_ref[...] += jnp.sum(dxv * o_ref[...], axis=0, keepdims=True)

    return pl.pallas_call(
        body, name="resid_bwd", grid=(s // tr,),
        in_specs=[_row_spec(tr, d), _row_spec(tr, d), _vec_spec(d)],
        out_specs=[_row_spec(tr, d), _vec_spec(d)],
        out_shape=[jax.ShapeDtypeStruct((s, d), BF16), jax.ShapeDtypeStruct((1, d), F32)],
        compiler_params=_params(("arbitrary",)),
    )(dxo, out, rg)


def _log_keep(z):
    e = jnp.exp(-jnp.abs(z))
    return -(jnp.maximum(z, 0.0) + jnp.log(1.0 + e)), e


class _Job:
    def __init__(self, ins, out_shapes, aliases, n_sems, start, finish, mid=None):
        self.ins, self.out_shapes, self.aliases, self.n_sems = list(ins), list(out_shapes), list(aliases), n_sems
        self.start, self.mid, self.finish = start, mid, finish


def _host(job, n_in, n_out):
    if job is None:
        return dict(ins=[], in_specs=[], out_specs=[], out_shape=[], scratch=[], aliases={})
    return dict(ins=job.ins, in_specs=[_ANY] * len(job.ins), out_specs=[_ANY] * len(job.out_shapes),
                out_shape=job.out_shapes,
                scratch=[pltpu.SemaphoreType.DMA((job.n_sems,)), pltpu.SemaphoreType.DMA((job.n_sems,))],
                aliases={n_in + a: n_out + b for a, b in job.aliases})


def _job_refs(job, refs, n_in, n_out):
    if job is None:
        return refs[:n_in], refs[n_in:n_in + n_out], None
    ji, jo = len(job.ins), len(job.out_shapes)
    own_in, job_in = refs[:n_in], refs[n_in:n_in + ji]
    own_out, job_out = refs[n_in + ji:n_in + ji + n_out], refs[n_in + ji + n_out:n_in + ji + n_out + jo]
    send_sems, recv_sems = refs[n_in + ji + n_out + jo:]
    return own_in, own_out, (job_in, job_out, send_sems, recv_sems)


def _sb_fwd(proj, projb, d, blk, job=None):
    s = proj.shape[0]
    hps, wid = SB_FWD_HEADS, SB_FWD_HEADS * HEAD_DIM
    groups, nq = d // wid, s // blk
    qc, kc, vc, zc = d // wid, 2 * d // wid, 3 * d // wid, 4 * d // wid
    assert nq <= HEAD_DIM
    mid_step = (groups - 1, 2 * nq // 3)

    def body(*refs):
        (q_ref, k_ref, v_ref, zb_ref), (att_ref, yb_ref, rs_ref), jargs = _job_refs(job, refs, 4, 3)
        grp, i = pl.program_id(0), pl.program_id(1)
        if job is not None:
            @pl.when((grp == 0) & (i == 0))
            def _():
                job.start(*jargs)

            if job.mid is not None:
                @pl.when((grp == mid_step[0]) & (i == mid_step[1]))
                def _():
                    job.mid(*jargs)

        r_io = lax.broadcasted_iota(jnp.int32, (blk, blk), 0)
        c_io = lax.broadcasted_iota(jnp.int32, (blk, blk), 1)
        tri = (r_io >= c_io).astype(BF16)
        strict = c_io < r_io
        lane = lax.broadcasted_iota(jnp.int32, (blk, HEAD_DIM), 1)
        heads = [slice(n * HEAD_DIM, (n + 1) * HEAD_DIM) for n in range(hps)]
        qs = [q_ref[:, hd] for hd in heads]

        def block(j, carry, masked):
            ks = pl.multiple_of(j * blk, blk)
            rng = range(hps)
            zs = [lax.dot_general(qs[n], k_ref[pl.ds(ks, blk), heads[n]], (NT, ((), ())),
                                  preferred_element_type=F32) for n in rng]
            lks = []
            for n in rng:
                lk, _ = _log_keep(zs[n])
                if masked:
                    lk = jnp.where(strict, lk, 0.0)
                lks.append(lk.astype(BF16))
            csums = [carry[n][0] + jnp.dot(lks[n], tri, preferred_element_type=F32) for n in rng]
            probs = []
            for n in rng:
                a = jnp.exp(zs[n] + csums[n])
                if masked:
                    a = jnp.where(strict, a, 0.0)
                probs.append(a.astype(BF16))
            out = []
            for n in rng:
                run, acc, rs = carry[n]
                acc = acc + jnp.dot(probs[n], v_ref[pl.ds(ks, blk), heads[n]], preferred_element_type=F32)
                out.append((csums[n][:, 0:1], acc, jnp.where(lane == j, run, rs)))
            return tuple(out)

        zero = (jnp.zeros((blk, 1), F32), jnp.zeros((blk, HEAD_DIM), F32), jnp.zeros((blk, HEAD_DIM), F32))
        carry = block(i, (zero,) * hps, True)
        carry = lax.fori_loop(0, i, lambda jj, cr: block(i - 1 - jj, cr, False), carry)
        for hd, (_, acc, rs) in zip(heads, carry):
            att_ref[:, hd] = acc
            rs_ref[:, hd] = rs
            zb = zb_ref[:, hd]
            yb_ref[:, hd] = (acc * (zb * _sigmoid(zb))).astype(BF16)

        if job is not None:
            @pl.when((grp == groups - 1) & (i == nq - 1))
            def _():
                job.finish(*jargs)

    blk_spec = pl.BlockSpec((blk, wid), lambda h, i: (i, h))
    hosted = _host(job, 4, 3)
    res = pl.pallas_call(
        body, name="sb_fwd", grid=(groups, nq),
        in_specs=[pl.BlockSpec((blk, wid), lambda h, i: (i, qc + h)),
                  pl.BlockSpec((s, wid), lambda h, i: (0, kc + h)),
                  pl.BlockSpec((s, wid), lambda h, i: (0, vc + h)),
                  pl.BlockSpec((blk, wid), lambda h, i: (i, zc + h))] + hosted["in_specs"],
        out_specs=[blk_spec, blk_spec, blk_spec] + hosted["out_specs"],
        out_shape=[jax.ShapeDtypeStruct((s, d), F32), jax.ShapeDtypeStruct((s, d), BF16),
                   jax.ShapeDtypeStruct((s, d), F32)] + hosted["out_shape"],
        scratch_shapes=hosted["scratch"], input_output_aliases=hosted["aliases"],
        compiler_params=_params(("arbitrary", "arbitrary")),
    )(projb, projb, projb, proj, *hosted["ins"])
    return res[0], res[1], res[2], list(res[3:])


def _sb_bwd(proj, projb, att, dyb, rsave, d, blk, job=None):
    s = proj.shape[0]
    hps, wid = SB_BWD_HEADS, SB_BWD_HEADS * HEAD_DIM
    groups, nq = d // wid, s // blk
    qc, kc, vc, zc = d // wid, 2 * d // wid, 3 * d // wid, 4 * d // wid
    scale = HEAD_DIM ** -0.5

    def body(*refs):
        ((q_ref, k_ref, v_ref, zb_ref, att_ref, dyb_ref, rs_ref), (dq_ref, dk_ref, dv_ref, dzb_ref),
         jargs) = _job_refs(job, refs, 7, 4)
        grp, i = pl.program_id(0), pl.program_id(1)
        if job is not None:
            @pl.when((grp == 0) & (i == 0))
            def _():
                job.start(*jargs)

        @pl.when(i == 0)
        def _():
            dk_ref[...] = jnp.zeros_like(dk_ref)
            dv_ref[...] = jnp.zeros_like(dv_ref)

        r_io = lax.broadcasted_iota(jnp.int32, (blk, blk), 0)
        c_io = lax.broadcasted_iota(jnp.int32, (blk, blk), 1)
        tri = (r_io >= c_io).astype(BF16)
        tri_up = (r_io <= c_io).astype(BF16)
        strict = c_io < r_io
        lane = lax.broadcasted_iota(jnp.int32, (blk, HEAD_DIM), 1)
        heads = [slice(n * HEAD_DIM, (n + 1) * HEAD_DIM) for n in range(hps)]
        qs, dos, rss = [], [], []
        for hd in heads:
            qs.append(q_ref[:, hd])
            zb, dyb_t = zb_ref[:, hd], dyb_ref[:, hd]
            sg = _sigmoid(zb)
            dzb_ref[:, hd] = dyb_t * att_ref[:, hd] * _dsilu(zb, sg)
            dos.append((dyb_t * (zb * sg)).astype(BF16))
            rss.append(rs_ref[:, hd])

        def block(j, carry, masked):
            ks = pl.multiple_of(j * blk, blk)
            rng = range(hps)
            kbs = [k_ref[pl.ds(ks, blk), heads[n]] for n in rng]
            vbs = [v_ref[pl.ds(ks, blk), heads[n]] for n in rng]
            zs = [lax.dot_general(qs[n], kbs[n], (NT, ((), ())), preferred_element_type=F32) for n in rng]
            das = [lax.dot_general(dos[n], vbs[n], (NT, ((), ())), preferred_element_type=F32) for n in rng]
            lks, betas = [], []
            for n in rng:
                lk, e = _log_keep(zs[n])
                betas.append(jnp.where(zs[n] >= 0, 1.0, e) / (1.0 + e))
                if masked:
                    lk = jnp.where(strict, lk, 0.0)
                lks.append(lk.astype(BF16))
            csums = []
            for n in rng:
                run = jnp.sum(jnp.where(lane == j, rss[n], 0.0), axis=1, keepdims=True)
                csums.append(run + jnp.dot(lks[n], tri, preferred_element_type=F32))
            probs, gs = [], []
            for n in rng:
                a = jnp.exp(zs[n] + csums[n])
                if masked:
                    a = jnp.where(strict, a, 0.0)
                probs.append(a.astype(BF16))
                gs.append(a * das[n])
            gcums = [carry[n][0] + jnp.dot(gs[n].astype(BF16), tri_up, preferred_element_type=F32) for n in rng]
            for n in rng:
                dv_ref[pl.ds(ks, blk), heads[n]] += lax.dot_general(probs[n], dos[n], (TN, ((), ())),
                                                                   preferred_element_type=F32)
            dzs = []
            for n in rng:
                dz = gs[n] - betas[n] * gcums[n]
                if masked:
                    dz = jnp.where(strict, dz, 0.0)
                dzs.append(dz.astype(BF16))
            out = []
            for n in rng:
                dq = carry[n][1] + jnp.dot(dzs[n], kbs[n], preferred_element_type=F32)
                dk_ref[pl.ds(ks, blk), heads[n]] += lax.dot_general(dzs[n], qs[n], (TN, ((), ())),
                                                                   preferred_element_type=F32)
                out.append((gcums[n][:, blk - 1:blk], dq))
            return tuple(out)

        zero = (jnp.zeros((blk, 1), F32), jnp.zeros((blk, HEAD_DIM), F32))
        carry = lax.fori_loop(0, i, lambda j, cr: block(j, cr, False), (zero,) * hps)
        carry = block(i, carry, True)
        for hd, (_, dq) in zip(heads, carry):
            dq_ref[:, hd] = dq * scale

        if job is not None:
            @pl.when((grp == groups - 1) & (i == nq - 1))
            def _():
                job.finish(*jargs)

    blk_spec = pl.BlockSpec((blk, wid), lambda h, i: (i, h))
    full_spec = pl.BlockSpec((s, wid), lambda h, i: (0, h), pipeline_mode=pl.Buffered(1))
    act = jax.ShapeDtypeStruct((s, d), F32)
    hosted = _host(job, 7, 4)
    res = pl.pallas_call(
        body, name="sb_bwd", grid=(groups, nq),
        in_specs=[pl.BlockSpec((blk, wid), lambda h, i: (i, qc + h)),
                  pl.BlockSpec((s, wid), lambda h, i: (0, kc + h)),
                  pl.BlockSpec((s, wid), lambda h, i: (0, vc + h)),
                  pl.BlockSpec((blk, wid), lambda h, i: (i, zc + h)),
                  blk_spec, blk_spec, blk_spec] + hosted["in_specs"],
        out_specs=[blk_spec, full_spec, full_spec, blk_spec] + hosted["out_specs"],
        out_shape=[act, act, act, act] + hosted["out_shape"],
        scratch_shapes=hosted["scratch"], input_output_aliases=hosted["aliases"],
        compiler_params=_params(("arbitrary", "arbitrary")),
    )(projb, projb, projb, proj, att, dyb, rsave, *hosted["ins"])
    return res[0], res[1], res[2], res[3], list(res[4:])


def _adamw(w, g, m, v):
    shape = w.shape
    cols = shape[-1]
    rows = w.size // cols
    w2, g2, m2, v2 = (t.reshape(rows, cols) for t in (w, g, m, v))
    tr, tc = _tile(rows, 512, 8), _tile(cols, 1024, 128)
    c1 = 1.0 - ADAM_B1 ** ADAM_STEP
    c2 = 1.0 - ADAM_B2 ** ADAM_STEP

    def body(w_ref, g_ref, m_ref, v_ref, d_ref, nm_ref, nv_ref):
        gv = g_ref[...]
        nm = ADAM_B1 * m_ref[...] + (1.0 - ADAM_B1) * gv
        nv = ADAM_B2 * v_ref[...] + (1.0 - ADAM_B2) * (gv * gv)
        d_ref[...] = -ADAM_LR * ((nm / c1) / (jnp.sqrt(nv / c2) + ADAM_EPS) + ADAM_WD * w_ref[...])
        nm_ref[...] = nm
        nv_ref[...] = nv

    spec = pl.BlockSpec((tr, tc), lambda i, j: (i, j))
    sd = jax.ShapeDtypeStruct((rows, cols), F32)
    outs = pl.pallas_call(
        body, name="adamw", grid=(rows // tr, cols // tc),
        in_specs=[spec] * 4, out_specs=[spec] * 3, out_shape=[sd] * 3,
        compiler_params=_params(("parallel", "parallel")),
    )(w2, g2, m2, v2)
    return tuple(o.reshape(shape) for o in outs)


def _sum_blocks(gathered, n, rows):
    width = gathered.shape[1]
    tw = _tile(width, 8192)

    def body(g_ref, o_ref):
        acc = g_ref[0:rows, :]
        for b in range(1, n):
            acc = acc + g_ref[b * rows:(b + 1) * rows, :]
        o_ref[...] = acc

    return pl.pallas_call(
        body, name="sum_blocks", grid=(width // tw,),
        in_specs=[pl.BlockSpec((n * rows, tw), lambda i: (0, i))],
        out_specs=pl.BlockSpec((rows, tw), lambda i: (0, i)),
        out_shape=jax.ShapeDtypeStruct((rows, width), F32),
        compiler_params=_params(("parallel",)),
    )(gathered)


def _silu_bf16(c_rows):
    def body(c_ref, o_ref):
        cv = c_ref[...]
        o_ref[...] = (cv * _sigmoid(cv)).astype(BF16)

    return pl.pallas_call(
        body, name="silu_c", out_shape=jax.ShapeDtypeStruct(c_rows.shape, BF16),
        in_specs=[pl.BlockSpec(memory_space=pltpu.VMEM)], out_specs=pl.BlockSpec(memory_space=pltpu.VMEM),
    )(c_rows)


def _place():
    x, y, c = lax.axis_index("x"), lax.axis_index("y"), lax.axis_index("c")
    chips = [(1 - x, y), (x, 1 - y), (1 - x, 1 - y)]
    return x, y, c, chips


def _all_gather_small(block, name):
    m_per, n = block.shape

    def body(x_ref, out_ref, send_sems, recv_sems, local_sem):
        x, y, c, chips = _place()
        me, sibling = (x, y, c), (x, y, 1 - c)

        def rows(px, py, pc):
            return out_ref.at[pl.ds((4 * px + 2 * py + pc) * m_per, m_per), :]

        def copy(k, blk, to, src=None):
            return pltpu.make_async_remote_copy(
                src_ref=rows(*blk) if src is None else src, dst_ref=rows(*blk),
                send_sem=send_sems.at[k], recv_sem=recv_sems.at[k], device_id=to, device_id_type=MESH)

        mine = pltpu.make_async_copy(x_ref, rows(*me), local_sem)
        mine.start()
        first = [copy(0, me, sibling, src=x_ref)]
        first += [copy(1 + j, me, (*chip, c), src=x_ref) for j, chip in enumerate(chips)]
        for cp in first:
            cp.start()
        passed = [copy(4 + j, (*chip, c), sibling) for j, chip in enumerate(chips)]
        for j, chip in enumerate(chips):
            copy(1 + j, (*chip, c), me).wait_recv()
            passed[j].start()
        copy(0, sibling, me).wait_recv()
        for j, chip in enumerate(chips):
            copy(4 + j, (*chip, 1 - c), me).wait_recv()
        for cp in first + passed:
            cp.wait_send()
        mine.wait()

    return pl.pallas_call(
        body, name=name, out_shape=jax.ShapeDtypeStruct((N_DEV * m_per, n), block.dtype),
        in_specs=[pl.BlockSpec(memory_space=pltpu.VMEM)], out_specs=pl.BlockSpec(memory_space=pltpu.VMEM),
        scratch_shapes=[pltpu.SemaphoreType.DMA((7,)), pltpu.SemaphoreType.DMA((7,)), pltpu.SemaphoreType.DMA],
        compiler_params=pltpu.CompilerParams(vmem_limit_bytes=V7X_VMEM_LIMIT),
    )(block)


_ANY = pl.BlockSpec(memory_space=pl.ANY)


def _my_chip():
    return 2 * lax.axis_index("x") + lax.axis_index("y")


def _place_shard(w):
    layers, r, cols = w.shape
    tr, tc = _tile(r, 512, 16), _tile(cols, 1024)

    def body(w_ref, o_ref):
        o_ref[...] = w_ref[...].astype(BF16)

    return pl.pallas_call(
        body, name="place_shard", grid=(layers, r // tr, cols // tc),
        in_specs=[pl.BlockSpec((None, tr, tc), lambda l, i, n: (l, i, n))],
        out_specs=pl.BlockSpec((None, None, tr, tc), lambda l, i, n: (_my_chip(), l, i, n)),
        out_shape=jax.ShapeDtypeStruct((N_CHIPS, layers, r, cols), BF16),
        compiler_params=_params(("parallel", "parallel", "parallel")),
    )(w)


def _gather_job(bufs, layer):
    n = len(bufs)
    layer_of = [layer] * n if isinstance(layer, int) else list(layer)

    def tools(outs, send_sems, recv_sems):
        x, y, c, chips = _place()

        def half(t, chip_idx, hc):
            h = outs[t].shape[2] // 2
            return outs[t].at[chip_idx, layer_of[t], pl.ds(hc * h, h), :]

        def copy(t, k, ref, to):
            return pltpu.make_async_remote_copy(src_ref=ref, dst_ref=ref, send_sem=send_sems.at[6 * t + k],
                                                recv_sem=recv_sems.at[6 * t + k], device_id=to, device_id_type=MESH)

        return x, y, c, chips, half, copy

    def start(ins, outs, send_sems, recv_sems):
        x, y, c, chips, half, copy = tools(outs, send_sems, recv_sems)
        for t in range(n):
            for k, chip in enumerate(chips):
                copy(t, k, half(t, 2 * x + y, c), (*chip, c)).start()

    def mid(ins, outs, send_sems, recv_sems):
        x, y, c, chips, half, copy = tools(outs, send_sems, recv_sems)
        for t in range(n):
            for k, (cx, cy) in enumerate(chips):
                landed = half(t, 2 * cx + cy, c)
                copy(t, k, landed, (cx, cy, c)).wait_recv()
                copy(t, 3 + k, landed, (x, y, 1 - c)).start()

    def finish(ins, outs, send_sems, recv_sems):
        x, y, c, chips, half, copy = tools(outs, send_sems, recv_sems)
        for t in range(n):
            for k, (cx, cy) in enumerate(chips):
                copy(t, 3 + k, half(t, 2 * cx + cy, 1 - c), (x, y, 1 - c)).wait_recv()
        for t in range(n):
            for k, (cx, cy) in enumerate(chips):
                copy(t, k, half(t, 2 * x + y, c), (cx, cy, c)).wait_send()
                copy(t, 3 + k, half(t, 2 * cx + cy, c), (x, y, 1 - c)).wait_send()

    return _Job(ins=bufs, out_shapes=[jax.ShapeDtypeStruct(b.shape, b.dtype) for b in bufs],
                aliases=[(t, t) for t in range(n)], n_sems=6 * n, start=start, mid=mid, finish=finish)


def _send_job(parts):
    n = len(parts)

    def copies(ins, outs, send_sems, recv_sems):
        x, y, c, chips = _place()
        return [pltpu.make_async_remote_copy(
            src_ref=ins[t].at[2 * cx + cy], dst_ref=outs[t].at[k], send_sem=send_sems.at[3 * t + k],
            recv_sem=recv_sems.at[3 * t + k], device_id=(cx, cy, c), device_id_type=MESH)
            for t in range(n) for k, (cx, cy) in enumerate(chips)]

    def start(*args):
        for cp in copies(*args):
            cp.start()

    def finish(*args):
        for cp in copies(*args):
            cp.wait()

    return _Job(ins=parts, out_shapes=[jax.ShapeDtypeStruct((3,) + p.shape[1:], p.dtype) for p in parts],
                aliases=[], n_sems=3 * n, start=start, finish=finish)


def _run_job(job, name):
    def body(*refs):
        _, _, jargs = _job_refs(job, refs, 0, 0)
        job.start(*jargs)
        if job.mid is not None:
            job.mid(*jargs)
        job.finish(*jargs)

    hosted = _host(job, 0, 0)
    return list(pl.pallas_call(
        body, name=name, out_shape=hosted["out_shape"], in_specs=hosted["in_specs"], out_specs=hosted["out_specs"],
        scratch_shapes=hosted["scratch"], input_output_aliases=hosted["aliases"],
    )(*hosted["ins"]))


def _swap_halves(grads):
    n = len(grads)

    def body(*refs):
        ins, outs = refs[:n], refs[n:2 * n]
        send_sems, recv_sems = refs[2 * n:]
        x, y, c, _ = _place()
        copies = []
        for t in range(n):
            h = ins[t].shape[1] // 2
            copies.append(pltpu.make_async_remote_copy(
                src_ref=ins[t].at[:, pl.ds((1 - c) * h, h), :], dst_ref=outs[t],
                send_sem=send_sems.at[t], recv_sem=recv_sems.at[t], device_id=(x, y, 1 - c), device_id_type=MESH))
            copies[-1].start()
        for cp in copies:
            cp.wait()

    return pl.pallas_call(
        body, name="swap_halves",
        out_shape=[jax.ShapeDtypeStruct((g.shape[0], g.shape[1] // 2, g.shape[2]), g.dtype) for g in grads],
        in_specs=[_ANY] * n, out_specs=[_ANY] * n,
        scratch_shapes=[pltpu.SemaphoreType.DMA((n,)), pltpu.SemaphoreType.DMA((n,))],
    )(*grads)


def _share_halves(bufs):
    n = len(bufs)

    def body(*refs):
        outs = refs[n:2 * n]
        send_sems, recv_sems = refs[2 * n:]
        x, y, c, _ = _place()

        def half(t, hc):
            h = outs[t].shape[1] // 2
            return outs[t].at[:, pl.ds(hc * h, h), :]

        def copy(t, ref):
            return pltpu.make_async_remote_copy(src_ref=ref, dst_ref=ref, send_sem=send_sems.at[t],
                                                recv_sem=recv_sems.at[t], device_id=(x, y, 1 - c),
                                                device_id_type=MESH)

        sends = [copy(t, half(t, c)) for t in range(n)]
        for cp in sends:
            cp.start()
        for t in range(n):
            copy(t, half(t, 1 - c)).wait_recv()
        for cp in sends:
            cp.wait_send()

    return pl.pallas_call(
        body, name="share_halves",
        out_shape=[jax.ShapeDtypeStruct(b.shape, b.dtype) for b in bufs],
        in_specs=[_ANY] * n, out_specs=[_ANY] * n, input_output_aliases={t: t for t in range(n)},
        scratch_shapes=[pltpu.SemaphoreType.DMA((n,)), pltpu.SemaphoreType.DMA((n,))],
    )(*bufs)


def _sum_sibling(grad, recv):
    _, r, cols = grad.shape
    h = r // 2
    tr, tc = _tile(h, 512, 16), _tile(cols, 1024)
    per = h // tr

    def body(g_ref, r_ref, o_ref):
        o_ref[...] = (g_ref[...].astype(F32) + r_ref[...].astype(F32)).astype(BF16)

    return pl.pallas_call(
        body, name="sum_sibling", grid=(N_CHIPS, per, cols // tc),
        in_specs=[pl.BlockSpec((None, tr, tc), lambda j, i, n: (j, lax.axis_index("c") * per + i, n)),
                  pl.BlockSpec((None, tr, tc), lambda j, i, n: (j, i, n))],
        out_specs=pl.BlockSpec((None, tr, tc), lambda j, i, n: (j, i, n)),
        out_shape=jax.ShapeDtypeStruct((N_CHIPS, h, cols), BF16),
        compiler_params=_params(("parallel", "parallel", "parallel")),
    )(grad, recv)


def _sum_owner(parts, recv, layer, layers, prev=None):
    _, h, cols = parts.shape
    tr, tc = _tile(h, 512, 16), _tile(cols, 1024)
    per = h // tr

    def body(p_ref, r_ref, *rest):
        o_ref = rest[-1]
        o_ref[...] = (p_ref[...].astype(F32) + r_ref[0].astype(F32) + r_ref[1].astype(F32)
                      + r_ref[2].astype(F32))

    in_specs = [pl.BlockSpec((None, tr, tc), lambda i, n: (_my_chip(), i, n)),
                pl.BlockSpec((3, tr, tc), lambda i, n: (0, i, n))]
    args = [parts, recv]
    aliases = {}
    if prev is not None:
        in_specs.append(_ANY)
        args.append(prev)
        aliases = {2: 0}
    return pl.pallas_call(
        body, name="sum_owner", grid=(per, cols // tc), in_specs=in_specs,
        out_specs=pl.BlockSpec((None, tr, tc), lambda i, n: (layer, lax.axis_index("c") * per + i, n)),
        out_shape=jax.ShapeDtypeStruct((layers, 2 * h, cols), F32),
        input_output_aliases=aliases,
        compiler_params=_params(("parallel", "parallel")),
    )(*args)


def _proj_fetching(h, w_buf, layer, tm, tn, epilogue):
    s, d = h.shape
    cs = w_buf.shape[3]
    per, gi = cs // tn, s // tm
    n_cols = N_CHIPS * cs

    def slot_chip(m):
        return 2 * ((lax.axis_index("x") + m % 2) % 2) + (lax.axis_index("y") + m // 2) % 2

    def body(h_ref, buf_in, proj_ref, projb_ref, buf, w_vmem, send_sems, recv_sems, w_sems):
        m, i, j = pl.program_id(0), pl.program_id(1), pl.program_id(2)
        x, y, c, chips = _place()
        sibling = (x, y, 1 - c)
        step = (m * gi + i) * per + j
        slot = step % 2

        def half(chip_idx, hc):
            return buf.at[chip_idx, layer, pl.ds(hc * (d // 2), d // 2), :]

        def copy(k, ref, to):
            return pltpu.make_async_remote_copy(src_ref=ref, dst_ref=ref, send_sem=send_sems.at[k],
                                                recv_sem=recv_sems.at[k], device_id=to, device_id_type=MESH)

        def fetch(mm, jj, into):
            cols = pl.ds(pl.multiple_of(jj * tn, tn), tn)
            return pltpu.make_async_copy(buf.at[slot_chip(mm), layer, :, cols], w_vmem.at[into], w_sems.at[into])

        @pl.when(step == 0)
        def _():
            for k, chip in enumerate(chips):
                copy(k, half(2 * x + y, c), (*chip, c)).start()
            fetch(0, 0, 0).start()

        row_end = j == per - 1
        slot_end = row_end & (i == gi - 1)
        for k, (cx, cy) in enumerate(chips):
            @pl.when(slot_end & (m == k))
            def _():
                landed = half(2 * cx + cy, c)
                copy(k, landed, (cx, cy, c)).wait_recv()
                copy(3 + k, landed, sibling).start()
                copy(3 + k, half(2 * cx + cy, 1 - c), sibling).wait_recv()

        @pl.when(jnp.logical_not(slot_end & (m == N_CHIPS - 1)))
        def _():
            fetch(jnp.where(slot_end, m + 1, m), jnp.where(row_end, 0, j + 1), 1 - slot).start()

        fetch(m, j, slot).wait()
        acc = jnp.dot(h_ref[...], w_vmem[slot], preferred_element_type=F32)
        vals = epilogue(slot_chip(m) * per + j, acc)
        proj_ref[...] = vals[0]
        projb_ref[...] = vals[1].astype(BF16)

        @pl.when(slot_end & (m == N_CHIPS - 1))
        def _():
            for k, (cx, cy) in enumerate(chips):
                copy(k, half(2 * x + y, c), (cx, cy, c)).wait_send()
                copy(3 + k, half(2 * cx + cy, c), sibling).wait_send()

    out_blk = pl.BlockSpec((tm, tn), lambda m, i, j: (i, slot_chip(m) * per + j))
    proj, projb, filled = pl.pallas_call(
        body, name="proj_fetching", grid=(N_CHIPS, gi, per),
        in_specs=[pl.BlockSpec((tm, d), lambda m, i, j: (i, 0)), _ANY],
        out_specs=[out_blk, out_blk, _ANY],
        out_shape=[jax.ShapeDtypeStruct((s, n_cols), F32), jax.ShapeDtypeStruct((s, n_cols), BF16),
                   jax.ShapeDtypeStruct(w_buf.shape, w_buf.dtype)],
        scratch_shapes=[pltpu.VMEM((2, d, tn), BF16), pltpu.SemaphoreType.DMA((6,)), pltpu.SemaphoreType.DMA((6,)),
                        pltpu.SemaphoreType.DMA((2,))],
        input_output_aliases={1: 2},
        compiler_params=_params(("arbitrary", "arbitrary", "arbitrary")),
    )(h, w_buf)
    return proj, projb, filled


def _layer_fwd(x, mod, wbuf, p, layer, cfg, make_jobs):
    d, tr, tm = cfg["d"], cfg["tr"], cfg["tm"]
    s = x.shape[0]
    weight = lambda n: _Sharded(wbuf[n], BIG_KIND[n], layer=layer)
    hosted = lambda key: make_jobs[key](wbuf) if key in make_jobs else (None, ())
    shift, scale, rg = mod[:, 0:d], mod[:, d:2 * d], mod[:, 2 * d:3 * d]
    h, h_t = _norm_fwd(x, p["norm_g"], scale, shift, tr)
    tn_in = cfg["tn_in"]
    q_lo, q_hi, q_scale = d // tn_in, 2 * d // tn_in, HEAD_DIM ** -0.5
    blk_in = pl.BlockSpec((tm, tn_in), lambda i, j, k: (i, j))

    def proj_out(col, acc):
        return acc, acc * jnp.where((col >= q_lo) & (col < q_hi), q_scale, 1.0)

    if make_jobs.get("fetch_w_in"):
        proj, projb, wbuf["w_in"] = _proj_fetching(h, wbuf["w_in"], layer, tm, tn_in, proj_out)
    else:
        job, names = hosted("proj")
        proj, projb, *job_out = _mm_nn("proj", h, weight("w_in"), tm=tm, tn=tn_in, tk=d, with_col=True,
                                       epilogue=proj_out, job=job,
                                       outs=[(jax.ShapeDtypeStruct((s, 7 * d), F32), blk_in),
                                             (jax.ShapeDtypeStruct((s, 7 * d), BF16), blk_in)])
        wbuf.update(zip(names, job_out))
    ya = _pool_fwd(proj, p["pool_w"], p["pool_scale"], d, tr)
    yc = _conv_fwd(proj, p["conv_w"], d, tr)
    job, names = hosted("attn")
    att, yb, rsave, job_out = _sb_fwd(proj, projb, d, cfg["blk"], job)
    wbuf.update(zip(names, job_out))
    gates, = _mm_nn("gates", h, weight("w_gate"), tm=tm, tn=cfg["tn_gate"], tk=d,
                    extras=[(p["b_gate"], pl.BlockSpec((1, cfg["tn_gate"]), lambda i, j, k: (0, j)))],
                    epilogue=lambda acc, b: (_sigmoid(acc + b),))
    tn = cfg["tn_d"]
    blk = pl.BlockSpec((tm, tn), lambda i, j, k: (i, j))
    sd = jax.ShapeDtypeStruct((s, d), F32)
    pa, = _mm_nn("branch_a", ya, weight("w_br_a"), tm=tm, tn=tn, tk=d // 2)
    pb, = _mm_nn("branch_b", yb, weight("w_br_b"), tm=tm, tn=tn, tk=cfg["tk_row"])
    gate_blk = lambda n: (gates, pl.BlockSpec((tm, tn), lambda i, j, k: (i, n * (d // tn) + j)))
    pc, merged = _mm_nn("branch_c", yc, weight("w_br_c"), tm=tm, tn=tn, tk=d // 2,
                        outs=[(sd, blk), (jax.ShapeDtypeStruct((s, d), BF16), blk)],
                        extras=[gate_blk(0), gate_blk(1), gate_blk(2), (pa, blk), (pb, blk)],
                        epilogue=lambda acc, g0, g1, g2, av, bv: (acc, g0 * av + g1 * bv + g2 * acc))
    out, x_next = _mm_nn("out_proj", merged, weight("w_out"), tm=tm, tn=tn, tk=cfg["tk_row"],
                         outs=[(sd, blk), (sd, blk)],
                         extras=[(x, blk), (rg, pl.BlockSpec((1, tn), lambda i, j, k: (0, j)))],
                         epilogue=lambda acc, xv, g: (acc, xv + g * acc))
    saved = dict(x=x, h_t=h_t, proj=proj, projb=projb, gates=gates, ya=ya, yb=yb, yc=yc, att=att, rsave=rsave,
                 pa=pa, pb=pb, pc=pc, merged=merged, out=out, scale=scale, rg=rg)
    return x_next, saved


def _reduce_prepare(grads):
    return [_sum_sibling(g, r) for g, r in zip(grads, _swap_halves(grads))]


def _layer_bwd(dxo, sv, wbuf, p, layer, cfg):
    d, tr, tm, tk_s = cfg["d"], cfg["tr"], cfg["tm"], cfg["tk_s"]
    weight = lambda n: _Sharded(wbuf[n], BIG_KIND[n], layer=layer)
    dout, drg = _resid_bwd(dxo, sv["out"], sv["rg"], tr)
    dmerged = _mm_nt("d_merged", dout, weight("w_out"), tm=tm, tn=cfg["tn_row"], tk=d)
    g_out = _mm_tn("g_w_out", sv["merged"], dout, "row", tm=cfg["tn_row"], tn=cfg["tn_d"], tk=tk_s)
    dpa, dpb, dpc, dlogit, g_bgate = _merge_bwd(dmerged, sv["gates"], sv["pa"], sv["pb"], sv["pc"], cfg["tr_small"])
    dya = _mm_nt("d_ya", dpa, weight("w_br_a"), tm=tm, tn=cfg["tn_half"], tk=cfg["tn_d"])
    dyb = _mm_nt("d_yb", dpb, weight("w_br_b"), tm=tm, tn=cfg["tn_row"], tk=d)
    dyc = _mm_nt("d_yc", dpc, weight("w_br_c"), tm=tm, tn=cfg["tn_half"], tk=cfg["tn_d"])
    g_a = _mm_tn("g_w_br_a", sv["ya"], dpa, "col", tm=cfg["tn_half"], tn=cfg["tn_d"], tk=tk_s)
    g_b = _mm_tn("g_w_br_b", sv["yb"], dpb, "row", tm=cfg["tn_row"], tn=cfg["tn_d"], tk=tk_s)
    g_c = _mm_tn("g_w_br_c", sv["yc"], dpc, "col", tm=cfg["tn_half"], tn=cfg["tn_d"], tk=tk_s)
    g_gate = _mm_grad("g_w_gate", sv["h_t"], dlogit, "col", tm=cfg["tm_g"], tn=cfg["tn_gate"], tk=tk_s)
    early = ("w_gate", "w_br_a", "w_br_b", "w_br_c", "w_out")
    parts = dict(zip(early, _reduce_prepare([g_gate, g_a, g_b, g_c, g_out])))
    d_a, g_pool_w, g_pool_scale = _pool_bwd(sv["proj"], dya, p["pool_w"], p["pool_scale"], d, tr)
    d_c, g_conv_w = _conv_bwd(sv["proj"], dyc, p["conv_w"], d, tr)
    dq, dk, dv, dzb, sent = _sb_bwd(sv["proj"], sv["projb"], sv["att"], dyb, sv["rsave"], d, cfg["blk"],
                                    _send_job([parts[n] for n in early]))
    theirs = dict(zip(early, sent))
    dproj = jnp.concatenate([d_a, dq.astype(BF16), dk.astype(BF16), dv.astype(BF16), dzb.astype(BF16), d_c], axis=1)
    g_in = _mm_grad("g_w_in", sv["h_t"], dproj, "col", tm=cfg["tm_g"], tn=cfg["tn_in"], tk=tk_s)
    parts["w_in"], = _reduce_prepare([g_in])
    dh_in, (theirs["w_in"],) = _mm_nt("d_h_in", dproj, weight("w_in"), tm=tm, tn=cfg["tn_d"], tk=cfg["tk_in"],
                                      job=_send_job([parts["w_in"]]))
    tn = cfg["tn_d"]
    dh = _mm_nt("d_h_gate", dlogit, weight("w_gate"), tm=tm, tn=tn, tk=cfg["tk_gate"],
                extras=[(dh_in, pl.BlockSpec((tm, tn), lambda i, j, k: (i, j)))],
                epilogue=lambda acc, prev: (prev + acc,))
    dx, dshift, dscale, g_norm = _norm_bwd(dh, sv["x"], p["norm_g"], sv["scale"], dxo, tr)
    small = dict(dmod=jnp.concatenate([dshift, dscale, drg], axis=1), norm_g=g_norm, pool_scale=g_pool_scale,
                 b_gate=g_bgate, conv_w=g_conv_w, pool_w=g_pool_w)
    return dx, parts, theirs, small


BIG = ("w_in", "w_gate", "w_br_a", "w_br_b", "w_br_c", "w_out")
BIG_KIND = dict(w_in="col", w_gate="col", w_br_a="col", w_br_b="row", w_br_c="col", w_out="row")


def _pad_to(v, n):
    return jnp.pad(v, (0, n - v.shape[0]))


def kernel(x, c, norm_g, w_ada, b_ada, w_in, pool_w, pool_scale, conv_w, w_br_a, w_br_b, w_br_c, w_gate, b_gate, w_out, final_g, loss_target, m_norm_g, m_w_ada, m_b_ada, m_w_in, m_pool_w, m_pool_scale, m_conv_w, m_w_br_a, m_w_br_b, m_w_br_c, m_w_gate, m_b_gate, m_w_out, m_final_g, v_norm_g, v_w_ada, v_b_ada, v_w_in, v_pool_w, v_pool_scale, v_conv_w, v_w_br_a, v_w_br_b, v_w_br_c, v_w_gate, v_b_gate, v_w_out, v_final_g):
    weights = dict(norm_g=norm_g, w_ada=w_ada, b_ada=b_ada, w_in=w_in, pool_w=pool_w, pool_scale=pool_scale,
                   conv_w=conv_w, w_br_a=w_br_a, w_br_b=w_br_b, w_br_c=w_br_c, w_gate=w_gate, b_gate=b_gate,
                   w_out=w_out, final_g=final_g)
    mom_m = dict(norm_g=m_norm_g, w_ada=m_w_ada, b_ada=m_b_ada, w_in=m_w_in, pool_w=m_pool_w,
                 pool_scale=m_pool_scale, conv_w=m_conv_w, w_br_a=m_w_br_a, w_br_b=m_w_br_b, w_br_c=m_w_br_c,
                 w_gate=m_w_gate, b_gate=m_b_gate, w_out=m_w_out, final_g=m_final_g)
    mom_v = dict(norm_g=v_norm_g, w_ada=v_w_ada, b_ada=v_b_ada, w_in=v_w_in, pool_w=v_pool_w,
                 pool_scale=v_pool_scale, conv_w=v_conv_w, w_br_a=v_w_br_a, w_br_b=v_w_br_b, w_br_c=v_w_br_c,
                 w_gate=v_w_gate, b_gate=v_b_gate, w_out=v_w_out, final_g=v_final_g)
    names = list(weights)

    _, s, d = x.shape
    layers = norm_g.shape[0]
    pw, gd, cc = d // 2, d // 8, d // 2
    ada_cols = 3 * d // N_CHIPS
    xi, yi, ci = lax.axis_index("x"), lax.axis_index("y"), lax.axis_index("c")
    me = 4 * xi + 2 * yi + ci
    my_chip = 2 * xi + yi

    cfg = dict(
        d=d, tr=_tile(s, 256, 16), tr_small=_tile(s, 128, 16), tm=_tile(s, 1024, 16), tk_s=_tile(s, 4096, 16),
        blk=_tile(s, min(256, max(s // 4, 16)), 16),
        tn_in=_tile(7 * d // N_CHIPS, 512), tk_in=_tile(7 * d // N_CHIPS, 1792),
        tn_gate=_tile(3 * d // N_CHIPS, 512), tk_gate=_tile(3 * d // N_CHIPS, 1536), tn_d=_tile(d // N_CHIPS, 512),
        tn_row=_tile(d // N_CHIPS, 512, 16), tk_row=_tile(d // N_CHIPS, 512, 16),
        tn_half=_tile(d // 2, 512), tm_g=_tile(d, 1024),
    )

    conv_flat = conv_w.reshape(-1)
    conv_len = -(-conv_flat.shape[0] // 1024) * 1024
    pack0 = jnp.concatenate([c.reshape(-1), _pad_to(conv_flat, conv_len), pool_w.reshape(-1)])
    w0 = -(-pack0.shape[0] // 1024) * 1024
    g0 = _all_gather_small(_pad_to(pack0, w0).reshape(8, w0 // 8), "gather_small").reshape(N_DEV, w0)
    c_all = g0[:, 0:d]
    chip_rows = g0[0::2]
    conv_full = jnp.concatenate(
        [chip_rows[j, d:d + conv_flat.shape[0]].reshape(conv_w.shape) for j in range(N_CHIPS)], axis=2)
    pool_full = jnp.concatenate(
        [chip_rows[j, d + conv_len:d + conv_len + pool_w.size].reshape(pool_w.shape) for j in range(N_CHIPS)],
        axis=2)
    pool_bf = pool_full.astype(BF16)

    sc16 = _silu_bf16(jnp.pad(c_all, ((0, 16 - N_DEV), (0, 0))))
    mods = []
    for l in range(layers):
        bias = lax.dynamic_slice(b_ada[l], (my_chip * ada_cols,), (ada_cols,)).reshape(1, ada_cols)
        tn = _tile(ada_cols, 512)
        mod_l, = _mm("mod", sc16, w_ada[l], dims=NN, grid=(1, ada_cols // tn, 1),
                     a_spec=pl.BlockSpec((16, d), lambda i, j, k: (0, 0)),
                     b_spec=pl.BlockSpec((d, tn), lambda i, j, k: (0, j)),
                     acc_shape=(16, tn),
                     outs=[(jax.ShapeDtypeStruct((16, ada_cols), F32), pl.BlockSpec((16, tn), lambda i, j, k: (0, j)))],
                     extras=[(bias, pl.BlockSpec((1, tn), lambda i, j, k: (0, j)))],
                     epilogue=lambda acc, b: (acc + b,))
        mods.append(mod_l[0:N_DEV])
    g1 = _all_gather_small(jnp.concatenate(mods, axis=1), "gather_mod")
    g1 = g1.reshape(N_CHIPS, 2, N_DEV, layers, ada_cols)[:, 0]
    mod_all = jnp.transpose(g1, (1, 2, 0, 3)).reshape(N_DEV, layers, 3 * d)
    mod_me = lax.dynamic_slice(mod_all, (me, 0, 0), (1, layers, 3 * d))[0]

    wbuf = {n: _place_shard(weights[n]) for n in BIG}
    params = [dict(norm_g=norm_g[l:l + 1], pool_scale=pool_scale[l:l + 1], b_gate=b_gate[l:l + 1],
                   conv_w=conv_full[l], pool_w=pool_bf[l]) for l in range(layers)]
    others = [n for n in BIG if n != "w_in"]

    def gather(names, layer):
        return lambda wb: (_gather_job([wb[n] for n in names], layer), names)

    act = x[0]
    saved = []
    for l in range(layers):
        nxt = ["w_in"] if l + 1 < layers else []
        if l == 0:
            make_jobs = {"fetch_w_in": True, "attn": gather(others + nxt, [0] * len(others) + [1] * len(nxt))}
        else:
            make_jobs = {"proj": gather(others, l)}
            if nxt:
                make_jobs["attn"] = gather(nxt, l + 1)
        act, sv = _layer_fwd(act, mod_me[l:l + 1], wbuf, params[l], l, cfg, make_jobs)
        saved.append(sv)
    loss_part, dact, g_final = _final_loss(act, final_g.reshape(1, d), loss_target[0], cfg["tr"])
    loss = lax.psum(loss_part[0, 0], ("x", "y", "c"))

    small_grads, parts, theirs = [None] * layers, [None] * layers, [None] * layers
    for l in reversed(range(layers)):
        dact, parts[l], theirs[l], small_grads[l] = _layer_bwd(dact, saved[l], wbuf, params[l], l, cfg)
    grad_x = dact.reshape(x.shape)
    owned = []
    for n in BIG:
        buf = None
        for l in range(layers):
            buf = _sum_owner(parts[l][n], theirs[l][n], l, layers, prev=buf)
        owned.append(buf)
    full = _share_halves(owned)
    grads = {n: f.reshape(weights[n].shape) for n, f in zip(BIG, full)}

    small_names = ("dmod", "norm_g", "pool_scale", "b_gate", "conv_w", "pool_w")
    pieces = [small_grads[l][n].reshape(-1) for n in small_names for l in range(layers)] + [g_final.reshape(-1)]
    pack1 = jnp.concatenate(pieces)
    w1 = -(-pack1.shape[0] // 1024) * 1024
    g2 = _all_gather_small(_pad_to(pack1, w1).reshape(8, w1 // 8), "gather_grads")
    total = _sum_blocks(g2, N_DEV, 8).reshape(-1)
    off = 0
    summed = {}
    for n in small_names:
        per = small_grads[0][n].size
        summed[n] = jnp.stack([total[off + l * per:off + (l + 1) * per].reshape(small_grads[0][n].shape)
                               for l in range(layers)])
        off += layers * per
    grads["final_g"] = total[off:off + d]
    grads["norm_g"] = summed["norm_g"].reshape(layers, d)
    grads["pool_scale"] = summed["pool_scale"].reshape(layers, pw)
    grads["b_gate"] = summed["b_gate"].reshape(layers, 3 * d)
    grads["b_ada"] = summed["dmod"].reshape(layers, 3 * d)
    cs = cc // N_CHIPS
    grads["conv_w"] = lax.dynamic_slice(summed["conv_w"], (0, 0, my_chip * cs), (layers, 3, cs))
    rs_ = gd // N_CHIPS
    grads["pool_w"] = lax.dynamic_slice(summed["pool_w"], (0, 0, my_chip * rs_, 0), (layers, N_GROUPS, rs_, gd))
    dmod_all = g2.reshape(N_DEV, w1)[:, 0:layers * 3 * d].reshape(N_DEV, layers, 3 * d)
    g_ada = []
    for l in range(layers):
        cols = lax.dynamic_slice(dmod_all[:, l], (0, my_chip * ada_cols), (N_DEV, ada_cols))
        cols16 = jnp.pad(cols, ((0, 16 - N_DEV), (0, 0)))
        tn = _tile(ada_cols, 512)
        tm = _tile(d, 1024)
        ga, = _mm("g_w_ada", sc16, cols16, dims=TN, grid=(d // tm, ada_cols // tn, 1),
                  a_spec=pl.BlockSpec((16, tm), lambda i, j, k: (0, i)),
                  b_spec=pl.BlockSpec((16, tn), lambda i, j, k: (0, j)),
                  acc_shape=(tm, tn),
                  outs=[(jax.ShapeDtypeStruct((d, ada_cols), F32), pl.BlockSpec((tm, tn), lambda i, j, k: (i, j)))])
        g_ada.append(ga)
    grads["w_ada"] = jnp.stack(g_ada)

    deltas, new_m, new_v = {}, {}, {}
    for n in names:
        deltas[n], new_m[n], new_v[n] = _adamw(weights[n], grads[n], mom_m[n], mom_v[n])
    return (loss, grad_x, *[grads[n] for n in names], *[deltas[n] for n in names],
            *[new_m[n] for n in names], *[new_v[n] for n in names])
```

```python
import functools

import jax
import jax.numpy as jnp
from jax import lax
from jax.experimental import pallas as pl
from jax.experimental.pallas import tpu as pltpu

F32 = jnp.float32
BF16 = jnp.bfloat16
MESH = pl.DeviceIdType.MESH

N_CHIPS = 4
N_DEV = 8
N_GROUPS = 4
POOL_WINDOWS = (2, 4, 8, 16)
POOL_HALO = 16
CONV_HALO = 8
HEAD_DIM = 128
SB_FWD_HEADS = 4
SB_BWD_HEADS = 4
RMS_EPS = 1e-6
ADAM_LR = 0.001
ADAM_B1 = 0.9
ADAM_B2 = 0.999
ADAM_EPS = 1e-08
ADAM_WD = 0.01
ADAM_STEP = 10
V7X_VMEM_LIMIT = 56 * 1024 * 1024


def _tile(n, pref, mult=128):
    best = None
    t = mult
    while t <= min(n, pref):
        if n % t == 0:
            best = t
        t += mult
    return n if best is None else best


def _params(sem=None):
    return pltpu.CompilerParams(dimension_semantics=sem, vmem_limit_bytes=V7X_VMEM_LIMIT)


def _sigmoid(z):
    return jax.nn.sigmoid(z)


def _dsilu(z, sg):
    return sg * (1.0 + z * (1.0 - sg))


NN = ((1,), (0,))
NT = ((1,), (1,))
TN = ((0,), (0,))


def _mm(name, a, b, *, dims, grid, a_spec, b_spec, acc_shape, outs, extras=(), epilogue=None, with_col=False,
        job=None):
    gi, gj, nk = grid
    ne, no = len(extras), len(outs)
    if epilogue is None:
        epilogue = lambda acc: (acc,)

    def body(*refs):
        own_in, orefs, jargs = _job_refs(job, refs[:-1], 2 + ne, no)
        a_ref, b_ref, ex, acc = own_in[0], own_in[1], own_in[2:], refs[-1]
        i, j, k = pl.program_id(0), pl.program_id(1), pl.program_id(2)
        lead = (j,) if with_col else ()
        if job is not None:
            @pl.when((i == 0) & (j == 0) & (k == 0))
            def _():
                job.start(*jargs)

            if job.mid is not None:
                @pl.when((i == gi - 1) & (j == gj // 2) & (k == 0))
                def _():
                    job.mid(*jargs)

        @pl.when(k == 0)
        def _():
            acc[...] = jnp.zeros_like(acc)

        acc[...] += lax.dot_general(a_ref[...].astype(BF16), b_ref[...].astype(BF16), (dims, ((), ())),
                                    preferred_element_type=F32)

        @pl.when(k == nk - 1)
        def _():
            vals = epilogue(*lead, acc[...], *[e[...] for e in ex])
            for o, v in zip(orefs, vals):
                o[...] = v.astype(o.dtype)

        if job is not None:
            @pl.when((i == gi - 1) & (j == gj - 1) & (k == nk - 1))
            def _():
                job.finish(*jargs)

    hosted = _host(job, 2 + ne, no)
    sem = ("parallel", "parallel", "arbitrary") if job is None else ("arbitrary",) * 3
    res = pl.pallas_call(
        body, name=name, grid=grid,
        in_specs=[a_spec, b_spec] + [s for _, s in extras] + hosted["in_specs"],
        out_specs=[s for _, s in outs] + hosted["out_specs"],
        out_shape=[sh for sh, _ in outs] + hosted["out_shape"],
        scratch_shapes=hosted["scratch"] + [pltpu.VMEM(acc_shape, F32)],
        input_output_aliases=hosted["aliases"],
        compiler_params=_params(sem),
    )(a, b, *[e for e, _ in extras], *hosted["ins"])
    return res


class _Sharded:
    def __init__(self, arr, kind, layer=None):
        self.arr, self.kind, self.layer = arr, kind, layer
        r, c = arr.shape[-2:]
        self.rows = r * (N_CHIPS if kind == "row" else 1)
        self.cols = c * (N_CHIPS if kind == "col" else 1)
        self.sr, self.sc = r, c

    def spec(self, br, bc, f):
        lead = (None,) if self.layer is None else (None, None)
        layer = self.layer
        if self.kind == "col":
            per = self.sc // bc
            assert per * bc == self.sc and self.sr % br == 0, (self.arr.shape, br, bc)

            def idx(*g):
                rb, cb = f(*g)
                return ((cb // per,) + (() if layer is None else (layer,)) + (rb, cb % per))
        else:
            per = self.sr // br
            assert per * br == self.sr and self.sc % bc == 0, (self.arr.shape, br, bc)

            def idx(*g):
                rb, cb = f(*g)
                return ((rb // per,) + (() if layer is None else (layer,)) + (rb % per, cb))
        return pl.BlockSpec(lead + (br, bc), idx)


def _grad_buffer(rows, cols, kind):
    if kind == "col":
        return jax.ShapeDtypeStruct((N_CHIPS, rows, cols // N_CHIPS), BF16)
    return jax.ShapeDtypeStruct((N_CHIPS, rows // N_CHIPS, cols), BF16)


def _mm_nn(name, a, w, *, tm, tn, tk, outs=None, extras=(), epilogue=None, out_dtype=F32, with_col=False,
           job=None):
    m, kdim = a.shape
    n = w.cols
    grid = (m // tm, n // tn, kdim // tk)
    if outs is None:
        outs = [(jax.ShapeDtypeStruct((m, n), out_dtype), pl.BlockSpec((tm, tn), lambda i, j, k: (i, j)))]
    return _mm(name, a, w.arr, dims=NN, grid=grid,
               a_spec=pl.BlockSpec((tm, tk), lambda i, j, k: (i, k)),
               b_spec=w.spec(tk, tn, lambda i, j, k: (k, j)),
               acc_shape=(tm, tn), outs=outs, extras=extras, epilogue=epilogue, with_col=with_col, job=job)


def _mm_grad(name, a_t, b, kind, *, tm, tn, tk):
    m, kdim = a_t.shape
    n = b.shape[1]
    out = _Sharded(_grad_buffer(m, n, kind), kind)
    return _mm(name, a_t, b, dims=NN, grid=(m // tm, n // tn, kdim // tk),
               a_spec=pl.BlockSpec((tm, tk), lambda i, j, k: (i, k)),
               b_spec=pl.BlockSpec((tk, tn), lambda i, j, k: (k, j)),
               acc_shape=(tm, tn), outs=[(out.arr, out.spec(tm, tn, lambda i, j, k: (i, j)))])[0]


def _mm_nt(name, a, w, *, tm, tn, tk, extras=(), epilogue=None, out_dtype=F32, job=None):
    m, kdim = a.shape
    n = w.rows
    grid = (m // tm, n // tn, kdim // tk)
    outs = [(jax.ShapeDtypeStruct((m, n), out_dtype), pl.BlockSpec((tm, tn), lambda i, j, k: (i, j)))]
    res = _mm(name, a, w.arr, dims=NT, grid=grid,
              a_spec=pl.BlockSpec((tm, tk), lambda i, j, k: (i, k)),
              b_spec=w.spec(tn, tk, lambda i, j, k: (j, k)),
              acc_shape=(tm, tn), outs=outs, extras=extras, epilogue=epilogue, job=job)
    return res[0] if job is None else (res[0], list(res[1:]))


def _mm_tn(name, a, b, kind, *, tm, tn, tk):
    kdim, m = a.shape
    n = b.shape[1]
    grid = (m // tm, n // tn, kdim // tk)
    out = _Sharded(_grad_buffer(m, n, kind), kind)
    outs = [(out.arr, out.spec(tm, tn, lambda i, j, k: (i, j)))]
    return _mm(name, a, b, dims=TN, grid=grid,
               a_spec=pl.BlockSpec((tk, tm), lambda i, j, k: (k, i)),
               b_spec=pl.BlockSpec((tk, tn), lambda i, j, k: (k, j)),
               acc_shape=(tm, tn), outs=outs)[0]


def _row_spec(tr, w, col=0):
    return pl.BlockSpec((tr, w), lambda i: (i, col))


def _vec_spec(w, col=0, rows=1):
    return pl.BlockSpec((rows, w), lambda i: (0, col))


def _norm_fwd(x, g, scale, shift, tr):
    s, d = x.shape

    def body(x_ref, g_ref, sc_ref, sh_ref, h_ref, ht_ref):
        xv = x_ref[...]
        r = lax.rsqrt(jnp.mean(xv * xv, axis=-1, keepdims=True) + RMS_EPS)
        y = xv * r * g_ref[...]
        hv = y * (1.0 + sc_ref[...]) + sh_ref[...]
        h_ref[...] = hv.astype(BF16)
        ht_ref[...] = hv.T.astype(BF16)

    return pl.pallas_call(
        body, name="norm_fwd", grid=(s // tr,),
        in_specs=[_row_spec(tr, d), _vec_spec(d), _vec_spec(d), _vec_spec(d)],
        out_specs=[_row_spec(tr, d), pl.BlockSpec((d, tr), lambda i: (0, i))],
        out_shape=[jax.ShapeDtypeStruct((s, d), BF16), jax.ShapeDtypeStruct((d, s), BF16)],
        compiler_params=_params(("parallel",)),
    )(x, g, scale, shift)


def _norm_bwd(dh, x, g, scale, dxo, tr):
    s, d = x.shape

    def body(dh_ref, x_ref, g_ref, sc_ref, dxo_ref, dx_ref, dsh_ref, dsc_ref, dg_ref):
        @pl.when(pl.program_id(0) == 0)
        def _():
            dsh_ref[...] = jnp.zeros_like(dsh_ref)
            dsc_ref[...] = jnp.zeros_like(dsc_ref)
            dg_ref[...] = jnp.zeros_like(dg_ref)

        xv, dhv, gv = x_ref[...], dh_ref[...], g_ref[...]
        r = lax.rsqrt(jnp.mean(xv * xv, axis=-1, keepdims=True) + RMS_EPS)
        xn = xv * r
        dsh_ref[...] += jnp.sum(dhv, axis=0, keepdims=True)
        dsc_ref[...] += jnp.sum(dhv * (xn * gv), axis=0, keepdims=True)
        dyg = dhv * (1.0 + sc_ref[...])
        dg_ref[...] += jnp.sum(dyg * xn, axis=0, keepdims=True)
        dxn = dyg * gv
        dx_ref[...] = dxo_ref[...] + r * (dxn - xn * jnp.mean(dxn * xn, axis=-1, keepdims=True))

    vec = jax.ShapeDtypeStruct((1, d), F32)
    return pl.pallas_call(
        body, name="norm_bwd", grid=(s // tr,),
        in_specs=[_row_spec(tr, d), _row_spec(tr, d), _vec_spec(d), _vec_spec(d), _row_spec(tr, d)],
        out_specs=[_row_spec(tr, d), _vec_spec(d), _vec_spec(d), _vec_spec(d)],
        out_shape=[jax.ShapeDtypeStruct((s, d), F32), vec, vec, vec],
        compiler_params=_params(("arbitrary",)),
    )(dh, x, g, scale, dxo)


def _final_loss(x, g, target, tr):
    s, d = x.shape

    def body(x_ref, g_ref, t_ref, loss_ref, dx_ref, dg_ref):
        @pl.when(pl.program_id(0) == 0)
        def _():
            loss_ref[...] = jnp.zeros_like(loss_ref)
            dg_ref[...] = jnp.zeros_like(dg_ref)

        xv, gv = x_ref[...], g_ref[...]
        r = lax.rsqrt(jnp.mean(xv * xv, axis=-1, keepdims=True) + RMS_EPS)
        xn = xv * r
        err = xn * gv - t_ref[...]
        per_row = jnp.mean(err * err, axis=-1, keepdims=True)
        loss_ref[...] += 0.5 * jnp.sum(per_row, axis=0, keepdims=True)
        dy = err * (1.0 / d)
        dg_ref[...] += jnp.sum(dy * xn, axis=0, keepdims=True)
        dxn = dy * gv
        dx_ref[...] = r * (dxn - xn * jnp.mean(dxn * xn, axis=-1, keepdims=True))

    return pl.pallas_call(
        body, name="final_loss", grid=(s // tr,),
        in_specs=[_row_spec(tr, d), _vec_spec(d), _row_spec(tr, d)],
        out_specs=[pl.BlockSpec((8, 128), lambda i: (0, 0)), _row_spec(tr, d), _vec_spec(d)],
        out_shape=[jax.ShapeDtypeStruct((8, 128), F32), jax.ShapeDtypeStruct((s, d), F32),
                   jax.ShapeDtypeStruct((1, d), F32)],
        compiler_params=_params(("arbitrary",)),
    )(x, g, target)


def _halo_before(tr, halo, w, col):
    per = tr // halo
    return pl.BlockSpec((halo, w), lambda i: (jnp.maximum(i * per - 1, 0), col))


def _halo_after(tr, halo, w, col, n_tiles):
    per = tr // halo
    return pl.BlockSpec((halo, w), lambda i: (jnp.minimum((i + 1) * per, n_tiles * per - 1), col))


def _pool_fwd(proj, pool_w, pool_scale, d, tr):
    s = proj.shape[0]
    pw, gd, hl = d // 2, d // 8, POOL_HALO

    def body(xa_ref, xh_ref, za_ref, w_ref, ps_ref, ya_ref, buf):
        i = pl.program_id(0)
        buf[0:hl, :] = jnp.where(i > 0, xh_ref[...], 0.0)
        buf[hl:hl + tr, :] = xa_ref[...]
        row = i * tr + lax.broadcasted_iota(jnp.int32, (tr, 1), 0)
        za = za_ref[...]
        gate = za * _sigmoid(za)
        for g, win in enumerate(POOL_WINDOWS):
            cs = slice(g * gd, (g + 1) * gd)
            xg = buf[hl:hl + tr, cs]
            acc = xg
            for j in range(1, win):
                acc = acc + buf[hl - j:hl - j + tr, cs]
            cnt = jnp.minimum(row + 1, win).astype(F32)
            mixed = acc / cnt - xg
            y = jnp.dot(mixed.astype(BF16), w_ref[g], preferred_element_type=F32)
            ya_ref[:, cs] = ((y * ps_ref[:, cs]) * gate[:, cs]).astype(BF16)

    return pl.pallas_call(
        body, name="pool_fwd", grid=(s // tr,),
        in_specs=[_row_spec(tr, pw, 0), _halo_before(tr, hl, pw, 0), _row_spec(tr, pw, 1),
                  pl.BlockSpec((N_GROUPS, gd, gd), lambda i: (0, 0, 0)), _vec_spec(pw)],
        out_specs=_row_spec(tr, pw), out_shape=jax.ShapeDtypeStruct((s, pw), BF16),
        scratch_shapes=[pltpu.VMEM((tr + hl, pw), F32)],
        compiler_params=_params(("parallel",)),
    )(proj, proj, proj, pool_w, pool_scale)


def _pool_bwd(proj, dya, pool_w, pool_scale, d, tr):
    s = proj.shape[0]
    pw, gd, hl = d // 2, d // 8, POOL_HALO
    n_tiles = s // tr

    def body(xa_ref, xh_ref, za_ref, zh_ref, dya_ref, dyh_ref, w_ref, ps_ref, da_ref, gw_ref, gs_ref, buf, dbuf):
        i = pl.program_id(0)

        @pl.when(i == 0)
        def _():
            gw_ref[...] = jnp.zeros_like(gw_ref)
            gs_ref[...] = jnp.zeros_like(gs_ref)

        buf[0:hl, :] = jnp.where(i > 0, xh_ref[...], 0.0)
        buf[hl:hl + tr, :] = xa_ref[...]
        row = i * tr + lax.broadcasted_iota(jnp.int32, (tr, 1), 0)
        row_h = (i + 1) * tr + lax.broadcasted_iota(jnp.int32, (hl, 1), 0)
        za, dya_t = za_ref[...], dya_ref[...]
        sg = _sigmoid(za)
        gate = za * sg
        dpre = dya_t * gate
        zh = zh_ref[...]
        dpre_h = jnp.where(i < n_tiles - 1, dyh_ref[...], 0.0) * (zh * _sigmoid(zh))
        ps = ps_ref[...]
        for g, win in enumerate(POOL_WINDOWS):
            cs = slice(g * gd, (g + 1) * gd)
            wg = w_ref[g]
            xg = buf[hl:hl + tr, cs]
            acc = xg
            for j in range(1, win):
                acc = acc + buf[hl - j:hl - j + tr, cs]
            cnt = jnp.minimum(row + 1, win).astype(F32)
            mixed = (acc / cnt - xg).astype(BF16)
            ylin = jnp.dot(mixed, wg, preferred_element_type=F32)
            gs_ref[:, cs] += jnp.sum(dpre[:, cs] * ylin, axis=0, keepdims=True)
            da_ref[:, pw + g * gd:pw + (g + 1) * gd] = (
                dya_t[:, cs] * (ylin * ps[:, cs]) * _dsilu(za[:, cs], sg[:, cs])).astype(BF16)
            dyl = (dpre[:, cs] * ps[:, cs]).astype(BF16)
            gw_ref[g] += lax.dot_general(mixed, dyl, (TN, ((), ())), preferred_element_type=F32)
            dmix = lax.dot_general(dyl, wg, (NT, ((), ())), preferred_element_type=F32)
            dyl_h = (dpre_h[:, cs] * ps[:, cs]).astype(BF16)
            dmix_h = lax.dot_general(dyl_h, wg, (NT, ((), ())), preferred_element_type=F32)
            cnt_h = jnp.minimum(row_h + 1, win).astype(F32)
            dbuf[0:tr, cs] = dmix / cnt
            dbuf[tr:tr + hl, cs] = dmix_h / cnt_h
            dx = dbuf[0:tr, cs] - dmix
            for j in range(1, win):
                dx = dx + dbuf[j:j + tr, cs]
            da_ref[:, cs] = dx.astype(BF16)

    return pl.pallas_call(
        body, name="pool_bwd", grid=(n_tiles,),
        in_specs=[_row_spec(tr, pw, 0), _halo_before(tr, hl, pw, 0),
                  _row_spec(tr, pw, 1), _halo_after(tr, hl, pw, 1, n_tiles),
                  _row_spec(tr, pw, 0), _halo_after(tr, hl, pw, 0, n_tiles),
                  pl.BlockSpec((N_GROUPS, gd, gd), lambda i: (0, 0, 0)), _vec_spec(pw)],
        out_specs=[_row_spec(tr, 2 * pw), pl.BlockSpec((N_GROUPS, gd, gd), lambda i: (0, 0, 0)), _vec_spec(pw)],
        out_shape=[jax.ShapeDtypeStruct((s, 2 * pw), BF16), jax.ShapeDtypeStruct((N_GROUPS, gd, gd), F32),
                   jax.ShapeDtypeStruct((1, pw), F32)],
        scratch_shapes=[pltpu.VMEM((tr + hl, pw), F32), pltpu.VMEM((tr + hl, pw), F32)],
        compiler_params=_params(("arbitrary",)),
    )(proj, proj, proj, proj, dya, dya, pool_w, pool_scale)


def _conv_fwd(proj, conv_w, d, tr):
    s = proj.shape[0]
    cc, hl = d // 2, CONV_HALO
    cu, cb, cg_, cz = 10, 11, 12, 13

    def body(u_ref, uh_ref, bg_ref, cg_ref, ch_ref, zc_ref, w_ref, yc_ref, buf):
        i = pl.program_id(0)
        buf[0:hl, :] = jnp.where(i > 0, ch_ref[...] * uh_ref[...], 0.0)
        buf[hl:hl + tr, :] = cg_ref[...] * u_ref[...]
        y = w_ref[0:1, :] * buf[hl - 2:hl - 2 + tr, :]
        y = y + w_ref[1:2, :] * buf[hl - 1:hl - 1 + tr, :]
        y = y + w_ref[2:3, :] * buf[hl:hl + tr, :]
        zc = zc_ref[...]
        yc_ref[...] = ((bg_ref[...] * y) * (zc * _sigmoid(zc))).astype(BF16)

    return pl.pallas_call(
        body, name="conv_fwd", grid=(s // tr,),
        in_specs=[_row_spec(tr, cc, cu), _halo_before(tr, hl, cc, cu), _row_spec(tr, cc, cb),
                  _row_spec(tr, cc, cg_), _halo_before(tr, hl, cc, cg_), _row_spec(tr, cc, cz),
                  _vec_spec(cc, rows=3)],
        out_specs=_row_spec(tr, cc), out_shape=jax.ShapeDtypeStruct((s, cc), BF16),
        scratch_shapes=[pltpu.VMEM((tr + hl, cc), F32)],
        compiler_params=_params(("parallel",)),
    )(proj, proj, proj, proj, proj, proj, conv_w)


def _conv_bwd(proj, dyc, conv_w, d, tr, job=None):
    s = proj.shape[0]
    cc, hl = d // 2, CONV_HALO
    cu, cb, cg_, cz = 10, 11, 12, 13
    n_tiles = s // tr

    def body(*refs):
        own_in, (dc_ref, gw_ref), jargs = _job_refs(job, refs[:-2], 11, 2)
        u_ref, uh_ref, bg_ref, bh_ref, cg_ref, ch_ref, zc_ref, zh_ref, dy_ref, dyh_ref, w_ref = own_in
        buf, dbuf = refs[-2:]
        i = pl.program_id(0)

        @pl.when(i == 0)
        def _():
            gw_ref[...] = jnp.zeros_like(gw_ref)
            if job is not None:
                job.start(*jargs)

        u, bg, cg, zc, dyc_t = u_ref[...], bg_ref[...], cg_ref[...], zc_ref[...], dy_ref[...]
        buf[0:hl, :] = jnp.where(i > 0, ch_ref[...] * uh_ref[...], 0.0)
        buf[hl:hl + tr, :] = cg * u
        v2, v1, v0 = buf[hl - 2:hl - 2 + tr, :], buf[hl - 1:hl - 1 + tr, :], buf[hl:hl + tr, :]
        w0, w1, w2 = w_ref[0:1, :], w_ref[1:2, :], w_ref[2:3, :]
        y = w0 * v2 + w1 * v1 + w2 * v0
        sg = _sigmoid(zc)
        gate = zc * sg
        dc_ref[:, cc:2 * cc] = (dyc_t * y * gate).astype(BF16)
        dc_ref[:, 3 * cc:4 * cc] = (dyc_t * bg * y * _dsilu(zc, sg)).astype(BF16)
        dy = dyc_t * bg * gate
        zh = zh_ref[...]
        dy_h = jnp.where(i < n_tiles - 1, dyh_ref[...], 0.0) * bh_ref[...] * (zh * _sigmoid(zh))
        gw_ref[0:1, :] += jnp.sum(dy * v2, axis=0, keepdims=True)
        gw_ref[1:2, :] += jnp.sum(dy * v1, axis=0, keepdims=True)
        gw_ref[2:3, :] += jnp.sum(dy * v0, axis=0, keepdims=True)
        dbuf[0:tr, :] = dy
        dbuf[tr:tr + hl, :] = dy_h
        dv = w2 * dy + w1 * dbuf[1:1 + tr, :] + w0 * dbuf[2:2 + tr, :]
        dc_ref[:, 0:cc] = (dv * cg).astype(BF16)
        dc_ref[:, 2 * cc:3 * cc] = (dv * u).astype(BF16)

        if job is not None:
            @pl.when(i == n_tiles - 1)
            def _():
                job.finish(*jargs)

    hosted = _host(job, 11, 2)
    res = pl.pallas_call(
        body, name="conv_bwd", grid=(n_tiles,),
        in_specs=[_row_spec(tr, cc, cu), _halo_before(tr, hl, cc, cu),
                  _row_spec(tr, cc, cb), _halo_after(tr, hl, cc, cb, n_tiles),
                  _row_spec(tr, cc, cg_), _halo_before(tr, hl, cc, cg_),
                  _row_spec(tr, cc, cz), _halo_after(tr, hl, cc, cz, n_tiles),
                  _row_spec(tr, cc, 0), _halo_after(tr, hl, cc, 0, n_tiles),
                  _vec_spec(cc, rows=3)] + hosted["in_specs"],
        out_specs=[_row_spec(tr, 4 * cc), _vec_spec(cc, rows=3)] + hosted["out_specs"],
        out_shape=[jax.ShapeDtypeStruct((s, 4 * cc), BF16), jax.ShapeDtypeStruct((3, cc), F32)]
        + hosted["out_shape"],
        scratch_shapes=hosted["scratch"] + [pltpu.VMEM((tr + hl, cc), F32), pltpu.VMEM((tr + hl, cc), F32)],
        input_output_aliases=hosted["aliases"],
        compiler_params=_params(("arbitrary",)),
    )(proj, proj, proj, proj, proj, proj, proj, proj, dyc, dyc, conv_w, *hosted["ins"])
    return res[0], res[1], list(res[2:])


def _merge_bwd(dm, gates, pa, pb, pc, tr):
    s, d = pa.shape

    def body(dm_ref, g_ref, a_ref, b_ref, c_ref, da_ref, db_ref, dc_ref, dg_ref, gb_ref):
        @pl.when(pl.program_id(0) == 0)
        def _():
            gb_ref[...] = jnp.zeros_like(gb_ref)

        dmv = dm_ref[...]
        for n, (p_ref, o_ref) in enumerate(((a_ref, da_ref), (b_ref, db_ref), (c_ref, dc_ref))):
            gv = g_ref[:, n * d:(n + 1) * d]
            o_ref[...] = (dmv * gv).astype(BF16)
            dlogit = (dmv * p_ref[...]) * (gv * (1.0 - gv))
            dg_ref[:, n * d:(n + 1) * d] = dlogit.astype(BF16)
            gb_ref[:, n * d:(n + 1) * d] += jnp.sum(dlogit, axis=0, keepdims=True)

    act = jax.ShapeDtypeStruct((s, d), BF16)
    return pl.pallas_call(
        body, name="merge_bwd", grid=(s // tr,),
        in_specs=[_row_spec(tr, d), _row_spec(tr, 3 * d), _row_spec(tr, d), _row_spec(tr, d), _row_spec(tr, d)],
        out_specs=[_row_spec(tr, d), _row_spec(tr, d), _row_spec(tr, d), _row_spec(tr, 3 * d), _vec_spec(3 * d)],
        out_shape=[act, act, act, jax.ShapeDtypeStruct((s, 3 * d), BF16), jax.ShapeDtypeStruct((1, 3 * d), F32)],
        compiler_params=_params(("arbitrary",)),
    )(dm, gates, pa, pb, pc)


def _resid_bwd(dxo, out, rg, tr):
    s, d = dxo.shape

    def body(dx_ref, o_ref, rg_ref, do_ref, drg_ref):
        @pl.when(pl.program_id(0) == 0)
        def _():
            drg_ref[...] = jnp.zeros_like(drg_ref)

        dxv = dx_ref[...]
        do_ref[...] = (dxv * rg_ref[...]).astype(BF16)
        drg_ref[...] += jnp.sum(dxv * o_ref[...], axis=0, keepdims=True)

    return pl.pallas_call(
        body, name="resid_bwd", grid=(s // tr,),
        in_specs=[_row_spec(tr, d), _row_spec(tr, d), _vec_spec(d)],
        out_specs=[_row_spec(tr, d), _vec_spec(d)],
        out_shape=[jax.ShapeDtypeStruct((s, d), BF16), jax.ShapeDtypeStruct((1, d), F32)],
        compiler_params=_params(("arbitrary",)),
    )(dxo, out, rg)


def _log_keep(z):
    e = jnp.exp(-jnp.abs(z))
    return -(jnp.maximum(z, 0.0) + jnp.log(1.0 + e)), e


class _Job:
    def __init__(self, ins, out_shapes, aliases, n_sems, start, finish, mid=None):
        self.ins, self.out_shapes, self.aliases, self.n_sems = list(ins), list(out_shapes), list(aliases), n_sems
        self.start, self.mid, self.finish = start, mid, finish


def _host(job, n_in, n_out):
    if job is None:
        return dict(ins=[], in_specs=[], out_specs=[], out_shape=[], scratch=[], aliases={})
    return dict(ins=job.ins, in_specs=[_ANY] * len(job.ins), out_specs=[_ANY] * len(job.out_shapes),
                out_shape=job.out_shapes,
                scratch=[pltpu.SemaphoreType.DMA((job.n_sems,)), pltpu.SemaphoreType.DMA((job.n_sems,))],
                aliases={n_in + a: n_out + b for a, b in job.aliases})


def _job_refs(job, refs, n_in, n_out):
    if job is None:
        return refs[:n_in], refs[n_in:n_in + n_out], None
    ji, jo = len(job.ins), len(job.out_shapes)
    own_in, job_in = refs[:n_in], refs[n_in:n_in + ji]
    own_out, job_out = refs[n_in + ji:n_in + ji + n_out], refs[n_in + ji + n_out:n_in + ji + n_out + jo]
    send_sems, recv_sems = refs[n_in + ji + n_out + jo:]
    return own_in, own_out, (job_in, job_out, send_sems, recv_sems)


def _sb_fwd(proj, projb, d, blk, job=None):
    s = proj.shape[0]
    hps = min(SB_FWD_HEADS, d // HEAD_DIM)
    wid = hps * HEAD_DIM
    groups, nq = d // wid, s // blk
    qc, kc, vc, zc = d // wid, 2 * d // wid, 3 * d // wid, 4 * d // wid
    assert nq <= HEAD_DIM
    mid_step = (groups - 1, 2 * nq // 3)

    def body(*refs):
        (q_ref, k_ref, v_ref, zb_ref), (att_ref, yb_ref, rs_ref), jargs = _job_refs(job, refs, 4, 3)
        grp, i = pl.program_id(0), pl.program_id(1)
        if job is not None:
            @pl.when((grp == 0) & (i == 0))
            def _():
                job.start(*jargs)

            if job.mid is not None:
                @pl.when((grp == mid_step[0]) & (i == mid_step[1]))
                def _():
                    job.mid(*jargs)

        r_io = lax.broadcasted_iota(jnp.int32, (blk, blk), 0)
        c_io = lax.broadcasted_iota(jnp.int32, (blk, blk), 1)
        tri = (r_io >= c_io).astype(BF16)
        strict = c_io < r_io
        lane = lax.broadcasted_iota(jnp.int32, (blk, HEAD_DIM), 1)
        heads = [slice(n * HEAD_DIM, (n + 1) * HEAD_DIM) for n in range(hps)]
        qs = [q_ref[:, hd] for hd in heads]

        def block(j, carry, masked):
            ks = pl.multiple_of(j * blk, blk)
            rng = range(hps)
            zs = [lax.dot_general(qs[n], k_ref[pl.ds(ks, blk), heads[n]], (NT, ((), ())),
                                  preferred_element_type=F32) for n in rng]
            lks = []
            for n in rng:
                lk, _ = _log_keep(zs[n])
                if masked:
                    lk = jnp.where(strict, lk, 0.0)
                lks.append(lk.astype(BF16))
            csums = [carry[n][0] + jnp.dot(lks[n], tri, preferred_element_type=F32) for n in rng]
            probs = []
            for n in rng:
                a = jnp.exp(zs[n] + csums[n])
                if masked:
                    a = jnp.where(strict, a, 0.0)
                probs.append(a.astype(BF16))
            out = []
            for n in rng:
                run, acc, rs = carry[n]
                acc = acc + jnp.dot(probs[n], v_ref[pl.ds(ks, blk), heads[n]], preferred_element_type=F32)
                out.append((csums[n][:, 0:1], acc, jnp.where(lane == j, run, rs)))
            return tuple(out)

        zero = (jnp.zeros((blk, 1), F32), jnp.zeros((blk, HEAD_DIM), F32), jnp.zeros((blk, HEAD_DIM), F32))
        carry = block(i, (zero,) * hps, True)
        carry = lax.fori_loop(0, i, lambda jj, cr: block(i - 1 - jj, cr, False), carry)
        for hd, (_, acc, rs) in zip(heads, carry):
            att_ref[:, hd] = acc
            rs_ref[:, hd] = rs
            zb = zb_ref[:, hd]
            yb_ref[:, hd] = (acc * (zb * _sigmoid(zb))).astype(BF16)

        if job is not None:
            @pl.when((grp == groups - 1) & (i == nq - 1))
            def _():
                job.finish(*jargs)

    blk_spec = pl.BlockSpec((blk, wid), lambda h, i: (i, h))
    hosted = _host(job, 4, 3)
    res = pl.pallas_call(
        body, name="sb_fwd", grid=(groups, nq),
        in_specs=[pl.BlockSpec((blk, wid), lambda h, i: (i, qc + h)),
                  pl.BlockSpec((s, wid), lambda h, i: (0, kc + h)),
                  pl.BlockSpec((s, wid), lambda h, i: (0, vc + h)),
                  pl.BlockSpec((blk, wid), lambda h, i: (i, zc + h))] + hosted["in_specs"],
        out_specs=[blk_spec, blk_spec, blk_spec] + hosted["out_specs"],
        out_shape=[jax.ShapeDtypeStruct((s, d), F32), jax.ShapeDtypeStruct((s, d), BF16),
                   jax.ShapeDtypeStruct((s, d), F32)] + hosted["out_shape"],
        scratch_shapes=hosted["scratch"], input_output_aliases=hosted["aliases"],
        compiler_params=_params(("arbitrary", "arbitrary")),
    )(projb, projb, projb, proj, *hosted["ins"])
    return res[0], res[1], res[2], list(res[3:])


def _sb_bwd(proj, projb, att, dyb, rsave, d, blk, job=None):
    s = proj.shape[0]
    hps, wid = SB_BWD_HEADS, SB_BWD_HEADS * HEAD_DIM
    groups, nq = d // wid, s // blk
    qc, kc, vc, zc = d // wid, 2 * d // wid, 3 * d // wid, 4 * d // wid
    scale = HEAD_DIM ** -0.5

    def body(*refs):
        ((q_ref, k_ref, v_ref, zb_ref, att_ref, dyb_ref, rs_ref), (dq_ref, dk_ref, dv_ref, dzb_ref),
         jargs) = _job_refs(job, refs, 7, 4)
        grp, i = pl.program_id(0), pl.program_id(1)
        if job is not None:
            @pl.when((grp == 0) & (i == 0))
            def _():
                job.start(*jargs)

        @pl.when(i == 0)
        def _():
            dk_ref[...] = jnp.zeros_like(dk_ref)
            dv_ref[...] = jnp.zeros_like(dv_ref)

        r_io = lax.broadcasted_iota(jnp.int32, (blk, blk), 0)
        c_io = lax.broadcasted_iota(jnp.int32, (blk, blk), 1)
        tri = (r_io >= c_io).astype(BF16)
        tri_up = (r_io <= c_io).astype(BF16)
        strict = c_io < r_io
        lane = lax.broadcasted_iota(jnp.int32, (blk, HEAD_DIM), 1)
        heads = [slice(n * HEAD_DIM, (n + 1) * HEAD_DIM) for n in range(hps)]
        qs, dos, rss = [], [], []
        for hd in heads:
            qs.append(q_ref[:, hd])
            zb, dyb_t = zb_ref[:, hd], dyb_ref[:, hd]
            sg = _sigmoid(zb)
            dzb_ref[:, hd] = dyb_t * att_ref[:, hd] * _dsilu(zb, sg)
            dos.append((dyb_t * (zb * sg)).astype(BF16))
            rss.append(rs_ref[:, hd])

        def block(j, carry, masked):
            ks = pl.multiple_of(j * blk, blk)
            rng = range(hps)
            kbs = [k_ref[pl.ds(ks, blk), heads[n]] for n in rng]
            vbs = [v_ref[pl.ds(ks, blk), heads[n]] for n in rng]
            zs = [lax.dot_general(qs[n], kbs[n], (NT, ((), ())), preferred_element_type=F32) for n in rng]
            das = [lax.dot_general(dos[n], vbs[n], (NT, ((), ())), preferred_element_type=F32) for n in rng]
            lks, betas = [], []
            for n in rng:
                lk, _ = _log_keep(zs[n])
                betas.append(1.0 - jnp.exp(lk))
                if masked:
                    lk = jnp.where(strict, lk, 0.0)
                lks.append(lk.astype(BF16))
            csums = []
            for n in rng:
                run = jnp.sum(jnp.where(lane == j, rss[n], 0.0), axis=1, keepdims=True)
                csums.append(run + jnp.dot(lks[n], tri, preferred_element_type=F32))
            probs, gs = [], []
            for n in rng:
                a = jnp.exp(zs[n] + csums[n])
                if masked:
                    a = jnp.where(strict, a, 0.0)
                probs.append(a.astype(BF16))
                gs.append(a * das[n])
            gcums = [carry[n][0] + jnp.dot(gs[n].astype(BF16), tri_up, preferred_element_type=F32) for n in rng]
            for n in rng:
                dv_ref[pl.ds(ks, blk), heads[n]] += lax.dot_general(probs[n], dos[n], (TN, ((), ())),
                                                                   preferred_element_type=F32)
            dzs = []
            for n in rng:
                dz = gs[n] - betas[n] * gcums[n]
                if masked:
                    dz = jnp.where(strict, dz, 0.0)
                dzs.append(dz.astype(BF16))
            out = []
            for n in rng:
                dq = carry[n][1] + jnp.dot(dzs[n], kbs[n], preferred_element_type=F32)
                dk_ref[pl.ds(ks, blk), heads[n]] += lax.dot_general(dzs[n], qs[n], (TN, ((), ())),
                                                                   preferred_element_type=F32)
                out.append((gcums[n][:, blk - 1:blk], dq))
            return tuple(out)

        zero = (jnp.zeros((blk, 1), F32), jnp.zeros((blk, HEAD_DIM), F32))
        carry = lax.fori_loop(0, i, lambda j, cr: block(j, cr, False), (zero,) * hps)
        carry = block(i, carry, True)
        for hd, (_, dq) in zip(heads, carry):
            dq_ref[:, hd] = dq * scale

        if job is not None:
            @pl.when((grp == groups - 1) & (i == nq - 1))
            def _():
                job.finish(*jargs)

    blk_spec = pl.BlockSpec((blk, wid), lambda h, i: (i, h))
    full_spec = pl.BlockSpec((s, wid), lambda h, i: (0, h), pipeline_mode=pl.Buffered(1))
    act = jax.ShapeDtypeStruct((s, d), F32)
    hosted = _host(job, 7, 4)
    res = pl.pallas_call(
        body, name="sb_bwd", grid=(groups, nq),
        in_specs=[pl.BlockSpec((blk, wid), lambda h, i: (i, qc + h)),
                  pl.BlockSpec((s, wid), lambda h, i: (0, kc + h)),
                  pl.BlockSpec((s, wid), lambda h, i: (0, vc + h)),
                  pl.BlockSpec((blk, wid), lambda h, i: (i, zc + h)),
                  blk_spec, blk_spec, blk_spec] + hosted["in_specs"],
        out_specs=[blk_spec, full_spec, full_spec, blk_spec] + hosted["out_specs"],
        out_shape=[act, act, act, act] + hosted["out_shape"],
        scratch_shapes=hosted["scratch"], input_output_aliases=hosted["aliases"],
        compiler_params=_params(("arbitrary", "arbitrary")),
    )(projb, projb, projb, proj, att, dyb, rsave, *hosted["ins"])
    return res[0], res[1], res[2], res[3], list(res[4:])


def _adamw(w, g, m, v):
    shape = w.shape
    cols = shape[-1]
    rows = w.size // cols
    w2, g2, m2, v2 = (t.reshape(rows, cols) for t in (w, g, m, v))
    tr, tc = _tile(rows, 512, 8), _tile(cols, 1024, 128)
    c1 = 1.0 - ADAM_B1 ** ADAM_STEP
    c2 = 1.0 - ADAM_B2 ** ADAM_STEP

    def body(w_ref, g_ref, m_ref, v_ref, d_ref, nm_ref, nv_ref):
        gv = g_ref[...]
        nm = ADAM_B1 * m_ref[...] + (1.0 - ADAM_B1) * gv
        nv = ADAM_B2 * v_ref[...] + (1.0 - ADAM_B2) * (gv * gv)
        d_ref[...] = -ADAM_LR * ((nm / c1) / (jnp.sqrt(nv / c2) + ADAM_EPS) + ADAM_WD * w_ref[...])
        nm_ref[...] = nm
        nv_ref[...] = nv

    spec = pl.BlockSpec((tr, tc), lambda i, j: (i, j))
    sd = jax.ShapeDtypeStruct((rows, cols), F32)
    outs = pl.pallas_call(
        body, name="adamw", grid=(rows // tr, cols // tc),
        in_specs=[spec] * 4, out_specs=[spec] * 3, out_shape=[sd] * 3,
        compiler_params=_params(("parallel", "parallel")),
    )(w2, g2, m2, v2)
    return tuple(o.reshape(shape) for o in outs)


def _sum_blocks(gathered, n, rows):
    width = gathered.shape[1]
    tw = _tile(width, 8192)

    def body(g_ref, o_ref):
        acc = g_ref[0:rows, :]
        for b in range(1, n):
            acc = acc + g_ref[b * rows:(b + 1) * rows, :]
        o_ref[...] = acc

    return pl.pallas_call(
        body, name="sum_blocks", grid=(width // tw,),
        in_specs=[pl.BlockSpec((n * rows, tw), lambda i: (0, i))],
        out_specs=pl.BlockSpec((rows, tw), lambda i: (0, i)),
        out_shape=jax.ShapeDtypeStruct((rows, width), F32),
        compiler_params=_params(("parallel",)),
    )(gathered)


def _silu_bf16(c_rows):
    def body(c_ref, o_ref):
        cv = c_ref[...]
        o_ref[...] = (cv * _sigmoid(cv)).astype(BF16)

    return pl.pallas_call(
        body, name="silu_c", out_shape=jax.ShapeDtypeStruct(c_rows.shape, BF16),
        in_specs=[pl.BlockSpec(memory_space=pltpu.VMEM)], out_specs=pl.BlockSpec(memory_space=pltpu.VMEM),
    )(c_rows)


def _place():
    x, y, c = lax.axis_index("x"), lax.axis_index("y"), lax.axis_index("c")
    chips = [(1 - x, y), (x, 1 - y), (1 - x, 1 - y)]
    return x, y, c, chips


def _all_gather_small(block, name):
    m_per, n = block.shape

    def body(x_ref, out_ref, send_sems, recv_sems, local_sem):
        x, y, c, chips = _place()
        me, sibling = (x, y, c), (x, y, 1 - c)

        def rows(px, py, pc):
            return out_ref.at[pl.ds((4 * px + 2 * py + pc) * m_per, m_per), :]

        def copy(k, blk, to, src=None):
            return pltpu.make_async_remote_copy(
                src_ref=rows(*blk) if src is None else src, dst_ref=rows(*blk),
                send_sem=send_sems.at[k], recv_sem=recv_sems.at[k], device_id=to, device_id_type=MESH)

        mine = pltpu.make_async_copy(x_ref, rows(*me), local_sem)
        mine.start()
        first = [copy(0, me, sibling, src=x_ref)]
        first += [copy(1 + j, me, (*chip, c), src=x_ref) for j, chip in enumerate(chips)]
        for cp in first:
            cp.start()
        passed = [copy(4 + j, (*chip, c), sibling) for j, chip in enumerate(chips)]
        for j, chip in enumerate(chips):
            copy(1 + j, (*chip, c), me).wait_recv()
            passed[j].start()
        copy(0, sibling, me).wait_recv()
        for j, chip in enumerate(chips):
            copy(4 + j, (*chip, 1 - c), me).wait_recv()
        for cp in first + passed:
            cp.wait_send()
        mine.wait()

    return pl.pallas_call(
        body, name=name, out_shape=jax.ShapeDtypeStruct((N_DEV * m_per, n), block.dtype),
        in_specs=[pl.BlockSpec(memory_space=pltpu.VMEM)], out_specs=pl.BlockSpec(memory_space=pltpu.VMEM),
        scratch_shapes=[pltpu.SemaphoreType.DMA((7,)), pltpu.SemaphoreType.DMA((7,)), pltpu.SemaphoreType.DMA],
        compiler_params=pltpu.CompilerParams(vmem_limit_bytes=V7X_VMEM_LIMIT),
    )(block)


_ANY = pl.BlockSpec(memory_space=pl.ANY)


def _my_chip():
    return 2 * lax.axis_index("x") + lax.axis_index("y")


def _place_shard(w):
    layers, r, cols = w.shape
    tr, tc = _tile(r, 512, 16), _tile(cols, 1024)

    def body(w_ref, o_ref):
        o_ref[...] = w_ref[...].astype(BF16)

    return pl.pallas_call(
        body, name="place_shard", grid=(layers, r // tr, cols // tc),
        in_specs=[pl.BlockSpec((None, tr, tc), lambda l, i, n: (l, i, n))],
        out_specs=pl.BlockSpec((None, None, tr, tc), lambda l, i, n: (_my_chip(), l, i, n)),
        out_shape=jax.ShapeDtypeStruct((N_CHIPS, layers, r, cols), BF16),
        compiler_params=_params(("parallel", "parallel", "parallel")),
    )(w)


def _gather_job(bufs, layer):
    n = len(bufs)
    layer_of = [layer] * n if isinstance(layer, int) else list(layer)

    def tools(outs, send_sems, recv_sems):
        x, y, c, chips = _place()

        def half(t, chip_idx, hc):
            h = outs[t].shape[2] // 2
            return outs[t].at[chip_idx, layer_of[t], pl.ds(hc * h, h), :]

        def copy(t, k, ref, to):
            return pltpu.make_async_remote_copy(src_ref=ref, dst_ref=ref, send_sem=send_sems.at[6 * t + k],
                                                recv_sem=recv_sems.at[6 * t + k], device_id=to, device_id_type=MESH)

        return x, y, c, chips, half, copy

    def start(ins, outs, send_sems, recv_sems):
        x, y, c, chips, half, copy = tools(outs, send_sems, recv_sems)
        for t in range(n):
            for k, chip in enumerate(chips):
                copy(t, k, half(t, 2 * x + y, c), (*chip, c)).start()

    def mid(ins, outs, send_sems, recv_sems):
        x, y, c, chips, half, copy = tools(outs, send_sems, recv_sems)
        for t in range(n):
            for k, (cx, cy) in enumerate(chips):
                landed = half(t, 2 * cx + cy, c)
                copy(t, k, landed, (cx, cy, c)).wait_recv()
                copy(t, 3 + k, landed, (x, y, 1 - c)).start()

    def finish(ins, outs, send_sems, recv_sems):
        x, y, c, chips, half, copy = tools(outs, send_sems, recv_sems)
        for t in range(n):
            for k, (cx, cy) in enumerate(chips):
                copy(t, 3 + k, half(t, 2 * cx + cy, 1 - c), (x, y, 1 - c)).wait_recv()
        for t in range(n):
            for k, (cx, cy) in enumerate(chips):
                copy(t, k, half(t, 2 * x + y, c), (cx, cy, c)).wait_send()
                copy(t, 3 + k, half(t, 2 * cx + cy, c), (x, y, 1 - c)).wait_send()

    return _Job(ins=bufs, out_shapes=[jax.ShapeDtypeStruct(b.shape, b.dtype) for b in bufs],
                aliases=[(t, t) for t in range(n)], n_sems=6 * n, start=start, mid=mid, finish=finish)


def _send_job(parts):
    n = len(parts)

    def copies(ins, outs, send_sems, recv_sems):
        x, y, c, chips = _place()
        return [pltpu.make_async_remote_copy(
            src_ref=ins[t].at[2 * cx + cy], dst_ref=outs[t].at[k], send_sem=send_sems.at[3 * t + k],
            recv_sem=recv_sems.at[3 * t + k], device_id=(cx, cy, c), device_id_type=MESH)
            for t in range(n) for k, (cx, cy) in enumerate(chips)]

    def start(*args):
        for cp in copies(*args):
            cp.start()

    def finish(*args):
        for cp in copies(*args):
            cp.wait()

    return _Job(ins=parts, out_shapes=[jax.ShapeDtypeStruct((3,) + p.shape[1:], p.dtype) for p in parts],
                aliases=[], n_sems=3 * n, start=start, finish=finish)


def _run_job(job, name):
    def body(*refs):
        _, _, jargs = _job_refs(job, refs, 0, 0)
        job.start(*jargs)
        if job.mid is not None:
            job.mid(*jargs)
        job.finish(*jargs)

    hosted = _host(job, 0, 0)
    return list(pl.pallas_call(
        body, name=name, out_shape=hosted["out_shape"], in_specs=hosted["in_specs"], out_specs=hosted["out_specs"],
        scratch_shapes=hosted["scratch"], input_output_aliases=hosted["aliases"],
    )(*hosted["ins"]))


def _swap_job(grads):
    n = len(grads)

    def copies(ins, outs, send_sems, recv_sems):
        x, y, c, _ = _place()
        return [pltpu.make_async_remote_copy(
            src_ref=ins[t].at[:, pl.ds((1 - c) * (ins[t].shape[1] // 2), ins[t].shape[1] // 2), :], dst_ref=outs[t],
            send_sem=send_sems.at[t], recv_sem=recv_sems.at[t], device_id=(x, y, 1 - c), device_id_type=MESH)
            for t in range(n)]

    def start(*args):
        for cp in copies(*args):
            cp.start()

    def finish(*args):
        for cp in copies(*args):
            cp.wait()

    return _Job(ins=grads, aliases=[], n_sems=n, start=start, finish=finish,
                out_shapes=[jax.ShapeDtypeStruct((g.shape[0], g.shape[1] // 2, g.shape[2]), g.dtype) for g in grads])


def _swap_halves(grads):
    return _run_job(_swap_job(grads), "swap_halves")


def _share_halves(bufs):
    n = len(bufs)

    def body(*refs):
        outs = refs[n:2 * n]
        send_sems, recv_sems = refs[2 * n:]
        x, y, c, _ = _place()

        def half(t, hc):
            h = outs[t].shape[1] // 2
            return outs[t].at[:, pl.ds(hc * h, h), :]

        def copy(t, ref):
            return pltpu.make_async_remote_copy(src_ref=ref, dst_ref=ref, send_sem=send_sems.at[t],
                                                recv_sem=recv_sems.at[t], device_id=(x, y, 1 - c),
                                                device_id_type=MESH)

        sends = [copy(t, half(t, c)) for t in range(n)]
        for cp in sends:
            cp.start()
        for t in range(n):
            copy(t, half(t, 1 - c)).wait_recv()
        for cp in sends:
            cp.wait_send()

    return pl.pallas_call(
        body, name="share_halves",
        out_shape=[jax.ShapeDtypeStruct(b.shape, b.dtype) for b in bufs],
        in_specs=[_ANY] * n, out_specs=[_ANY] * n, input_output_aliases={t: t for t in range(n)},
        scratch_shapes=[pltpu.SemaphoreType.DMA((n,)), pltpu.SemaphoreType.DMA((n,))],
    )(*bufs)


def _sum_sibling(grad, recv):
    _, r, cols = grad.shape
    h = r // 2
    tr, tc = _tile(h, 512, 16), _tile(cols, 1024)
    per = h // tr

    def body(g_ref, r_ref, o_ref):
        o_ref[...] = (g_ref[...].astype(F32) + r_ref[...].astype(F32)).astype(BF16)

    return pl.pallas_call(
        body, name="sum_sibling", grid=(N_CHIPS, per, cols // tc),
        in_specs=[pl.BlockSpec((None, tr, tc), lambda j, i, n: (j, lax.axis_index("c") * per + i, n)),
                  pl.BlockSpec((None, tr, tc), lambda j, i, n: (j, i, n))],
        out_specs=pl.BlockSpec((None, tr, tc), lambda j, i, n: (j, i, n)),
        out_shape=jax.ShapeDtypeStruct((N_CHIPS, h, cols), BF16),
        compiler_params=_params(("parallel", "parallel", "parallel")),
    )(grad, recv)


def _sum_owner(parts, recv, layer, layers, prev=None):
    _, h, cols = parts.shape
    tr, tc = _tile(h, 512, 16), _tile(cols, 1024)
    per = h // tr

    def body(p_ref, r_ref, *rest):
        o_ref = rest[-1]
        o_ref[...] = (p_ref[...].astype(F32) + r_ref[0].astype(F32) + r_ref[1].astype(F32)
                      + r_ref[2].astype(F32))

    in_specs = [pl.BlockSpec((None, tr, tc), lambda i, n: (_my_chip(), i, n)),
                pl.BlockSpec((3, tr, tc), lambda i, n: (0, i, n))]
    args = [parts, recv]
    aliases = {}
    if prev is not None:
        in_specs.append(_ANY)
        args.append(prev)
        aliases = {2: 0}
    return pl.pallas_call(
        body, name="sum_owner", grid=(per, cols // tc), in_specs=in_specs,
        out_specs=pl.BlockSpec((None, tr, tc), lambda i, n: (layer, lax.axis_index("c") * per + i, n)),
        out_shape=jax.ShapeDtypeStruct((layers, 2 * h, cols), F32),
        input_output_aliases=aliases,
        compiler_params=_params(("parallel", "parallel")),
    )(*args)


def _proj_fetching(h, w_buf, layer, tm, tn, epilogue):
    s, d = h.shape
    cs = w_buf.shape[3]
    per, gi = cs // tn, s // tm
    n_cols = N_CHIPS * cs

    def slot_chip(m):
        return 2 * ((lax.axis_index("x") + m % 2) % 2) + (lax.axis_index("y") + m // 2) % 2

    def body(h_ref, buf_in, proj_ref, projb_ref, buf, w_vmem, send_sems, recv_sems, w_sems):
        m, i, j = pl.program_id(0), pl.program_id(1), pl.program_id(2)
        x, y, c, chips = _place()
        sibling = (x, y, 1 - c)
        step = (m * gi + i) * per + j
        slot = step % 2

        def half(chip_idx, hc):
            return buf.at[chip_idx, layer, pl.ds(hc * (d // 2), d // 2), :]

        def copy(k, ref, to):
            return pltpu.make_async_remote_copy(src_ref=ref, dst_ref=ref, send_sem=send_sems.at[k],
                                                recv_sem=recv_sems.at[k], device_id=to, device_id_type=MESH)

        def fetch(mm, jj, into):
            cols = pl.ds(pl.multiple_of(jj * tn, tn), tn)
            return pltpu.make_async_copy(buf.at[slot_chip(mm), layer, :, cols], w_vmem.at[into], w_sems.at[into])

        @pl.when(step == 0)
        def _():
            for k, chip in enumerate(chips):
                copy(k, half(2 * x + y, c), (*chip, c)).start()
            fetch(0, 0, 0).start()

        row_end = j == per - 1
        slot_end = row_end & (i == gi - 1)
        for k, (cx, cy) in enumerate(chips):
            @pl.when(slot_end & (m == k))
            def _():
                landed = half(2 * cx + cy, c)
                copy(k, landed, (cx, cy, c)).wait_recv()
                copy(3 + k, landed, sibling).start()
                copy(3 + k, half(2 * cx + cy, 1 - c), sibling).wait_recv()

        @pl.when(jnp.logical_not(slot_end & (m == N_CHIPS - 1)))
        def _():
            fetch(jnp.where(slot_end, m + 1, m), jnp.where(row_end, 0, j + 1), 1 - slot).start()

        fetch(m, j, slot).wait()
        acc = jnp.dot(h_ref[...], w_vmem[slot], preferred_element_type=F32)
        vals = epilogue(slot_chip(m) * per + j, acc)
        proj_ref[...] = vals[0]
        projb_ref[...] = vals[1].astype(BF16)

        @pl.when(slot_end & (m == N_CHIPS - 1))
        def _():
            for k, (cx, cy) in enumerate(chips):
                copy(k, half(2 * x + y, c), (cx, cy, c)).wait_send()
                copy(3 + k, half(2 * cx + cy, c), sibling).wait_send()

    out_blk = pl.BlockSpec((tm, tn), lambda m, i, j: (i, slot_chip(m) * per + j))
    proj, projb, filled = pl.pallas_call(
        body, name="proj_fetching", grid=(N_CHIPS, gi, per),
        in_specs=[pl.BlockSpec((tm, d), lambda m, i, j: (i, 0)), _ANY],
        out_specs=[out_blk, out_blk, _ANY],
        out_shape=[jax.ShapeDtypeStruct((s, n_cols), F32), jax.ShapeDtypeStruct((s, n_cols), BF16),
                   jax.ShapeDtypeStruct(w_buf.shape, w_buf.dtype)],
        scratch_shapes=[pltpu.VMEM((2, d, tn), BF16), pltpu.SemaphoreType.DMA((6,)), pltpu.SemaphoreType.DMA((6,)),
                        pltpu.SemaphoreType.DMA((2,))],
        input_output_aliases={1: 2},
        compiler_params=_params(("arbitrary", "arbitrary", "arbitrary")),
    )(h, w_buf)
    return proj, projb, filled


def _layer_fwd(x, mod, wbuf, p, layer, cfg, make_jobs):
    d, tr, tm = cfg["d"], cfg["tr"], cfg["tm"]
    s = x.shape[0]
    weight = lambda n: _Sharded(wbuf[n], BIG_KIND[n], layer=layer)
    hosted = lambda key: make_jobs[key](wbuf) if key in make_jobs else (None, ())
    shift, scale, rg = mod[:, 0:d], mod[:, d:2 * d], mod[:, 2 * d:3 * d]
    h, h_t = _norm_fwd(x, p["norm_g"], scale, shift, tr)
    tn_in = cfg["tn_in"]
    q_lo, q_hi, q_scale = d // tn_in, 2 * d // tn_in, HEAD_DIM ** -0.5
    blk_in = pl.BlockSpec((tm, tn_in), lambda i, j, k: (i, j))

    def proj_out(col, acc):
        return acc, acc * jnp.where((col >= q_lo) & (col < q_hi), q_scale, 1.0)

    if make_jobs.get("fetch_w_in"):
        proj, projb, wbuf["w_in"] = _proj_fetching(h, wbuf["w_in"], layer, tm, tn_in, proj_out)
    else:
        job, names = hosted("proj")
        proj, projb, *job_out = _mm_nn("proj", h, weight("w_in"), tm=tm, tn=tn_in, tk=d, with_col=True,
                                       epilogue=proj_out, job=job,
                                       outs=[(jax.ShapeDtypeStruct((s, 7 * d), F32), blk_in),
                                             (jax.ShapeDtypeStruct((s, 7 * d), BF16), blk_in)])
        wbuf.update(zip(names, job_out))
    ya = _pool_fwd(proj, p["pool_w"], p["pool_scale"], d, tr)
    yc = _conv_fwd(proj, p["conv_w"], d, tr)
    job, names = hosted("attn")
    att, yb, rsave, job_out = _sb_fwd(proj, projb, d, cfg["blk"], job)
    wbuf.update(zip(names, job_out))
    gates, = _mm_nn("gates", h, weight("w_gate"), tm=tm, tn=cfg["tn_gate"], tk=d,
                    extras=[(p["b_gate"], pl.BlockSpec((1, cfg["tn_gate"]), lambda i, j, k: (0, j)))],
                    epilogue=lambda acc, b: (_sigmoid(acc + b),))
    tn = cfg["tn_d"]
    blk = pl.BlockSpec((tm, tn), lambda i, j, k: (i, j))
    sd = jax.ShapeDtypeStruct((s, d), F32)
    pa, = _mm_nn("branch_a", ya, weight("w_br_a"), tm=tm, tn=tn, tk=d // 2)
    pb, = _mm_nn("branch_b", yb, weight("w_br_b"), tm=tm, tn=tn, tk=cfg["tk_row"])
    gate_blk = lambda n: (gates, pl.BlockSpec((tm, tn), lambda i, j, k: (i, n * (d // tn) + j)))
    pc, merged = _mm_nn("branch_c", yc, weight("w_br_c"), tm=tm, tn=tn, tk=d // 2,
                        outs=[(sd, blk), (jax.ShapeDtypeStruct((s, d), BF16), blk)],
                        extras=[gate_blk(0), gate_blk(1), gate_blk(2), (pa, blk), (pb, blk)],
                        epilogue=lambda acc, g0, g1, g2, av, bv: (acc, g0 * av + g1 * bv + g2 * acc))
    out, x_next = _mm_nn("out_proj", merged, weight("w_out"), tm=tm, tn=tn, tk=cfg["tk_row"],
                         outs=[(sd, blk), (sd, blk)],
                         extras=[(x, blk), (rg, pl.BlockSpec((1, tn), lambda i, j, k: (0, j)))],
                         epilogue=lambda acc, xv, g: (acc, xv + g * acc))
    saved = dict(x=x, h_t=h_t, proj=proj, projb=projb, gates=gates, ya=ya, yb=yb, yc=yc, att=att, rsave=rsave,
                 pa=pa, pb=pb, pc=pc, merged=merged, out=out, scale=scale, rg=rg)
    return x_next, saved


def _reduce_prepare(grads):
    return [_sum_sibling(g, r) for g, r in zip(grads, _swap_halves(grads))]


def _layer_bwd(dxo, sv, wbuf, p, layer, cfg):
    d, tr, tm, tk_s = cfg["d"], cfg["tr"], cfg["tm"], cfg["tk_s"]
    weight = lambda n: _Sharded(wbuf[n], BIG_KIND[n], layer=layer)
    dout, drg = _resid_bwd(dxo, sv["out"], sv["rg"], tr)
    dmerged = _mm_nt("d_merged", dout, weight("w_out"), tm=tm, tn=cfg["tn_row"], tk=d)
    g_out = _mm_tn("g_w_out", sv["merged"], dout, "row", tm=cfg["tn_row"], tn=cfg["tn_d"], tk=tk_s)
    dpa, dpb, dpc, dlogit, g_bgate = _merge_bwd(dmerged, sv["gates"], sv["pa"], sv["pb"], sv["pc"], cfg["tr_small"])
    dya = _mm_nt("d_ya", dpa, weight("w_br_a"), tm=tm, tn=cfg["tn_half"], tk=cfg["tn_d"])
    dyb = _mm_nt("d_yb", dpb, weight("w_br_b"), tm=tm, tn=cfg["tn_row"], tk=d)
    dyc = _mm_nt("d_yc", dpc, weight("w_br_c"), tm=tm, tn=cfg["tn_half"], tk=cfg["tn_d"])
    g_a = _mm_tn("g_w_br_a", sv["ya"], dpa, "col", tm=cfg["tn_half"], tn=cfg["tn_d"], tk=tk_s)
    g_b = _mm_tn("g_w_br_b", sv["yb"], dpb, "row", tm=cfg["tn_row"], tn=cfg["tn_d"], tk=tk_s)
    g_c = _mm_tn("g_w_br_c", sv["yc"], dpc, "col", tm=cfg["tn_half"], tn=cfg["tn_d"], tk=tk_s)
    g_gate = _mm_grad("g_w_gate", sv["h_t"], dlogit, "col", tm=cfg["tm_g"], tn=cfg["tn_gate"], tk=tk_s)
    early = ("w_gate", "w_br_a", "w_br_b", "w_br_c", "w_out")
    early_grads = [g_gate, g_a, g_b, g_c, g_out]
    d_a, g_pool_w, g_pool_scale = _pool_bwd(sv["proj"], dya, p["pool_w"], p["pool_scale"], d, tr)
    d_c, g_conv_w, swapped = _conv_bwd(sv["proj"], dyc, p["conv_w"], d, tr, _swap_job(early_grads))
    parts = dict(zip(early, [_sum_sibling(g, r) for g, r in zip(early_grads, swapped)]))
    dq, dk, dv, dzb, sent = _sb_bwd(sv["proj"], sv["projb"], sv["att"], dyb, sv["rsave"], d, cfg["blk"],
                                    _send_job([parts[n] for n in early]))
    theirs = dict(zip(early, sent))
    dproj = jnp.concatenate([d_a, dq.astype(BF16), dk.astype(BF16), dv.astype(BF16), dzb.astype(BF16), d_c], axis=1)
    g_in = _mm_grad("g_w_in", sv["h_t"], dproj, "col", tm=cfg["tm_g"], tn=cfg["tn_in"], tk=tk_s)
    parts["w_in"], = _reduce_prepare([g_in])
    dh_in, (theirs["w_in"],) = _mm_nt("d_h_in", dproj, weight("w_in"), tm=tm, tn=cfg["tn_d"], tk=cfg["tk_in"],
                                      job=_send_job([parts["w_in"]]))
    tn = cfg["tn_d"]
    dh = _mm_nt("d_h_gate", dlogit, weight("w_gate"), tm=tm, tn=tn, tk=cfg["tk_gate"],
                extras=[(dh_in, pl.BlockSpec((tm, tn), lambda i, j, k: (i, j)))],
                epilogue=lambda acc, prev: (prev + acc,))
    dx, dshift, dscale, g_norm = _norm_bwd(dh, sv["x"], p["norm_g"], sv["scale"], dxo, tr)
    small = dict(dmod=jnp.concatenate([dshift, dscale, drg], axis=1), norm_g=g_norm, pool_scale=g_pool_scale,
                 b_gate=g_bgate, conv_w=g_conv_w, pool_w=g_pool_w)
    return dx, parts, theirs, small


BIG = ("w_in", "w_gate", "w_br_a", "w_br_b", "w_br_c", "w_out")
BIG_KIND = dict(w_in="col", w_gate="col", w_br_a="col", w_br_b="row", w_br_c="col", w_out="row")


def _pad_to(v, n):
    return jnp.pad(v, (0, n - v.shape[0]))


def kernel(x, c, norm_g, w_ada, b_ada, w_in, pool_w, pool_scale, conv_w, w_br_a, w_br_b, w_br_c, w_gate, b_gate, w_out, final_g, loss_target, m_norm_g, m_w_ada, m_b_ada, m_w_in, m_pool_w, m_pool_scale, m_conv_w, m_w_br_a, m_w_br_b, m_w_br_c, m_w_gate, m_b_gate, m_w_out, m_final_g, v_norm_g, v_w_ada, v_b_ada, v_w_in, v_pool_w, v_pool_scale, v_conv_w, v_w_br_a, v_w_br_b, v_w_br_c, v_w_gate, v_b_gate, v_w_out, v_final_g):
    weights = dict(norm_g=norm_g, w_ada=w_ada, b_ada=b_ada, w_in=w_in, pool_w=pool_w, pool_scale=pool_scale,
                   conv_w=conv_w, w_br_a=w_br_a, w_br_b=w_br_b, w_br_c=w_br_c, w_gate=w_gate, b_gate=b_gate,
                   w_out=w_out, final_g=final_g)
    mom_m = dict(norm_g=m_norm_g, w_ada=m_w_ada, b_ada=m_b_ada, w_in=m_w_in, pool_w=m_pool_w,
                 pool_scale=m_pool_scale, conv_w=m_conv_w, w_br_a=m_w_br_a, w_br_b=m_w_br_b, w_br_c=m_w_br_c,
                 w_gate=m_w_gate, b_gate=m_b_gate, w_out=m_w_out, final_g=m_final_g)
    mom_v = dict(norm_g=v_norm_g, w_ada=v_w_ada, b_ada=v_b_ada, w_in=v_w_in, pool_w=v_pool_w,
                 pool_scale=v_pool_scale, conv_w=v_conv_w, w_br_a=v_w_br_a, w_br_b=v_w_br_b, w_br_c=v_w_br_c,
                 w_gate=v_w_gate, b_gate=v_b_gate, w_out=v_w_out, final_g=v_final_g)
    names = list(weights)

    _, s, d = x.shape
    layers = norm_g.shape[0]
    pw, gd, cc = d // 2, d // 8, d // 2
    ada_cols = 3 * d // N_CHIPS
    xi, yi, ci = lax.axis_index("x"), lax.axis_index("y"), lax.axis_index("c")
    me = 4 * xi + 2 * yi + ci
    my_chip = 2 * xi + yi

    cfg = dict(
        d=d, tr=_tile(s, 256, 16), tr_small=_tile(s, 128, 16), tm=_tile(s, 1024, 16), tk_s=_tile(s, 4096, 16),
        blk=_tile(s, min(256, max(s // 4, 16)), 16),
        tn_in=_tile(7 * d // N_CHIPS, 512), tk_in=_tile(7 * d // N_CHIPS, 3584),
        tn_gate=_tile(3 * d // N_CHIPS, 512), tk_gate=_tile(3 * d // N_CHIPS, 1536), tn_d=_tile(d // N_CHIPS, 512),
        tn_row=_tile(d // N_CHIPS, 512, 16), tk_row=_tile(d // N_CHIPS, 512, 16),
        tn_half=_tile(d // 2, 512), tm_g=_tile(d, 1024),
    )

    conv_flat = conv_w.reshape(-1)
    conv_len = -(-conv_flat.shape[0] // 1024) * 1024
    pack0 = jnp.concatenate([c.reshape(-1), _pad_to(conv_flat, conv_len), pool_w.reshape(-1)])
    w0 = -(-pack0.shape[0] // 1024) * 1024
    g0 = _all_gather_small(_pad_to(pack0, w0).reshape(8, w0 // 8), "gather_small").reshape(N_DEV, w0)
    c_all = g0[:, 0:d]
    chip_rows = g0[0::2]
    conv_full = jnp.concatenate(
        [chip_rows[j, d:d + conv_flat.shape[0]].reshape(conv_w.shape) for j in range(N_CHIPS)], axis=2)
    pool_full = jnp.concatenate(
        [chip_rows[j, d + conv_len:d + conv_len + pool_w.size].reshape(pool_w.shape) for j in range(N_CHIPS)],
        axis=2)
    pool_bf = pool_full.astype(BF16)

    sc16 = _silu_bf16(jnp.pad(c_all, ((0, 16 - N_DEV), (0, 0))))
    mods = []
    for l in range(layers):
        bias = lax.dynamic_slice(b_ada[l], (my_chip * ada_cols,), (ada_cols,)).reshape(1, ada_cols)
        tn = _tile(ada_cols, 512)
        mod_l, = _mm("mod", sc16, w_ada[l], dims=NN, grid=(1, ada_cols // tn, 1),
                     a_spec=pl.BlockSpec((16, d), lambda i, j, k: (0, 0)),
                     b_spec=pl.BlockSpec((d, tn), lambda i, j, k: (0, j)),
                     acc_shape=(16, tn),
                     outs=[(jax.ShapeDtypeStruct((16, ada_cols), F32), pl.BlockSpec((16, tn), lambda i, j, k: (0, j)))],
                     extras=[(bias, pl.BlockSpec((1, tn), lambda i, j, k: (0, j)))],
                     epilogue=lambda acc, b: (acc + b,))
        mods.append(mod_l[0:N_DEV])
    g1 = _all_gather_small(jnp.concatenate(mods, axis=1), "gather_mod")
    g1 = g1.reshape(N_CHIPS, 2, N_DEV, layers, ada_cols)[:, 0]
    mod_all = jnp.transpose(g1, (1, 2, 0, 3)).reshape(N_DEV, layers, 3 * d)
    mod_me = lax.dynamic_slice(mod_all, (me, 0, 0), (1, layers, 3 * d))[0]

    wbuf = {n: _place_shard(weights[n]) for n in BIG}
    params = [dict(norm_g=norm_g[l:l + 1], pool_scale=pool_scale[l:l + 1], b_gate=b_gate[l:l + 1],
                   conv_w=conv_full[l], pool_w=pool_bf[l]) for l in range(layers)]
    others = [n for n in BIG if n != "w_in"]

    def gather(names, layer):
        return lambda wb: (_gather_job([wb[n] for n in names], layer), names)

    act = x[0]
    saved = []
    for l in range(layers):
        nxt = ["w_in"] if l + 1 < layers else []
        if l == 0:
            make_jobs = {"fetch_w_in": True, "attn": gather(others + nxt, [0] * len(others) + [1] * len(nxt))}
        else:
            make_jobs = {"proj": gather(others, l)}
            if nxt:
                make_jobs["attn"] = gather(nxt, l + 1)
        act, sv = _layer_fwd(act, mod_me[l:l + 1], wbuf, params[l], l, cfg, make_jobs)
        saved.append(sv)
    loss_part, dact, g_final = _final_loss(act, final_g.reshape(1, d), loss_target[0], cfg["tr"])
    loss = lax.psum(loss_part[0, 0], ("x", "y", "c"))

    small_grads, parts, theirs = [None] * layers, [None] * layers, [None] * layers
    for l in reversed(range(layers)):
        dact, parts[l], theirs[l], small_grads[l] = _layer_bwd(dact, saved[l], wbuf, params[l], l, cfg)
    grad_x = dact.reshape(x.shape)
    owned = []
    for n in BIG:
        buf = None
        for l in range(layers):
            buf = _sum_owner(parts[l][n], theirs[l][n], l, layers, prev=buf)
        owned.append(buf)
    full = _share_halves(owned)
    grads = {n: f.reshape(weights[n].shape) for n, f in zip(BIG, full)}

    small_names = ("dmod", "norm_g", "pool_scale", "b_gate", "conv_w", "pool_w")
    pieces = [small_grads[l][n].reshape(-1) for n in small_names for l in range(layers)] + [g_final.reshape(-1)]
    pack1 = jnp.concatenate(pieces)
    w1 = -(-pack1.shape[0] // 1024) * 1024
    g2 = _all_gather_small(_pad_to(pack1, w1).reshape(8, w1 // 8), "gather_grads")
    total = _sum_blocks(g2, N_DEV, 8).reshape(-1)
    off = 0
    summed = {}
    for n in small_names:
        per = small_grads[0][n].size
        summed[n] = jnp.stack([total[off + l * per:off + (l + 1) * per].reshape(small_grads[0][n].shape)
                               for l in range(layers)])
        off += layers * per
    grads["final_g"] = total[off:off + d]
    grads["norm_g"] = summed["norm_g"].reshape(layers, d)
    grads["pool_scale"] = summed["pool_scale"].reshape(layers, pw)
    grads["b_gate"] = summed["b_gate"].reshape(layers, 3 * d)
    grads["b_ada"] = summed["dmod"].reshape(layers, 3 * d)
    cs = cc // N_CHIPS
    grads["conv_w"] = lax.dynamic_slice(summed["conv_w"], (0, 0, my_chip * cs), (layers, 3, cs))
    rs_ = gd // N_CHIPS
    grads["pool_w"] = lax.dynamic_slice(summed["pool_w"], (0, 0, my_chip * rs_, 0), (layers, N_GROUPS, rs_, gd))
    dmod_all = g2.reshape(N_DEV, w1)[:, 0:layers * 3 * d].reshape(N_DEV, layers, 3 * d)
    g_ada = []
    for l in range(layers):
        cols = lax.dynamic_slice(dmod_all[:, l], (0, my_chip * ada_cols), (N_DEV, ada_cols))
        cols16 = jnp.pad(cols, ((0, 16 - N_DEV), (0, 0)))
        tn = _tile(ada_cols, 512)
        tm = _tile(d, 1024)
        ga, = _mm("g_w_ada", sc16, cols16, dims=TN, grid=(d // tm, ada_cols // tn, 1),
                  a_spec=pl.BlockSpec((16, tm), lambda i, j, k: (0, i)),
                  b_spec=pl.BlockSpec((16, tn), lambda i, j, k: (0, j)),
                  acc_shape=(tm, tn),
                  outs=[(jax.ShapeDtypeStruct((d, ada_cols), F32), pl.BlockSpec((tm, tn), lambda i, j, k: (i, j)))])
        g_ada.append(ga)
    grads["w_ada"] = jnp.stack(g_ada)

    deltas, new_m, new_v = {}, {}, {}
    for n in names:
        deltas[n], new_m[n], new_v[n] = _adamw(weights[n], grads[n], mom_m[n], mom_v[n])
    return (loss, grad_x, *[grads[n] for n in names], *[deltas[n] for n in names],
            *[new_m[n] for n in names], *[new_v[n] for n in names])
```

```python
import functools

import jax
import jax.numpy as jnp
from jax import lax
from jax.experimental import pallas as pl
from jax.experimental.pallas import tpu as pltpu

F32 = jnp.float32
BF16 = jnp.bfloat16
MESH = pl.DeviceIdType.MESH

N_CHIPS = 4
N_DEV = 8
N_GROUPS = 4
POOL_WINDOWS = (2, 4, 8, 16)
POOL_HALO = 16
CONV_HALO = 8
HEAD_DIM = 128
SB_FWD_HEADS = 4
SB_BWD_HEADS = 4
RMS_EPS = 1e-6
ADAM_LR = 0.001
ADAM_B1 = 0.9
ADAM_B2 = 0.999
ADAM_EPS = 1e-08
ADAM_WD = 0.01
ADAM_STEP = 10
V7X_VMEM_LIMIT = 56 * 1024 * 1024


def _tile(n, pref, mult=128):
    best = None
    t = mult
    while t <= min(n, pref):
        if n % t == 0:
            best = t
        t += mult
    return n if best is None else best


def _params(sem=None):
    return pltpu.CompilerParams(dimension_semantics=sem, vmem_limit_bytes=V7X_VMEM_LIMIT)


def _sigmoid(z):
    return jax.nn.sigmoid(z)


def _dsilu(z, sg):
    return sg * (1.0 + z * (1.0 - sg))


NN = ((1,), (0,))
NT = ((1,), (1,))
TN = ((0,), (0,))


def _mm(name, a, b, *, dims, grid, a_spec, b_spec, acc_shape, outs, extras=(), epilogue=None, with_col=False,
        job=None):
    gi, gj, nk = grid
    ne, no = len(extras), len(outs)
    if epilogue is None:
        epilogue = lambda acc: (acc,)

    def body(*refs):
        own_in, orefs, jargs = _job_refs(job, refs[:-1], 2 + ne, no)
        a_ref, b_ref, ex, acc = own_in[0], own_in[1], own_in[2:], refs[-1]
        i, j, k = pl.program_id(0), pl.program_id(1), pl.program_id(2)
        lead = (j,) if with_col else ()
        if job is not None:
            @pl.when((i == 0) & (j == 0) & (k == 0))
            def _():
                job.start(*jargs)

            if job.mid is not None:
                @pl.when((i == gi - 1) & (j == gj // 2) & (k == 0))
                def _():
                    job.mid(*jargs)

        @pl.when(k == 0)
        def _():
            acc[...] = jnp.zeros_like(acc)

        acc[...] += lax.dot_general(a_ref[...].astype(BF16), b_ref[...].astype(BF16), (dims, ((), ())),
                                    preferred_element_type=F32)

        @pl.when(k == nk - 1)
        def _():
            vals = epilogue(*lead, acc[...], *[e[...] for e in ex])
            for o, v in zip(orefs, vals):
                o[...] = v.astype(o.dtype)

        if job is not None:
            @pl.when((i == gi - 1) & (j == gj - 1) & (k == nk - 1))
            def _():
                job.finish(*jargs)

    hosted = _host(job, 2 + ne, no)
    sem = ("parallel", "parallel", "arbitrary") if job is None else ("arbitrary",) * 3
    res = pl.pallas_call(
        body, name=name, grid=grid,
        in_specs=[a_spec, b_spec] + [s for _, s in extras] + hosted["in_specs"],
        out_specs=[s for _, s in outs] + hosted["out_specs"],
        out_shape=[sh for sh, _ in outs] + hosted["out_shape"],
        scratch_shapes=hosted["scratch"] + [pltpu.VMEM(acc_shape, F32)],
        input_output_aliases=hosted["aliases"],
        compiler_params=_params(sem),
    )(a, b, *[e for e, _ in extras], *hosted["ins"])
    return res


class _Sharded:
    def __init__(self, arr, kind, layer=None):
        self.arr, self.kind, self.layer = arr, kind, layer
        r, c = arr.shape[-2:]
        self.rows = r * (N_CHIPS if kind == "row" else 1)
        self.cols = c * (N_CHIPS if kind == "col" else 1)
        self.sr, self.sc = r, c

    def spec(self, br, bc, f):
        lead = (None,) if self.layer is None else (None, None)
        layer = self.layer
        if self.kind == "col":
            per = self.sc // bc
            assert per * bc == self.sc and self.sr % br == 0, (self.arr.shape, br, bc)

            def idx(*g):
                rb, cb = f(*g)
                return ((cb // per,) + (() if layer is None else (layer,)) + (rb, cb % per))
        else:
            per = self.sr // br
            assert per * br == self.sr and self.sc % bc == 0, (self.arr.shape, br, bc)

            def idx(*g):
                rb, cb = f(*g)
                return ((rb // per,) + (() if layer is None else (layer,)) + (rb % per, cb))
        return pl.BlockSpec(lead + (br, bc), idx)


def _grad_buffer(rows, cols, kind):
    if kind == "col":
        return jax.ShapeDtypeStruct((N_CHIPS, rows, cols // N_CHIPS), BF16)
    return jax.ShapeDtypeStruct((N_CHIPS, rows // N_CHIPS, cols), BF16)


def _mm_nn(name, a, w, *, tm, tn, tk, outs=None, extras=(), epilogue=None, out_dtype=F32, with_col=False,
           job=None):
    m, kdim = a.shape
    n = w.cols
    grid = (m // tm, n // tn, kdim // tk)
    if outs is None:
        outs = [(jax.ShapeDtypeStruct((m, n), out_dtype), pl.BlockSpec((tm, tn), lambda i, j, k: (i, j)))]
    return _mm(name, a, w.arr, dims=NN, grid=grid,
               a_spec=pl.BlockSpec((tm, tk), lambda i, j, k: (i, k)),
               b_spec=w.spec(tk, tn, lambda i, j, k: (k, j)),
               acc_shape=(tm, tn), outs=outs, extras=extras, epilogue=epilogue, with_col=with_col, job=job)


def _mm_grad(name, a_t, b, kind, *, tm, tn, tk):
    m, kdim = a_t.shape
    n = b.shape[1]
    out = _Sharded(_grad_buffer(m, n, kind), kind)
    return _mm(name, a_t, b, dims=NN, grid=(m // tm, n // tn, kdim // tk),
               a_spec=pl.BlockSpec((tm, tk), lambda i, j, k: (i, k)),
               b_spec=pl.BlockSpec((tk, tn), lambda i, j, k: (k, j)),
               acc_shape=(tm, tn), outs=[(out.arr, out.spec(tm, tn, lambda i, j, k: (i, j)))])[0]


def _mm_nt(name, a, w, *, tm, tn, tk, extras=(), epilogue=None, out_dtype=F32, job=None):
    m, kdim = a.shape
    n = w.rows
    grid = (m // tm, n // tn, kdim // tk)
    outs = [(jax.ShapeDtypeStruct((m, n), out_dtype), pl.BlockSpec((tm, tn), lambda i, j, k: (i, j)))]
    res = _mm(name, a, w.arr, dims=NT, grid=grid,
              a_spec=pl.BlockSpec((tm, tk), lambda i, j, k: (i, k)),
              b_spec=w.spec(tn, tk, lambda i, j, k: (j, k)),
              acc_shape=(tm, tn), outs=outs, extras=extras, epilogue=epilogue, job=job)
    return res[0] if job is None else (res[0], list(res[1:]))


def _mm_tn(name, a, b, kind, *, tm, tn, tk):
    kdim, m = a.shape
    n = b.shape[1]
    grid = (m // tm, n // tn, kdim // tk)
    out = _Sharded(_grad_buffer(m, n, kind), kind)
    outs = [(out.arr, out.spec(tm, tn, lambda i, j, k: (i, j)))]
    return _mm(name, a, b, dims=TN, grid=grid,
               a_spec=pl.BlockSpec((tk, tm), lambda i, j, k: (k, i)),
               b_spec=pl.BlockSpec((tk, tn), lambda i, j, k: (k, j)),
               acc_shape=(tm, tn), outs=outs)[0]


def _row_spec(tr, w, col=0):
    return pl.BlockSpec((tr, w), lambda i: (i, col))


def _vec_spec(w, col=0, rows=1):
    return pl.BlockSpec((rows, w), lambda i: (0, col))


def _norm_fwd(x, g, scale, shift, tr):
    s, d = x.shape

    def body(x_ref, g_ref, sc_ref, sh_ref, h_ref, ht_ref):
        xv = x_ref[...]
        r = lax.rsqrt(jnp.mean(xv * xv, axis=-1, keepdims=True) + RMS_EPS)
        y = xv * r * g_ref[...]
        hv = y * (1.0 + sc_ref[...]) + sh_ref[...]
        h_ref[...] = hv.astype(BF16)
        ht_ref[...] = hv.T.astype(BF16)

    return pl.pallas_call(
        body, name="norm_fwd", grid=(s // tr,),
        in_specs=[_row_spec(tr, d), _vec_spec(d), _vec_spec(d), _vec_spec(d)],
        out_specs=[_row_spec(tr, d), pl.BlockSpec((d, tr), lambda i: (0, i))],
        out_shape=[jax.ShapeDtypeStruct((s, d), BF16), jax.ShapeDtypeStruct((d, s), BF16)],
        compiler_params=_params(("parallel",)),
    )(x, g, scale, shift)


def _norm_bwd(dh, x, g, scale, dxo, tr):
    s, d = x.shape

    def body(dh_ref, x_ref, g_ref, sc_ref, dxo_ref, dx_ref, dsh_ref, dsc_ref, dg_ref):
        @pl.when(pl.program_id(0) == 0)
        def _():
            dsh_ref[...] = jnp.zeros_like(dsh_ref)
            dsc_ref[...] = jnp.zeros_like(dsc_ref)
            dg_ref[...] = jnp.zeros_like(dg_ref)

        xv, dhv, gv = x_ref[...], dh_ref[...], g_ref[...]
        r = lax.rsqrt(jnp.mean(xv * xv, axis=-1, keepdims=True) + RMS_EPS)
        xn = xv * r
        dsh_ref[...] += jnp.sum(dhv, axis=0, keepdims=True)
        dsc_ref[...] += jnp.sum(dhv * (xn * gv), axis=0, keepdims=True)
        dyg = dhv * (1.0 + sc_ref[...])
        dg_ref[...] += jnp.sum(dyg * xn, axis=0, keepdims=True)
        dxn = dyg * gv
        dx_ref[...] = dxo_ref[...] + r * (dxn - xn * jnp.mean(dxn * xn, axis=-1, keepdims=True))

    vec = jax.ShapeDtypeStruct((1, d), F32)
    return pl.pallas_call(
        body, name="norm_bwd", grid=(s // tr,),
        in_specs=[_row_spec(tr, d), _row_spec(tr, d), _vec_spec(d), _vec_spec(d), _row_spec(tr, d)],
        out_specs=[_row_spec(tr, d), _vec_spec(d), _vec_spec(d), _vec_spec(d)],
        out_shape=[jax.ShapeDtypeStruct((s, d), F32), vec, vec, vec],
        compiler_params=_params(("arbitrary",)),
    )(dh, x, g, scale, dxo)


def _final_loss(x, g, target, tr):
    s, d = x.shape

    def body(x_ref, g_ref, t_ref, loss_ref, dx_ref, dg_ref):
        @pl.when(pl.program_id(0) == 0)
        def _():
            loss_ref[...] = jnp.zeros_like(loss_ref)
            dg_ref[...] = jnp.zeros_like(dg_ref)

        xv, gv = x_ref[...], g_ref[...]
        r = lax.rsqrt(jnp.mean(xv * xv, axis=-1, keepdims=True) + RMS_EPS)
        xn = xv * r
        err = xn * gv - t_ref[...]
        per_row = jnp.mean(err * err, axis=-1, keepdims=True)
        loss_ref[...] += 0.5 * jnp.sum(per_row, axis=0, keepdims=True)
        dy = err * (1.0 / d)
        dg_ref[...] += jnp.sum(dy * xn, axis=0, keepdims=True)
        dxn = dy * gv
        dx_ref[...] = r * (dxn - xn * jnp.mean(dxn * xn, axis=-1, keepdims=True))

    return pl.pallas_call(
        body, name="final_loss", grid=(s // tr,),
        in_specs=[_row_spec(tr, d), _vec_spec(d), _row_spec(tr, d)],
        out_specs=[pl.BlockSpec((8, 128), lambda i: (0, 0)), _row_spec(tr, d), _vec_spec(d)],
        out_shape=[jax.ShapeDtypeStruct((8, 128), F32), jax.ShapeDtypeStruct((s, d), F32),
                   jax.ShapeDtypeStruct((1, d), F32)],
        compiler_params=_params(("arbitrary",)),
    )(x, g, target)


def _halo_before(tr, halo, w, col):
    per = tr // halo
    return pl.BlockSpec((halo, w), lambda i: (jnp.maximum(i * per - 1, 0), col))


def _halo_after(tr, halo, w, col, n_tiles):
    per = tr // halo
    return pl.BlockSpec((halo, w), lambda i: (jnp.minimum((i + 1) * per, n_tiles * per - 1), col))


def _pool_fwd(proj, pool_w, pool_scale, d, tr):
    s = proj.shape[0]
    pw, gd, hl = d // 2, d // 8, POOL_HALO

    def body(xa_ref, xh_ref, za_ref, w_ref, ps_ref, ya_ref, buf):
        i = pl.program_id(0)
        buf[0:hl, :] = jnp.where(i > 0, xh_ref[...], 0.0)
        buf[hl:hl + tr, :] = xa_ref[...]
        row = i * tr + lax.broadcasted_iota(jnp.int32, (tr, 1), 0)
        za = za_ref[...]
        gate = za * _sigmoid(za)
        for g, win in enumerate(POOL_WINDOWS):
            cs = slice(g * gd, (g + 1) * gd)
            xg = buf[hl:hl + tr, cs]
            acc = xg
            for j in range(1, win):
                acc = acc + buf[hl - j:hl - j + tr, cs]
            cnt = jnp.minimum(row + 1, win).astype(F32)
            mixed = acc / cnt - xg
            y = jnp.dot(mixed.astype(BF16), w_ref[g], preferred_element_type=F32)
            ya_ref[:, cs] = ((y * ps_ref[:, cs]) * gate[:, cs]).astype(BF16)

    return pl.pallas_call(
        body, name="pool_fwd", grid=(s // tr,),
        in_specs=[_row_spec(tr, pw, 0), _halo_before(tr, hl, pw, 0), _row_spec(tr, pw, 1),
                  pl.BlockSpec((N_GROUPS, gd, gd), lambda i: (0, 0, 0)), _vec_spec(pw)],
        out_specs=_row_spec(tr, pw), out_shape=jax.ShapeDtypeStruct((s, pw), BF16),
        scratch_shapes=[pltpu.VMEM((tr + hl, pw), F32)],
        compiler_params=_params(("parallel",)),
    )(proj, proj, proj, pool_w, pool_scale)


def _pool_bwd(proj, dya, pool_w, pool_scale, d, tr):
    s = proj.shape[0]
    pw, gd, hl = d // 2, d // 8, POOL_HALO
    n_tiles = s // tr

    def body(xa_ref, xh_ref, za_ref, zh_ref, dya_ref, dyh_ref, w_ref, ps_ref, da_ref, gw_ref, gs_ref, buf, dbuf):
        i = pl.program_id(0)

        @pl.when(i == 0)
        def _():
            gw_ref[...] = jnp.zeros_like(gw_ref)
            gs_ref[...] = jnp.zeros_like(gs_ref)

        buf[0:hl, :] = jnp.where(i > 0, xh_ref[...], 0.0)
        buf[hl:hl + tr, :] = xa_ref[...]
        row = i * tr + lax.broadcasted_iota(jnp.int32, (tr, 1), 0)
        row_h = (i + 1) * tr + lax.broadcasted_iota(jnp.int32, (hl, 1), 0)
        za, dya_t = za_ref[...], dya_ref[...]
        sg = _sigmoid(za)
        gate = za * sg
        dpre = dya_t * gate
        zh = zh_ref[...]
        dpre_h = jnp.where(i < n_tiles - 1, dyh_ref[...], 0.0) * (zh * _sigmoid(zh))
        ps = ps_ref[...]
        for g, win in enumerate(POOL_WINDOWS):
            cs = slice(g * gd, (g + 1) * gd)
            wg = w_ref[g]
            xg = buf[hl:hl + tr, cs]
            acc = xg
            for j in range(1, win):
                acc = acc + buf[hl - j:hl - j + tr, cs]
            cnt = jnp.minimum(row + 1, win).astype(F32)
            mixed = (acc / cnt - xg).astype(BF16)
            ylin = jnp.dot(mixed, wg, preferred_element_type=F32)
            gs_ref[:, cs] += jnp.sum(dpre[:, cs] * ylin, axis=0, keepdims=True)
            da_ref[:, pw + g * gd:pw + (g + 1) * gd] = (
                dya_t[:, cs] * (ylin * ps[:, cs]) * _dsilu(za[:, cs], sg[:, cs])).astype(BF16)
            dyl = (dpre[:, cs] * ps[:, cs]).astype(BF16)
            gw_ref[g] += lax.dot_general(mixed, dyl, (TN, ((), ())), preferred_element_type=F32)
            dmix = lax.dot_general(dyl, wg, (NT, ((), ())), preferred_element_type=F32)
            dyl_h = (dpre_h[:, cs] * ps[:, cs]).astype(BF16)
            dmix_h = lax.dot_general(dyl_h, wg, (NT, ((), ())), preferred_element_type=F32)
            cnt_h = jnp.minimum(row_h + 1, win).astype(F32)
            dbuf[0:tr, cs] = dmix / cnt
            dbuf[tr:tr + hl, cs] = dmix_h / cnt_h
            dx = dbuf[0:tr, cs] - dmix
            for j in range(1, win):
                dx = dx + dbuf[j:j + tr, cs]
            da_ref[:, cs] = dx.astype(BF16)

    return pl.pallas_call(
        body, name="pool_bwd", grid=(n_tiles,),
        in_specs=[_row_spec(tr, pw, 0), _halo_before(tr, hl, pw, 0),
                  _row_spec(tr, pw, 1), _halo_after(tr, hl, pw, 1, n_tiles),
                  _row_spec(tr, pw, 0), _halo_after(tr, hl, pw, 0, n_tiles),
                  pl.BlockSpec((N_GROUPS, gd, gd), lambda i: (0, 0, 0)), _vec_spec(pw)],
        out_specs=[_row_spec(tr, 2 * pw), pl.BlockSpec((N_GROUPS, gd, gd), lambda i: (0, 0, 0)), _vec_spec(pw)],
        out_shape=[jax.ShapeDtypeStruct((s, 2 * pw), BF16), jax.ShapeDtypeStruct((N_GROUPS, gd, gd), F32),
                   jax.ShapeDtypeStruct((1, pw), F32)],
        scratch_shapes=[pltpu.VMEM((tr + hl, pw), F32), pltpu.VMEM((tr + hl, pw), F32)],
        compiler_params=_params(("arbitrary",)),
    )(proj, proj, proj, proj, dya, dya, pool_w, pool_scale)


def _conv_fwd(proj, conv_w, d, tr):
    s = proj.shape[0]
    cc, hl = d // 2, CONV_HALO
    cu, cb, cg_, cz = 10, 11, 12, 13

    def body(u_ref, uh_ref, bg_ref, cg_ref, ch_ref, zc_ref, w_ref, yc_ref, buf):
        i = pl.program_id(0)
        buf[0:hl, :] = jnp.where(i > 0, ch_ref[...] * uh_ref[...], 0.0)
        buf[hl:hl + tr, :] = cg_ref[...] * u_ref[...]
        y = w_ref[0:1, :] * buf[hl - 2:hl - 2 + tr, :]
        y = y + w_ref[1:2, :] * buf[hl - 1:hl - 1 + tr, :]
        y = y + w_ref[2:3, :] * buf[hl:hl + tr, :]
        zc = zc_ref[...]
        yc_ref[...] = ((bg_ref[...] * y) * (zc * _sigmoid(zc))).astype(BF16)

    return pl.pallas_call(
        body, name="conv_fwd", grid=(s // tr,),
        in_specs=[_row_spec(tr, cc, cu), _halo_before(tr, hl, cc, cu), _row_spec(tr, cc, cb),
                  _row_spec(tr, cc, cg_), _halo_before(tr, hl, cc, cg_), _row_spec(tr, cc, cz),
                  _vec_spec(cc, rows=3)],
        out_specs=_row_spec(tr, cc), out_shape=jax.ShapeDtypeStruct((s, cc), BF16),
        scratch_shapes=[pltpu.VMEM((tr + hl, cc), F32)],
        compiler_params=_params(("parallel",)),
    )(proj, proj, proj, proj, proj, proj, conv_w)


def _conv_bwd(proj, dyc, conv_w, d, tr, job=None):
    s = proj.shape[0]
    cc, hl = d // 2, CONV_HALO
    cu, cb, cg_, cz = 10, 11, 12, 13
    n_tiles = s // tr

    def body(*refs):
        own_in, (dc_ref, gw_ref), jargs = _job_refs(job, refs[:-2], 11, 2)
        u_ref, uh_ref, bg_ref, bh_ref, cg_ref, ch_ref, zc_ref, zh_ref, dy_ref, dyh_ref, w_ref = own_in
        buf, dbuf = refs[-2:]
        i = pl.program_id(0)

        @pl.when(i == 0)
        def _():
            gw_ref[...] = jnp.zeros_like(gw_ref)
            if job is not None:
                job.start(*jargs)

        u, bg, cg, zc, dyc_t = u_ref[...], bg_ref[...], cg_ref[...], zc_ref[...], dy_ref[...]
        buf[0:hl, :] = jnp.where(i > 0, ch_ref[...] * uh_ref[...], 0.0)
        buf[hl:hl + tr, :] = cg * u
        v2, v1, v0 = buf[hl - 2:hl - 2 + tr, :], buf[hl - 1:hl - 1 + tr, :], buf[hl:hl + tr, :]
        w0, w1, w2 = w_ref[0:1, :], w_ref[1:2, :], w_ref[2:3, :]
        y = w0 * v2 + w1 * v1 + w2 * v0
        sg = _sigmoid(zc)
        gate = zc * sg
        dc_ref[:, cc:2 * cc] = (dyc_t * y * gate).astype(BF16)
        dc_ref[:, 3 * cc:4 * cc] = (dyc_t * bg * y * _dsilu(zc, sg)).astype(BF16)
        dy = dyc_t * bg * gate
        zh = zh_ref[...]
        dy_h = jnp.where(i < n_tiles - 1, dyh_ref[...], 0.0) * bh_ref[...] * (zh * _sigmoid(zh))
        gw_ref[0:1, :] += jnp.sum(dy * v2, axis=0, keepdims=True)
        gw_ref[1:2, :] += jnp.sum(dy * v1, axis=0, keepdims=True)
        gw_ref[2:3, :] += jnp.sum(dy * v0, axis=0, keepdims=True)
        dbuf[0:tr, :] = dy
        dbuf[tr:tr + hl, :] = dy_h
        dv = w2 * dy + w1 * dbuf[1:1 + tr, :] + w0 * dbuf[2:2 + tr, :]
        dc_ref[:, 0:cc] = (dv * cg).astype(BF16)
        dc_ref[:, 2 * cc:3 * cc] = (dv * u).astype(BF16)

        if job is not None:
            @pl.when(i == n_tiles - 1)
            def _():
                job.finish(*jargs)

    hosted = _host(job, 11, 2)
    res = pl.pallas_call(
        body, name="conv_bwd", grid=(n_tiles,),
        in_specs=[_row_spec(tr, cc, cu), _halo_before(tr, hl, cc, cu),
                  _row_spec(tr, cc, cb), _halo_after(tr, hl, cc, cb, n_tiles),
                  _row_spec(tr, cc, cg_), _halo_before(tr, hl, cc, cg_),
                  _row_spec(tr, cc, cz), _halo_after(tr, hl, cc, cz, n_tiles),
                  _row_spec(tr, cc, 0), _halo_after(tr, hl, cc, 0, n_tiles),
                  _vec_spec(cc, rows=3)] + hosted["in_specs"],
        out_specs=[_row_spec(tr, 4 * cc), _vec_spec(cc, rows=3)] + hosted["out_specs"],
        out_shape=[jax.ShapeDtypeStruct((s, 4 * cc), BF16), jax.ShapeDtypeStruct((3, cc), F32)]
        + hosted["out_shape"],
        scratch_shapes=hosted["scratch"] + [pltpu.VMEM((tr + hl, cc), F32), pltpu.VMEM((tr + hl, cc), F32)],
        input_output_aliases=hosted["aliases"],
        compiler_params=_params(("arbitrary",)),
    )(proj, proj, proj, proj, proj, proj, proj, proj, dyc, dyc, conv_w, *hosted["ins"])
    return res[0], res[1], list(res[2:])


def _merge_bwd(dm, gates, pa, pb, pc, tr):
    s, d = pa.shape

    def body(dm_ref, g_ref, a_ref, b_ref, c_ref, da_ref, db_ref, dc_ref, dg_ref, gb_ref):
        @pl.when(pl.program_id(0) == 0)
        def _():
            gb_ref[...] = jnp.zeros_like(gb_ref)

        dmv = dm_ref[...]
        for n, (p_ref, o_ref) in enumerate(((a_ref, da_ref), (b_ref, db_ref), (c_ref, dc_ref))):
            gv = g_ref[:, n * d:(n + 1) * d]
            o_ref[...] = (dmv * gv).astype(BF16)
            dlogit = (dmv * p_ref[...]) * (gv * (1.0 - gv))
            dg_ref[:, n * d:(n + 1) * d] = dlogit.astype(BF16)
            gb_ref[:, n * d:(n + 1) * d] += jnp.sum(dlogit, axis=0, keepdims=True)

    act = jax.ShapeDtypeStruct((s, d), BF16)
    return pl.pallas_call(
        body, name="merge_bwd", grid=(s // tr,),
        in_specs=[_row_spec(tr, d), _row_spec(tr, 3 * d), _row_spec(tr, d), _row_spec(tr, d), _row_spec(tr, d)],
        out_specs=[_row_spec(tr, d), _row_spec(tr, d), _row_spec(tr, d), _row_spec(tr, 3 * d), _vec_spec(3 * d)],
        out_shape=[act, act, act, jax.ShapeDtypeStruct((s, 3 * d), BF16), jax.ShapeDtypeStruct((1, 3 * d), F32)],
        compiler_params=_params(("arbitrary",)),
    )(dm, gates, pa, pb, pc)


def _resid_bwd(dxo, out, rg, tr):
    s, d = dxo.shape

    def body(dx_ref, o_ref, rg_ref, do_ref, drg_ref):
        @pl.when(pl.program_id(0) == 0)
        def _():
            drg_ref[...] = jnp.zeros_like(drg_ref)

        dxv = dx_ref[...]
        do_ref[...] = (dxv * rg_ref[...]).astype(BF16)
        drg_ref[...] += jnp.sum(dxv * o_ref[...], axis=0, keepdims=True)

    return pl.pallas_call(
        body, name="resid_bwd", grid=(s // tr,),
        in_specs=[_row_spec(tr, d), _row_spec(tr, d), _vec_spec(d)],
        out_specs=[_row_spec(tr, d), _vec_spec(d)],
        out_shape=[jax.ShapeDtypeStruct((s, d), BF16), jax.ShapeDtypeStruct((1, d), F32)],
        compiler_params=_params(("arbitrary",)),
    )(dxo, out, rg)


def _log_keep(z):
    e = jnp.exp(-jnp.abs(z))
    return -(jnp.maximum(z, 0.0) + jnp.log(1.0 + e)), e


class _Job:
    def __init__(self, ins, out_shapes, aliases, n_sems, start, finish, mid=None):
        self.ins, self.out_shapes, self.aliases, self.n_sems = list(ins), list(out_shapes), list(aliases), n_sems
        self.start, self.mid, self.finish = start, mid, finish


def _host(job, n_in, n_out):
    if job is None:
        return dict(ins=[], in_specs=[], out_specs=[], out_shape=[], scratch=[], aliases={})
    return dict(ins=job.ins, in_specs=[_ANY] * len(job.ins), out_specs=[_ANY] * len(job.out_shapes),
                out_shape=job.out_shapes,
                scratch=[pltpu.SemaphoreType.DMA((job.n_sems,)), pltpu.SemaphoreType.DMA((job.n_sems,))],
                aliases={n_in + a: n_out + b for a, b in job.aliases})


def _job_refs(job, refs, n_in, n_out):
    if job is None:
        return refs[:n_in], refs[n_in:n_in + n_out], None
    ji, jo = len(job.ins), len(job.out_shapes)
    own_in, job_in = refs[:n_in], refs[n_in:n_in + ji]
    own_out, job_out = refs[n_in + ji:n_in + ji + n_out], refs[n_in + ji + n_out:n_in + ji + n_out + jo]
    send_sems, recv_sems = refs[n_in + ji + n_out + jo:]
    return own_in, own_out, (job_in, job_out, send_sems, recv_sems)


def _sb_fwd(proj, projb, d, blk, job=None):
    s = proj.shape[0]
    hps = min(SB_FWD_HEADS, d // HEAD_DIM)
    wid = hps * HEAD_DIM
    groups, nq = d // wid, s // blk
    qc, kc, vc, zc = d // wid, 2 * d // wid, 3 * d // wid, 4 * d // wid
    assert nq <= HEAD_DIM
    mid_step = (groups - 1, 2 * nq // 3)

    def body(*refs):
        (q_ref, k_ref, v_ref, zb_ref), (att_ref, yb_ref, rs_ref), jargs = _job_refs(job, refs, 4, 3)
        grp, i = pl.program_id(0), pl.program_id(1)
        if job is not None:
            @pl.when((grp == 0) & (i == 0))
            def _():
                job.start(*jargs)

            if job.mid is not None:
                @pl.when((grp == mid_step[0]) & (i == mid_step[1]))
                def _():
                    job.mid(*jargs)

        r_io = lax.broadcasted_iota(jnp.int32, (blk, blk), 0)
        c_io = lax.broadcasted_iota(jnp.int32, (blk, blk), 1)
        tri = (r_io >= c_io).astype(BF16)
        strict = c_io < r_io
        lane = lax.broadcasted_iota(jnp.int32, (blk, HEAD_DIM), 1)
        heads = [slice(n * HEAD_DIM, (n + 1) * HEAD_DIM) for n in range(hps)]
        qs = [q_ref[:, hd] for hd in heads]

        def block(j, carry, masked):
            ks = pl.multiple_of(j * blk, blk)
            rng = range(hps)
            zs = [lax.dot_general(qs[n], k_ref[pl.ds(ks, blk), heads[n]], (NT, ((), ())),
                                  preferred_element_type=F32) for n in rng]
            lks = []
            for n in rng:
                lk, _ = _log_keep(zs[n])
                if masked:
                    lk = jnp.where(strict, lk, 0.0)
                lks.append(lk.astype(BF16))
            csums = [carry[n][0] + jnp.dot(lks[n], tri, preferred_element_type=F32) for n in rng]
            probs = []
            for n in rng:
                a = jnp.exp(zs[n] + csums[n])
                if masked:
                    a = jnp.where(strict, a, 0.0)
                probs.append(a.astype(BF16))
            out = []
            for n in rng:
                run, acc, rs = carry[n]
                acc = acc + jnp.dot(probs[n], v_ref[pl.ds(ks, blk), heads[n]], preferred_element_type=F32)
                out.append((csums[n][:, 0:1], acc, jnp.where(lane == j, run, rs)))
            return tuple(out)

        zero = (jnp.zeros((blk, 1), F32), jnp.zeros((blk, HEAD_DIM), F32), jnp.zeros((blk, HEAD_DIM), F32))
        carry = block(i, (zero,) * hps, True)
        carry = lax.fori_loop(0, i, lambda jj, cr: block(i - 1 - jj, cr, False), carry)
        for hd, (_, acc, rs) in zip(heads, carry):
            att_ref[:, hd] = acc
            rs_ref[:, hd] = rs
            zb = zb_ref[:, hd]
            yb_ref[:, hd] = (acc * (zb * _sigmoid(zb))).astype(BF16)

        if job is not None:
            @pl.when((grp == groups - 1) & (i == nq - 1))
            def _():
                job.finish(*jargs)

    blk_spec = pl.BlockSpec((blk, wid), lambda h, i: (i, h))
    hosted = _host(job, 4, 3)
    res = pl.pallas_call(
        body, name="sb_fwd", grid=(groups, nq),
        in_specs=[pl.BlockSpec((blk, wid), lambda h, i: (i, qc + h)),
                  pl.BlockSpec((s, wid), lambda h, i: (0, kc + h)),
                  pl.BlockSpec((s, wid), lambda h, i: (0, vc + h)),
                  pl.BlockSpec((blk, wid), lambda h, i: (i, zc + h))] + hosted["in_specs"],
        out_specs=[blk_spec, blk_spec, blk_spec] + hosted["out_specs"],
        out_shape=[jax.ShapeDtypeStruct((s, d), F32), jax.ShapeDtypeStruct((s, d), BF16),
                   jax.ShapeDtypeStruct((s, d), F32)] + hosted["out_shape"],
        scratch_shapes=hosted["scratch"], input_output_aliases=hosted["aliases"],
        compiler_params=_params(("arbitrary", "arbitrary")),
    )(projb, projb, projb, proj, *hosted["ins"])
    return res[0], res[1], res[2], list(res[3:])


def _sb_bwd(proj, projb, att, dyb, rsave, d, blk, job=None):
    s = proj.shape[0]
    hps, wid = SB_BWD_HEADS, SB_BWD_HEADS * HEAD_DIM
    groups, nq = d // wid, s // blk
    qc, kc, vc, zc = d // wid, 2 * d // wid, 3 * d // wid, 4 * d // wid
    scale = HEAD_DIM ** -0.5

    def body(*refs):
        ((q_ref, k_ref, v_ref, zb_ref, att_ref, dyb_ref, rs_ref), (dq_ref, dk_ref, dv_ref, dzb_ref),
         jargs) = _job_refs(job, refs, 7, 4)
        grp, i = pl.program_id(0), pl.program_id(1)
        if job is not None:
            @pl.when((grp == 0) & (i == 0))
            def _():
                job.start(*jargs)

        @pl.when(i == 0)
        def _():
            dk_ref[...] = jnp.zeros_like(dk_ref)
            dv_ref[...] = jnp.zeros_like(dv_ref)

        r_io = lax.broadcasted_iota(jnp.int32, (blk, blk), 0)
        c_io = lax.broadcasted_iota(jnp.int32, (blk, blk), 1)
        tri = (r_io >= c_io).astype(BF16)
        tri_up = (r_io <= c_io).astype(BF16)
        strict = c_io < r_io
        lane = lax.broadcasted_iota(jnp.int32, (blk, HEAD_DIM), 1)
        heads = [slice(n * HEAD_DIM, (n + 1) * HEAD_DIM) for n in range(hps)]
        qs, dos, rss = [], [], []
        for hd in heads:
            qs.append(q_ref[:, hd])
            zb, dyb_t = zb_ref[:, hd], dyb_ref[:, hd]
            sg = _sigmoid(zb)
            dzb_ref[:, hd] = dyb_t * att_ref[:, hd] * _dsilu(zb, sg)
            dos.append((dyb_t * (zb * sg)).astype(BF16))
            rss.append(rs_ref[:, hd])

        def block(j, carry, masked):
            ks = pl.multiple_of(j * blk, blk)
            rng = range(hps)
            kbs = [k_ref[pl.ds(ks, blk), heads[n]] for n in rng]
            vbs = [v_ref[pl.ds(ks, blk), heads[n]] for n in rng]
            zs = [lax.dot_general(qs[n], kbs[n], (NT, ((), ())), preferred_element_type=F32) for n in rng]
            das = [lax.dot_general(dos[n], vbs[n], (NT, ((), ())), preferred_element_type=F32) for n in rng]
            lks, betas = [], []
            for n in rng:
                lk, _ = _log_keep(zs[n])
                betas.append(1.0 - jnp.exp(lk))
                if masked:
                    lk = jnp.where(strict, lk, 0.0)
                lks.append(lk.astype(BF16))
            csums = []
            for n in rng:
                run = jnp.sum(jnp.where(lane == j, rss[n], 0.0), axis=1, keepdims=True)
                csums.append(run + jnp.dot(lks[n], tri, preferred_element_type=F32))
            probs, gs = [], []
            for n in rng:
                a = jnp.exp(zs[n] + csums[n])
                if masked:
                    a = jnp.where(strict, a, 0.0)
                probs.append(a.astype(BF16))
                gs.append(a * das[n])
            gcums = [carry[n][0] + jnp.dot(gs[n].astype(BF16), tri_up, preferred_element_type=F32) for n in rng]
            for n in rng:
                dv_ref[pl.ds(ks, blk), heads[n]] += lax.dot_general(probs[n], dos[n], (TN, ((), ())),
                                                                   preferred_element_type=F32)
            dzs = []
            for n in rng:
                dz = gs[n] - betas[n] * gcums[n]
                if masked:
                    dz = jnp.where(strict, dz, 0.0)
                dzs.append(dz.astype(BF16))
            out = []
            for n in rng:
                dq = carry[n][1] + jnp.dot(dzs[n], kbs[n], preferred_element_type=F32)
                dk_ref[pl.ds(ks, blk), heads[n]] += lax.dot_general(dzs[n], qs[n], (TN, ((), ())),
                                                                   preferred_element_type=F32)
                out.append((gcums[n][:, blk - 1:blk], dq))
            return tuple(out)

        zero = (jnp.zeros((blk, 1), F32), jnp.zeros((blk, HEAD_DIM), F32))
        carry = lax.fori_loop(0, i, lambda j, cr: block(j, cr, False), (zero,) * hps)
        carry = block(i, carry, True)
        for hd, (_, dq) in zip(heads, carry):
            dq_ref[:, hd] = dq * scale

        if job is not None:
            @pl.when((grp == groups - 1) & (i == nq - 1))
            def _():
                job.finish(*jargs)

    blk_spec = pl.BlockSpec((blk, wid), lambda h, i: (i, h))
    full_spec = pl.BlockSpec((s, wid), lambda h, i: (0, h), pipeline_mode=pl.Buffered(1))
    act = jax.ShapeDtypeStruct((s, d), F32)
    hosted = _host(job, 7, 4)
    res = pl.pallas_call(
        body, name="sb_bwd", grid=(groups, nq),
        in_specs=[pl.BlockSpec((blk, wid), lambda h, i: (i, qc + h)),
                  pl.BlockSpec((s, wid), lambda h, i: (0, kc + h)),
                  pl.BlockSpec((s, wid), lambda h, i: (0, vc + h)),
                  pl.BlockSpec((blk, wid), lambda h, i: (i, zc + h)),
                  blk_spec, blk_spec, blk_spec] + hosted["in_specs"],
        out_specs=[blk_spec, full_spec, full_spec, blk_spec] + hosted["out_specs"],
        out_shape=[act, act, act, act] + hosted["out_shape"],
        scratch_shapes=hosted["scratch"], input_output_aliases=hosted["aliases"],
        compiler_params=_params(("arbitrary", "arbitrary")),
    )(projb, projb, projb, proj, att, dyb, rsave, *hosted["ins"])
    return res[0], res[1], res[2], res[3], list(res[4:])


def _adamw(w, g, m, v):
    shape = w.shape
    cols = shape[-1]
    rows = w.size // cols
    w2, g2, m2, v2 = (t.reshape(rows, cols) for t in (w, g, m, v))
    tr, tc = _tile(rows, 512, 8), _tile(cols, 1024, 128)
    c1 = 1.0 - ADAM_B1 ** ADAM_STEP
    c2 = 1.0 - ADAM_B2 ** ADAM_STEP

    def body(w_ref, g_ref, m_ref, v_ref, d_ref, nm_ref, nv_ref):
        gv = g_ref[...]
        nm = ADAM_B1 * m_ref[...] + (1.0 - ADAM_B1) * gv
        nv = ADAM_B2 * v_ref[...] + (1.0 - ADAM_B2) * (gv * gv)
        d_ref[...] = -ADAM_LR * ((nm / c1) / (jnp.sqrt(nv / c2) + ADAM_EPS) + ADAM_WD * w_ref[...])
        nm_ref[...] = nm
        nv_ref[...] = nv

    spec = pl.BlockSpec((tr, tc), lambda i, j: (i, j))
    sd = jax.ShapeDtypeStruct((rows, cols), F32)
    outs = pl.pallas_call(
        body, name="adamw", grid=(rows // tr, cols // tc),
        in_specs=[spec] * 4, out_specs=[spec] * 3, out_shape=[sd] * 3,
        compiler_params=_params(("parallel", "parallel")),
    )(w2, g2, m2, v2)
    return tuple(o.reshape(shape) for o in outs)


def _sum_blocks(gathered, n, rows):
    width = gathered.shape[1]
    tw = _tile(width, 8192)

    def body(g_ref, o_ref):
        acc = g_ref[0:rows, :]
        for b in range(1, n):
            acc = acc + g_ref[b * rows:(b + 1) * rows, :]
        o_ref[...] = acc

    return pl.pallas_call(
        body, name="sum_blocks", grid=(width // tw,),
        in_specs=[pl.BlockSpec((n * rows, tw), lambda i: (0, i))],
        out_specs=pl.BlockSpec((rows, tw), lambda i: (0, i)),
        out_shape=jax.ShapeDtypeStruct((rows, width), F32),
        compiler_params=_params(("parallel",)),
    )(gathered)


def _silu_bf16(c_rows):
    def body(c_ref, o_ref):
        cv = c_ref[...]
        o_ref[...] = (cv * _sigmoid(cv)).astype(BF16)

    return pl.pallas_call(
        body, name="silu_c", out_shape=jax.ShapeDtypeStruct(c_rows.shape, BF16),
        in_specs=[pl.BlockSpec(memory_space=pltpu.VMEM)], out_specs=pl.BlockSpec(memory_space=pltpu.VMEM),
    )(c_rows)


def _place():
    x, y, c = lax.axis_index("x"), lax.axis_index("y"), lax.axis_index("c")
    chips = [(1 - x, y), (x, 1 - y), (1 - x, 1 - y)]
    return x, y, c, chips


def _all_gather_small(block, name):
    m_per, n = block.shape

    def body(x_ref, out_ref, send_sems, recv_sems, local_sem):
        x, y, c, chips = _place()
        me, sibling = (x, y, c), (x, y, 1 - c)

        def rows(px, py, pc):
            return out_ref.at[pl.ds((4 * px + 2 * py + pc) * m_per, m_per), :]

        def copy(k, blk, to, src=None):
            return pltpu.make_async_remote_copy(
                src_ref=rows(*blk) if src is None else src, dst_ref=rows(*blk),
                send_sem=send_sems.at[k], recv_sem=recv_sems.at[k], device_id=to, device_id_type=MESH)

        mine = pltpu.make_async_copy(x_ref, rows(*me), local_sem)
        mine.start()
        first = [copy(0, me, sibling, src=x_ref)]
        first += [copy(1 + j, me, (*chip, c), src=x_ref) for j, chip in enumerate(chips)]
        for cp in first:
            cp.start()
        passed = [copy(4 + j, (*chip, c), sibling) for j, chip in enumerate(chips)]
        for j, chip in enumerate(chips):
            copy(1 + j, (*chip, c), me).wait_recv()
            passed[j].start()
        copy(0, sibling, me).wait_recv()
        for j, chip in enumerate(chips):
            copy(4 + j, (*chip, 1 - c), me).wait_recv()
        for cp in first + passed:
            cp.wait_send()
        mine.wait()

    return pl.pallas_call(
        body, name=name, out_shape=jax.ShapeDtypeStruct((N_DEV * m_per, n), block.dtype),
        in_specs=[pl.BlockSpec(memory_space=pltpu.VMEM)], out_specs=pl.BlockSpec(memory_space=pltpu.VMEM),
        scratch_shapes=[pltpu.SemaphoreType.DMA((7,)), pltpu.SemaphoreType.DMA((7,)), pltpu.SemaphoreType.DMA],
        compiler_params=pltpu.CompilerParams(vmem_limit_bytes=V7X_VMEM_LIMIT),
    )(block)


_ANY = pl.BlockSpec(memory_space=pl.ANY)


def _my_chip():
    return 2 * lax.axis_index("x") + lax.axis_index("y")


def _place_shard(w):
    layers, r, cols = w.shape
    tr, tc = _tile(r, 512, 16), _tile(cols, 1024)

    def body(w_ref, o_ref):
        o_ref[...] = w_ref[...].astype(BF16)

    return pl.pallas_call(
        body, name="place_shard", grid=(layers, r // tr, cols // tc),
        in_specs=[pl.BlockSpec((None, tr, tc), lambda l, i, n: (l, i, n))],
        out_specs=pl.BlockSpec((None, None, tr, tc), lambda l, i, n: (_my_chip(), l, i, n)),
        out_shape=jax.ShapeDtypeStruct((N_CHIPS, layers, r, cols), BF16),
        compiler_params=_params(("parallel", "parallel", "parallel")),
    )(w)


def _gather_job(bufs, layer):
    n = len(bufs)
    layer_of = [layer] * n if isinstance(layer, int) else list(layer)

    def tools(outs, send_sems, recv_sems):
        x, y, c, chips = _place()

        def half(t, chip_idx, hc):
            h = outs[t].shape[2] // 2
            return outs[t].at[chip_idx, layer_of[t], pl.ds(hc * h, h), :]

        def copy(t, k, ref, to):
            return pltpu.make_async_remote_copy(src_ref=ref, dst_ref=ref, send_sem=send_sems.at[6 * t + k],
                                                recv_sem=recv_sems.at[6 * t + k], device_id=to, device_id_type=MESH)

        return x, y, c, chips, half, copy

    def start(ins, outs, send_sems, recv_sems):
        x, y, c, chips, half, copy = tools(outs, send_sems, recv_sems)
        for t in range(n):
            for k, chip in enumerate(chips):
                copy(t, k, half(t, 2 * x + y, c), (*chip, c)).start()

    def mid(ins, outs, send_sems, recv_sems):
        x, y, c, chips, half, copy = tools(outs, send_sems, recv_sems)
        for t in range(n):
            for k, (cx, cy) in enumerate(chips):
                landed = half(t, 2 * cx + cy, c)
                copy(t, k, landed, (cx, cy, c)).wait_recv()
                copy(t, 3 + k, landed, (x, y, 1 - c)).start()

    def finish(ins, outs, send_sems, recv_sems):
        x, y, c, chips, half, copy = tools(outs, send_sems, recv_sems)
        for t in range(n):
            for k, (cx, cy) in enumerate(chips):
                copy(t, 3 + k, half(t, 2 * cx + cy, 1 - c), (x, y, 1 - c)).wait_recv()
        for t in range(n):
            for k, (cx, cy) in enumerate(chips):
                copy(t, k, half(t, 2 * x + y, c), (cx, cy, c)).wait_send()
                copy(t, 3 + k, half(t, 2 * cx + cy, c), (x, y, 1 - c)).wait_send()

    return _Job(ins=bufs, out_shapes=[jax.ShapeDtypeStruct(b.shape, b.dtype) for b in bufs],
                aliases=[(t, t) for t in range(n)], n_sems=6 * n, start=start, mid=mid, finish=finish)


def _send_job(parts):
    n = len(parts)

    def copies(ins, outs, send_sems, recv_sems):
        x, y, c, chips = _place()
        return [pltpu.make_async_remote_copy(
            src_ref=ins[t].at[2 * cx + cy], dst_ref=outs[t].at[k], send_sem=send_sems.at[3 * t + k],
            recv_sem=recv_sems.at[3 * t + k], device_id=(cx, cy, c), device_id_type=MESH)
            for t in range(n) for k, (cx, cy) in enumerate(chips)]

    def start(*args):
        for cp in copies(*args):
            cp.start()

    def finish(*args):
        for cp in copies(*args):
            cp.wait()

    return _Job(ins=parts, out_shapes=[jax.ShapeDtypeStruct((3,) + p.shape[1:], p.dtype) for p in parts],
                aliases=[], n_sems=3 * n, start=start, finish=finish)


def _run_job(job, name):
    def body(*refs):
        _, _, jargs = _job_refs(job, refs, 0, 0)
        job.start(*jargs)
        if job.mid is not None:
            job.mid(*jargs)
        job.finish(*jargs)

    hosted = _host(job, 0, 0)
    return list(pl.pallas_call(
        body, name=name, out_shape=hosted["out_shape"], in_specs=hosted["in_specs"], out_specs=hosted["out_specs"],
        scratch_shapes=hosted["scratch"], input_output_aliases=hosted["aliases"],
    )(*hosted["ins"]))


def _swap_job(grads):
    n = len(grads)

    def copies(ins, outs, send_sems, recv_sems):
        x, y, c, _ = _place()
        return [pltpu.make_async_remote_copy(
            src_ref=ins[t].at[:, pl.ds((1 - c) * (ins[t].shape[1] // 2), ins[t].shape[1] // 2), :], dst_ref=outs[t],
            send_sem=send_sems.at[t], recv_sem=recv_sems.at[t], device_id=(x, y, 1 - c), device_id_type=MESH)
            for t in range(n)]

    def start(*args):
        for cp in copies(*args):
            cp.start()

    def finish(*args):
        for cp in copies(*args):
            cp.wait()

    return _Job(ins=grads, aliases=[], n_sems=n, start=start, finish=finish,
                out_shapes=[jax.ShapeDtypeStruct((g.shape[0], g.shape[1] // 2, g.shape[2]), g.dtype) for g in grads])


def _swap_halves(grads):
    return _run_job(_swap_job(grads), "swap_halves")


def _share_halves(bufs):
    n = len(bufs)

    def body(*refs):
        outs = refs[n:2 * n]
        send_sems, recv_sems = refs[2 * n:]
        x, y, c, _ = _place()

        def half(t, hc):
            h = outs[t].shape[1] // 2
            return outs[t].at[:, pl.ds(hc * h, h), :]

        def copy(t, ref):
            return pltpu.make_async_remote_copy(src_ref=ref, dst_ref=ref, send_sem=send_sems.at[t],
                                                recv_sem=recv_sems.at[t], device_id=(x, y, 1 - c),
                                                device_id_type=MESH)

        sends = [copy(t, half(t, c)) for t in range(n)]
        for cp in sends:
            cp.start()
        for t in range(n):
            copy(t, half(t, 1 - c)).wait_recv()
        for cp in sends:
            cp.wait_send()

    return pl.pallas_call(
        body, name="share_halves",
        out_shape=[jax.ShapeDtypeStruct(b.shape, b.dtype) for b in bufs],
        in_specs=[_ANY] * n, out_specs=[_ANY] * n, input_output_aliases={t: t for t in range(n)},
        scratch_shapes=[pltpu.SemaphoreType.DMA((n,)), pltpu.SemaphoreType.DMA((n,))],
    )(*bufs)


def _sum_sibling(grad, recv):
    _, r, cols = grad.shape
    h = r // 2
    tr, tc = _tile(h, 512, 16), _tile(cols, 1024)
    per = h // tr

    def body(g_ref, r_ref, o_ref):
        o_ref[...] = (g_ref[...].astype(F32) + r_ref[...].astype(F32)).astype(BF16)

    return pl.pallas_call(
        body, name="sum_sibling", grid=(N_CHIPS, per, cols // tc),
        in_specs=[pl.BlockSpec((None, tr, tc), lambda j, i, n: (j, lax.axis_index("c") * per + i, n)),
                  pl.BlockSpec((None, tr, tc), lambda j, i, n: (j, i, n))],
        out_specs=pl.BlockSpec((None, tr, tc), lambda j, i, n: (j, i, n)),
        out_shape=jax.ShapeDtypeStruct((N_CHIPS, h, cols), BF16),
        compiler_params=_params(("parallel", "parallel", "parallel")),
    )(grad, recv)


def _sum_owner(parts, recv, layer, layers, prev=None):
    _, h, cols = parts.shape
    tr, tc = _tile(h, 512, 16), _tile(cols, 1024)
    per = h // tr

    def body(p_ref, r_ref, *rest):
        o_ref = rest[-1]
        o_ref[...] = (p_ref[...].astype(F32) + r_ref[0].astype(F32) + r_ref[1].astype(F32)
                      + r_ref[2].astype(F32))

    in_specs = [pl.BlockSpec((None, tr, tc), lambda i, n: (_my_chip(), i, n)),
                pl.BlockSpec((3, tr, tc), lambda i, n: (0, i, n))]
    args = [parts, recv]
    aliases = {}
    if prev is not None:
        in_specs.append(_ANY)
        args.append(prev)
        aliases = {2: 0}
    return pl.pallas_call(
        body, name="sum_owner", grid=(per, cols // tc), in_specs=in_specs,
        out_specs=pl.BlockSpec((None, tr, tc), lambda i, n: (layer, lax.axis_index("c") * per + i, n)),
        out_shape=jax.ShapeDtypeStruct((layers, 2 * h, cols), F32),
        input_output_aliases=aliases,
        compiler_params=_params(("parallel", "parallel")),
    )(*args)


def _proj_fetching(h, w_buf, layer, tm, tn, epilogue):
    s, d = h.shape
    cs = w_buf.shape[3]
    per, gi = cs // tn, s // tm
    n_cols = N_CHIPS * cs

    def slot_chip(m):
        return 2 * ((lax.axis_index("x") + m % 2) % 2) + (lax.axis_index("y") + m // 2) % 2

    def body(h_ref, buf_in, proj_ref, projb_ref, buf, w_vmem, send_sems, recv_sems, w_sems):
        m, i, j = pl.program_id(0), pl.program_id(1), pl.program_id(2)
        x, y, c, chips = _place()
        sibling = (x, y, 1 - c)
        step = (m * gi + i) * per + j
        slot = step % 2

        def half(chip_idx, hc):
            return buf.at[chip_idx, layer, pl.ds(hc * (d // 2), d // 2), :]

        def copy(k, ref, to):
            return pltpu.make_async_remote_copy(src_ref=ref, dst_ref=ref, send_sem=send_sems.at[k],
                                                recv_sem=recv_sems.at[k], device_id=to, device_id_type=MESH)

        def fetch(mm, jj, into):
            cols = pl.ds(pl.multiple_of(jj * tn, tn), tn)
            return pltpu.make_async_copy(buf.at[slot_chip(mm), layer, :, cols], w_vmem.at[into], w_sems.at[into])

        @pl.when(step == 0)
        def _():
            for k, chip in enumerate(chips):
                copy(k, half(2 * x + y, c), (*chip, c)).start()
            fetch(0, 0, 0).start()

        row_end = j == per - 1
        slot_end = row_end & (i == gi - 1)
        for k, (cx, cy) in enumerate(chips):
            @pl.when(slot_end & (m == k))
            def _():
                landed = half(2 * cx + cy, c)
                copy(k, landed, (cx, cy, c)).wait_recv()
                copy(3 + k, landed, sibling).start()
                copy(3 + k, half(2 * cx + cy, 1 - c), sibling).wait_recv()

        @pl.when(jnp.logical_not(slot_end & (m == N_CHIPS - 1)))
        def _():
            fetch(jnp.where(slot_end, m + 1, m), jnp.where(row_end, 0, j + 1), 1 - slot).start()

        fetch(m, j, slot).wait()
        acc = jnp.dot(h_ref[...], w_vmem[slot], preferred_element_type=F32)
        vals = epilogue(slot_chip(m) * per + j, acc)
        proj_ref[...] = vals[0]
        projb_ref[...] = vals[1].astype(BF16)

        @pl.when(slot_end & (m == N_CHIPS - 1))
        def _():
            for k, (cx, cy) in enumerate(chips):
                copy(k, half(2 * x + y, c), (cx, cy, c)).wait_send()
                copy(3 + k, half(2 * cx + cy, c), sibling).wait_send()

    out_blk = pl.BlockSpec((tm, tn), lambda m, i, j: (i, slot_chip(m) * per + j))
    proj, projb, filled = pl.pallas_call(
        body, name="proj_fetching", grid=(N_CHIPS, gi, per),
        in_specs=[pl.BlockSpec((tm, d), lambda m, i, j: (i, 0)), _ANY],
        out_specs=[out_blk, out_blk, _ANY],
        out_shape=[jax.ShapeDtypeStruct((s, n_cols), F32), jax.ShapeDtypeStruct((s, n_cols), BF16),
                   jax.ShapeDtypeStruct(w_buf.shape, w_buf.dtype)],
        scratch_shapes=[pltpu.VMEM((2, d, tn), BF16), pltpu.SemaphoreType.DMA((6,)), pltpu.SemaphoreType.DMA((6,)),
                        pltpu.SemaphoreType.DMA((2,))],
        input_output_aliases={1: 2},
        compiler_params=_params(("arbitrary", "arbitrary", "arbitrary")),
    )(h, w_buf)
    return proj, projb, filled


def _layer_fwd(x, mod, wbuf, p, layer, cfg, make_jobs):
    d, tr, tm = cfg["d"], cfg["tr"], cfg["tm"]
    s = x.shape[0]
    weight = lambda n: _Sharded(wbuf[n], BIG_KIND[n], layer=layer)
    hosted = lambda key: make_jobs[key](wbuf) if key in make_jobs else (None, ())
    shift, scale, rg = mod[:, 0:d], mod[:, d:2 * d], mod[:, 2 * d:3 * d]
    h, h_t = _norm_fwd(x, p["norm_g"], scale, shift, tr)
    tn_in = cfg["tn_in"]
    q_lo, q_hi, q_scale = d // tn_in, 2 * d // tn_in, HEAD_DIM ** -0.5
    blk_in = pl.BlockSpec((tm, tn_in), lambda i, j, k: (i, j))

    def proj_out(col, acc):
        return acc, acc * jnp.where((col >= q_lo) & (col < q_hi), q_scale, 1.0)

    if make_jobs.get("fetch_w_in"):
        proj, projb, wbuf["w_in"] = _proj_fetching(h, wbuf["w_in"], layer, tm, tn_in, proj_out)
    else:
        job, names = hosted("proj")
        proj, projb, *job_out = _mm_nn("proj", h, weight("w_in"), tm=tm, tn=tn_in, tk=d, with_col=True,
                                       epilogue=proj_out, job=job,
                                       outs=[(jax.ShapeDtypeStruct((s, 7 * d), F32), blk_in),
                                             (jax.ShapeDtypeStruct((s, 7 * d), BF16), blk_in)])
        wbuf.update(zip(names, job_out))
    ya = _pool_fwd(proj, p["pool_w"], p["pool_scale"], d, tr)
    yc = _conv_fwd(proj, p["conv_w"], d, tr)
    job, names = hosted("attn")
    att, yb, rsave, job_out = _sb_fwd(proj, projb, d, cfg["blk"], job)
    wbuf.update(zip(names, job_out))
    job, names = hosted("gates")
    gates, *job_out = _mm_nn("gates", h, weight("w_gate"), tm=tm, tn=cfg["tn_gate"], tk=d, job=job,
                             extras=[(p["b_gate"], pl.BlockSpec((1, cfg["tn_gate"]), lambda i, j, k: (0, j)))],
                             epilogue=lambda acc, b: (_sigmoid(acc + b),))
    wbuf.update(zip(names, job_out))
    tn = cfg["tn_d"]
    blk = pl.BlockSpec((tm, tn), lambda i, j, k: (i, j))
    sd = jax.ShapeDtypeStruct((s, d), F32)
    pa, = _mm_nn("branch_a", ya, weight("w_br_a"), tm=tm, tn=tn, tk=d // 2)
    pb, = _mm_nn("branch_b", yb, weight("w_br_b"), tm=tm, tn=tn, tk=cfg["tk_row"])
    gate_blk = lambda n: (gates, pl.BlockSpec((tm, tn), lambda i, j, k: (i, n * (d // tn) + j)))
    pc, merged = _mm_nn("branch_c", yc, weight("w_br_c"), tm=tm, tn=tn, tk=d // 2,
                        outs=[(sd, blk), (jax.ShapeDtypeStruct((s, d), BF16), blk)],
                        extras=[gate_blk(0), gate_blk(1), gate_blk(2), (pa, blk), (pb, blk)],
                        epilogue=lambda acc, g0, g1, g2, av, bv: (acc, g0 * av + g1 * bv + g2 * acc))
    out, x_next = _mm_nn("out_proj", merged, weight("w_out"), tm=tm, tn=tn, tk=cfg["tk_row"],
                         outs=[(sd, blk), (sd, blk)],
                         extras=[(x, blk), (rg, pl.BlockSpec((1, tn), lambda i, j, k: (0, j)))],
                         epilogue=lambda acc, xv, g: (acc, xv + g * acc))
    saved = dict(x=x, h_t=h_t, proj=proj, projb=projb, gates=gates, ya=ya, yb=yb, yc=yc, att=att, rsave=rsave,
                 pa=pa, pb=pb, pc=pc, merged=merged, out=out, scale=scale, rg=rg)
    return x_next, saved


def _reduce_prepare(grads):
    return [_sum_sibling(g, r) for g, r in zip(grads, _swap_halves(grads))]


def _layer_bwd(dxo, sv, wbuf, p, layer, cfg):
    d, tr, tm, tk_s = cfg["d"], cfg["tr"], cfg["tm"], cfg["tk_s"]
    weight = lambda n: _Sharded(wbuf[n], BIG_KIND[n], layer=layer)
    dout, drg = _resid_bwd(dxo, sv["out"], sv["rg"], tr)
    dmerged = _mm_nt("d_merged", dout, weight("w_out"), tm=tm, tn=cfg["tn_row"], tk=d)
    g_out = _mm_tn("g_w_out", sv["merged"], dout, "row", tm=cfg["tn_row"], tn=cfg["tn_d"], tk=tk_s)
    dpa, dpb, dpc, dlogit, g_bgate = _merge_bwd(dmerged, sv["gates"], sv["pa"], sv["pb"], sv["pc"], cfg["tr_small"])
    dya = _mm_nt("d_ya", dpa, weight("w_br_a"), tm=tm, tn=cfg["tn_half"], tk=cfg["tn_d"])
    dyb = _mm_nt("d_yb", dpb, weight("w_br_b"), tm=tm, tn=cfg["tn_row"], tk=d)
    dyc = _mm_nt("d_yc", dpc, weight("w_br_c"), tm=tm, tn=cfg["tn_half"], tk=cfg["tn_d"])
    g_a = _mm_tn("g_w_br_a", sv["ya"], dpa, "col", tm=cfg["tn_half"], tn=cfg["tn_d"], tk=tk_s)
    g_b = _mm_tn("g_w_br_b", sv["yb"], dpb, "row", tm=cfg["tn_row"], tn=cfg["tn_d"], tk=tk_s)
    g_c = _mm_tn("g_w_br_c", sv["yc"], dpc, "col", tm=cfg["tn_half"], tn=cfg["tn_d"], tk=tk_s)
    g_gate = _mm_grad("g_w_gate", sv["h_t"], dlogit, "col", tm=cfg["tm_g"], tn=cfg["tn_gate"], tk=tk_s)
    early = ("w_gate", "w_br_a", "w_br_b", "w_br_c", "w_out")
    early_grads = [g_gate, g_a, g_b, g_c, g_out]
    d_a, g_pool_w, g_pool_scale = _pool_bwd(sv["proj"], dya, p["pool_w"], p["pool_scale"], d, tr)
    d_c, g_conv_w, swapped = _conv_bwd(sv["proj"], dyc, p["conv_w"], d, tr, _swap_job(early_grads))
    parts = dict(zip(early, [_sum_sibling(g, r) for g, r in zip(early_grads, swapped)]))
    dq, dk, dv, dzb, sent = _sb_bwd(sv["proj"], sv["projb"], sv["att"], dyb, sv["rsave"], d, cfg["blk"],
                                    _send_job([parts[n] for n in early]))
    theirs = dict(zip(early, sent))
    dproj = jnp.concatenate([d_a, dq.astype(BF16), dk.astype(BF16), dv.astype(BF16), dzb.astype(BF16), d_c], axis=1)
    g_in = _mm_grad("g_w_in", sv["h_t"], dproj, "col", tm=cfg["tm_g"], tn=cfg["tn_in"], tk=tk_s)
    parts["w_in"], = _reduce_prepare([g_in])
    dh_in, (theirs["w_in"],) = _mm_nt("d_h_in", dproj, weight("w_in"), tm=tm, tn=cfg["tn_d"], tk=cfg["tk_in"],
                                      job=_send_job([parts["w_in"]]))
    tn = cfg["tn_d"]
    dh = _mm_nt("d_h_gate", dlogit, weight("w_gate"), tm=tm, tn=tn, tk=cfg["tk_gate"],
                extras=[(dh_in, pl.BlockSpec((tm, tn), lambda i, j, k: (i, j)))],
                epilogue=lambda acc, prev: (prev + acc,))
    dx, dshift, dscale, g_norm = _norm_bwd(dh, sv["x"], p["norm_g"], sv["scale"], dxo, tr)
    small = dict(dmod=jnp.concatenate([dshift, dscale, drg], axis=1), norm_g=g_norm, pool_scale=g_pool_scale,
                 b_gate=g_bgate, conv_w=g_conv_w, pool_w=g_pool_w)
    return dx, parts, theirs, small


BIG = ("w_in", "w_gate", "w_br_a", "w_br_b", "w_br_c", "w_out")
BIG_KIND = dict(w_in="col", w_gate="col", w_br_a="col", w_br_b="row", w_br_c="col", w_out="row")


def _pad_to(v, n):
    return jnp.pad(v, (0, n - v.shape[0]))


def kernel(x, c, norm_g, w_ada, b_ada, w_in, pool_w, pool_scale, conv_w, w_br_a, w_br_b, w_br_c, w_gate, b_gate, w_out, final_g, loss_target, m_norm_g, m_w_ada, m_b_ada, m_w_in, m_pool_w, m_pool_scale, m_conv_w, m_w_br_a, m_w_br_b, m_w_br_c, m_w_gate, m_b_gate, m_w_out, m_final_g, v_norm_g, v_w_ada, v_b_ada, v_w_in, v_pool_w, v_pool_scale, v_conv_w, v_w_br_a, v_w_br_b, v_w_br_c, v_w_gate, v_b_gate, v_w_out, v_final_g):
    weights = dict(norm_g=norm_g, w_ada=w_ada, b_ada=b_ada, w_in=w_in, pool_w=pool_w, pool_scale=pool_scale,
                   conv_w=conv_w, w_br_a=w_br_a, w_br_b=w_br_b, w_br_c=w_br_c, w_gate=w_gate, b_gate=b_gate,
                   w_out=w_out, final_g=final_g)
    mom_m = dict(norm_g=m_norm_g, w_ada=m_w_ada, b_ada=m_b_ada, w_in=m_w_in, pool_w=m_pool_w,
                 pool_scale=m_pool_scale, conv_w=m_conv_w, w_br_a=m_w_br_a, w_br_b=m_w_br_b, w_br_c=m_w_br_c,
                 w_gate=m_w_gate, b_gate=m_b_gate, w_out=m_w_out, final_g=m_final_g)
    mom_v = dict(norm_g=v_norm_g, w_ada=v_w_ada, b_ada=v_b_ada, w_in=v_w_in, pool_w=v_pool_w,
                 pool_scale=v_pool_scale, conv_w=v_conv_w, w_br_a=v_w_br_a, w_br_b=v_w_br_b, w_br_c=v_w_br_c,
                 w_gate=v_w_gate, b_gate=v_b_gate, w_out=v_w_out, final_g=v_final_g)
    names = list(weights)

    _, s, d = x.shape
    layers = norm_g.shape[0]
    pw, gd, cc = d // 2, d // 8, d // 2
    ada_cols = 3 * d // N_CHIPS
    xi, yi, ci = lax.axis_index("x"), lax.axis_index("y"), lax.axis_index("c")
    me = 4 * xi + 2 * yi + ci
    my_chip = 2 * xi + yi

    cfg = dict(
        d=d, tr=_tile(s, 256, 16), tr_small=_tile(s, 128, 16), tm=_tile(s, 1024, 16), tk_s=_tile(s, 4096, 16),
        blk=_tile(s, min(256, max(s // 4, 16)), 16),
        tn_in=_tile(7 * d // N_CHIPS, 512), tk_in=_tile(7 * d // N_CHIPS, 3584),
        tn_gate=_tile(3 * d // N_CHIPS, 512), tk_gate=_tile(3 * d // N_CHIPS, 1536), tn_d=_tile(d // N_CHIPS, 512),
        tn_row=_tile(d // N_CHIPS, 512, 16), tk_row=_tile(d // N_CHIPS, 512, 16),
        tn_half=_tile(d // 2, 512), tm_g=_tile(d, 1024),
    )

    conv_flat = conv_w.reshape(-1)
    conv_len = -(-conv_flat.shape[0] // 1024) * 1024
    pack0 = jnp.concatenate([c.reshape(-1), _pad_to(conv_flat, conv_len), pool_w.reshape(-1)])
    w0 = -(-pack0.shape[0] // 1024) * 1024
    g0 = _all_gather_small(_pad_to(pack0, w0).reshape(8, w0 // 8), "gather_small").reshape(N_DEV, w0)
    c_all = g0[:, 0:d]
    chip_rows = g0[0::2]
    conv_full = jnp.concatenate(
        [chip_rows[j, d:d + conv_flat.shape[0]].reshape(conv_w.shape) for j in range(N_CHIPS)], axis=2)
    pool_full = jnp.concatenate(
        [chip_rows[j, d + conv_len:d + conv_len + pool_w.size].reshape(pool_w.shape) for j in range(N_CHIPS)],
        axis=2)
    pool_bf = pool_full.astype(BF16)

    sc16 = _silu_bf16(jnp.pad(c_all, ((0, 16 - N_DEV), (0, 0))))
    mods = []
    for l in range(layers):
        bias = lax.dynamic_slice(b_ada[l], (my_chip * ada_cols,), (ada_cols,)).reshape(1, ada_cols)
        tn = _tile(ada_cols, 512)
        mod_l, = _mm("mod", sc16, w_ada[l], dims=NN, grid=(1, ada_cols // tn, 1),
                     a_spec=pl.BlockSpec((16, d), lambda i, j, k: (0, 0)),
                     b_spec=pl.BlockSpec((d, tn), lambda i, j, k: (0, j)),
                     acc_shape=(16, tn),
                     outs=[(jax.ShapeDtypeStruct((16, ada_cols), F32), pl.BlockSpec((16, tn), lambda i, j, k: (0, j)))],
                     extras=[(bias, pl.BlockSpec((1, tn), lambda i, j, k: (0, j)))],
                     epilogue=lambda acc, b: (acc + b,))
        mods.append(mod_l[0:N_DEV])
    g1 = _all_gather_small(jnp.concatenate(mods, axis=1), "gather_mod")
    g1 = g1.reshape(N_CHIPS, 2, N_DEV, layers, ada_cols)[:, 0]
    mod_all = jnp.transpose(g1, (1, 2, 0, 3)).reshape(N_DEV, layers, 3 * d)
    mod_me = lax.dynamic_slice(mod_all, (me, 0, 0), (1, layers, 3 * d))[0]

    wbuf = {n: _place_shard(weights[n]) for n in BIG}
    params = [dict(norm_g=norm_g[l:l + 1], pool_scale=pool_scale[l:l + 1], b_gate=b_gate[l:l + 1],
                   conv_w=conv_full[l], pool_w=pool_bf[l]) for l in range(layers)]
    others = [n for n in BIG if n != "w_in"]

    def gather(names, layer):
        return lambda wb: (_gather_job([wb[n] for n in names], layer), names)

    act = x[0]
    saved = []
    for l in range(layers):
        nxt = ["w_in"] if l + 1 < layers else []
        if l == 0:
            late = [n for n in others if n != "w_gate"]
            make_jobs = {"fetch_w_in": True, "attn": gather(["w_gate"] + nxt, [0] + [1] * len(nxt)),
                         "gates": gather(late, 0)}
        else:
            make_jobs = {"proj": gather(others, l)}
            if nxt:
                make_jobs["attn"] = gather(nxt, l + 1)
        act, sv = _layer_fwd(act, mod_me[l:l + 1], wbuf, params[l], l, cfg, make_jobs)
        saved.append(sv)
    loss_part, dact, g_final = _final_loss(act, final_g.reshape(1, d), loss_target[0], cfg["tr"])
    loss = lax.psum(loss_part[0, 0], ("x", "y", "c"))

    small_grads, parts, theirs = [None] * layers, [None] * layers, [None] * layers
    for l in reversed(range(layers)):
        dact, parts[l], theirs[l], small_grads[l] = _layer_bwd(dact, saved[l], wbuf, params[l], l, cfg)
    grad_x = dact.reshape(x.shape)
    owned = []
    for n in BIG:
        buf = None
        for l in range(layers):
            buf = _sum_owner(parts[l][n], theirs[l][n], l, layers, prev=buf)
        owned.append(buf)
    full = _share_halves(owned)
    grads = {n: f.reshape(weights[n].shape) for n, f in zip(BIG, full)}

    small_names = ("dmod", "norm_g", "pool_scale", "b_gate", "conv_w", "pool_w")
    pieces = [small_grads[l][n].reshape(-1) for n in small_names for l in range(layers)] + [g_final.reshape(-1)]
    pack1 = jnp.concatenate(pieces)
    w1 = -(-pack1.shape[0] // 1024) * 1024
    g2 = _all_gather_small(_pad_to(pack1, w1).reshape(8, w1 // 8), "gather_grads")
    total = _sum_blocks(g2, N_DEV, 8).reshape(-1)
    off = 0
    summed = {}
    for n in small_names:
        per = small_grads[0][n].size
        summed[n] = jnp.stack([total[off + l * per:off + (l + 1) * per].reshape(small_grads[0][n].shape)
                               for l in range(layers)])
        off += layers * per
    grads["final_g"] = total[off:off + d]
    grads["norm_g"] = summed["norm_g"].reshape(layers, d)
    grads["pool_scale"] = summed["pool_scale"].reshape(layers, pw)
    grads["b_gate"] = summed["b_gate"].reshape(layers, 3 * d)
    grads["b_ada"] = summed["dmod"].reshape(layers, 3 * d)
    cs = cc // N_CHIPS
    grads["conv_w"] = lax.dynamic_slice(summed["conv_w"], (0, 0, my_chip * cs), (layers, 3, cs))
    rs_ = gd // N_CHIPS
    grads["pool_w"] = lax.dynamic_slice(summed["pool_w"], (0, 0, my_chip * rs_, 0), (layers, N_GROUPS, rs_, gd))
    dmod_all = g2.reshape(N_DEV, w1)[:, 0:layers * 3 * d].reshape(N_DEV, layers, 3 * d)
    g_ada = []
    for l in range(layers):
        cols = lax.dynamic_slice(dmod_all[:, l], (0, my_chip * ada_cols), (N_DEV, ada_cols))
        cols16 = jnp.pad(cols, ((0, 16 - N_DEV), (0, 0)))
        tn = _tile(ada_cols, 512)
        tm = _tile(d, 1024)
        ga, = _mm("g_w_ada", sc16, cols16, dims=TN, grid=(d // tm, ada_cols // tn, 1),
                  a_spec=pl.BlockSpec((16, tm), lambda i, j, k: (0, i)),
                  b_spec=pl.BlockSpec((16, tn), lambda i, j, k: (0, j)),
                  acc_shape=(tm, tn),
                  outs=[(jax.ShapeDtypeStruct((d, ada_cols), F32), pl.BlockSpec((tm, tn), lambda i, j, k: (i, j)))])
        g_ada.append(ga)
    grads["w_ada"] = jnp.stack(g_ada)

    deltas, new_m, new_v = {}, {}, {}
    for n in names:
        deltas[n], new_m[n], new_v[n] = _adamw(weights[n], grads[n], mom_m[n], mom_v[n])
    return (loss, grad_x, *[grads[n] for n in names], *[deltas[n] for n in names],
            *[new_m[n] for n in names], *[new_v[n] for n in names])
```
